```python
import jax, jax.numpy as jnp
from jax import lax
import numpy as np

D_MODEL = 1024
BATCH = 8
SEQ = 2048
DEPTH = 1

CHUNK = 64
N_META = 16
CONV_DIM = 1024
CONV_WIDTH = 31
RET_HEADS = 4
RET_QK_DIM = 256
RET_V_DIM = 512
ROPE_BASE = 10000.0
N_GROUPS = 4
EXPERTS_PER_GROUP = 4
N_EXPERTS = N_GROUPS * EXPERTS_PER_GROUP
D_EXPERT = 512
TOP_K_INNER = 2
EPS = 1e-6
IN_SPLITS = (2 * CONV_DIM, RET_HEADS * RET_QK_DIM, RET_HEADS * RET_QK_DIM,
             RET_HEADS * RET_V_DIM, RET_HEADS * RET_V_DIM, 2 * D_MODEL)
D_IN = sum(IN_SPLITS)

kernel_name = "hybrid_conv_retention_hmoe_block"


def rmsnorm(x, g):
    xf = x.astype(jnp.float32)
    y = xf * lax.rsqrt(jnp.mean(xf * xf, axis=-1, keepdims=True) + EPS)
    return (y * g.astype(jnp.float32)).astype(x.dtype)


def layernorm(x, g, b):
    xf = x.astype(jnp.float32)
    mu = jnp.mean(xf, axis=-1, keepdims=True)
    var = jnp.mean(jnp.square(xf - mu), axis=-1, keepdims=True)
    y = (xf - mu) * lax.rsqrt(var + EPS)
    return (y * g.astype(jnp.float32) + b.astype(jnp.float32)).astype(x.dtype)


def conv_module(u, w_dw, b_dw, ln_g, ln_b, w_pw):
    a, gate = jnp.split(u, 2, axis=-1)
    h = a * jax.nn.sigmoid(gate)
    h = jnp.pad(h, ((0, 0), (CONV_WIDTH - 1, 0), (0, 0)))
    h = lax.conv_general_dilated(h, w_dw[:, None, :].astype(h.dtype), window_strides=(1,),
                                 padding='VALID', dimension_numbers=('NWC', 'WIO', 'NWC'),
                                 feature_group_count=CONV_DIM) + b_dw
    h = jax.nn.silu(layernorm(h, ln_g, ln_b))
    return h @ w_pw


def rotary(x, pos):
    half = x.shape[-1] // 2
    inv = ROPE_BASE ** (-jnp.arange(half, dtype=jnp.float32) / half)
    ang = pos[:, None] * inv[None, :]
    cos, sin = jnp.cos(ang)[:, None, :], jnp.sin(ang)[:, None, :]
    x1, x2 = x[..., :half], x[..., half:]
    return jnp.concatenate([x1 * cos - x2 * sin, x2 * cos + x1 * sin], axis=-1)


def retention(q, k, v, gn_g):
    B, L = q.shape[0], q.shape[1]
    pos = jnp.arange(L, dtype=jnp.float32)
    q = rotary(q.astype(jnp.float32).reshape(B, L, RET_HEADS, RET_QK_DIM), pos)
    k = rotary(k.astype(jnp.float32).reshape(B, L, RET_HEADS, RET_QK_DIM), pos) * (RET_QK_DIM ** -0.5)
    v = v.astype(jnp.float32).reshape(B, L, RET_HEADS, RET_V_DIM)
    pad_front = (CHUNK - N_META % CHUNK) % CHUNK
    padw = ((0, 0), (pad_front, 0), (0, 0), (0, 0))
    q, k, v = jnp.pad(q, padw), jnp.pad(k, padw), jnp.pad(v, padw)
    Lp = L + pad_front
    nc = Lp // CHUNK

    def to_chunks(t):
        return t.reshape(B, nc, CHUNK, RET_HEADS, t.shape[-1]).transpose(1, 0, 3, 2, 4)

    qc, kc, vc = to_chunks(q), to_chunks(k), to_chunks(v)
    log_gamma = jnp.log(1.0 - 2.0 ** (-5.0 - jnp.arange(RET_HEADS, dtype=jnp.float32)))
    idx = jnp.arange(CHUNK, dtype=jnp.float32)
    intra_decay = jnp.exp(log_gamma[:, None, None] * jnp.abs(idx[:, None] - idx[None, :]))
    q_decay = jnp.exp(log_gamma[:, None] * (idx + 1.0))[:, :, None]
    k_decay = jnp.exp(log_gamma[:, None] * (CHUNK - 1.0 - idx))[:, :, None]
    chunk_decay = jnp.exp(log_gamma * CHUNK)[:, None, None]

    def step(state, inp):
        qi, ki, vi = inp
        scores = jnp.einsum('bhnd,bhmd->bhnm', qi, ki) * intra_decay
        out = (jnp.einsum('bhnm,bhme->bhne', scores, vi)
               + jnp.einsum('bhnd,bhde->bhne', qi, state) * q_decay)
        state = state * chunk_decay + jnp.einsum('bhmd,bhme->bhde', ki * k_decay, vi)
        return state, out

    s0 = jnp.zeros((B, RET_HEADS, RET_QK_DIM, RET_V_DIM), jnp.float32)
    _, o = lax.scan(step, s0, (qc, kc, vc))
    o = o.transpose(1, 0, 3, 2, 4).reshape(B, Lp, RET_HEADS, RET_V_DIM)[:, pad_front:]
    mu = jnp.mean(o, axis=-1, keepdims=True)
    var = jnp.mean(jnp.square(o - mu), axis=-1, keepdims=True)
    o = ((o - mu) * lax.rsqrt(var + EPS)).reshape(B, L, RET_HEADS * RET_V_DIM)
    return o * gn_g.astype(jnp.float32)


def hier_moe(h, w_group, b_group, w_exp_r, b_exp_r, w_gate, w_up, w_down):
    B, L, D = h.shape
    xt = h.reshape(B * L, D)
    glog = (xt @ w_group + b_group).astype(jnp.float32)
    gprob = jax.nn.softmax(glog, axis=-1)
    top_group = jnp.argmax(gprob, axis=-1)
    p_group = jnp.take_along_axis(gprob, top_group[:, None], axis=-1)
    elog = (xt @ w_exp_r + b_exp_r).astype(jnp.float32).reshape(-1, N_GROUPS, EXPERTS_PER_GROUP)
    sel = jnp.take_along_axis(elog, top_group[:, None, None], axis=1)[:, 0]
    vals, eidx = lax.top_k(sel, TOP_K_INNER)
    weights = p_group * jax.nn.softmax(vals, axis=-1)
    gid = top_group[:, None] * EXPERTS_PER_GROUP + eidx
    comb = jnp.einsum('tk,tke->te', weights,
                      jax.nn.one_hot(gid, N_EXPERTS, dtype=jnp.float32)).astype(xt.dtype)
    hid = jax.nn.silu(jnp.einsum('td,edf->tef', xt, w_gate)) * jnp.einsum('td,edf->tef', xt, w_up)
    out = jnp.einsum('tef,efd->td', hid * comb[:, :, None], w_down)
    return out.reshape(B, L, D)


def setup_inputs(seed: int = 0) -> dict:
    key = jax.random.key(seed)
    ks = jax.random.split(key, 24)

    def nrm(k, shape, fan_in):
        return jax.random.normal(k, shape, jnp.float32) * (fan_in ** -0.5)

    def gain(k, shape):
        return 1.0 + 0.02 * jax.random.normal(k, shape, jnp.float32)

    def small(k, shape, s):
        return s * jax.random.normal(k, shape, jnp.float32)

    return {
        "x": jax.random.normal(ks[0], (BATCH, SEQ, D_MODEL), jnp.float32),
        "meta_tokens": jax.random.normal(ks[1], (N_META, D_MODEL), jnp.float32),
        "norm_mix_g": gain(ks[2], (DEPTH, D_MODEL)),
        "w_in": nrm(ks[3], (DEPTH, D_MODEL, D_IN), D_MODEL),
        "conv_dw_w": nrm(ks[4], (DEPTH, CONV_WIDTH, CONV_DIM), CONV_WIDTH),
        "conv_dw_b": small(ks[5], (DEPTH, CONV_DIM), 0.02),
        "conv_ln_g": gain(ks[6], (DEPTH, CONV_DIM)),
        "conv_ln_b": small(ks[7], (DEPTH, CONV_DIM), 0.02),
        "conv_pw_w": nrm(ks[8], (DEPTH, CONV_DIM, D_MODEL), CONV_DIM),
        "ret_gn_g": gain(ks[9], (DEPTH, RET_HEADS * RET_V_DIM)),
        "ret_w_o": nrm(ks[10], (DEPTH, RET_HEADS * RET_V_DIM, D_MODEL), RET_HEADS * RET_V_DIM),
        "w_out": nrm(ks[11], (DEPTH, D_MODEL, D_MODEL), D_MODEL),
        "norm_ffn_g": gain(ks[12], (DEPTH, D_MODEL)),
        "w_group_router": nrm(ks[13], (DEPTH, D_MODEL, N_GROUPS), D_MODEL),
        "b_group_router": small(ks[14], (DEPTH, N_GROUPS), 0.01),
        "w_expert_router": nrm(ks[15], (DEPTH, D_MODEL, N_EXPERTS), D_MODEL),
        "b_expert_router": small(ks[16], (DEPTH, N_EXPERTS), 0.01),
        "w_expert_gate": nrm(ks[17], (DEPTH, N_EXPERTS, D_MODEL, D_EXPERT), D_MODEL),
        "w_expert_up": nrm(ks[18], (DEPTH, N_EXPERTS, D_MODEL, D_EXPERT), D_MODEL),
        "w_expert_down": nrm(ks[19], (DEPTH, N_EXPERTS, D_EXPERT, D_MODEL), D_EXPERT),
        "norm_final_g": gain(ks[20], (D_MODEL,)),
    }


def reference(x, meta_tokens, norm_mix_g, w_in, conv_dw_w, conv_dw_b, conv_ln_g, conv_ln_b,
              conv_pw_w, ret_gn_g, ret_w_o, w_out, norm_ffn_g, w_group_router, b_group_router,
              w_expert_router, b_expert_router, w_expert_gate, w_expert_up, w_expert_down,
              norm_final_g):
    B = x.shape[0]
    meta = jnp.broadcast_to(meta_tokens[None].astype(x.dtype), (B, N_META, D_MODEL))
    h = jnp.concatenate([meta, x], axis=1)
    cuts = [int(c) for c in np.cumsum(IN_SPLITS)[:-1]]
    for layer in range(DEPTH):
        u = rmsnorm(h, norm_mix_g[layer])
        proj = u @ w_in[layer]
        conv_in, q, k, v, g_ret, g_merge = jnp.split(proj, cuts, axis=-1)
        y_conv = conv_module(conv_in, conv_dw_w[layer], conv_dw_b[layer], conv_ln_g[layer],
                             conv_ln_b[layer], conv_pw_w[layer])
        o_ret = retention(q, k, v, ret_gn_g[layer])
        y_ret = (jax.nn.silu(g_ret.astype(jnp.float32)) * o_ret).astype(h.dtype) @ ret_w_o[layer]
        gate_a, gate_b = jnp.split(g_merge, 2, axis=-1)
        merged = jax.nn.sigmoid(gate_a) * y_conv + jax.nn.sigmoid(gate_b) * y_ret
        h = h + merged @ w_out[layer]
        u = rmsnorm(h, norm_ffn_g[layer])
        h = h + hier_moe(u, w_group_router[layer], b_group_router[layer], w_expert_router[layer],
                         b_expert_router[layer], w_expert_gate[layer], w_expert_up[layer],
                         w_expert_down[layer])
    out = rmsnorm(h, norm_final_g)
    return out[:, N_META:]
```

```python
import functools
import math

import jax
import jax.numpy as jnp
from jax import lax
from jax.experimental import pallas as pl
from jax.experimental.pallas import tpu as pltpu

D_MODEL = 1024
CHUNK = 64
N_META = 16
CONV_DIM = 1024
CONV_WIDTH = 31
RET_HEADS = 4
RET_QK_DIM = 256
RET_V_DIM = 512
ROPE_BASE = 10000.0
N_GROUPS = 4
EXPERTS_PER_GROUP = 4
N_EXPERTS = N_GROUPS * EXPERTS_PER_GROUP
D_EXPERT = 512
EPS = 1e-6
D_IN = 2 * CONV_DIM + 2 * RET_HEADS * RET_QK_DIM + 2 * RET_HEADS * RET_V_DIM + 2 * D_MODEL

LANES = 128
SUBLANES = 8
CONV_HALO = 32
RET_BLOCK = 256
ROUTER_LANES = LANES
VMEM_LIMIT = 48 * 1024 * 1024

F32 = jnp.float32
BF16 = jnp.bfloat16


def _sigmoid(x):
    return 1.0 / (1.0 + jnp.exp(-x))


def _dot(a, b):
    return jnp.dot(a, b, preferred_element_type=F32)


def _in_proj_kernel(x_ref, g_ref, w_ref, o_ref, u_ref):
    @pl.when(pl.program_id(1) == 0)
    def _():
        x = x_ref[...]
        ms = jnp.mean(x * x, axis=-1, keepdims=True)
        u_ref[...] = (x * lax.rsqrt(ms + EPS) * g_ref[...]).astype(BF16)

    o_ref[...] = _dot(u_ref[...], w_ref[...]).astype(o_ref.dtype)


def _in_proj(x, g, w_bf, tm, tn):
    t, d = x.shape
    n = w_bf.shape[1]
    return pl.pallas_call(
        _in_proj_kernel,
        grid=(t // tm, n // tn),
        in_specs=[
            pl.BlockSpec((tm, d), lambda i, j: (i, 0)),
            pl.BlockSpec((1, d), lambda i, j: (0, 0)),
            pl.BlockSpec((d, tn), lambda i, j: (0, j)),
        ],
        out_specs=pl.BlockSpec((tm, tn), lambda i, j: (i, j)),
        out_shape=jax.ShapeDtypeStruct((t, n), BF16),
        scratch_shapes=[pltpu.VMEM((tm, d), BF16)],
        compiler_params=pltpu.CompilerParams(
            dimension_semantics=("parallel", "arbitrary"),
            vmem_limit_bytes=VMEM_LIMIT),
        name="in_proj",
    )(x, g, w_bf)


def _conv_kernel(a_ref, gate_ref, ma_ref, mg_ref, wdw_ref, bdw_ref, lng_ref, lnb_ref,
                 wpw_ref, o_ref, hbuf_ref, shift_ref, cbuf_ref, *, tt):
    i = pl.program_id(1)

    @pl.when(i == 0)
    def _():
        hbuf_ref[0:CONV_HALO - N_META, :] = jnp.zeros((CONV_HALO - N_META, CONV_DIM), F32)
        ma = ma_ref[...].astype(F32)
        mg = mg_ref[...].astype(F32)
        hbuf_ref[CONV_HALO - N_META:CONV_HALO, :] = ma * _sigmoid(mg)

    @pl.when(i > 0)
    def _():
        hbuf_ref[0:CONV_HALO, :] = hbuf_ref[tt:tt + CONV_HALO, :]

    a = a_ref[...].astype(F32)
    g = gate_ref[...].astype(F32)
    hbuf_ref[CONV_HALO:CONV_HALO + tt, :] = a * _sigmoid(g)

    span = tt + CONV_HALO - SUBLANES
    for s in range(1, SUBLANES):
        shift_ref[s - 1, 0:span, :] = hbuf_ref[s:s + span, :]

    first = CONV_HALO - (CONV_WIDTH - 1)
    for cb in range(CONV_DIM // LANES):
        cols = slice(cb * LANES, (cb + 1) * LANES)
        taps = [jnp.broadcast_to(wdw_ref[j:j + 1, cols], (SUBLANES, LANES))
                for j in range(CONV_WIDTH)]
        bias = jnp.broadcast_to(bdw_ref[:, cols], (SUBLANES, LANES))

        def row_body(r, carry, cols=cols, taps=taps, bias=bias):
            r0 = pl.multiple_of(r * SUBLANES, SUBLANES)
            acc = bias
            for j in range(CONV_WIDTH):
                shift, base = (first + j) % SUBLANES, (first + j) // SUBLANES * SUBLANES
                rows = pl.ds(pl.multiple_of(r0 + base, SUBLANES), SUBLANES)
                if shift == 0:
                    win = hbuf_ref[rows, cols]
                else:
                    win = shift_ref[shift - 1, rows, cols]
                acc = acc + taps[j] * win
            cbuf_ref[pl.ds(r0, SUBLANES), cols] = acc
            return carry

        lax.fori_loop(0, tt // SUBLANES, row_body, 0)

    c = cbuf_ref[...]
    mu = jnp.mean(c, axis=-1, keepdims=True)
    cc = c - mu
    var = jnp.mean(cc * cc, axis=-1, keepdims=True)
    y = cc * lax.rsqrt(var + EPS) * lng_ref[...] + lnb_ref[...]
    y = y * _sigmoid(y)
    o_ref[...] = _dot(y.astype(BF16), wpw_ref[...]).astype(o_ref.dtype)


def _conv(proj, proj_meta, wdw, bdw, lng, lnb, wpw_bf, batch, seq, tt):
    t = proj.shape[0]
    nt = seq // tt
    const = lambda b, i: (0, 0)
    return pl.pallas_call(
        functools.partial(_conv_kernel, tt=tt),
        grid=(batch, nt),
        in_specs=[
            pl.BlockSpec((tt, CONV_DIM), lambda b, i: (b * nt + i, 0)),
            pl.BlockSpec((tt, CONV_DIM), lambda b, i: (b * nt + i, 1)),
            pl.BlockSpec((N_META, CONV_DIM), lambda b, i: (0, 0)),
            pl.BlockSpec((N_META, CONV_DIM), lambda b, i: (0, 1)),
            pl.BlockSpec((CONV_WIDTH, CONV_DIM), const),
            pl.BlockSpec((1, CONV_DIM), const),
            pl.BlockSpec((1, CONV_DIM), const),
            pl.BlockSpec((1, CONV_DIM), const),
            pl.BlockSpec((CONV_DIM, D_MODEL), const),
        ],
        out_specs=pl.BlockSpec((tt, D_MODEL), lambda b, i: (b * nt + i, 0)),
        out_shape=jax.ShapeDtypeStruct((t, D_MODEL), BF16),
        scratch_shapes=[pltpu.VMEM((CONV_HALO + tt, CONV_DIM), F32),
                        pltpu.VMEM((SUBLANES - 1, CONV_HALO + tt, CONV_DIM), F32),
                        pltpu.VMEM((tt, CONV_DIM), F32)],
        compiler_params=pltpu.CompilerParams(
            dimension_semantics=("parallel", "arbitrary"),
            vmem_limit_bytes=VMEM_LIMIT),
        name="conv_module",
    )(proj, proj, proj_meta, proj_meta, wdw, bdw, lng, lnb, wpw_bf)


def _rotary(x, cos, sin):
    half = x.shape[-1] // 2
    x1, x2 = x[:, :half], x[:, half:]
    return jnp.concatenate([x1 * cos - x2 * sin, x2 * cos + x1 * sin], axis=-1)


def _ret_kernel(q_ref, k_ref, v_ref, gret_ref, gm_ref, yconv_ref, x_ref, cos_ref, sin_ref,
                mk_ref, mv_ref, mcos_ref, msin_ref, dmat_ref, qdec_ref, kdec_ref, mkdec_ref,
                bdec_ref, gn_ref, wo_ref, wout_ref, o_ref, state_ref):
    i = pl.program_id(1)
    k_scale = RET_QK_DIM ** -0.5

    @pl.when(i == 0)
    def _():
        mcos, msin = mcos_ref[...], msin_ref[...]
        for h in range(RET_HEADS):
            mk = mk_ref[:, h * RET_QK_DIM:(h + 1) * RET_QK_DIM].astype(F32)
            mk = _rotary(mk, mcos, msin) * k_scale * mkdec_ref[h]
            mv = mv_ref[:, h * RET_V_DIM:(h + 1) * RET_V_DIM]
            state_ref[h] = lax.dot_general(mk.astype(BF16), mv, (((0,), (0,)), ((), ())),
                                           preferred_element_type=F32)

    cos, sin = cos_ref[...], sin_ref[...]
    y_ret = jnp.zeros((RET_BLOCK, D_MODEL), F32)
    for h in range(RET_HEADS):
        qk = slice(h * RET_QK_DIM, (h + 1) * RET_QK_DIM)
        vv = slice(h * RET_V_DIM, (h + 1) * RET_V_DIM)
        q = _rotary(q_ref[:, qk].astype(F32), cos, sin)
        k = _rotary(k_ref[:, qk].astype(F32), cos, sin) * k_scale
        v = v_ref[:, vv]
        q_bf = q.astype(BF16)
        scores = lax.dot_general(q_bf, k.astype(BF16), (((1,), (1,)), ((), ())),
                                 preferred_element_type=F32) * dmat_ref[h]
        state = state_ref[h]
        o = _dot(scores.astype(BF16), v) + _dot(q_bf, state.astype(BF16)) * qdec_ref[h]
        state_ref[h] = state * bdec_ref[h] + lax.dot_general(
            (k * kdec_ref[h]).astype(BF16), v, (((0,), (0,)), ((), ())),
            preferred_element_type=F32)
        mu = jnp.mean(o, axis=-1, keepdims=True)
        oc = o - mu
        var = jnp.mean(oc * oc, axis=-1, keepdims=True)
        on = oc * lax.rsqrt(var + EPS) * gn_ref[:, vv]
        gr = gret_ref[:, vv].astype(F32)
        gated = (gr * _sigmoid(gr) * on).astype(BF16)
        y_ret = y_ret + _dot(gated, wo_ref[vv, :])

    ga = gm_ref[:, :D_MODEL].astype(F32)
    gb = gm_ref[:, D_MODEL:].astype(F32)
    merged = _sigmoid(ga) * yconv_ref[...].astype(F32) + _sigmoid(gb) * y_ret
    o_ref[...] = x_ref[...] + _dot(merged.astype(BF16), wout_ref[...])


def _retention(proj, proj_meta, y_conv, x2d, tables, gn, wo_bf, wout_bf, batch, seq):
    t = proj.shape[0]
    nb = seq // RET_BLOCK
    hq = RET_HEADS * RET_QK_DIM
    hv = RET_HEADS * RET_V_DIM
    q_col, k_col = 2 * CONV_DIM // hq, 2 * CONV_DIM // hq + 1
    v_col = (2 * CONV_DIM + 2 * hq) // hv
    gret_col, gm_col = v_col + 1, v_col + 2
    row = lambda b, i: b * nb + i
    const2 = lambda b, i: (0, 0)
    const3 = lambda b, i: (0, 0, 0)
    cos, sin, mcos, msin, dmat, qdec, kdec, mkdec, bdec = tables
    return pl.pallas_call(
        _ret_kernel,
        grid=(batch, nb),
        in_specs=[
            pl.BlockSpec((RET_BLOCK, hq), lambda b, i: (row(b, i), q_col)),
            pl.BlockSpec((RET_BLOCK, hq), lambda b, i: (row(b, i), k_col)),
            pl.BlockSpec((RET_BLOCK, hv), lambda b, i: (row(b, i), v_col)),
            pl.BlockSpec((RET_BLOCK, hv), lambda b, i: (row(b, i), gret_col)),
            pl.BlockSpec((RET_BLOCK, 2 * D_MODEL), lambda b, i: (row(b, i), gm_col)),
            pl.BlockSpec((RET_BLOCK, D_MODEL), lambda b, i: (row(b, i), 0)),
            pl.BlockSpec((RET_BLOCK, D_MODEL), lambda b, i: (row(b, i), 0)),
            pl.BlockSpec((RET_BLOCK, RET_QK_DIM // 2), lambda b, i: (i, 0)),
            pl.BlockSpec((RET_BLOCK, RET_QK_DIM // 2), lambda b, i: (i, 0)),
            pl.BlockSpec((N_META, hq), lambda b, i: (0, k_col)),
            pl.BlockSpec((N_META, hv), lambda b, i: (0, v_col)),
            pl.BlockSpec((N_META, RET_QK_DIM // 2), const2),
            pl.BlockSpec((N_META, RET_QK_DIM // 2), const2),
            pl.BlockSpec((RET_HEADS, RET_BLOCK, RET_BLOCK), const3),
            pl.BlockSpec((RET_HEADS, RET_BLOCK, 1), const3),
            pl.BlockSpec((RET_HEADS, RET_BLOCK, 1), const3),
            pl.BlockSpec((RET_HEADS, N_META, 1), const3),
            pl.BlockSpec((RET_HEADS, 1, 1), const3),
            pl.BlockSpec((1, hv), const2),
            pl.BlockSpec((hv, D_MODEL), const2),
            pl.BlockSpec((D_MODEL, D_MODEL), const2),
        ],
        out_specs=pl.BlockSpec((RET_BLOCK, D_MODEL), lambda b, i: (row(b, i), 0)),
        out_shape=jax.ShapeDtypeStruct((t, D_MODEL), F32),
        scratch_shapes=[pltpu.VMEM((RET_HEADS, RET_QK_DIM, RET_V_DIM), F32)],
        compiler_params=pltpu.CompilerParams(
            dimension_semantics=("parallel", "arbitrary"),
            vmem_limit_bytes=VMEM_LIMIT),
        name="retention_mix",
    )(proj, proj, proj, proj, proj, y_conv, x2d, cos, sin, proj_meta, proj_meta, mcos, msin,
      dmat, qdec, kdec, mkdec, bdec, gn, wo_bf, wout_bf)


def _retention_tables(seq):
    half = RET_QK_DIM // 2
    inv = ROPE_BASE ** (-jnp.arange(half, dtype=F32) / half)
    pos = jnp.arange(N_META + seq, dtype=F32)
    ang = pos[:, None] * inv[None, :]
    cos_all, sin_all = jnp.cos(ang), jnp.sin(ang)
    log_gamma = jnp.log(1.0 - 2.0 ** (-5.0 - jnp.arange(RET_HEADS, dtype=F32)))
    idx = jnp.arange(RET_BLOCK, dtype=F32)
    chunk = jnp.arange(RET_BLOCK, dtype=jnp.int32) // CHUNK
    visible = chunk[None, :] <= chunk[:, None]
    dmat = jnp.where(visible[None],
                     jnp.exp(log_gamma[:, None, None] * jnp.abs(idx[:, None] - idx[None, :])),
                     0.0)
    qdec = jnp.exp(log_gamma[:, None] * (idx + 1.0))[:, :, None]
    kdec = jnp.exp(log_gamma[:, None] * (RET_BLOCK - 1.0 - idx))[:, :, None]
    midx = jnp.arange(N_META, dtype=F32)
    mkdec = jnp.exp(log_gamma[:, None] * (N_META - 1.0 - midx))[:, :, None]
    bdec = jnp.exp(log_gamma * RET_BLOCK)[:, None, None]
    return (cos_all[N_META:], sin_all[N_META:], cos_all[:N_META], sin_all[:N_META],
            dmat, qdec, kdec, mkdec, bdec)


def _route(logits):
    lane = lax.broadcasted_iota(jnp.int32, logits.shape, 1)
    neg = jnp.float32(-jnp.inf)
    big = jnp.int32(ROUTER_LANES)

    def first_max(masked):
        val = jnp.max(masked, axis=-1, keepdims=True)
        idx = jnp.min(jnp.where(masked == val, lane, big), axis=-1, keepdims=True)
        return val, idx

    gmask = lane < N_GROUPS
    gmax, gidx = first_max(jnp.where(gmask, logits, neg))
    denom = jnp.sum(jnp.where(gmask, jnp.exp(logits - gmax), 0.0), axis=-1, keepdims=True)
    p_group = 1.0 / denom
    assert EXPERTS_PER_GROUP & (EXPERTS_PER_GROUP - 1) == 0
    shift = EXPERTS_PER_GROUP.bit_length() - 1
    lane_group = (lane - N_GROUPS) >> shift
    in_group = jnp.where(lane_group == gidx, logits, neg)
    v1, i1 = first_max(in_group)
    v2, i2 = first_max(jnp.where(lane == i1, neg, in_group))
    e2 = jnp.exp(v2 - v1)
    w1 = p_group / (1.0 + e2)
    w2 = p_group * e2 / (1.0 + e2)
    return jnp.where(lane == i1, w1, 0.0) + jnp.where(lane == i2, w2, 0.0)


def _moe_kernel(h_ref, g_ref, wr_ref, br_ref, wgu_ref, wd_ref, gf_ref, o_ref,
                u_ref, comb_ref, acc_ref):
    e = pl.program_id(1)

    @pl.when(e == 0)
    def _():
        h = h_ref[...]
        ms = jnp.mean(h * h, axis=-1, keepdims=True)
        u = h * lax.rsqrt(ms + EPS) * g_ref[...]
        u_hi = u.astype(BF16)
        u_lo = (u - u_hi.astype(F32)).astype(BF16)
        w_hi = wr_ref[0]
        w_lo = wr_ref[1]
        logits = _dot(u_hi, w_hi) + (_dot(u_hi, w_lo) + _dot(u_lo, w_hi)) + br_ref[...]
        u_ref[...] = u_hi
        comb_ref[...] = _route(logits)
        acc_ref[...] = jnp.zeros_like(acc_ref)

    lane = lax.broadcasted_iota(jnp.int32, comb_ref.shape, 1)
    w_e = jnp.sum(jnp.where(lane == e + N_GROUPS, comb_ref[...], 0.0), axis=-1, keepdims=True)
    gu = _dot(u_ref[...], wgu_ref[0])
    gate, up = gu[:, :D_EXPERT], gu[:, D_EXPERT:]
    hid = gate * _sigmoid(gate) * up * w_e
    acc_ref[...] += _dot(hid.astype(BF16), wd_ref[0])

    @pl.when(e == N_EXPERTS - 1)
    def _():
        h2 = h_ref[...] + acc_ref[...]
        ms = jnp.mean(h2 * h2, axis=-1, keepdims=True)
        o_ref[...] = h2 * lax.rsqrt(ms + EPS) * gf_ref[...]


def _moe(h1, g, wr2, br, wgu_bf, wd_bf, gf, tm):
    t = h1.shape[0]
    const2 = lambda i, e: (0, 0)
    return pl.pallas_call(
        _moe_kernel,
        grid=(t // tm, N_EXPERTS),
        in_specs=[
            pl.BlockSpec((tm, D_MODEL), lambda i, e: (i, 0)),
            pl.BlockSpec((1, D_MODEL), const2),
            pl.BlockSpec((2, D_MODEL, ROUTER_LANES), lambda i, e: (0, 0, 0)),
            pl.BlockSpec((1, ROUTER_LANES), const2),
            pl.BlockSpec((1, D_MODEL, 2 * D_EXPERT), lambda i, e: (e, 0, 0)),
            pl.BlockSpec((1, D_EXPERT, D_MODEL), lambda i, e: (e, 0, 0)),
            pl.BlockSpec((1, D_MODEL), const2),
        ],
        out_specs=pl.BlockSpec((tm, D_MODEL), lambda i, e: (i, 0)),
        out_shape=jax.ShapeDtypeStruct((t, D_MODEL), F32),
        scratch_shapes=[pltpu.VMEM((tm, D_MODEL), BF16),
                        pltpu.VMEM((tm, ROUTER_LANES), F32),
                        pltpu.VMEM((tm, D_MODEL), F32)],
        compiler_params=pltpu.CompilerParams(
            dimension_semantics=("parallel", "arbitrary"),
            vmem_limit_bytes=VMEM_LIMIT),
        name="hier_moe",
    )(h1, g, wr2, br, wgu_bf, wd_bf, gf)


def _split_bf16(w):
    hi = w.astype(BF16)
    lo = (w - hi.astype(F32)).astype(BF16)
    return jnp.stack([hi, lo])


def kernel(x, meta_tokens, norm_mix_g, w_in, conv_dw_w, conv_dw_b, conv_ln_g, conv_ln_b,
           conv_pw_w, ret_gn_g, ret_w_o, w_out, norm_ffn_g, w_group_router, b_group_router,
           w_expert_router, b_expert_router, w_expert_gate, w_expert_up, w_expert_down,
           norm_final_g):
    batch, seq, d = x.shape
    assert d == D_MODEL and seq % RET_BLOCK == 0 and w_in.shape[0] == 1
    t = batch * seq
    x2d = x.reshape(t, d)
    row = lambda v: v.reshape(1, -1)

    w_in_bf = w_in[0].astype(BF16)
    tm = min(1024, t)
    proj = _in_proj(x2d, row(norm_mix_g[0]), w_in_bf, tm, 1024)
    proj_meta = _in_proj(meta_tokens, row(norm_mix_g[0]), w_in_bf, N_META, 1024)

    y_conv = _conv(proj, proj_meta, conv_dw_w[0], row(conv_dw_b[0]), row(conv_ln_g[0]),
                   row(conv_ln_b[0]), conv_pw_w[0].astype(BF16), batch, seq, min(256, seq))
    h1 = _retention(proj, proj_meta, y_conv, x2d, _retention_tables(seq), row(ret_gn_g[0]),
                    ret_w_o[0].astype(BF16), w_out[0].astype(BF16), batch, seq)

    w_router = jnp.concatenate([w_group_router[0], w_expert_router[0]], axis=1)
    w_router = jnp.pad(w_router, ((0, 0), (0, ROUTER_LANES - w_router.shape[1])))
    b_router = jnp.concatenate([b_group_router[0], b_expert_router[0]])
    b_router = jnp.pad(b_router, (0, ROUTER_LANES - b_router.shape[0])).reshape(1, -1)
    w_gu = jnp.concatenate([w_expert_gate[0], w_expert_up[0]], axis=-1).astype(BF16)
    out = _moe(h1, row(norm_ffn_g[0]), _split_bf16(w_router), b_router, w_gu,
               w_expert_down[0].astype(BF16), row(norm_final_g), min(1024, t))
    return out.reshape(batch, seq, d)
```

```python
import functools
import math

import jax
import jax.numpy as jnp
from jax import lax
from jax.experimental import pallas as pl
from jax.experimental.pallas import tpu as pltpu

D_MODEL = 1024
CHUNK = 64
N_META = 16
CONV_DIM = 1024
CONV_WIDTH = 31
RET_HEADS = 4
RET_QK_DIM = 256
RET_V_DIM = 512
ROPE_BASE = 10000.0
N_GROUPS = 4
EXPERTS_PER_GROUP = 4
N_EXPERTS = N_GROUPS * EXPERTS_PER_GROUP
D_EXPERT = 512
EPS = 1e-6
D_IN = 2 * CONV_DIM + 2 * RET_HEADS * RET_QK_DIM + 2 * RET_HEADS * RET_V_DIM + 2 * D_MODEL

LANES = 128
SUBLANES = 8
CONV_HALO = 32
CONV_ROWS_PER_ITER = 4
RET_BLOCK = 256
ROUTER_LANES = LANES
VMEM_LIMIT = 48 * 1024 * 1024

F32 = jnp.float32
BF16 = jnp.bfloat16


def _sigmoid(x):
    return 1.0 / (1.0 + jnp.exp(-x))


def _dot(a, b):
    return jnp.dot(a, b, preferred_element_type=F32)


def _in_proj_kernel(x_ref, g_ref, w_ref, o_ref, u_ref):
    @pl.when(pl.program_id(1) == 0)
    def _():
        x = x_ref[...]
        ms = jnp.mean(x * x, axis=-1, keepdims=True)
        u_ref[...] = (x * lax.rsqrt(ms + EPS) * g_ref[...]).astype(BF16)

    o_ref[...] = _dot(u_ref[...], w_ref[...]).astype(o_ref.dtype)


def _in_proj(x, g, w_bf, tm, tn):
    t, d = x.shape
    n = w_bf.shape[1]
    return pl.pallas_call(
        _in_proj_kernel,
        grid=(t // tm, n // tn),
        in_specs=[
            pl.BlockSpec((tm, d), lambda i, j: (i, 0)),
            pl.BlockSpec((1, d), lambda i, j: (0, 0)),
            pl.BlockSpec((d, tn), lambda i, j: (0, j)),
        ],
        out_specs=pl.BlockSpec((tm, tn), lambda i, j: (i, j)),
        out_shape=jax.ShapeDtypeStruct((t, n), BF16),
        scratch_shapes=[pltpu.VMEM((tm, d), BF16)],
        compiler_params=pltpu.CompilerParams(
            dimension_semantics=("parallel", "arbitrary"),
            vmem_limit_bytes=VMEM_LIMIT),
        name="in_proj",
    )(x, g, w_bf)


def _conv_kernel(a_ref, gate_ref, ma_ref, mg_ref, wdw_ref, bdw_ref, lng_ref, lnb_ref,
                 wpw_ref, o_ref, hbuf_ref, shift_ref, cbuf_ref, *, tt):
    i = pl.program_id(1)

    @pl.when(i == 0)
    def _():
        hbuf_ref[0:CONV_HALO - N_META, :] = jnp.zeros((CONV_HALO - N_META, CONV_DIM), F32)
        ma = ma_ref[...].astype(F32)
        mg = mg_ref[...].astype(F32)
        hbuf_ref[CONV_HALO - N_META:CONV_HALO, :] = ma * _sigmoid(mg)

    @pl.when(i > 0)
    def _():
        hbuf_ref[0:CONV_HALO, :] = hbuf_ref[tt:tt + CONV_HALO, :]

    a = a_ref[...].astype(F32)
    g = gate_ref[...].astype(F32)
    hbuf_ref[CONV_HALO:CONV_HALO + tt, :] = a * _sigmoid(g)

    span = tt + CONV_HALO - SUBLANES
    for s in range(1, SUBLANES):
        shift_ref[s - 1, 0:span, :] = hbuf_ref[s:s + span, :]

    first = CONV_HALO - (CONV_WIDTH - 1)
    for cb in range(CONV_DIM // LANES):
        cols = slice(cb * LANES, (cb + 1) * LANES)
        taps = [jnp.broadcast_to(wdw_ref[j:j + 1, cols], (SUBLANES, LANES))
                for j in range(CONV_WIDTH)]
        bias = jnp.broadcast_to(bdw_ref[:, cols], (SUBLANES, LANES))

        def row_body(r, carry, cols=cols, taps=taps, bias=bias):
            r0 = pl.multiple_of(r * (SUBLANES * CONV_ROWS_PER_ITER), SUBLANES)
            accs = [bias] * CONV_ROWS_PER_ITER
            for shift in range(SUBLANES):
                js = [j for j in range(CONV_WIDTH) if (first + j) % SUBLANES == shift]
                tiles = [(first + j) // SUBLANES for j in js]
                wins = {}
                for m in range(min(tiles), max(tiles) + CONV_ROWS_PER_ITER):
                    rows = pl.ds(pl.multiple_of(r0 + m * SUBLANES, SUBLANES), SUBLANES)
                    wins[m] = (hbuf_ref[rows, cols] if shift == 0
                               else shift_ref[shift - 1, rows, cols])
                for c in range(CONV_ROWS_PER_ITER):
                    for j, m in zip(js, tiles):
                        accs[c] = accs[c] + taps[j] * wins[m + c]
            for c in range(CONV_ROWS_PER_ITER):
                rows = pl.ds(pl.multiple_of(r0 + c * SUBLANES, SUBLANES), SUBLANES)
                cbuf_ref[rows, cols] = accs[c]
            return carry

        lax.fori_loop(0, tt // (SUBLANES * CONV_ROWS_PER_ITER), row_body, 0)

    c = cbuf_ref[...]
    mu = jnp.mean(c, axis=-1, keepdims=True)
    cc = c - mu
    var = jnp.mean(cc * cc, axis=-1, keepdims=True)
    y = cc * lax.rsqrt(var + EPS) * lng_ref[...] + lnb_ref[...]
    y = y * _sigmoid(y)
    o_ref[...] = _dot(y.astype(BF16), wpw_ref[...]).astype(o_ref.dtype)


def _conv(proj, proj_meta, wdw, bdw, lng, lnb, wpw_bf, batch, seq, tt):
    t = proj.shape[0]
    nt = seq // tt
    const = lambda b, i: (0, 0)
    return pl.pallas_call(
        functools.partial(_conv_kernel, tt=tt),
        grid=(batch, nt),
        in_specs=[
            pl.BlockSpec((tt, CONV_DIM), lambda b, i: (b * nt + i, 0)),
            pl.BlockSpec((tt, CONV_DIM), lambda b, i: (b * nt + i, 1)),
            pl.BlockSpec((N_META, CONV_DIM), lambda b, i: (0, 0)),
            pl.BlockSpec((N_META, CONV_DIM), lambda b, i: (0, 1)),
            pl.BlockSpec((CONV_WIDTH, CONV_DIM), const),
            pl.BlockSpec((1, CONV_DIM), const),
            pl.BlockSpec((1, CONV_DIM), const),
            pl.BlockSpec((1, CONV_DIM), const),
            pl.BlockSpec((CONV_DIM, D_MODEL), const),
        ],
        out_specs=pl.BlockSpec((tt, D_MODEL), lambda b, i: (b * nt + i, 0)),
        out_shape=jax.ShapeDtypeStruct((t, D_MODEL), BF16),
        scratch_shapes=[pltpu.VMEM((CONV_HALO + tt, CONV_DIM), F32),
                        pltpu.VMEM((SUBLANES - 1, CONV_HALO + tt, CONV_DIM), F32),
                        pltpu.VMEM((tt, CONV_DIM), F32)],
        compiler_params=pltpu.CompilerParams(
            dimension_semantics=("parallel", "arbitrary"),
            vmem_limit_bytes=VMEM_LIMIT),
        name="conv_module",
    )(proj, proj, proj_meta, proj_meta, wdw, bdw, lng, lnb, wpw_bf)


def _rotary(x, cos, sin):
    half = x.shape[-1] // 2
    x1, x2 = x[:, :half], x[:, half:]
    return jnp.concatenate([x1 * cos - x2 * sin, x2 * cos + x1 * sin], axis=-1)


def _ret_kernel(q_ref, k_ref, v_ref, gret_ref, gm_ref, yconv_ref, x_ref, cos_ref, sin_ref,
                mk_ref, mv_ref, mcos_ref, msin_ref, dmat_ref, qdec_ref, kdec_ref, mkdec_ref,
                bdec_ref, gn_ref, wo_ref, wout_ref, o_ref, state_ref):
    i = pl.program_id(1)
    k_scale = RET_QK_DIM ** -0.5

    @pl.when(i == 0)
    def _():
        mcos, msin = mcos_ref[...], msin_ref[...]
        for h in range(RET_HEADS):
            mk = mk_ref[:, h * RET_QK_DIM:(h + 1) * RET_QK_DIM].astype(F32)
            mk = _rotary(mk, mcos, msin) * k_scale * mkdec_ref[h]
            mv = mv_ref[:, h * RET_V_DIM:(h + 1) * RET_V_DIM]
            state_ref[h] = lax.dot_general(mk.astype(BF16), mv, (((0,), (0,)), ((), ())),
                                           preferred_element_type=F32)

    cos, sin = cos_ref[...], sin_ref[...]
    y_ret = jnp.zeros((RET_BLOCK, D_MODEL), F32)
    for h in range(RET_HEADS):
        qk = slice(h * RET_QK_DIM, (h + 1) * RET_QK_DIM)
        vv = slice(h * RET_V_DIM, (h + 1) * RET_V_DIM)
        q = _rotary(q_ref[:, qk].astype(F32), cos, sin)
        k = _rotary(k_ref[:, qk].astype(F32), cos, sin) * k_scale
        v = v_ref[:, vv]
        q_bf = q.astype(BF16)
        scores = lax.dot_general(q_bf, k.astype(BF16), (((1,), (1,)), ((), ())),
                                 preferred_element_type=F32) * dmat_ref[h]
        state = state_ref[h]
        o = _dot(scores.astype(BF16), v) + _dot(q_bf, state.astype(BF16)) * qdec_ref[h]
        state_ref[h] = state * bdec_ref[h] + lax.dot_general(
            (k * kdec_ref[h]).astype(BF16), v, (((0,), (0,)), ((), ())),
            preferred_element_type=F32)
        mu = jnp.mean(o, axis=-1, keepdims=True)
        oc = o - mu
        var = jnp.mean(oc * oc, axis=-1, keepdims=True)
        on = oc * lax.rsqrt(var + EPS) * gn_ref[:, vv]
        gr = gret_ref[:, vv].astype(F32)
        gated = (gr * _sigmoid(gr) * on).astype(BF16)
        y_ret = y_ret + _dot(gated, wo_ref[vv, :])

    ga = gm_ref[:, :D_MODEL].astype(F32)
    gb = gm_ref[:, D_MODEL:].astype(F32)
    merged = _sigmoid(ga) * yconv_ref[...].astype(F32) + _sigmoid(gb) * y_ret
    o_ref[...] = x_ref[...] + _dot(merged.astype(BF16), wout_ref[...])


def _retention(proj, proj_meta, y_conv, x2d, tables, gn, wo_bf, wout_bf, batch, seq):
    t = proj.shape[0]
    nb = seq // RET_BLOCK
    hq = RET_HEADS * RET_QK_DIM
    hv = RET_HEADS * RET_V_DIM
    q_col, k_col = 2 * CONV_DIM // hq, 2 * CONV_DIM // hq + 1
    v_col = (2 * CONV_DIM + 2 * hq) // hv
    gret_col, gm_col = v_col + 1, v_col + 2
    row = lambda b, i: b * nb + i
    const2 = lambda b, i: (0, 0)
    const3 = lambda b, i: (0, 0, 0)
    cos, sin, mcos, msin, dmat, qdec, kdec, mkdec, bdec = tables
    return pl.pallas_call(
        _ret_kernel,
        grid=(batch, nb),
        in_specs=[
            pl.BlockSpec((RET_BLOCK, hq), lambda b, i: (row(b, i), q_col)),
            pl.BlockSpec((RET_BLOCK, hq), lambda b, i: (row(b, i), k_col)),
            pl.BlockSpec((RET_BLOCK, hv), lambda b, i: (row(b, i), v_col)),
            pl.BlockSpec((RET_BLOCK, hv), lambda b, i: (row(b, i), gret_col)),
            pl.BlockSpec((RET_BLOCK, 2 * D_MODEL), lambda b, i: (row(b, i), gm_col)),
            pl.BlockSpec((RET_BLOCK, D_MODEL), lambda b, i: (row(b, i), 0)),
            pl.BlockSpec((RET_BLOCK, D_MODEL), lambda b, i: (row(b, i), 0)),
            pl.BlockSpec((RET_BLOCK, RET_QK_DIM // 2), lambda b, i: (i, 0)),
            pl.BlockSpec((RET_BLOCK, RET_QK_DIM // 2), lambda b, i: (i, 0)),
            pl.BlockSpec((N_META, hq), lambda b, i: (0, k_col)),
            pl.BlockSpec((N_META, hv), lambda b, i: (0, v_col)),
            pl.BlockSpec((N_META, RET_QK_DIM // 2), const2),
            pl.BlockSpec((N_META, RET_QK_DIM // 2), const2),
            pl.BlockSpec((RET_HEADS, RET_BLOCK, RET_BLOCK), const3),
            pl.BlockSpec((RET_HEADS, RET_BLOCK, 1), const3),
            pl.BlockSpec((RET_HEADS, RET_BLOCK, 1), const3),
            pl.BlockSpec((RET_HEADS, N_META, 1), const3),
            pl.BlockSpec((RET_HEADS, 1, 1), const3),
            pl.BlockSpec((1, hv), const2),
            pl.BlockSpec((hv, D_MODEL), const2),
            pl.BlockSpec((D_MODEL, D_MODEL), const2),
        ],
        out_specs=pl.BlockSpec((RET_BLOCK, D_MODEL), lambda b, i: (row(b, i), 0)),
        out_shape=jax.ShapeDtypeStruct((t, D_MODEL), F32),
        scratch_shapes=[pltpu.VMEM((RET_HEADS, RET_QK_DIM, RET_V_DIM), F32)],
        compiler_params=pltpu.CompilerParams(
            dimension_semantics=("parallel", "arbitrary"),
            vmem_limit_bytes=VMEM_LIMIT),
        name="retention_mix",
    )(proj, proj, proj, proj, proj, y_conv, x2d, cos, sin, proj_meta, proj_meta, mcos, msin,
      dmat, qdec, kdec, mkdec, bdec, gn, wo_bf, wout_bf)


def _retention_tables(seq):
    half = RET_QK_DIM // 2
    inv = ROPE_BASE ** (-jnp.arange(half, dtype=F32) / half)
    pos = jnp.arange(N_META + seq, dtype=F32)
    ang = pos[:, None] * inv[None, :]
    cos_all, sin_all = jnp.cos(ang), jnp.sin(ang)
    log_gamma = jnp.log(1.0 - 2.0 ** (-5.0 - jnp.arange(RET_HEADS, dtype=F32)))
    idx = jnp.arange(RET_BLOCK, dtype=F32)
    chunk = jnp.arange(RET_BLOCK, dtype=jnp.int32) // CHUNK
    visible = chunk[None, :] <= chunk[:, None]
    dmat = jnp.where(visible[None],
                     jnp.exp(log_gamma[:, None, None] * jnp.abs(idx[:, None] - idx[None, :])),
                     0.0)
    qdec = jnp.exp(log_gamma[:, None] * (idx + 1.0))[:, :, None]
    kdec = jnp.exp(log_gamma[:, None] * (RET_BLOCK - 1.0 - idx))[:, :, None]
    midx = jnp.arange(N_META, dtype=F32)
    mkdec = jnp.exp(log_gamma[:, None] * (N_META - 1.0 - midx))[:, :, None]
    bdec = jnp.exp(log_gamma * RET_BLOCK)[:, None, None]
    return (cos_all[N_META:], sin_all[N_META:], cos_all[:N_META], sin_all[:N_META],
            dmat, qdec, kdec, mkdec, bdec)


def _route(logits):
    lane = lax.broadcasted_iota(jnp.int32, logits.shape, 1)
    neg = jnp.float32(-jnp.inf)
    big = jnp.int32(ROUTER_LANES)

    def first_max(masked):
        val = jnp.max(masked, axis=-1, keepdims=True)
        idx = jnp.min(jnp.where(masked == val, lane, big), axis=-1, keepdims=True)
        return val, idx

    gmask = lane < N_GROUPS
    gmax, gidx = first_max(jnp.where(gmask, logits, neg))
    denom = jnp.sum(jnp.where(gmask, jnp.exp(logits - gmax), 0.0), axis=-1, keepdims=True)
    p_group = 1.0 / denom
    assert EXPERTS_PER_GROUP & (EXPERTS_PER_GROUP - 1) == 0
    shift = EXPERTS_PER_GROUP.bit_length() - 1
    lane_group = (lane - N_GROUPS) >> shift
    in_group = jnp.where(lane_group == gidx, logits, neg)
    v1, i1 = first_max(in_group)
    v2, i2 = first_max(jnp.where(lane == i1, neg, in_group))
    e2 = jnp.exp(v2 - v1)
    w1 = p_group / (1.0 + e2)
    w2 = p_group * e2 / (1.0 + e2)
    return jnp.where(lane == i1, w1, 0.0) + jnp.where(lane == i2, w2, 0.0)


def _moe_kernel(h_ref, g_ref, wr_ref, br_ref, wgu_ref, wd_ref, gf_ref, o_ref,
                u_ref, comb_ref, acc_ref):
    e = pl.program_id(1)

    @pl.when(e == 0)
    def _():
        h = h_ref[...]
        ms = jnp.mean(h * h, axis=-1, keepdims=True)
        u = h * lax.rsqrt(ms + EPS) * g_ref[...]
        u_hi = u.astype(BF16)
        u_lo = (u - u_hi.astype(F32)).astype(BF16)
        w_hi = wr_ref[0]
        w_lo = wr_ref[1]
        logits = _dot(u_hi, w_hi) + (_dot(u_hi, w_lo) + _dot(u_lo, w_hi)) + br_ref[...]
        u_ref[...] = u_hi
        comb_ref[...] = _route(logits)
        acc_ref[...] = jnp.zeros_like(acc_ref)

    lane = lax.broadcasted_iota(jnp.int32, comb_ref.shape, 1)
    w_e = jnp.sum(jnp.where(lane == e + N_GROUPS, comb_ref[...], 0.0), axis=-1, keepdims=True)
    gu = _dot(u_ref[...], wgu_ref[0])
    gate, up = gu[:, :D_EXPERT], gu[:, D_EXPERT:]
    hid = gate * _sigmoid(gate) * up * w_e
    acc_ref[...] += _dot(hid.astype(BF16), wd_ref[0])

    @pl.when(e == N_EXPERTS - 1)
    def _():
        h2 = h_ref[...] + acc_ref[...]
        ms = jnp.mean(h2 * h2, axis=-1, keepdims=True)
        o_ref[...] = h2 * lax.rsqrt(ms + EPS) * gf_ref[...]


def _moe(h1, g, wr2, br, wgu_bf, wd_bf, gf, tm):
    t = h1.shape[0]
    const2 = lambda i, e: (0, 0)
    return pl.pallas_call(
        _moe_kernel,
        grid=(t // tm, N_EXPERTS),
        in_specs=[
            pl.BlockSpec((tm, D_MODEL), lambda i, e: (i, 0)),
            pl.BlockSpec((1, D_MODEL), const2),
            pl.BlockSpec((2, D_MODEL, ROUTER_LANES), lambda i, e: (0, 0, 0)),
            pl.BlockSpec((1, ROUTER_LANES), const2),
            pl.BlockSpec((1, D_MODEL, 2 * D_EXPERT), lambda i, e: (e, 0, 0)),
            pl.BlockSpec((1, D_EXPERT, D_MODEL), lambda i, e: (e, 0, 0)),
            pl.BlockSpec((1, D_MODEL), const2),
        ],
        out_specs=pl.BlockSpec((tm, D_MODEL), lambda i, e: (i, 0)),
        out_shape=jax.ShapeDtypeStruct((t, D_MODEL), F32),
        scratch_shapes=[pltpu.VMEM((tm, D_MODEL), BF16),
                        pltpu.VMEM((tm, ROUTER_LANES), F32),
                        pltpu.VMEM((tm, D_MODEL), F32)],
        compiler_params=pltpu.CompilerParams(
            dimension_semantics=("parallel", "arbitrary"),
            vmem_limit_bytes=VMEM_LIMIT),
        name="hier_moe",
    )(h1, g, wr2, br, wgu_bf, wd_bf, gf)


def _split_bf16(w):
    hi = w.astype(BF16)
    lo = (w - hi.astype(F32)).astype(BF16)
    return jnp.stack([hi, lo])


def kernel(x, meta_tokens, norm_mix_g, w_in, conv_dw_w, conv_dw_b, conv_ln_g, conv_ln_b,
           conv_pw_w, ret_gn_g, ret_w_o, w_out, norm_ffn_g, w_group_router, b_group_router,
           w_expert_router, b_expert_router, w_expert_gate, w_expert_up, w_expert_down,
           norm_final_g):
    batch, seq, d = x.shape
    assert d == D_MODEL and seq % RET_BLOCK == 0 and w_in.shape[0] == 1
    t = batch * seq
    x2d = x.reshape(t, d)
    row = lambda v: v.reshape(1, -1)

    w_in_bf = w_in[0].astype(BF16)
    tm = min(1024, t)
    proj = _in_proj(x2d, row(norm_mix_g[0]), w_in_bf, tm, 1024)
    proj_meta = _in_proj(meta_tokens, row(norm_mix_g[0]), w_in_bf, N_META, 1024)

    y_conv = _conv(proj, proj_meta, conv_dw_w[0], row(conv_dw_b[0]), row(conv_ln_g[0]),
                   row(conv_ln_b[0]), conv_pw_w[0].astype(BF16), batch, seq, min(256, seq))
    h1 = _retention(proj, proj_meta, y_conv, x2d, _retention_tables(seq), row(ret_gn_g[0]),
                    ret_w_o[0].astype(BF16), w_out[0].astype(BF16), batch, seq)

    w_router = jnp.concatenate([w_group_router[0], w_expert_router[0]], axis=1)
    w_router = jnp.pad(w_router, ((0, 0), (0, ROUTER_LANES - w_router.shape[1])))
    b_router = jnp.concatenate([b_group_router[0], b_expert_router[0]])
    b_router = jnp.pad(b_router, (0, ROUTER_LANES - b_router.shape[0])).reshape(1, -1)
    w_gu = jnp.concatenate([w_expert_gate[0], w_expert_up[0]], axis=-1).astype(BF16)
    out = _moe(h1, row(norm_ffn_g[0]), _split_bf16(w_router), b_router, w_gu,
               w_expert_down[0].astype(BF16), row(norm_final_g), min(1024, t))
    return out.reshape(batch, seq, d)
```

```python
import functools
import math

import jax
import jax.numpy as jnp
from jax import lax
from jax.experimental import pallas as pl
from jax.experimental.pallas import tpu as pltpu

D_MODEL = 1024
CHUNK = 64
N_META = 16
CONV_DIM = 1024
CONV_WIDTH = 31
RET_HEADS = 4
RET_QK_DIM = 256
RET_V_DIM = 512
ROPE_BASE = 10000.0
N_GROUPS = 4
EXPERTS_PER_GROUP = 4
N_EXPERTS = N_GROUPS * EXPERTS_PER_GROUP
D_EXPERT = 512
EPS = 1e-6
D_IN = 2 * CONV_DIM + 2 * RET_HEADS * RET_QK_DIM + 2 * RET_HEADS * RET_V_DIM + 2 * D_MODEL

LANES = 128
SUBLANES = 8
CONV_HALO = 32
IN_PROJ_COLS = 2048
CONV_ROWS_PER_ITER = 4
RET_BLOCK = 256
ROUTER_LANES = LANES
MOE_BLOCK = 128
VMEM_LIMIT = 48 * 1024 * 1024
MOE_VMEM_LIMIT = 60 * 1024 * 1024

F32 = jnp.float32
BF16 = jnp.bfloat16


def _sigmoid(x):
    return 1.0 / (1.0 + jnp.exp(-x))


def _dot(a, b):
    return jnp.dot(a, b, preferred_element_type=F32)


def _in_proj_kernel(x_ref, g_ref, w_ref, o_ref, u_ref):
    @pl.when(pl.program_id(1) == 0)
    def _():
        x = x_ref[...]
        ms = jnp.mean(x * x, axis=-1, keepdims=True)
        u_ref[...] = (x * lax.rsqrt(ms + EPS) * g_ref[...]).astype(BF16)

    o_ref[...] = _dot(u_ref[...], w_ref[...]).astype(o_ref.dtype)


def _in_proj(x, g, w_bf, tm, tn):
    t, d = x.shape
    n = w_bf.shape[1]
    return pl.pallas_call(
        _in_proj_kernel,
        grid=(t // tm, n // tn),
        in_specs=[
            pl.BlockSpec((tm, d), lambda i, j: (i, 0)),
            pl.BlockSpec((1, d), lambda i, j: (0, 0)),
            pl.BlockSpec((d, tn), lambda i, j: (0, j)),
        ],
        out_specs=pl.BlockSpec((tm, tn), lambda i, j: (i, j)),
        out_shape=jax.ShapeDtypeStruct((t, n), BF16),
        scratch_shapes=[pltpu.VMEM((tm, d), BF16)],
        compiler_params=pltpu.CompilerParams(
            dimension_semantics=("parallel", "arbitrary"),
            vmem_limit_bytes=VMEM_LIMIT),
        name="in_proj",
    )(x, g, w_bf)


def _conv_kernel(a_ref, gate_ref, ma_ref, mg_ref, wdw_ref, bdw_ref, lng_ref, lnb_ref,
                 wpw_ref, o_ref, hbuf_ref, shift_ref, cbuf_ref, *, tt):
    i = pl.program_id(1)

    @pl.when(i == 0)
    def _():
        hbuf_ref[0:CONV_HALO - N_META, :] = jnp.zeros((CONV_HALO - N_META, CONV_DIM), F32)
        ma = ma_ref[...].astype(F32)
        mg = mg_ref[...].astype(F32)
        hbuf_ref[CONV_HALO - N_META:CONV_HALO, :] = ma * _sigmoid(mg)

    @pl.when(i > 0)
    def _():
        hbuf_ref[0:CONV_HALO, :] = hbuf_ref[tt:tt + CONV_HALO, :]

    a = a_ref[...].astype(F32)
    g = gate_ref[...].astype(F32)
    hbuf_ref[CONV_HALO:CONV_HALO + tt, :] = a * _sigmoid(g)

    span = tt + CONV_HALO - SUBLANES
    for s in range(1, SUBLANES):
        shift_ref[s - 1, 0:span, :] = hbuf_ref[s:s + span, :]

    first = CONV_HALO - (CONV_WIDTH - 1)
    for cb in range(CONV_DIM // LANES):
        cols = slice(cb * LANES, (cb + 1) * LANES)
        taps = [jnp.broadcast_to(wdw_ref[j:j + 1, cols], (SUBLANES, LANES))
                for j in range(CONV_WIDTH)]
        bias = jnp.broadcast_to(bdw_ref[:, cols], (SUBLANES, LANES))

        def row_body(r, carry, cols=cols, taps=taps, bias=bias):
            r0 = pl.multiple_of(r * (SUBLANES * CONV_ROWS_PER_ITER), SUBLANES)
            accs = [bias] * CONV_ROWS_PER_ITER
            for shift in range(SUBLANES):
                js = [j for j in range(CONV_WIDTH) if (first + j) % SUBLANES == shift]
                tiles = [(first + j) // SUBLANES for j in js]
                wins = {}
                for m in range(min(tiles), max(tiles) + CONV_ROWS_PER_ITER):
                    rows = pl.ds(pl.multiple_of(r0 + m * SUBLANES, SUBLANES), SUBLANES)
                    wins[m] = (hbuf_ref[rows, cols] if shift == 0
                               else shift_ref[shift - 1, rows, cols])
                for c in range(CONV_ROWS_PER_ITER):
                    for j, m in zip(js, tiles):
                        accs[c] = accs[c] + taps[j] * wins[m + c]
            for c in range(CONV_ROWS_PER_ITER):
                rows = pl.ds(pl.multiple_of(r0 + c * SUBLANES, SUBLANES), SUBLANES)
                cbuf_ref[rows, cols] = accs[c]
            return carry

        lax.fori_loop(0, tt // (SUBLANES * CONV_ROWS_PER_ITER), row_body, 0)

    c = cbuf_ref[...]
    mu = jnp.mean(c, axis=-1, keepdims=True)
    cc = c - mu
    var = jnp.mean(cc * cc, axis=-1, keepdims=True)
    y = cc * lax.rsqrt(var + EPS) * lng_ref[...] + lnb_ref[...]
    y = y * _sigmoid(y)
    o_ref[...] = _dot(y.astype(BF16), wpw_ref[...]).astype(o_ref.dtype)


def _conv(proj, proj_meta, wdw, bdw, lng, lnb, wpw_bf, batch, seq, tt):
    t = proj.shape[0]
    nt = seq // tt
    const = lambda b, i: (0, 0)
    return pl.pallas_call(
        functools.partial(_conv_kernel, tt=tt),
        grid=(batch, nt),
        in_specs=[
            pl.BlockSpec((tt, CONV_DIM), lambda b, i: (b * nt + i, 0)),
            pl.BlockSpec((tt, CONV_DIM), lambda b, i: (b * nt + i, 1)),
            pl.BlockSpec((N_META, CONV_DIM), lambda b, i: (0, 0)),
            pl.BlockSpec((N_META, CONV_DIM), lambda b, i: (0, 1)),
            pl.BlockSpec((CONV_WIDTH, CONV_DIM), const),
            pl.BlockSpec((1, CONV_DIM), const),
            pl.BlockSpec((1, CONV_DIM), const),
            pl.BlockSpec((1, CONV_DIM), const),
            pl.BlockSpec((CONV_DIM, D_MODEL), const),
        ],
        out_specs=pl.BlockSpec((tt, D_MODEL), lambda b, i: (b * nt + i, 0)),
        out_shape=jax.ShapeDtypeStruct((t, D_MODEL), BF16),
        scratch_shapes=[pltpu.VMEM((CONV_HALO + tt, CONV_DIM), F32),
                        pltpu.VMEM((SUBLANES - 1, CONV_HALO + tt, CONV_DIM), F32),
                        pltpu.VMEM((tt, CONV_DIM), F32)],
        compiler_params=pltpu.CompilerParams(
            dimension_semantics=("parallel", "arbitrary"),
            vmem_limit_bytes=VMEM_LIMIT),
        name="conv_module",
    )(proj, proj, proj_meta, proj_meta, wdw, bdw, lng, lnb, wpw_bf)


def _rotary(x, cos, sin):
    half = x.shape[-1] // 2
    x1, x2 = x[:, :half], x[:, half:]
    return jnp.concatenate([x1 * cos - x2 * sin, x2 * cos + x1 * sin], axis=-1)


def _ret_kernel(q_ref, k_ref, v_ref, gret_ref, gm_ref, yconv_ref, x_ref, cos_ref, sin_ref,
                mk_ref, mv_ref, mcos_ref, msin_ref, dmat_ref, qdec_ref, kdec_ref, mkdec_ref,
                bdec_ref, gn_ref, wo_ref, wout_ref, o_ref, state_ref):
    i = pl.program_id(1)
    k_scale = RET_QK_DIM ** -0.5

    @pl.when(i == 0)
    def _():
        mcos, msin = mcos_ref[...], msin_ref[...]
        for h in range(RET_HEADS):
            mk = mk_ref[:, h * RET_QK_DIM:(h + 1) * RET_QK_DIM].astype(F32)
            mk = _rotary(mk, mcos, msin) * k_scale * mkdec_ref[h]
            mv = mv_ref[:, h * RET_V_DIM:(h + 1) * RET_V_DIM]
            state_ref[h] = lax.dot_general(mk.astype(BF16), mv, (((0,), (0,)), ((), ())),
                                           preferred_element_type=F32)

    cos, sin = cos_ref[...], sin_ref[...]
    y_ret = jnp.zeros((RET_BLOCK, D_MODEL), F32)
    for h in range(RET_HEADS):
        qk = slice(h * RET_QK_DIM, (h + 1) * RET_QK_DIM)
        vv = slice(h * RET_V_DIM, (h + 1) * RET_V_DIM)
        q = _rotary(q_ref[:, qk].astype(F32), cos, sin)
        k = _rotary(k_ref[:, qk].astype(F32), cos, sin) * k_scale
        v = v_ref[:, vv]
        q_bf = q.astype(BF16)
        scores = lax.dot_general(q_bf, k.astype(BF16), (((1,), (1,)), ((), ())),
                                 preferred_element_type=F32) * dmat_ref[h]
        state = state_ref[h]
        o = _dot(scores.astype(BF16), v) + _dot(q_bf, state.astype(BF16)) * qdec_ref[h]
        state_ref[h] = state * bdec_ref[h] + lax.dot_general(
            (k * kdec_ref[h]).astype(BF16), v, (((0,), (0,)), ((), ())),
            preferred_element_type=F32)
        mu = jnp.mean(o, axis=-1, keepdims=True)
        oc = o - mu
        var = jnp.mean(oc * oc, axis=-1, keepdims=True)
        on = oc * lax.rsqrt(var + EPS) * gn_ref[:, vv]
        gr = gret_ref[:, vv].astype(F32)
        gated = (gr * _sigmoid(gr) * on).astype(BF16)
        y_ret = y_ret + _dot(gated, wo_ref[vv, :])

    ga = gm_ref[:, :D_MODEL].astype(F32)
    gb = gm_ref[:, D_MODEL:].astype(F32)
    merged = _sigmoid(ga) * yconv_ref[...].astype(F32) + _sigmoid(gb) * y_ret
    o_ref[...] = x_ref[...] + _dot(merged.astype(BF16), wout_ref[...])


def _retention(proj, proj_meta, y_conv, x2d, tables, gn, wo_bf, wout_bf, batch, seq):
    t = proj.shape[0]
    nb = seq // RET_BLOCK
    hq = RET_HEADS * RET_QK_DIM
    hv = RET_HEADS * RET_V_DIM
    q_col, k_col = 2 * CONV_DIM // hq, 2 * CONV_DIM // hq + 1
    v_col = (2 * CONV_DIM + 2 * hq) // hv
    gret_col, gm_col = v_col + 1, v_col + 2
    row = lambda b, i: b * nb + i
    const2 = lambda b, i: (0, 0)
    const3 = lambda b, i: (0, 0, 0)
    cos, sin, mcos, msin, dmat, qdec, kdec, mkdec, bdec = tables
    return pl.pallas_call(
        _ret_kernel,
        grid=(batch, nb),
        in_specs=[
            pl.BlockSpec((RET_BLOCK, hq), lambda b, i: (row(b, i), q_col)),
            pl.BlockSpec((RET_BLOCK, hq), lambda b, i: (row(b, i), k_col)),
            pl.BlockSpec((RET_BLOCK, hv), lambda b, i: (row(b, i), v_col)),
            pl.BlockSpec((RET_BLOCK, hv), lambda b, i: (row(b, i), gret_col)),
            pl.BlockSpec((RET_BLOCK, 2 * D_MODEL), lambda b, i: (row(b, i), gm_col)),
            pl.BlockSpec((RET_BLOCK, D_MODEL), lambda b, i: (row(b, i), 0)),
            pl.BlockSpec((RET_BLOCK, D_MODEL), lambda b, i: (row(b, i), 0)),
            pl.BlockSpec((RET_BLOCK, RET_QK_DIM // 2), lambda b, i: (i, 0)),
            pl.BlockSpec((RET_BLOCK, RET_QK_DIM // 2), lambda b, i: (i, 0)),
            pl.BlockSpec((N_META, hq), lambda b, i: (0, k_col)),
            pl.BlockSpec((N_META, hv), lambda b, i: (0, v_col)),
            pl.BlockSpec((N_META, RET_QK_DIM // 2), const2),
            pl.BlockSpec((N_META, RET_QK_DIM // 2), const2),
            pl.BlockSpec((RET_HEADS, RET_BLOCK, RET_BLOCK), const3),
            pl.BlockSpec((RET_HEADS, RET_BLOCK, 1), const3),
            pl.BlockSpec((RET_HEADS, RET_BLOCK, 1), const3),
            pl.BlockSpec((RET_HEADS, N_META, 1), const3),
            pl.BlockSpec((RET_HEADS, 1, 1), const3),
            pl.BlockSpec((1, hv), const2),
            pl.BlockSpec((hv, D_MODEL), const2),
            pl.BlockSpec((D_MODEL, D_MODEL), const2),
        ],
        out_specs=pl.BlockSpec((RET_BLOCK, D_MODEL), lambda b, i: (row(b, i), 0)),
        out_shape=jax.ShapeDtypeStruct((t, D_MODEL), F32),
        scratch_shapes=[pltpu.VMEM((RET_HEADS, RET_QK_DIM, RET_V_DIM), F32)],
        compiler_params=pltpu.CompilerParams(
            dimension_semantics=("parallel", "arbitrary"),
            vmem_limit_bytes=VMEM_LIMIT),
        name="retention_mix",
    )(proj, proj, proj, proj, proj, y_conv, x2d, cos, sin, proj_meta, proj_meta, mcos, msin,
      dmat, qdec, kdec, mkdec, bdec, gn, wo_bf, wout_bf)


def _retention_tables(seq):
    half = RET_QK_DIM // 2
    inv = ROPE_BASE ** (-jnp.arange(half, dtype=F32) / half)
    pos = jnp.arange(N_META + seq, dtype=F32)
    ang = pos[:, None] * inv[None, :]
    cos_all, sin_all = jnp.cos(ang), jnp.sin(ang)
    log_gamma = jnp.log(1.0 - 2.0 ** (-5.0 - jnp.arange(RET_HEADS, dtype=F32)))
    idx = jnp.arange(RET_BLOCK, dtype=F32)
    chunk = jnp.arange(RET_BLOCK, dtype=jnp.int32) // CHUNK
    visible = chunk[None, :] <= chunk[:, None]
    dmat = jnp.where(visible[None],
                     jnp.exp(log_gamma[:, None, None] * jnp.abs(idx[:, None] - idx[None, :])),
                     0.0)
    qdec = jnp.exp(log_gamma[:, None] * (idx + 1.0))[:, :, None]
    kdec = jnp.exp(log_gamma[:, None] * (RET_BLOCK - 1.0 - idx))[:, :, None]
    midx = jnp.arange(N_META, dtype=F32)
    mkdec = jnp.exp(log_gamma[:, None] * (N_META - 1.0 - midx))[:, :, None]
    bdec = jnp.exp(log_gamma * RET_BLOCK)[:, None, None]
    return (cos_all[N_META:], sin_all[N_META:], cos_all[:N_META], sin_all[:N_META],
            dmat, qdec, kdec, mkdec, bdec)


def _route(logits):
    lane = lax.broadcasted_iota(jnp.int32, logits.shape, 1)
    neg = jnp.float32(-jnp.inf)
    big = jnp.int32(ROUTER_LANES)

    def first_max(masked):
        val = jnp.max(masked, axis=-1, keepdims=True)
        idx = jnp.min(jnp.where(masked == val, lane, big), axis=-1, keepdims=True)
        return val, idx

    gmask = lane < N_GROUPS
    gmax, gidx = first_max(jnp.where(gmask, logits, neg))
    denom = jnp.sum(jnp.where(gmask, jnp.exp(logits - gmax), 0.0), axis=-1, keepdims=True)
    p_group = 1.0 / denom
    assert EXPERTS_PER_GROUP & (EXPERTS_PER_GROUP - 1) == 0
    shift = EXPERTS_PER_GROUP.bit_length() - 1
    lane_group = (lane - N_GROUPS) >> shift
    in_group = jnp.where(lane_group == gidx, logits, neg)
    v1, i1 = first_max(in_group)
    v2, i2 = first_max(jnp.where(lane == i1, neg, in_group))
    e2 = jnp.exp(v2 - v1)
    w1 = p_group / (1.0 + e2)
    w2 = p_group * e2 / (1.0 + e2)
    return jnp.where(lane == i1, w1, 0.0) + jnp.where(lane == i2, w2, 0.0), gidx


def _moe_kernel(h_ref, g_ref, wr_ref, br_ref, wgu_ref, wd_ref, gf_ref, o_ref,
                xs_ref, cs_ref, pos_ref, y_ref, seg_ref):
    grp = pl.program_id(1)
    tm = h_ref.shape[0]

    @pl.when(grp == 0)
    def _():
        h = h_ref[...]
        ms = jnp.mean(h * h, axis=-1, keepdims=True)
        u = h * lax.rsqrt(ms + EPS) * g_ref[...]
        u_hi = u.astype(BF16)
        u_lo = (u - u_hi.astype(F32)).astype(BF16)
        w_hi = wr_ref[0]
        w_lo = wr_ref[1]
        logits = _dot(u_hi, w_hi) + (_dot(u_hi, w_lo) + _dot(u_lo, w_hi)) + br_ref[...]
        comb, gidx = _route(logits)

        lane = lax.broadcasted_iota(jnp.int32, (tm, ROUTER_LANES), 1)
        onehot = jnp.where(lane == gidx, 1.0, 0.0)
        counts = jnp.sum(onehot, axis=0, keepdims=True)
        row = lax.broadcasted_iota(jnp.int32, (tm, tm), 0)
        col = lax.broadcasted_iota(jnp.int32, (tm, tm), 1)
        earlier = jnp.where(col < row, 1.0, 0.0).astype(BF16)
        prefix = _dot(earlier, onehot.astype(BF16))
        lane_row = lax.broadcasted_iota(jnp.int32, (1, ROUTER_LANES), 1)
        start = jnp.int32(0)
        starts = jnp.zeros((1, ROUTER_LANES), F32)
        for gg in range(N_GROUPS):
            seg_ref[gg] = start
            starts = starts + jnp.where(lane_row == gg, start.astype(F32), 0.0)
            start = start + jnp.sum(jnp.where(lane_row == gg, counts, 0.0)).astype(jnp.int32)
        seg_ref[N_GROUPS] = start
        pos = jnp.sum(onehot * (prefix + starts), axis=-1, keepdims=True)
        pos_lanes = jnp.broadcast_to(pos, (tm, ROUTER_LANES))
        pos_ref[...] = pos_lanes
        pos_row = pos_lanes.T[0:1, :].astype(jnp.int32)
        perm = jnp.where(row == pos_row, 1.0, 0.0).astype(BF16)
        xs_ref[...] = _dot(perm, u_hi).astype(BF16)
        c_hi = comb.astype(BF16)
        c_lo = (comb - c_hi.astype(F32)).astype(BF16)
        cs_ref[...] = _dot(perm, c_hi) + _dot(perm, c_lo)
        y_ref[...] = jnp.zeros_like(y_ref)

    seg_lo, seg_hi = seg_ref[grp], seg_ref[grp + 1]
    blk_lo = seg_lo // MOE_BLOCK
    blk_hi = jnp.where(seg_hi > seg_lo, (seg_hi + MOE_BLOCK - 1) // MOE_BLOCK, blk_lo)

    def block_body(blk, carry):
        rows = pl.ds(pl.multiple_of(blk * MOE_BLOCK, MOE_BLOCK), MOE_BLOCK)
        xb = xs_ref[rows, :]
        cb = cs_ref[rows, :]
        lane = lax.broadcasted_iota(jnp.int32, cb.shape, 1)
        acc = jnp.zeros((MOE_BLOCK, D_MODEL), F32)
        for e in range(EXPERTS_PER_GROUP):
            expert_lane = N_GROUPS + grp * EXPERTS_PER_GROUP + e
            w_e = jnp.sum(jnp.where(lane == expert_lane, cb, 0.0), axis=-1, keepdims=True)
            gu = _dot(xb, wgu_ref[e])
            gate, up = gu[:, :D_EXPERT], gu[:, D_EXPERT:]
            hid = gate * _sigmoid(gate) * up * w_e
            acc = acc + _dot(hid.astype(BF16), wd_ref[e])
        y_ref[rows, :] += acc
        return carry

    lax.fori_loop(blk_lo, blk_hi, block_body, 0)

    @pl.when(grp == N_GROUPS - 1)
    def _():
        pos = pos_ref[:, 0:1].astype(jnp.int32)
        col = lax.broadcasted_iota(jnp.int32, (tm, tm), 1)
        unperm = jnp.where(col == pos, 1.0, 0.0).astype(BF16)
        h2 = h_ref[...] + _dot(unperm, y_ref[...].astype(BF16))
        ms = jnp.mean(h2 * h2, axis=-1, keepdims=True)
        o_ref[...] = h2 * lax.rsqrt(ms + EPS) * gf_ref[...]


def _moe(h1, g, wr2, br, wgu_bf, wd_bf, gf, tm):
    t = h1.shape[0]
    const2 = lambda i, e: (0, 0)
    return pl.pallas_call(
        _moe_kernel,
        grid=(t // tm, N_GROUPS),
        in_specs=[
            pl.BlockSpec((tm, D_MODEL), lambda i, e: (i, 0)),
            pl.BlockSpec((1, D_MODEL), const2),
            pl.BlockSpec((2, D_MODEL, ROUTER_LANES), lambda i, e: (0, 0, 0)),
            pl.BlockSpec((1, ROUTER_LANES), const2),
            pl.BlockSpec((EXPERTS_PER_GROUP, D_MODEL, 2 * D_EXPERT), lambda i, e: (e, 0, 0)),
            pl.BlockSpec((EXPERTS_PER_GROUP, D_EXPERT, D_MODEL), lambda i, e: (e, 0, 0)),
            pl.BlockSpec((1, D_MODEL), const2),
        ],
        out_specs=pl.BlockSpec((tm, D_MODEL), lambda i, e: (i, 0)),
        out_shape=jax.ShapeDtypeStruct((t, D_MODEL), F32),
        scratch_shapes=[pltpu.VMEM((tm, D_MODEL), BF16),
                        pltpu.VMEM((tm, ROUTER_LANES), F32),
                        pltpu.VMEM((tm, ROUTER_LANES), F32),
                        pltpu.VMEM((tm, D_MODEL), F32),
                        pltpu.SMEM((SUBLANES,), jnp.int32)],
        compiler_params=pltpu.CompilerParams(
            dimension_semantics=("parallel", "arbitrary"),
            vmem_limit_bytes=MOE_VMEM_LIMIT),
        name="hier_moe",
    )(h1, g, wr2, br, wgu_bf, wd_bf, gf)


def _split_bf16(w):
    hi = w.astype(BF16)
    lo = (w - hi.astype(F32)).astype(BF16)
    return jnp.stack([hi, lo])


def kernel(x, meta_tokens, norm_mix_g, w_in, conv_dw_w, conv_dw_b, conv_ln_g, conv_ln_b,
           conv_pw_w, ret_gn_g, ret_w_o, w_out, norm_ffn_g, w_group_router, b_group_router,
           w_expert_router, b_expert_router, w_expert_gate, w_expert_up, w_expert_down,
           norm_final_g):
    batch, seq, d = x.shape
    assert d == D_MODEL and seq % RET_BLOCK == 0 and w_in.shape[0] == 1
    t = batch * seq
    x2d = x.reshape(t, d)
    row = lambda v: v.reshape(1, -1)

    w_in_bf = w_in[0].astype(BF16)
    tm = min(1024, t)
    proj = _in_proj(x2d, row(norm_mix_g[0]), w_in_bf, tm, IN_PROJ_COLS)
    proj_meta = _in_proj(meta_tokens, row(norm_mix_g[0]), w_in_bf, N_META, IN_PROJ_COLS)

    y_conv = _conv(proj, proj_meta, conv_dw_w[0], row(conv_dw_b[0]), row(conv_ln_g[0]),
                   row(conv_ln_b[0]), conv_pw_w[0].astype(BF16), batch, seq, min(256, seq))
    h1 = _retention(proj, proj_meta, y_conv, x2d, _retention_tables(seq), row(ret_gn_g[0]),
                    ret_w_o[0].astype(BF16), w_out[0].astype(BF16), batch, seq)

    w_router = jnp.concatenate([w_group_router[0], w_expert_router[0]], axis=1)
    w_router = jnp.pad(w_router, ((0, 0), (0, ROUTER_LANES - w_router.shape[1])))
    b_router = jnp.concatenate([b_group_router[0], b_expert_router[0]])
    b_router = jnp.pad(b_router, (0, ROUTER_LANES - b_router.shape[0])).reshape(1, -1)
    w_gu = jnp.concatenate([w_expert_gate[0], w_expert_up[0]], axis=-1).astype(BF16)
    out = _moe(h1, row(norm_ffn_g[0]), _split_bf16(w_router), b_router, w_gu,
               w_expert_down[0].astype(BF16), row(norm_final_g), min(1024, t))
    return out.reshape(batch, seq, d)
```

```python
import functools
import math

import jax
import jax.numpy as jnp
from jax import lax
from jax.experimental import pallas as pl
from jax.experimental.pallas import tpu as pltpu

D_MODEL = 1024
CHUNK = 64
N_META = 16
CONV_DIM = 1024
CONV_WIDTH = 31
RET_HEADS = 4
RET_QK_DIM = 256
RET_V_DIM = 512
ROPE_BASE = 10000.0
N_GROUPS = 4
EXPERTS_PER_GROUP = 4
N_EXPERTS = N_GROUPS * EXPERTS_PER_GROUP
D_EXPERT = 512
EPS = 1e-6
D_IN = 2 * CONV_DIM + 2 * RET_HEADS * RET_QK_DIM + 2 * RET_HEADS * RET_V_DIM + 2 * D_MODEL

LANES = 128
SUBLANES = 8
CONV_HALO = 32
IN_PROJ_COLS = 2048
PROJ_TILE = 1024
CONV_TILE = 256
CONV_ROWS_PER_ITER = 4
RET_BLOCK = 256
ROUTER_LANES = LANES
MOE_BLOCK = 128
VMEM_LIMIT = 48 * 1024 * 1024
PROJ_VMEM_LIMIT = 56 * 1024 * 1024
MOE_VMEM_LIMIT = 60 * 1024 * 1024

F32 = jnp.float32
BF16 = jnp.bfloat16


def _sigmoid(x):
    return 1.0 / (1.0 + jnp.exp(-x))


def _dot(a, b):
    return jnp.dot(a, b, preferred_element_type=F32)


def _in_proj_kernel(x_ref, g_ref, w_ref, o_ref, u_ref):
    @pl.when(pl.program_id(1) == 0)
    def _():
        x = x_ref[...]
        ms = jnp.mean(x * x, axis=-1, keepdims=True)
        u_ref[...] = (x * lax.rsqrt(ms + EPS) * g_ref[...]).astype(BF16)

    o_ref[...] = _dot(u_ref[...], w_ref[...]).astype(o_ref.dtype)


def _in_proj(x, g, w_bf, tm, tn, n):
    t, d = x.shape
    return pl.pallas_call(
        _in_proj_kernel,
        grid=(t // tm, n // tn),
        in_specs=[
            pl.BlockSpec((tm, d), lambda i, j: (i, 0)),
            pl.BlockSpec((1, d), lambda i, j: (0, 0)),
            pl.BlockSpec((d, tn), lambda i, j: (0, j)),
        ],
        out_specs=pl.BlockSpec((tm, tn), lambda i, j: (i, j)),
        out_shape=jax.ShapeDtypeStruct((t, n), BF16),
        scratch_shapes=[pltpu.VMEM((tm, d), BF16)],
        compiler_params=pltpu.CompilerParams(
            dimension_semantics=("parallel", "arbitrary"),
            vmem_limit_bytes=VMEM_LIMIT),
        name="in_proj",
    )(x, g, w_bf)


def _conv_shifts(win_ref, shift_ref):
    span = CONV_TILE + CONV_HALO - SUBLANES
    for s in range(1, SUBLANES):
        shift_ref[s - 1, 0:span, :] = win_ref[s:s + span, :]


def _conv_taps(cb, r0, win_ref, shift_ref, cbuf_ref, wdw_ref, bdw_ref):
    first = CONV_HALO - (CONV_WIDTH - 1)
    cols = slice(cb * LANES, (cb + 1) * LANES)
    bias = jnp.broadcast_to(bdw_ref[:, cols], (SUBLANES, LANES))
    accs = [bias] * CONV_ROWS_PER_ITER
    for shift in range(SUBLANES):
        js = [j for j in range(CONV_WIDTH) if (first + j) % SUBLANES == shift]
        tiles = [(first + j) // SUBLANES for j in js]
        taps = [jnp.broadcast_to(wdw_ref[j:j + 1, cols], (SUBLANES, LANES)) for j in js]
        wins = {}
        for m in range(min(tiles), max(tiles) + CONV_ROWS_PER_ITER):
            rows = slice(r0 + m * SUBLANES, r0 + (m + 1) * SUBLANES)
            wins[m] = (win_ref[rows, cols] if shift == 0
                       else shift_ref[shift - 1, rows, cols])
        for c in range(CONV_ROWS_PER_ITER):
            for tap, m in zip(taps, tiles):
                accs[c] = accs[c] + tap * wins[m + c]
    for c in range(CONV_ROWS_PER_ITER):
        cbuf_ref[r0 + c * SUBLANES:r0 + (c + 1) * SUBLANES, cols] = accs[c]


def _proj_conv_kernel(x_ref, g_ref, w_ref, ma_ref, mg_ref, wdw_ref, bdw_ref, lng_ref, lnb_ref,
                      wpw_ref, proj_ref, yconv_ref, u_ref, hbuf_ref, win_ref, shift_ref,
                      cbuf_ref, *, tiles_per_seq):
    i, j = pl.program_id(0), pl.program_id(1)
    tm = x_ref.shape[0]

    @pl.when(j == 0)
    def _():
        x = x_ref[...]
        ms = jnp.mean(x * x, axis=-1, keepdims=True)
        u = (x * lax.rsqrt(ms + EPS) * g_ref[...]).astype(BF16)
        u_ref[...] = u

        @pl.when(i % tiles_per_seq == 0)
        def _():
            hbuf_ref[0:CONV_HALO - N_META, :] = jnp.zeros((CONV_HALO - N_META, CONV_DIM), F32)
            ma = ma_ref[...].astype(F32)
            mg = mg_ref[...].astype(F32)
            hbuf_ref[CONV_HALO - N_META:CONV_HALO, :] = ma * _sigmoid(mg)

        @pl.when(i % tiles_per_seq != 0)
        def _():
            hbuf_ref[0:CONV_HALO, :] = hbuf_ref[tm:tm + CONV_HALO, :]

        glu = _dot(u, w_ref[...])
        hbuf_ref[CONV_HALO:CONV_HALO + tm, :] = glu[:, :CONV_DIM] * _sigmoid(glu[:, CONV_DIM:])

    @pl.when(j > 0)
    def _():
        base = pl.multiple_of((j - 1) * CONV_TILE, CONV_TILE)
        win_ref[...] = hbuf_ref[pl.ds(base, CONV_HALO + CONV_TILE), :]
        _conv_shifts(win_ref, shift_ref)
        proj_ref[0] = _dot(u_ref[...], w_ref[...]).astype(proj_ref.dtype)
        for cb in range(CONV_DIM // LANES):
            for r0 in range(0, CONV_TILE, SUBLANES * CONV_ROWS_PER_ITER):
                _conv_taps(cb, r0, win_ref, shift_ref, cbuf_ref, wdw_ref, bdw_ref)
        c = cbuf_ref[...]
        mu = jnp.mean(c, axis=-1, keepdims=True)
        cc = c - mu
        var = jnp.mean(cc * cc, axis=-1, keepdims=True)
        y = cc * lax.rsqrt(var + EPS) * lng_ref[...] + lnb_ref[...]
        y = y * _sigmoid(y)
        yconv_ref[pl.ds(base, CONV_TILE), :] = _dot(y.astype(BF16),
                                                    wpw_ref[...]).astype(yconv_ref.dtype)


def _proj_conv(x2d, g, w_bf, proj_meta, wdw, bdw, lng, lnb, wpw_bf, seq, tm):
    t, d = x2d.shape
    n_steps = D_IN // IN_PROJ_COLS
    assert IN_PROJ_COLS == 2 * CONV_DIM and (n_steps - 1) * CONV_TILE == tm and seq % tm == 0
    const = lambda i, j: (0, 0)
    return pl.pallas_call(
        functools.partial(_proj_conv_kernel, tiles_per_seq=seq // tm),
        grid=(t // tm, n_steps),
        in_specs=[
            pl.BlockSpec((tm, d), lambda i, j: (i, 0)),
            pl.BlockSpec((1, d), const),
            pl.BlockSpec((d, IN_PROJ_COLS), lambda i, j: (0, j)),
            pl.BlockSpec((N_META, CONV_DIM), lambda i, j: (0, 0)),
            pl.BlockSpec((N_META, CONV_DIM), lambda i, j: (0, 1)),
            pl.BlockSpec((CONV_WIDTH, CONV_DIM), const),
            pl.BlockSpec((1, CONV_DIM), const),
            pl.BlockSpec((1, CONV_DIM), const),
            pl.BlockSpec((1, CONV_DIM), const),
            pl.BlockSpec((CONV_DIM, D_MODEL), const),
        ],
        out_specs=[
            pl.BlockSpec((1, tm, IN_PROJ_COLS), lambda i, j: (jnp.maximum(j - 1, 0), i, 0)),
            pl.BlockSpec((tm, D_MODEL), lambda i, j: (i, 0)),
        ],
        out_shape=[jax.ShapeDtypeStruct((n_steps - 1, t, IN_PROJ_COLS), BF16),
                   jax.ShapeDtypeStruct((t, D_MODEL), BF16)],
        scratch_shapes=[pltpu.VMEM((tm, d), BF16),
                        pltpu.VMEM((CONV_HALO + tm, CONV_DIM), F32),
                        pltpu.VMEM((CONV_HALO + CONV_TILE, CONV_DIM), F32),
                        pltpu.VMEM((SUBLANES - 1, CONV_HALO + CONV_TILE, CONV_DIM), F32),
                        pltpu.VMEM((CONV_TILE, CONV_DIM), F32)],
        compiler_params=pltpu.CompilerParams(
            dimension_semantics=("arbitrary", "arbitrary"),
            vmem_limit_bytes=PROJ_VMEM_LIMIT),
        name="proj_conv",
    )(x2d, g, w_bf, proj_meta, proj_meta, wdw, bdw, lng, lnb, wpw_bf)


def _rotary(x, cos, sin):
    half = x.shape[-1] // 2
    x1, x2 = x[:, :half], x[:, half:]
    return jnp.concatenate([x1 * cos - x2 * sin, x2 * cos + x1 * sin], axis=-1)


def _ret_kernel(qk_ref, v_ref, gret_ref, gm_ref, yconv_ref, x_ref, cos_ref, sin_ref,
                mk_ref, mv_ref, mcos_ref, msin_ref, dmat_ref, qdec_ref, kdec_ref, mkdec_ref,
                bdec_ref, gn_ref, wo_ref, wout_ref, o_ref, state_ref):
    i = pl.program_id(1)
    k_scale = RET_QK_DIM ** -0.5

    @pl.when(i == 0)
    def _():
        mcos, msin = mcos_ref[...], msin_ref[...]
        for h in range(RET_HEADS):
            mk = mk_ref[:, h * RET_QK_DIM:(h + 1) * RET_QK_DIM].astype(F32)
            mk = _rotary(mk, mcos, msin) * k_scale * mkdec_ref[h]
            mv = mv_ref[:, h * RET_V_DIM:(h + 1) * RET_V_DIM]
            state_ref[h] = lax.dot_general(mk.astype(BF16), mv, (((0,), (0,)), ((), ())),
                                           preferred_element_type=F32)

    cos, sin = cos_ref[...], sin_ref[...]
    y_ret = jnp.zeros((RET_BLOCK, D_MODEL), F32)
    for h in range(RET_HEADS):
        qq = slice(h * RET_QK_DIM, (h + 1) * RET_QK_DIM)
        kk = slice((RET_HEADS + h) * RET_QK_DIM, (RET_HEADS + h + 1) * RET_QK_DIM)
        vv = slice(h * RET_V_DIM, (h + 1) * RET_V_DIM)
        q = _rotary(qk_ref[0, :, qq].astype(F32), cos, sin)
        k = _rotary(qk_ref[0, :, kk].astype(F32), cos, sin) * k_scale
        v = v_ref[0, :, vv]
        q_bf = q.astype(BF16)
        scores = lax.dot_general(q_bf, k.astype(BF16), (((1,), (1,)), ((), ())),
                                 preferred_element_type=F32) * dmat_ref[h]
        state = state_ref[h]
        o = _dot(scores.astype(BF16), v) + _dot(q_bf, state.astype(BF16)) * qdec_ref[h]
        state_ref[h] = state * bdec_ref[h] + lax.dot_general(
            (k * kdec_ref[h]).astype(BF16), v, (((0,), (0,)), ((), ())),
            preferred_element_type=F32)
        mu = jnp.mean(o, axis=-1, keepdims=True)
        oc = o - mu
        var = jnp.mean(oc * oc, axis=-1, keepdims=True)
        on = oc * lax.rsqrt(var + EPS) * gn_ref[:, vv]
        gr = gret_ref[0, :, vv].astype(F32)
        gated = (gr * _sigmoid(gr) * on).astype(BF16)
        y_ret = y_ret + _dot(gated, wo_ref[vv, :])

    ga = gm_ref[0, :, :D_MODEL].astype(F32)
    gb = gm_ref[0, :, D_MODEL:].astype(F32)
    merged = _sigmoid(ga) * yconv_ref[...].astype(F32) + _sigmoid(gb) * y_ret
    o_ref[...] = x_ref[...] + _dot(merged.astype(BF16), wout_ref[...])


def _retention(proj, proj_meta, y_conv, x2d, tables, gn, wo_bf, wout_bf, batch, seq):
    t = proj.shape[1]
    nb = seq // RET_BLOCK
    hq = RET_HEADS * RET_QK_DIM
    hv = RET_HEADS * RET_V_DIM
    assert 2 * hq == hv == 2 * D_MODEL == proj.shape[2]
    slab = lambda s: pl.BlockSpec((1, RET_BLOCK, hv), lambda b, i: (s, row(b, i), 0))
    mk_col, mv_col = 2 * CONV_DIM // hq + 1, (2 * CONV_DIM + 2 * hq) // hv
    row = lambda b, i: b * nb + i
    const2 = lambda b, i: (0, 0)
    const3 = lambda b, i: (0, 0, 0)
    cos, sin, mcos, msin, dmat, qdec, kdec, mkdec, bdec = tables
    return pl.pallas_call(
        _ret_kernel,
        grid=(batch, nb),
        in_specs=[
            slab(0), slab(1), slab(2), slab(3),
            pl.BlockSpec((RET_BLOCK, D_MODEL), lambda b, i: (row(b, i), 0)),
            pl.BlockSpec((RET_BLOCK, D_MODEL), lambda b, i: (row(b, i), 0)),
            pl.BlockSpec((RET_BLOCK, RET_QK_DIM // 2), lambda b, i: (i, 0)),
            pl.BlockSpec((RET_BLOCK, RET_QK_DIM // 2), lambda b, i: (i, 0)),
            pl.BlockSpec((N_META, hq), lambda b, i: (0, mk_col)),
            pl.BlockSpec((N_META, hv), lambda b, i: (0, mv_col)),
            pl.BlockSpec((N_META, RET_QK_DIM // 2), const2),
            pl.BlockSpec((N_META, RET_QK_DIM // 2), const2),
            pl.BlockSpec((RET_HEADS, RET_BLOCK, RET_BLOCK), const3),
            pl.BlockSpec((RET_HEADS, RET_BLOCK, 1), const3),
            pl.BlockSpec((RET_HEADS, RET_BLOCK, 1), const3),
            pl.BlockSpec((RET_HEADS, N_META, 1), const3),
            pl.BlockSpec((RET_HEADS, 1, 1), const3),
            pl.BlockSpec((1, hv), const2),
            pl.BlockSpec((hv, D_MODEL), const2),
            pl.BlockSpec((D_MODEL, D_MODEL), const2),
        ],
        out_specs=pl.BlockSpec((RET_BLOCK, D_MODEL), lambda b, i: (row(b, i), 0)),
        out_shape=jax.ShapeDtypeStruct((t, D_MODEL), F32),
        scratch_shapes=[pltpu.VMEM((RET_HEADS, RET_QK_DIM, RET_V_DIM), F32)],
        compiler_params=pltpu.CompilerParams(
            dimension_semantics=("parallel", "arbitrary"),
            vmem_limit_bytes=VMEM_LIMIT),
        name="retention_mix",
    )(proj, proj, proj, proj, y_conv, x2d, cos, sin, proj_meta, proj_meta, mcos, msin,
      dmat, qdec, kdec, mkdec, bdec, gn, wo_bf, wout_bf)


def _retention_tables(seq):
    half = RET_QK_DIM // 2
    inv = ROPE_BASE ** (-jnp.arange(half, dtype=F32) / half)
    pos = jnp.arange(N_META + seq, dtype=F32)
    ang = pos[:, None] * inv[None, :]
    cos_all, sin_all = jnp.cos(ang), jnp.sin(ang)
    log_gamma = jnp.log(1.0 - 2.0 ** (-5.0 - jnp.arange(RET_HEADS, dtype=F32)))
    idx = jnp.arange(RET_BLOCK, dtype=F32)
    chunk = jnp.arange(RET_BLOCK, dtype=jnp.int32) // CHUNK
    visible = chunk[None, :] <= chunk[:, None]
    dmat = jnp.where(visible[None],
                     jnp.exp(log_gamma[:, None, None] * jnp.abs(idx[:, None] - idx[None, :])),
                     0.0)
    qdec = jnp.exp(log_gamma[:, None] * (idx + 1.0))[:, :, None]
    kdec = jnp.exp(log_gamma[:, None] * (RET_BLOCK - 1.0 - idx))[:, :, None]
    midx = jnp.arange(N_META, dtype=F32)
    mkdec = jnp.exp(log_gamma[:, None] * (N_META - 1.0 - midx))[:, :, None]
    bdec = jnp.exp(log_gamma * RET_BLOCK)[:, None, None]
    return (cos_all[N_META:], sin_all[N_META:], cos_all[:N_META], sin_all[:N_META],
            dmat, qdec, kdec, mkdec, bdec)


def _route(logits):
    lane = lax.broadcasted_iota(jnp.int32, logits.shape, 1)
    neg = jnp.float32(-jnp.inf)
    big = jnp.int32(ROUTER_LANES)

    def first_max(masked):
        val = jnp.max(masked, axis=-1, keepdims=True)
        idx = jnp.min(jnp.where(masked == val, lane, big), axis=-1, keepdims=True)
        return val, idx

    gmask = lane < N_GROUPS
    gmax, gidx = first_max(jnp.where(gmask, logits, neg))
    denom = jnp.sum(jnp.where(gmask, jnp.exp(logits - gmax), 0.0), axis=-1, keepdims=True)
    p_group = 1.0 / denom
    assert EXPERTS_PER_GROUP & (EXPERTS_PER_GROUP - 1) == 0
    shift = EXPERTS_PER_GROUP.bit_length() - 1
    lane_group = (lane - N_GROUPS) >> shift
    in_group = jnp.where(lane_group == gidx, logits, neg)
    v1, i1 = first_max(in_group)
    v2, i2 = first_max(jnp.where(lane == i1, neg, in_group))
    e2 = jnp.exp(v2 - v1)
    w1 = p_group / (1.0 + e2)
    w2 = p_group * e2 / (1.0 + e2)
    return jnp.where(lane == i1, w1, 0.0) + jnp.where(lane == i2, w2, 0.0), gidx


def _moe_kernel(h_ref, g_ref, wr_ref, br_ref, wgu_ref, wd_ref, gf_ref, o_ref,
                xs_ref, cs_ref, pos_ref, y_ref, seg_ref):
    grp = pl.program_id(1)
    tm = h_ref.shape[0]

    @pl.when(grp == 0)
    def _():
        h = h_ref[...]
        ms = jnp.mean(h * h, axis=-1, keepdims=True)
        u = h * lax.rsqrt(ms + EPS) * g_ref[...]
        u_hi = u.astype(BF16)
        u_lo = (u - u_hi.astype(F32)).astype(BF16)
        w_hi = wr_ref[0]
        w_lo = wr_ref[1]
        logits = _dot(u_hi, w_hi) + (_dot(u_hi, w_lo) + _dot(u_lo, w_hi)) + br_ref[...]
        comb, gidx = _route(logits)

        lane = lax.broadcasted_iota(jnp.int32, (tm, ROUTER_LANES), 1)
        onehot = jnp.where(lane == gidx, 1.0, 0.0)
        counts = jnp.sum(onehot, axis=0, keepdims=True)
        row = lax.broadcasted_iota(jnp.int32, (tm, tm), 0)
        col = lax.broadcasted_iota(jnp.int32, (tm, tm), 1)
        earlier = jnp.where(col < row, 1.0, 0.0).astype(BF16)
        prefix = _dot(earlier, onehot.astype(BF16))
        lane_row = lax.broadcasted_iota(jnp.int32, (1, ROUTER_LANES), 1)
        start = jnp.int32(0)
        starts = jnp.zeros((1, ROUTER_LANES), F32)
        for gg in range(N_GROUPS):
            seg_ref[gg] = start
            starts = starts + jnp.where(lane_row == gg, start.astype(F32), 0.0)
            start = start + jnp.sum(jnp.where(lane_row == gg, counts, 0.0)).astype(jnp.int32)
        seg_ref[N_GROUPS] = start
        pos = jnp.sum(onehot * (prefix + starts), axis=-1, keepdims=True)
        pos_lanes = jnp.broadcast_to(pos, (tm, ROUTER_LANES))
        pos_ref[...] = pos_lanes
        pos_row = pos_lanes.T[0:1, :].astype(jnp.int32)
        perm = jnp.where(row == pos_row, 1.0, 0.0).astype(BF16)
        c_hi = comb.astype(BF16)
        c_lo = (comb - c_hi.astype(F32)).astype(BF16)
        moved = _dot(perm, jnp.concatenate([u_hi, c_hi, c_lo], axis=-1))
        xs_ref[...] = moved[:, :D_MODEL].astype(BF16)
        cs_ref[...] = (moved[:, D_MODEL:D_MODEL + ROUTER_LANES]
                       + moved[:, D_MODEL + ROUTER_LANES:])
        y_ref[...] = jnp.zeros_like(y_ref)

    seg_lo, seg_hi = seg_ref[grp], seg_ref[grp + 1]
    blk_lo = seg_lo // MOE_BLOCK
    blk_hi = jnp.where(seg_hi > seg_lo, (seg_hi + MOE_BLOCK - 1) // MOE_BLOCK, blk_lo)

    def block_body(blk, carry):
        rows = pl.ds(pl.multiple_of(blk * MOE_BLOCK, MOE_BLOCK), MOE_BLOCK)
        xb = xs_ref[rows, :]
        cb = cs_ref[rows, :]
        lane = lax.broadcasted_iota(jnp.int32, cb.shape, 1)
        acc = jnp.zeros((MOE_BLOCK, D_MODEL), F32)
        for e in range(EXPERTS_PER_GROUP):
            expert_lane = N_GROUPS + grp * EXPERTS_PER_GROUP + e
            w_e = jnp.sum(jnp.where(lane == expert_lane, cb, 0.0), axis=-1, keepdims=True)
            gu = _dot(xb, wgu_ref[e])
            gate, up = gu[:, :D_EXPERT], gu[:, D_EXPERT:]
            hid = gate * _sigmoid(gate) * up * w_e
            acc = acc + _dot(hid.astype(BF16), wd_ref[e])
        y_ref[rows, :] += acc
        return carry

    lax.fori_loop(blk_lo, blk_hi, block_body, 0)

    @pl.when(grp == N_GROUPS - 1)
    def _():
        pos = pos_ref[:, 0:1].astype(jnp.int32)
        col = lax.broadcasted_iota(jnp.int32, (tm, tm), 1)
        unperm = jnp.where(col == pos, 1.0, 0.0).astype(BF16)
        h2 = h_ref[...] + _dot(unperm, y_ref[...].astype(BF16))
        ms = jnp.mean(h2 * h2, axis=-1, keepdims=True)
        o_ref[...] = h2 * lax.rsqrt(ms + EPS) * gf_ref[...]


def _moe(h1, g, wr2, br, wgu_bf, wd_bf, gf, tm):
    t = h1.shape[0]
    const2 = lambda i, e: (0, 0)
    return pl.pallas_call(
        _moe_kernel,
        grid=(t // tm, N_GROUPS),
        in_specs=[
            pl.BlockSpec((tm, D_MODEL), lambda i, e: (i, 0)),
            pl.BlockSpec((1, D_MODEL), const2),
            pl.BlockSpec((2, D_MODEL, ROUTER_LANES), lambda i, e: (0, 0, 0)),
            pl.BlockSpec((1, ROUTER_LANES), const2),
            pl.BlockSpec((EXPERTS_PER_GROUP, D_MODEL, 2 * D_EXPERT), lambda i, e: (e, 0, 0)),
            pl.BlockSpec((EXPERTS_PER_GROUP, D_EXPERT, D_MODEL), lambda i, e: (e, 0, 0)),
            pl.BlockSpec((1, D_MODEL), const2),
        ],
        out_specs=pl.BlockSpec((tm, D_MODEL), lambda i, e: (i, 0)),
        out_shape=jax.ShapeDtypeStruct((t, D_MODEL), F32),
        scratch_shapes=[pltpu.VMEM((tm, D_MODEL), BF16),
                        pltpu.VMEM((tm, ROUTER_LANES), F32),
                        pltpu.VMEM((tm, ROUTER_LANES), F32),
                        pltpu.VMEM((tm, D_MODEL), F32),
                        pltpu.SMEM((SUBLANES,), jnp.int32)],
        compiler_params=pltpu.CompilerParams(
            dimension_semantics=("parallel", "arbitrary"),
            vmem_limit_bytes=MOE_VMEM_LIMIT),
        name="hier_moe",
    )(h1, g, wr2, br, wgu_bf, wd_bf, gf)


def _split_bf16(w):
    hi = w.astype(BF16)
    lo = (w - hi.astype(F32)).astype(BF16)
    return jnp.stack([hi, lo])


def kernel(x, meta_tokens, norm_mix_g, w_in, conv_dw_w, conv_dw_b, conv_ln_g, conv_ln_b,
           conv_pw_w, ret_gn_g, ret_w_o, w_out, norm_ffn_g, w_group_router, b_group_router,
           w_expert_router, b_expert_router, w_expert_gate, w_expert_up, w_expert_down,
           norm_final_g):
    batch, seq, d = x.shape
    assert d == D_MODEL and seq % RET_BLOCK == 0 and w_in.shape[0] == 1
    t = batch * seq
    x2d = x.reshape(t, d)
    row = lambda v: v.reshape(1, -1)

    w_in_bf = w_in[0].astype(BF16)
    meta_cols = 2 * CONV_DIM + 2 * RET_HEADS * RET_QK_DIM + RET_HEADS * RET_V_DIM
    proj_meta = _in_proj(meta_tokens, row(norm_mix_g[0]), w_in_bf, N_META, IN_PROJ_COLS,
                         meta_cols)
    proj, y_conv = _proj_conv(x2d, row(norm_mix_g[0]), w_in_bf, proj_meta, conv_dw_w[0],
                              row(conv_dw_b[0]), row(conv_ln_g[0]), row(conv_ln_b[0]),
                              conv_pw_w[0].astype(BF16), seq, PROJ_TILE)
    h1 = _retention(proj, proj_meta, y_conv, x2d, _retention_tables(seq), row(ret_gn_g[0]),
                    ret_w_o[0].astype(BF16), w_out[0].astype(BF16), batch, seq)

    w_router = jnp.concatenate([w_group_router[0], w_expert_router[0]], axis=1)
    w_router = jnp.pad(w_router, ((0, 0), (0, ROUTER_LANES - w_router.shape[1])))
    b_router = jnp.concatenate([b_group_router[0], b_expert_router[0]])
    b_router = jnp.pad(b_router, (0, ROUTER_LANES - b_router.shape[0])).reshape(1, -1)
    w_gu = jnp.concatenate([w_expert_gate[0], w_expert_up[0]], axis=-1).astype(BF16)
    out = _moe(h1, row(norm_ffn_g[0]), _split_bf16(w_router), b_router, w_gu,
               w_expert_down[0].astype(BF16), row(norm_final_g), min(1024, t))
    return out.reshape(batch, seq, d)
```

```python
import functools
import math

import jax
import jax.numpy as jnp
from jax import lax
from jax.experimental import pallas as pl
from jax.experimental.pallas import tpu as pltpu

D_MODEL = 1024
CHUNK = 64
N_META = 16
CONV_DIM = 1024
CONV_WIDTH = 31
RET_HEADS = 4
RET_QK_DIM = 256
RET_V_DIM = 512
ROPE_BASE = 10000.0
N_GROUPS = 4
EXPERTS_PER_GROUP = 4
N_EXPERTS = N_GROUPS * EXPERTS_PER_GROUP
D_EXPERT = 512
EPS = 1e-6
D_IN = 2 * CONV_DIM + 2 * RET_HEADS * RET_QK_DIM + 2 * RET_HEADS * RET_V_DIM + 2 * D_MODEL

LANES = 128
SUBLANES = 8
CONV_HALO = 32
IN_PROJ_COLS = 2048
PROJ_TILE = 1024
CONV_TILE = 256
CONV_ROWS_PER_ITER = 4
RET_BLOCK = 256
ROUTER_LANES = LANES
MOE_BLOCK = 128
VMEM_LIMIT = 48 * 1024 * 1024
PROJ_VMEM_LIMIT = 56 * 1024 * 1024
MOE_VMEM_LIMIT = 60 * 1024 * 1024

F32 = jnp.float32
BF16 = jnp.bfloat16


def _sigmoid(x):
    return 1.0 / (1.0 + jnp.exp(-x))


def _dot(a, b):
    return jnp.dot(a, b, preferred_element_type=F32)


def _in_proj_kernel(x_ref, g_ref, w_ref, o_ref, u_ref):
    @pl.when(pl.program_id(1) == 0)
    def _():
        x = x_ref[...]
        ms = jnp.mean(x * x, axis=-1, keepdims=True)
        u_ref[...] = (x * lax.rsqrt(ms + EPS) * g_ref[...]).astype(BF16)

    o_ref[...] = _dot(u_ref[...], w_ref[...]).astype(o_ref.dtype)


def _in_proj(x, g, w_bf, tm, tn, n):
    t, d = x.shape
    return pl.pallas_call(
        _in_proj_kernel,
        grid=(t // tm, n // tn),
        in_specs=[
            pl.BlockSpec((tm, d), lambda i, j: (i, 0)),
            pl.BlockSpec((1, d), lambda i, j: (0, 0)),
            pl.BlockSpec((d, tn), lambda i, j: (0, j)),
        ],
        out_specs=pl.BlockSpec((tm, tn), lambda i, j: (i, j)),
        out_shape=jax.ShapeDtypeStruct((t, n), BF16),
        scratch_shapes=[pltpu.VMEM((tm, d), BF16)],
        compiler_params=pltpu.CompilerParams(
            dimension_semantics=("parallel", "arbitrary"),
            vmem_limit_bytes=VMEM_LIMIT),
        name="in_proj",
    )(x, g, w_bf)


def _conv_shifts(win_ref, shift_ref):
    span = CONV_TILE + CONV_HALO - SUBLANES
    for s in range(1, SUBLANES):
        shift_ref[s - 1, 0:span, :] = win_ref[s:s + span, :]


def _conv_taps(cb, r0, win_ref, shift_ref, cbuf_ref, wdw_ref, bdw_ref):
    def group(m):
        return slice(r0 + m * SUBLANES, r0 + (m + 1) * SUBLANES)

    first = CONV_HALO - (CONV_WIDTH - 1)
    cols = pl.ds(pl.multiple_of(cb * LANES, LANES), LANES)
    bias = jnp.broadcast_to(bdw_ref[:, cols], (SUBLANES, LANES))
    accs = [bias] * CONV_ROWS_PER_ITER
    for shift in range(SUBLANES):
        js = [j for j in range(CONV_WIDTH) if (first + j) % SUBLANES == shift]
        tiles = [(first + j) // SUBLANES for j in js]
        taps = [jnp.broadcast_to(wdw_ref[j:j + 1, cols], (SUBLANES, LANES)) for j in js]
        wins = {}
        for m in range(min(tiles), max(tiles) + CONV_ROWS_PER_ITER):
            wins[m] = (win_ref[group(m), cols] if shift == 0
                       else shift_ref[shift - 1, group(m), cols])
        for c in range(CONV_ROWS_PER_ITER):
            for tap, m in zip(taps, tiles):
                accs[c] = accs[c] + tap * wins[m + c]
    for c in range(CONV_ROWS_PER_ITER):
        cbuf_ref[group(c), cols] = accs[c]


def _proj_conv_kernel(x_ref, g_ref, w_ref, ma_ref, mg_ref, wdw_ref, bdw_ref, lng_ref, lnb_ref,
                      wpw_ref, proj_ref, yconv_ref, u_ref, hbuf_ref, win_ref, shift_ref,
                      cbuf_ref, *, tiles_per_seq):
    i, j = pl.program_id(0), pl.program_id(1)
    tm = x_ref.shape[0]

    @pl.when(j == 0)
    def _():
        x = x_ref[...]
        ms = jnp.mean(x * x, axis=-1, keepdims=True)
        u = (x * lax.rsqrt(ms + EPS) * g_ref[...]).astype(BF16)
        u_ref[...] = u

        @pl.when(i % tiles_per_seq == 0)
        def _():
            hbuf_ref[0:CONV_HALO - N_META, :] = jnp.zeros((CONV_HALO - N_META, CONV_DIM), F32)
            ma = ma_ref[...].astype(F32)
            mg = mg_ref[...].astype(F32)
            hbuf_ref[CONV_HALO - N_META:CONV_HALO, :] = ma * _sigmoid(mg)

        @pl.when(i % tiles_per_seq != 0)
        def _():
            hbuf_ref[0:CONV_HALO, :] = hbuf_ref[tm:tm + CONV_HALO, :]

        glu = _dot(u, w_ref[...])
        hbuf_ref[CONV_HALO:CONV_HALO + tm, :] = glu[:, :CONV_DIM] * _sigmoid(glu[:, CONV_DIM:])

    @pl.when(j > 0)
    def _():
        base = pl.multiple_of((j - 1) * CONV_TILE, CONV_TILE)
        win_ref[...] = hbuf_ref[pl.ds(base, CONV_HALO + CONV_TILE), :]
        _conv_shifts(win_ref, shift_ref)
        proj_ref[0] = _dot(u_ref[...], w_ref[...]).astype(proj_ref.dtype)

        def block_body(cb, carry):
            for r0 in range(0, CONV_TILE, SUBLANES * CONV_ROWS_PER_ITER):
                _conv_taps(cb, r0, win_ref, shift_ref, cbuf_ref, wdw_ref, bdw_ref)
            return carry

        lax.fori_loop(0, CONV_DIM // LANES, block_body, 0)
        c = cbuf_ref[...]
        mu = jnp.mean(c, axis=-1, keepdims=True)
        cc = c - mu
        var = jnp.mean(cc * cc, axis=-1, keepdims=True)
        y = cc * lax.rsqrt(var + EPS) * lng_ref[...] + lnb_ref[...]
        y = y * _sigmoid(y)
        yconv_ref[pl.ds(base, CONV_TILE), :] = _dot(y.astype(BF16),
                                                    wpw_ref[...]).astype(yconv_ref.dtype)


def _proj_conv(x2d, g, w_bf, proj_meta, wdw, bdw, lng, lnb, wpw_bf, seq, tm):
    t, d = x2d.shape
    n_steps = D_IN // IN_PROJ_COLS
    assert IN_PROJ_COLS == 2 * CONV_DIM and (n_steps - 1) * CONV_TILE == tm and seq % tm == 0
    const = lambda i, j: (0, 0)
    return pl.pallas_call(
        functools.partial(_proj_conv_kernel, tiles_per_seq=seq // tm),
        grid=(t // tm, n_steps),
        in_specs=[
            pl.BlockSpec((tm, d), lambda i, j: (i, 0)),
            pl.BlockSpec((1, d), const),
            pl.BlockSpec((d, IN_PROJ_COLS), lambda i, j: (0, j)),
            pl.BlockSpec((N_META, CONV_DIM), lambda i, j: (0, 0)),
            pl.BlockSpec((N_META, CONV_DIM), lambda i, j: (0, 1)),
            pl.BlockSpec((CONV_WIDTH, CONV_DIM), const),
            pl.BlockSpec((1, CONV_DIM), const),
            pl.BlockSpec((1, CONV_DIM), const),
            pl.BlockSpec((1, CONV_DIM), const),
            pl.BlockSpec((CONV_DIM, D_MODEL), const),
        ],
        out_specs=[
            pl.BlockSpec((1, tm, IN_PROJ_COLS), lambda i, j: (jnp.maximum(j - 1, 0), i, 0)),
            pl.BlockSpec((tm, D_MODEL), lambda i, j: (i, 0)),
        ],
        out_shape=[jax.ShapeDtypeStruct((n_steps - 1, t, IN_PROJ_COLS), BF16),
                   jax.ShapeDtypeStruct((t, D_MODEL), BF16)],
        scratch_shapes=[pltpu.VMEM((tm, d), BF16),
                        pltpu.VMEM((CONV_HALO + tm, CONV_DIM), F32),
                        pltpu.VMEM((CONV_HALO + CONV_TILE, CONV_DIM), F32),
                        pltpu.VMEM((SUBLANES - 1, CONV_HALO + CONV_TILE, CONV_DIM), F32),
                        pltpu.VMEM((CONV_TILE, CONV_DIM), F32)],
        compiler_params=pltpu.CompilerParams(
            dimension_semantics=("arbitrary", "arbitrary"),
            vmem_limit_bytes=PROJ_VMEM_LIMIT),
        name="proj_conv",
    )(x2d, g, w_bf, proj_meta, proj_meta, wdw, bdw, lng, lnb, wpw_bf)


def _rotary(x, cos, sin):
    half = x.shape[-1] // 2
    x1, x2 = x[:, :half], x[:, half:]
    return jnp.concatenate([x1 * cos - x2 * sin, x2 * cos + x1 * sin], axis=-1)


def _ret_kernel(qk_ref, v_ref, gret_ref, gm_ref, yconv_ref, x_ref, cos_ref, sin_ref,
                mk_ref, mv_ref, mcos_ref, msin_ref, dmat_ref, qdec_ref, kdec_ref, mkdec_ref,
                bdec_ref, gn_ref, wo_ref, wout_ref, eg_ref, eu_ref, ed_ref,
                o_ref, egu_ref, edn_ref, state_ref):
    i = pl.program_id(1)
    k_scale = RET_QK_DIM ** -0.5

    egu_ref[:, :D_EXPERT] = eg_ref[...].astype(BF16)
    egu_ref[:, D_EXPERT:] = eu_ref[...].astype(BF16)
    edn_ref[...] = ed_ref[...].astype(BF16)

    @pl.when(i == 0)
    def _():
        mcos, msin = mcos_ref[...], msin_ref[...]
        for h in range(RET_HEADS):
            mk = mk_ref[:, h * RET_QK_DIM:(h + 1) * RET_QK_DIM].astype(F32)
            mk = _rotary(mk, mcos, msin) * k_scale * mkdec_ref[h]
            mv = mv_ref[:, h * RET_V_DIM:(h + 1) * RET_V_DIM]
            state_ref[h] = lax.dot_general(mk.astype(BF16), mv, (((0,), (0,)), ((), ())),
                                           preferred_element_type=F32)

    cos, sin = cos_ref[...], sin_ref[...]
    y_ret = jnp.zeros((RET_BLOCK, D_MODEL), F32)
    for h in range(RET_HEADS):
        qq = slice(h * RET_QK_DIM, (h + 1) * RET_QK_DIM)
        kk = slice((RET_HEADS + h) * RET_QK_DIM, (RET_HEADS + h + 1) * RET_QK_DIM)
        vv = slice(h * RET_V_DIM, (h + 1) * RET_V_DIM)
        q = _rotary(qk_ref[0, :, qq].astype(F32), cos, sin)
        k = _rotary(qk_ref[0, :, kk].astype(F32), cos, sin) * k_scale
        v = v_ref[0, :, vv]
        q_bf = q.astype(BF16)
        scores = lax.dot_general(q_bf, k.astype(BF16), (((1,), (1,)), ((), ())),
                                 preferred_element_type=F32) * dmat_ref[h]
        state = state_ref[h]
        o = _dot(scores.astype(BF16), v) + _dot(q_bf, state.astype(BF16)) * qdec_ref[h]
        state_ref[h] = state * bdec_ref[h] + lax.dot_general(
            (k * kdec_ref[h]).astype(BF16), v, (((0,), (0,)), ((), ())),
            preferred_element_type=F32)
        mu = jnp.mean(o, axis=-1, keepdims=True)
        oc = o - mu
        var = jnp.mean(oc * oc, axis=-1, keepdims=True)
        on = oc * lax.rsqrt(var + EPS) * gn_ref[:, vv]
        gr = gret_ref[0, :, vv].astype(F32)
        gated = (gr * _sigmoid(gr) * on).astype(BF16)
        y_ret = y_ret + _dot(gated, wo_ref[vv, :])

    ga = gm_ref[0, :, :D_MODEL].astype(F32)
    gb = gm_ref[0, :, D_MODEL:].astype(F32)
    merged = _sigmoid(ga) * yconv_ref[...].astype(F32) + _sigmoid(gb) * y_ret
    o_ref[...] = x_ref[...] + _dot(merged.astype(BF16), wout_ref[...])


def _retention(proj, proj_meta, y_conv, x2d, tables, gn, wo_bf, wout_bf, w_gate, w_up, w_down,
               batch, seq):
    t = proj.shape[1]
    nb = seq // RET_BLOCK
    n_steps = batch * nb
    gu_rows, dn_rows = N_EXPERTS * D_MODEL // n_steps, N_EXPERTS * D_EXPERT // n_steps
    assert gu_rows * n_steps == N_EXPERTS * D_MODEL and dn_rows * n_steps == N_EXPERTS * D_EXPERT
    assert gu_rows % (2 * SUBLANES) == 0 and dn_rows % (2 * SUBLANES) == 0
    hq = RET_HEADS * RET_QK_DIM
    hv = RET_HEADS * RET_V_DIM
    assert 2 * hq == hv == 2 * D_MODEL == proj.shape[2]
    slab = lambda s: pl.BlockSpec((1, RET_BLOCK, hv), lambda b, i: (s, row(b, i), 0))
    mk_col, mv_col = 2 * CONV_DIM // hq + 1, (2 * CONV_DIM + 2 * hq) // hv
    row = lambda b, i: b * nb + i
    const2 = lambda b, i: (0, 0)
    const3 = lambda b, i: (0, 0, 0)
    cos, sin, mcos, msin, dmat, qdec, kdec, mkdec, bdec = tables
    return pl.pallas_call(
        _ret_kernel,
        grid=(batch, nb),
        in_specs=[
            slab(0), slab(1), slab(2), slab(3),
            pl.BlockSpec((RET_BLOCK, D_MODEL), lambda b, i: (row(b, i), 0)),
            pl.BlockSpec((RET_BLOCK, D_MODEL), lambda b, i: (row(b, i), 0)),
            pl.BlockSpec((RET_BLOCK, RET_QK_DIM // 2), lambda b, i: (i, 0)),
            pl.BlockSpec((RET_BLOCK, RET_QK_DIM // 2), lambda b, i: (i, 0)),
            pl.BlockSpec((N_META, hq), lambda b, i: (0, mk_col)),
            pl.BlockSpec((N_META, hv), lambda b, i: (0, mv_col)),
            pl.BlockSpec((N_META, RET_QK_DIM // 2), const2),
            pl.BlockSpec((N_META, RET_QK_DIM // 2), const2),
            pl.BlockSpec((RET_HEADS, RET_BLOCK, RET_BLOCK), const3),
            pl.BlockSpec((RET_HEADS, RET_BLOCK, 1), const3),
            pl.BlockSpec((RET_HEADS, RET_BLOCK, 1), const3),
            pl.BlockSpec((RET_HEADS, N_META, 1), const3),
            pl.BlockSpec((RET_HEADS, 1, 1), const3),
            pl.BlockSpec((1, hv), const2),
            pl.BlockSpec((hv, D_MODEL), const2),
            pl.BlockSpec((D_MODEL, D_MODEL), const2),
            pl.BlockSpec((gu_rows, D_EXPERT), lambda b, i: (row(b, i), 0)),
            pl.BlockSpec((gu_rows, D_EXPERT), lambda b, i: (row(b, i), 0)),
            pl.BlockSpec((dn_rows, D_MODEL), lambda b, i: (row(b, i), 0)),
        ],
        out_specs=[
            pl.BlockSpec((RET_BLOCK, D_MODEL), lambda b, i: (row(b, i), 0)),
            pl.BlockSpec((gu_rows, 2 * D_EXPERT), lambda b, i: (row(b, i), 0)),
            pl.BlockSpec((dn_rows, D_MODEL), lambda b, i: (row(b, i), 0)),
        ],
        out_shape=[jax.ShapeDtypeStruct((t, D_MODEL), F32),
                   jax.ShapeDtypeStruct((N_EXPERTS * D_MODEL, 2 * D_EXPERT), BF16),
                   jax.ShapeDtypeStruct((N_EXPERTS * D_EXPERT, D_MODEL), BF16)],
        scratch_shapes=[pltpu.VMEM((RET_HEADS, RET_QK_DIM, RET_V_DIM), F32)],
        compiler_params=pltpu.CompilerParams(
            dimension_semantics=("parallel", "arbitrary"),
            vmem_limit_bytes=VMEM_LIMIT),
        name="retention_mix",
    )(proj, proj, proj, proj, y_conv, x2d, cos, sin, proj_meta, proj_meta, mcos, msin,
      dmat, qdec, kdec, mkdec, bdec, gn, wo_bf, wout_bf,
      w_gate.reshape(N_EXPERTS * D_MODEL, D_EXPERT), w_up.reshape(N_EXPERTS * D_MODEL, D_EXPERT),
      w_down.reshape(N_EXPERTS * D_EXPERT, D_MODEL))


def _retention_tables(seq):
    half = RET_QK_DIM // 2
    inv = ROPE_BASE ** (-jnp.arange(half, dtype=F32) / half)
    pos = jnp.arange(N_META + seq, dtype=F32)
    ang = pos[:, None] * inv[None, :]
    cos_all, sin_all = jnp.cos(ang), jnp.sin(ang)
    log_gamma = jnp.log(1.0 - 2.0 ** (-5.0 - jnp.arange(RET_HEADS, dtype=F32)))
    idx = jnp.arange(RET_BLOCK, dtype=F32)
    chunk = jnp.arange(RET_BLOCK, dtype=jnp.int32) // CHUNK
    visible = chunk[None, :] <= chunk[:, None]
    dmat = jnp.where(visible[None],
                     jnp.exp(log_gamma[:, None, None] * jnp.abs(idx[:, None] - idx[None, :])),
                     0.0)
    qdec = jnp.exp(log_gamma[:, None] * (idx + 1.0))[:, :, None]
    kdec = jnp.exp(log_gamma[:, None] * (RET_BLOCK - 1.0 - idx))[:, :, None]
    midx = jnp.arange(N_META, dtype=F32)
    mkdec = jnp.exp(log_gamma[:, None] * (N_META - 1.0 - midx))[:, :, None]
    bdec = jnp.exp(log_gamma * RET_BLOCK)[:, None, None]
    return (cos_all[N_META:], sin_all[N_META:], cos_all[:N_META], sin_all[:N_META],
            dmat, qdec, kdec, mkdec, bdec)


def _route(logits):
    lane = lax.broadcasted_iota(jnp.int32, logits.shape, 1)
    neg = jnp.float32(-jnp.inf)
    big = jnp.int32(ROUTER_LANES)

    def first_max(masked):
        val = jnp.max(masked, axis=-1, keepdims=True)
        idx = jnp.min(jnp.where(masked == val, lane, big), axis=-1, keepdims=True)
        return val, idx

    gmask = lane < N_GROUPS
    gmax, gidx = first_max(jnp.where(gmask, logits, neg))
    denom = jnp.sum(jnp.where(gmask, jnp.exp(logits - gmax), 0.0), axis=-1, keepdims=True)
    p_group = 1.0 / denom
    assert EXPERTS_PER_GROUP & (EXPERTS_PER_GROUP - 1) == 0
    shift = EXPERTS_PER_GROUP.bit_length() - 1
    lane_group = (lane - N_GROUPS) >> shift
    in_group = jnp.where(lane_group == gidx, logits, neg)
    v1, i1 = first_max(in_group)
    v2, i2 = first_max(jnp.where(lane == i1, neg, in_group))
    e2 = jnp.exp(v2 - v1)
    w1 = p_group / (1.0 + e2)
    w2 = p_group * e2 / (1.0 + e2)
    return jnp.where(lane == i1, w1, 0.0) + jnp.where(lane == i2, w2, 0.0), gidx


def _moe_kernel(h_ref, g_ref, wr_ref, br_ref, wgu_ref, wd_ref, gf_ref, o_ref,
                xs_ref, cs_ref, pos_ref, y_ref, seg_ref):
    grp = pl.program_id(1)
    tm = h_ref.shape[0]

    @pl.when(grp == 0)
    def _():
        h = h_ref[...]
        ms = jnp.mean(h * h, axis=-1, keepdims=True)
        u = h * lax.rsqrt(ms + EPS) * g_ref[...]
        u_hi = u.astype(BF16)
        u_lo = (u - u_hi.astype(F32)).astype(BF16)
        hi_part = _dot(u_hi, wr_ref[...])
        lo_part = _dot(u_lo, wr_ref[:, :ROUTER_LANES])
        logits = (hi_part[:, :ROUTER_LANES] + (hi_part[:, ROUTER_LANES:] + lo_part)
                  + br_ref[...])
        comb, gidx = _route(logits)

        lane = lax.broadcasted_iota(jnp.int32, (tm, ROUTER_LANES), 1)
        onehot = jnp.where(lane == gidx, 1.0, 0.0)
        counts = jnp.sum(onehot, axis=0, keepdims=True)
        row = lax.broadcasted_iota(jnp.int32, (tm, tm), 0)
        col = lax.broadcasted_iota(jnp.int32, (tm, tm), 1)
        earlier = jnp.where(col < row, 1.0, 0.0).astype(BF16)
        prefix = _dot(earlier, onehot.astype(BF16))
        lane_row = lax.broadcasted_iota(jnp.int32, (1, ROUTER_LANES), 1)
        start = jnp.int32(0)
        starts = jnp.zeros((1, ROUTER_LANES), F32)
        for gg in range(N_GROUPS):
            seg_ref[gg] = start
            starts = starts + jnp.where(lane_row == gg, start.astype(F32), 0.0)
            start = start + jnp.sum(jnp.where(lane_row == gg, counts, 0.0)).astype(jnp.int32)
        seg_ref[N_GROUPS] = start
        pos = jnp.sum(onehot * (prefix + starts), axis=-1, keepdims=True)
        pos_lanes = jnp.broadcast_to(pos, (tm, ROUTER_LANES))
        pos_ref[...] = pos_lanes
        pos_row = pos_lanes.T[0:1, :].astype(jnp.int32)
        perm = jnp.where(row == pos_row, 1.0, 0.0).astype(BF16)
        c_hi = comb.astype(BF16)
        c_lo = (comb - c_hi.astype(F32)).astype(BF16)
        moved = _dot(perm, jnp.concatenate([u_hi, c_hi, c_lo], axis=-1))
        xs_ref[...] = moved[:, :D_MODEL].astype(BF16)
        cs_ref[...] = (moved[:, D_MODEL:D_MODEL + ROUTER_LANES]
                       + moved[:, D_MODEL + ROUTER_LANES:])
        y_ref[...] = jnp.zeros_like(y_ref)

    seg_lo, seg_hi = seg_ref[grp], seg_ref[grp + 1]
    blk_lo = seg_lo // MOE_BLOCK
    blk_hi = jnp.where(seg_hi > seg_lo, (seg_hi + MOE_BLOCK - 1) // MOE_BLOCK, blk_lo)

    def block_body(blk, carry):
        rows = pl.ds(pl.multiple_of(blk * MOE_BLOCK, MOE_BLOCK), MOE_BLOCK)
        xb = xs_ref[rows, :]
        cb = cs_ref[rows, :]
        lane = lax.broadcasted_iota(jnp.int32, cb.shape, 1)
        acc = jnp.zeros((MOE_BLOCK, D_MODEL), F32)
        for e in range(EXPERTS_PER_GROUP):
            expert_lane = N_GROUPS + grp * EXPERTS_PER_GROUP + e
            w_e = jnp.sum(jnp.where(lane == expert_lane, cb, 0.0), axis=-1, keepdims=True)
            gu = _dot(xb, wgu_ref[e])
            gate, up = gu[:, :D_EXPERT], gu[:, D_EXPERT:]
            hid = gate * _sigmoid(gate) * up * w_e
            acc = acc + _dot(hid.astype(BF16), wd_ref[e])
        y_ref[rows, :] += acc
        return carry

    lax.fori_loop(blk_lo, blk_hi, block_body, 0)

    @pl.when(grp == N_GROUPS - 1)
    def _():
        pos = pos_ref[:, 0:1].astype(jnp.int32)
        col = lax.broadcasted_iota(jnp.int32, (tm, tm), 1)
        unperm = jnp.where(col == pos, 1.0, 0.0).astype(BF16)
        h2 = h_ref[...] + _dot(unperm, y_ref[...].astype(BF16))
        ms = jnp.mean(h2 * h2, axis=-1, keepdims=True)
        o_ref[...] = h2 * lax.rsqrt(ms + EPS) * gf_ref[...]


def _moe(h1, g, wr2, br, wgu_bf, wd_bf, gf, tm):
    t = h1.shape[0]
    const2 = lambda i, e: (0, 0)
    return pl.pallas_call(
        _moe_kernel,
        grid=(t // tm, N_GROUPS),
        in_specs=[
            pl.BlockSpec((tm, D_MODEL), lambda i, e: (i, 0)),
            pl.BlockSpec((1, D_MODEL), const2),
            pl.BlockSpec((D_MODEL, 2 * ROUTER_LANES), const2),
            pl.BlockSpec((1, ROUTER_LANES), const2),
            pl.BlockSpec((EXPERTS_PER_GROUP, D_MODEL, 2 * D_EXPERT), lambda i, e: (e, 0, 0)),
            pl.BlockSpec((EXPERTS_PER_GROUP, D_EXPERT, D_MODEL), lambda i, e: (e, 0, 0)),
            pl.BlockSpec((1, D_MODEL), const2),
        ],
        out_specs=pl.BlockSpec((tm, D_MODEL), lambda i, e: (i, 0)),
        out_shape=jax.ShapeDtypeStruct((t, D_MODEL), F32),
        scratch_shapes=[pltpu.VMEM((tm, D_MODEL), BF16),
                        pltpu.VMEM((tm, ROUTER_LANES), F32),
                        pltpu.VMEM((tm, ROUTER_LANES), F32),
                        pltpu.VMEM((tm, D_MODEL), F32),
                        pltpu.SMEM((SUBLANES,), jnp.int32)],
        compiler_params=pltpu.CompilerParams(
            dimension_semantics=("parallel", "arbitrary"),
            vmem_limit_bytes=MOE_VMEM_LIMIT),
        name="hier_moe",
    )(h1, g, wr2, br, wgu_bf, wd_bf, gf)


def _split_bf16(w):
    hi = w.astype(BF16)
    lo = (w - hi.astype(F32)).astype(BF16)
    return jnp.concatenate([hi, lo], axis=-1)


def kernel(x, meta_tokens, norm_mix_g, w_in, conv_dw_w, conv_dw_b, conv_ln_g, conv_ln_b,
           conv_pw_w, ret_gn_g, ret_w_o, w_out, norm_ffn_g, w_group_router, b_group_router,
           w_expert_router, b_expert_router, w_expert_gate, w_expert_up, w_expert_down,
           norm_final_g):
    batch, seq, d = x.shape
    assert d == D_MODEL and seq % RET_BLOCK == 0 and w_in.shape[0] == 1
    t = batch * seq
    x2d = x.reshape(t, d)
    row = lambda v: v.reshape(1, -1)

    w_in_bf = w_in[0].astype(BF16)
    meta_cols = 2 * CONV_DIM + 2 * RET_HEADS * RET_QK_DIM + RET_HEADS * RET_V_DIM
    proj_meta = _in_proj(meta_tokens, row(norm_mix_g[0]), w_in_bf, N_META, IN_PROJ_COLS,
                         meta_cols)
    proj, y_conv = _proj_conv(x2d, row(norm_mix_g[0]), w_in_bf, proj_meta, conv_dw_w[0],
                              row(conv_dw_b[0]), row(conv_ln_g[0]), row(conv_ln_b[0]),
                              conv_pw_w[0].astype(BF16), seq, PROJ_TILE)
    h1, w_gu, w_dn = _retention(proj, proj_meta, y_conv, x2d, _retention_tables(seq),
                                row(ret_gn_g[0]), ret_w_o[0].astype(BF16), w_out[0].astype(BF16),
                                w_expert_gate[0], w_expert_up[0], w_expert_down[0], batch, seq)
    w_gu = w_gu.reshape(N_EXPERTS, D_MODEL, 2 * D_EXPERT)
    w_dn = w_dn.reshape(N_EXPERTS, D_EXPERT, D_MODEL)

    w_router = jnp.concatenate([w_group_router[0], w_expert_router[0]], axis=1)
    w_router = jnp.pad(w_router, ((0, 0), (0, ROUTER_LANES - w_router.shape[1])))
    b_router = jnp.concatenate([b_group_router[0], b_expert_router[0]])
    b_router = jnp.pad(b_router, (0, ROUTER_LANES - b_router.shape[0])).reshape(1, -1)
    out = _moe(h1, row(norm_ffn_g[0]), _split_bf16(w_router), b_router, w_gu, w_dn,
               row(norm_final_g), min(1024, t))
    return out.reshape(batch, seq, d)
```

```python
import functools
import math

import jax
import jax.numpy as jnp
from jax import lax
from jax.experimental import pallas as pl
from jax.experimental.pallas import tpu as pltpu

D_MODEL = 1024
CHUNK = 64
N_META = 16
CONV_DIM = 1024
CONV_WIDTH = 31
RET_HEADS = 4
RET_QK_DIM = 256
RET_V_DIM = 512
ROPE_BASE = 10000.0
N_GROUPS = 4
EXPERTS_PER_GROUP = 4
N_EXPERTS = N_GROUPS * EXPERTS_PER_GROUP
D_EXPERT = 512
EPS = 1e-6
D_IN = 2 * CONV_DIM + 2 * RET_HEADS * RET_QK_DIM + 2 * RET_HEADS * RET_V_DIM + 2 * D_MODEL

LANES = 128
SUBLANES = 8
CONV_HALO = 32
IN_PROJ_COLS = 2048
PROJ_TILE = 1024
CONV_TILE = 256
CONV_ROWS_PER_ITER = 4
RET_BLOCK = 256
ROUTER_LANES = LANES
MOE_BLOCK = 128
VMEM_LIMIT = 48 * 1024 * 1024
PROJ_VMEM_LIMIT = 56 * 1024 * 1024
MOE_VMEM_LIMIT = 60 * 1024 * 1024

F32 = jnp.float32
BF16 = jnp.bfloat16


def _sigmoid(x):
    return 1.0 / (1.0 + jnp.exp(-x))


def _dot(a, b):
    return jnp.dot(a, b, preferred_element_type=F32)


def _meta_proj_kernel(x_ref, g_ref, w_ref, o_ref, wbf_ref, u_ref):
    @pl.when(pl.program_id(0) == 0)
    def _():
        x = x_ref[...]
        ms = jnp.mean(x * x, axis=-1, keepdims=True)
        u_ref[...] = (x * lax.rsqrt(ms + EPS) * g_ref[...]).astype(BF16)

    w = w_ref[...].astype(BF16)
    wbf_ref[...] = w
    o_ref[...] = _dot(u_ref[...], w).astype(o_ref.dtype)


def _meta_proj(meta, g, w, tn):
    t, d = meta.shape
    n = w.shape[1]
    return pl.pallas_call(
        _meta_proj_kernel,
        grid=(n // tn,),
        in_specs=[
            pl.BlockSpec((t, d), lambda j: (0, 0)),
            pl.BlockSpec((1, d), lambda j: (0, 0)),
            pl.BlockSpec((d, tn), lambda j: (0, j)),
        ],
        out_specs=[pl.BlockSpec((t, tn), lambda j: (0, j)),
                   pl.BlockSpec((d, tn), lambda j: (0, j))],
        out_shape=[jax.ShapeDtypeStruct((t, n), BF16),
                   jax.ShapeDtypeStruct((d, n), BF16)],
        scratch_shapes=[pltpu.VMEM((t, d), BF16)],
        compiler_params=pltpu.CompilerParams(
            dimension_semantics=("arbitrary",),
            vmem_limit_bytes=VMEM_LIMIT),
        name="meta_proj",
    )(meta, g, w)


def _conv_shifts(win_ref, shift_ref):
    span = CONV_TILE + CONV_HALO - SUBLANES
    for s in range(1, SUBLANES):
        shift_ref[s - 1, 0:span, :] = win_ref[s:s + span, :]


def _conv_taps(cb, r0, win_ref, shift_ref, cbuf_ref, wdw_ref, bdw_ref):
    def group(m):
        return slice(r0 + m * SUBLANES, r0 + (m + 1) * SUBLANES)

    first = CONV_HALO - (CONV_WIDTH - 1)
    cols = pl.ds(pl.multiple_of(cb * LANES, LANES), LANES)
    bias = jnp.broadcast_to(bdw_ref[:, cols], (SUBLANES, LANES))
    accs = [bias] * CONV_ROWS_PER_ITER
    for shift in range(SUBLANES):
        js = [j for j in range(CONV_WIDTH) if (first + j) % SUBLANES == shift]
        tiles = [(first + j) // SUBLANES for j in js]
        taps = [jnp.broadcast_to(wdw_ref[j:j + 1, cols], (SUBLANES, LANES)) for j in js]
        wins = {}
        for m in range(min(tiles), max(tiles) + CONV_ROWS_PER_ITER):
            wins[m] = (win_ref[group(m), cols] if shift == 0
                       else shift_ref[shift - 1, group(m), cols])
        for c in range(CONV_ROWS_PER_ITER):
            for tap, m in zip(taps, tiles):
                accs[c] = accs[c] + tap * wins[m + c]
    for c in range(CONV_ROWS_PER_ITER):
        cbuf_ref[group(c), cols] = accs[c]


def _proj_conv_kernel(x_ref, g_ref, w_ref, ma_ref, mg_ref, wdw_ref, bdw_ref, lng_ref, lnb_ref,
                      wpw_ref, proj_ref, yconv_ref, u_ref, hbuf_ref, win_ref, shift_ref,
                      cbuf_ref, *, tiles_per_seq):
    i, j = pl.program_id(0), pl.program_id(1)
    tm = x_ref.shape[0]

    @pl.when(j == 0)
    def _():
        x = x_ref[...]
        ms = jnp.mean(x * x, axis=-1, keepdims=True)
        u = (x * lax.rsqrt(ms + EPS) * g_ref[...]).astype(BF16)
        u_ref[...] = u

        @pl.when(i % tiles_per_seq == 0)
        def _():
            hbuf_ref[0:CONV_HALO - N_META, :] = jnp.zeros((CONV_HALO - N_META, CONV_DIM), F32)
            ma = ma_ref[...].astype(F32)
            mg = mg_ref[...].astype(F32)
            hbuf_ref[CONV_HALO - N_META:CONV_HALO, :] = ma * _sigmoid(mg)

        @pl.when(i % tiles_per_seq != 0)
        def _():
            hbuf_ref[0:CONV_HALO, :] = hbuf_ref[tm:tm + CONV_HALO, :]

        glu = _dot(u, w_ref[...])
        hbuf_ref[CONV_HALO:CONV_HALO + tm, :] = glu[:, :CONV_DIM] * _sigmoid(glu[:, CONV_DIM:])

    @pl.when(j > 0)
    def _():
        base = pl.multiple_of((j - 1) * CONV_TILE, CONV_TILE)
        win_ref[...] = hbuf_ref[pl.ds(base, CONV_HALO + CONV_TILE), :]
        _conv_shifts(win_ref, shift_ref)
        proj_ref[0] = _dot(u_ref[...], w_ref[...]).astype(proj_ref.dtype)

        def block_body(cb, carry):
            for r0 in range(0, CONV_TILE, SUBLANES * CONV_ROWS_PER_ITER):
                _conv_taps(cb, r0, win_ref, shift_ref, cbuf_ref, wdw_ref, bdw_ref)
            return carry

        lax.fori_loop(0, CONV_DIM // LANES, block_body, 0)
        c = cbuf_ref[...]
        mu = jnp.mean(c, axis=-1, keepdims=True)
        cc = c - mu
        var = jnp.mean(cc * cc, axis=-1, keepdims=True)
        y = cc * lax.rsqrt(var + EPS) * lng_ref[...] + lnb_ref[...]
        y = y * _sigmoid(y)
        yconv_ref[pl.ds(base, CONV_TILE), :] = _dot(y.astype(BF16),
                                                    wpw_ref[...]).astype(yconv_ref.dtype)


def _proj_conv(x2d, g, w_bf, proj_meta, wdw, bdw, lng, lnb, wpw_bf, seq, tm):
    t, d = x2d.shape
    n_steps = D_IN // IN_PROJ_COLS
    assert IN_PROJ_COLS == 2 * CONV_DIM and (n_steps - 1) * CONV_TILE == tm and seq % tm == 0
    const = lambda i, j: (0, 0)
    return pl.pallas_call(
        functools.partial(_proj_conv_kernel, tiles_per_seq=seq // tm),
        grid=(t // tm, n_steps),
        in_specs=[
            pl.BlockSpec((tm, d), lambda i, j: (i, 0)),
            pl.BlockSpec((1, d), const),
            pl.BlockSpec((d, IN_PROJ_COLS), lambda i, j: (0, j)),
            pl.BlockSpec((N_META, CONV_DIM), lambda i, j: (0, 0)),
            pl.BlockSpec((N_META, CONV_DIM), lambda i, j: (0, 1)),
            pl.BlockSpec((CONV_WIDTH, CONV_DIM), const),
            pl.BlockSpec((1, CONV_DIM), const),
            pl.BlockSpec((1, CONV_DIM), const),
            pl.BlockSpec((1, CONV_DIM), const),
            pl.BlockSpec((CONV_DIM, D_MODEL), const),
        ],
        out_specs=[
            pl.BlockSpec((1, tm, IN_PROJ_COLS), lambda i, j: (jnp.maximum(j - 1, 0), i, 0)),
            pl.BlockSpec((tm, D_MODEL), lambda i, j: (i, 0)),
        ],
        out_shape=[jax.ShapeDtypeStruct((n_steps - 1, t, IN_PROJ_COLS), BF16),
                   jax.ShapeDtypeStruct((t, D_MODEL), BF16)],
        scratch_shapes=[pltpu.VMEM((tm, d), BF16),
                        pltpu.VMEM((CONV_HALO + tm, CONV_DIM), F32),
                        pltpu.VMEM((CONV_HALO + CONV_TILE, CONV_DIM), F32),
                        pltpu.VMEM((SUBLANES - 1, CONV_HALO + CONV_TILE, CONV_DIM), F32),
                        pltpu.VMEM((CONV_TILE, CONV_DIM), F32)],
        compiler_params=pltpu.CompilerParams(
            dimension_semantics=("arbitrary", "arbitrary"),
            vmem_limit_bytes=PROJ_VMEM_LIMIT),
        name="proj_conv",
    )(x2d, g, w_bf, proj_meta, proj_meta, wdw, bdw, lng, lnb, wpw_bf)


def _rotary(x, cos, sin):
    half = x.shape[-1] // 2
    x1, x2 = x[:, :half], x[:, half:]
    return jnp.concatenate([x1 * cos - x2 * sin, x2 * cos + x1 * sin], axis=-1)


def _ret_kernel(qk_ref, v_ref, gret_ref, gm_ref, yconv_ref, x_ref, cos_ref, sin_ref,
                mk_ref, mv_ref, mcos_ref, msin_ref, dmat_ref, qdec_ref, kdec_ref, mkdec_ref,
                bdec_ref, gn_ref, wo_ref, wout_ref, eg_ref, eu_ref, ed_ref,
                o_ref, egu_ref, edn_ref, state_ref):
    i = pl.program_id(1)
    k_scale = RET_QK_DIM ** -0.5

    egu_ref[:, :D_EXPERT] = eg_ref[...].astype(BF16)
    egu_ref[:, D_EXPERT:] = eu_ref[...].astype(BF16)
    edn_ref[...] = ed_ref[...].astype(BF16)

    @pl.when(i == 0)
    def _():
        mcos, msin = mcos_ref[...], msin_ref[...]
        for h in range(RET_HEADS):
            mk = mk_ref[:, h * RET_QK_DIM:(h + 1) * RET_QK_DIM].astype(F32)
            mk = _rotary(mk, mcos, msin) * k_scale * mkdec_ref[h]
            mv = mv_ref[:, h * RET_V_DIM:(h + 1) * RET_V_DIM]
            state_ref[h] = lax.dot_general(mk.astype(BF16), mv, (((0,), (0,)), ((), ())),
                                           preferred_element_type=F32)

    cos, sin = cos_ref[...], sin_ref[...]
    y_ret = jnp.zeros((RET_BLOCK, D_MODEL), F32)
    for h in range(RET_HEADS):
        qq = slice(h * RET_QK_DIM, (h + 1) * RET_QK_DIM)
        kk = slice((RET_HEADS + h) * RET_QK_DIM, (RET_HEADS + h + 1) * RET_QK_DIM)
        vv = slice(h * RET_V_DIM, (h + 1) * RET_V_DIM)
        q = _rotary(qk_ref[0, :, qq].astype(F32), cos, sin)
        k = _rotary(qk_ref[0, :, kk].astype(F32), cos, sin) * k_scale
        v = v_ref[0, :, vv]
        q_bf = q.astype(BF16)
        scores = lax.dot_general(q_bf, k.astype(BF16), (((1,), (1,)), ((), ())),
                                 preferred_element_type=F32) * dmat_ref[h]
        state = state_ref[h]
        o = _dot(scores.astype(BF16), v) + _dot(q_bf, state.astype(BF16)) * qdec_ref[h]
        state_ref[h] = state * bdec_ref[h] + lax.dot_general(
            (k * kdec_ref[h]).astype(BF16), v, (((0,), (0,)), ((), ())),
            preferred_element_type=F32)
        mu = jnp.mean(o, axis=-1, keepdims=True)
        oc = o - mu
        var = jnp.mean(oc * oc, axis=-1, keepdims=True)
        on = oc * lax.rsqrt(var + EPS) * gn_ref[:, vv]
        gr = gret_ref[0, :, vv]
        gated = gr * _sigmoid(gr) * on.astype(BF16)
        y_ret = y_ret + _dot(gated, wo_ref[vv, :])

    ga = gm_ref[0, :, :D_MODEL]
    gb = gm_ref[0, :, D_MODEL:]
    merged = _sigmoid(ga) * yconv_ref[...] + _sigmoid(gb) * y_ret.astype(BF16)
    o_ref[...] = x_ref[...] + _dot(merged, wout_ref[...])


def _retention(proj, proj_meta, y_conv, x2d, tables, gn, wo_bf, wout_bf, w_gate, w_up, w_down,
               batch, seq):
    t = proj.shape[1]
    nb = seq // RET_BLOCK
    n_steps = batch * nb
    gu_rows, dn_rows = N_EXPERTS * D_MODEL // n_steps, N_EXPERTS * D_EXPERT // n_steps
    assert gu_rows * n_steps == N_EXPERTS * D_MODEL and dn_rows * n_steps == N_EXPERTS * D_EXPERT
    assert gu_rows % (2 * SUBLANES) == 0 and dn_rows % (2 * SUBLANES) == 0
    hq = RET_HEADS * RET_QK_DIM
    hv = RET_HEADS * RET_V_DIM
    assert 2 * hq == hv == 2 * D_MODEL == proj.shape[2]
    slab = lambda s: pl.BlockSpec((1, RET_BLOCK, hv), lambda b, i: (s, row(b, i), 0))
    mk_col, mv_col = 2 * CONV_DIM // hq + 1, (2 * CONV_DIM + 2 * hq) // hv
    row = lambda b, i: b * nb + i
    const2 = lambda b, i: (0, 0)
    const3 = lambda b, i: (0, 0, 0)
    cos, sin, mcos, msin, dmat, qdec, kdec, mkdec, bdec = tables
    return pl.pallas_call(
        _ret_kernel,
        grid=(batch, nb),
        in_specs=[
            slab(0), slab(1), slab(2), slab(3),
            pl.BlockSpec((RET_BLOCK, D_MODEL), lambda b, i: (row(b, i), 0)),
            pl.BlockSpec((RET_BLOCK, D_MODEL), lambda b, i: (row(b, i), 0)),
            pl.BlockSpec((RET_BLOCK, RET_QK_DIM // 2), lambda b, i: (i, 0)),
            pl.BlockSpec((RET_BLOCK, RET_QK_DIM // 2), lambda b, i: (i, 0)),
            pl.BlockSpec((N_META, hq), lambda b, i: (0, mk_col)),
            pl.BlockSpec((N_META, hv), lambda b, i: (0, mv_col)),
            pl.BlockSpec((N_META, RET_QK_DIM // 2), const2),
            pl.BlockSpec((N_META, RET_QK_DIM // 2), const2),
            pl.BlockSpec((RET_HEADS, RET_BLOCK, RET_BLOCK), const3),
            pl.BlockSpec((RET_HEADS, RET_BLOCK, 1), const3),
            pl.BlockSpec((RET_HEADS, RET_BLOCK, 1), const3),
            pl.BlockSpec((RET_HEADS, N_META, 1), const3),
            pl.BlockSpec((RET_HEADS, 1, 1), const3),
            pl.BlockSpec((1, hv), const2),
            pl.BlockSpec((hv, D_MODEL), const2),
            pl.BlockSpec((D_MODEL, D_MODEL), const2),
            pl.BlockSpec((gu_rows, D_EXPERT), lambda b, i: (row(b, i), 0)),
            pl.BlockSpec((gu_rows, D_EXPERT), lambda b, i: (row(b, i), 0)),
            pl.BlockSpec((dn_rows, D_MODEL), lambda b, i: (row(b, i), 0)),
        ],
        out_specs=[
            pl.BlockSpec((RET_BLOCK, D_MODEL), lambda b, i: (row(b, i), 0)),
            pl.BlockSpec((gu_rows, 2 * D_EXPERT), lambda b, i: (row(b, i), 0)),
            pl.BlockSpec((dn_rows, D_MODEL), lambda b, i: (row(b, i), 0)),
        ],
        out_shape=[jax.ShapeDtypeStruct((t, D_MODEL), F32),
                   jax.ShapeDtypeStruct((N_EXPERTS * D_MODEL, 2 * D_EXPERT), BF16),
                   jax.ShapeDtypeStruct((N_EXPERTS * D_EXPERT, D_MODEL), BF16)],
        scratch_shapes=[pltpu.VMEM((RET_HEADS, RET_QK_DIM, RET_V_DIM), F32)],
        compiler_params=pltpu.CompilerParams(
            dimension_semantics=("parallel", "arbitrary"),
            vmem_limit_bytes=VMEM_LIMIT),
        name="retention_mix",
    )(proj, proj, proj, proj, y_conv, x2d, cos, sin, proj_meta, proj_meta, mcos, msin,
      dmat, qdec, kdec, mkdec, bdec, gn, wo_bf, wout_bf,
      w_gate.reshape(N_EXPERTS * D_MODEL, D_EXPERT), w_up.reshape(N_EXPERTS * D_MODEL, D_EXPERT),
      w_down.reshape(N_EXPERTS * D_EXPERT, D_MODEL))


def _retention_tables(seq):
    half = RET_QK_DIM // 2
    inv = ROPE_BASE ** (-jnp.arange(half, dtype=F32) / half)
    pos = jnp.arange(N_META + seq, dtype=F32)
    ang = pos[:, None] * inv[None, :]
    cos_all, sin_all = jnp.cos(ang), jnp.sin(ang)
    log_gamma = jnp.log(1.0 - 2.0 ** (-5.0 - jnp.arange(RET_HEADS, dtype=F32)))
    idx = jnp.arange(RET_BLOCK, dtype=F32)
    chunk = jnp.arange(RET_BLOCK, dtype=jnp.int32) // CHUNK
    visible = chunk[None, :] <= chunk[:, None]
    dmat = jnp.where(visible[None],
                     jnp.exp(log_gamma[:, None, None] * jnp.abs(idx[:, None] - idx[None, :])),
                     0.0)
    qdec = jnp.exp(log_gamma[:, None] * (idx + 1.0))[:, :, None]
    kdec = jnp.exp(log_gamma[:, None] * (RET_BLOCK - 1.0 - idx))[:, :, None]
    midx = jnp.arange(N_META, dtype=F32)
    mkdec = jnp.exp(log_gamma[:, None] * (N_META - 1.0 - midx))[:, :, None]
    bdec = jnp.exp(log_gamma * RET_BLOCK)[:, None, None]
    return (cos_all[N_META:], sin_all[N_META:], cos_all[:N_META], sin_all[:N_META],
            dmat, qdec, kdec, mkdec, bdec)


def _route(logits):
    lane = lax.broadcasted_iota(jnp.int32, logits.shape, 1)
    neg = jnp.float32(-jnp.inf)
    big = jnp.int32(ROUTER_LANES)

    def first_max(masked):
        val = jnp.max(masked, axis=-1, keepdims=True)
        idx = jnp.min(jnp.where(masked == val, lane, big), axis=-1, keepdims=True)
        return val, idx

    gmask = lane < N_GROUPS
    gmax, gidx = first_max(jnp.where(gmask, logits, neg))
    denom = jnp.sum(jnp.where(gmask, jnp.exp(logits - gmax), 0.0), axis=-1, keepdims=True)
    p_group = 1.0 / denom
    assert EXPERTS_PER_GROUP & (EXPERTS_PER_GROUP - 1) == 0
    shift = EXPERTS_PER_GROUP.bit_length() - 1
    lane_group = (lane - N_GROUPS) >> shift
    in_group = jnp.where(lane_group == gidx, logits, neg)
    v1, i1 = first_max(in_group)
    v2, i2 = first_max(jnp.where(lane == i1, neg, in_group))
    e2 = jnp.exp(v2 - v1)
    w1 = p_group / (1.0 + e2)
    w2 = p_group * e2 / (1.0 + e2)
    return jnp.where(lane == i1, w1, 0.0) + jnp.where(lane == i2, w2, 0.0), gidx


def _moe_kernel(h_ref, g_ref, wr_ref, br_ref, wgu_ref, wd_ref, gf_ref, o_ref,
                xs_ref, cs_ref, pos_ref, y_ref, seg_ref):
    grp = pl.program_id(1)
    tm = h_ref.shape[0]

    @pl.when(grp == 0)
    def _():
        h = h_ref[...]
        ms = jnp.mean(h * h, axis=-1, keepdims=True)
        u = h * lax.rsqrt(ms + EPS) * g_ref[...]
        u_hi = u.astype(BF16)
        u_lo = (u - u_hi.astype(F32)).astype(BF16)
        hi_part = _dot(u_hi, wr_ref[...])
        lo_part = _dot(u_lo, wr_ref[:, :ROUTER_LANES])
        logits = (hi_part[:, :ROUTER_LANES] + (hi_part[:, ROUTER_LANES:] + lo_part)
                  + br_ref[...])
        comb, gidx = _route(logits)

        lane = lax.broadcasted_iota(jnp.int32, (tm, ROUTER_LANES), 1)
        onehot = jnp.where(lane == gidx, 1.0, 0.0)
        counts = jnp.sum(onehot, axis=0, keepdims=True)
        row = lax.broadcasted_iota(jnp.int32, (tm, tm), 0)
        col = lax.broadcasted_iota(jnp.int32, (tm, tm), 1)
        earlier = jnp.where(col < row, 1.0, 0.0).astype(BF16)
        prefix = _dot(earlier, onehot.astype(BF16))
        lane_row = lax.broadcasted_iota(jnp.int32, (1, ROUTER_LANES), 1)
        start = jnp.int32(0)
        starts = jnp.zeros((1, ROUTER_LANES), F32)
        for gg in range(N_GROUPS):
            seg_ref[gg] = start
            starts = starts + jnp.where(lane_row == gg, start.astype(F32), 0.0)
            start = start + jnp.sum(jnp.where(lane_row == gg, counts, 0.0)).astype(jnp.int32)
        seg_ref[N_GROUPS] = start
        pos = jnp.sum(onehot * (prefix + starts), axis=-1, keepdims=True)
        pos_lanes = jnp.broadcast_to(pos, (tm, ROUTER_LANES))
        pos_ref[...] = pos_lanes
        pos_row = pos_lanes.T[0:1, :].astype(jnp.int32)
        perm = jnp.where(row == pos_row, 1.0, 0.0).astype(BF16)
        c_hi = comb.astype(BF16)
        c_lo = (comb - c_hi.astype(F32)).astype(BF16)
        moved = _dot(perm, jnp.concatenate([u_hi, c_hi, c_lo], axis=-1))
        xs_ref[...] = moved[:, :D_MODEL].astype(BF16)
        cs_ref[...] = (moved[:, D_MODEL:D_MODEL + ROUTER_LANES]
                       + moved[:, D_MODEL + ROUTER_LANES:])
        y_ref[...] = jnp.zeros_like(y_ref)

    seg_lo, seg_hi = seg_ref[grp], seg_ref[grp + 1]
    blk_lo = seg_lo // MOE_BLOCK
    blk_hi = jnp.where(seg_hi > seg_lo, (seg_hi + MOE_BLOCK - 1) // MOE_BLOCK, blk_lo)

    def block_body(blk, carry):
        rows = pl.ds(pl.multiple_of(blk * MOE_BLOCK, MOE_BLOCK), MOE_BLOCK)
        xb = xs_ref[rows, :]
        cb = cs_ref[rows, :]
        lane = lax.broadcasted_iota(jnp.int32, cb.shape, 1)
        acc = jnp.zeros((MOE_BLOCK, D_MODEL), F32)
        for e in range(EXPERTS_PER_GROUP):
            expert_lane = N_GROUPS + grp * EXPERTS_PER_GROUP + e
            w_e = jnp.sum(jnp.where(lane == expert_lane, cb, 0.0), axis=-1, keepdims=True)
            gu = _dot(xb, wgu_ref[e])
            gate, up = gu[:, :D_EXPERT], gu[:, D_EXPERT:]
            hid = gate * _sigmoid(gate) * up * w_e
            acc = acc + _dot(hid.astype(BF16), wd_ref[e])
        y_ref[rows, :] += acc
        return carry

    lax.fori_loop(blk_lo, blk_hi, block_body, 0)

    @pl.when(grp == N_GROUPS - 1)
    def _():
        pos = pos_ref[:, 0:1].astype(jnp.int32)
        col = lax.broadcasted_iota(jnp.int32, (tm, tm), 1)
        unperm = jnp.where(col == pos, 1.0, 0.0).astype(BF16)
        h2 = h_ref[...] + _dot(unperm, y_ref[...].astype(BF16))
        ms = jnp.mean(h2 * h2, axis=-1, keepdims=True)
        o_ref[...] = h2 * lax.rsqrt(ms + EPS) * gf_ref[...]


def _moe(h1, g, wr2, br, wgu_bf, wd_bf, gf, tm):
    t = h1.shape[0]
    const2 = lambda i, e: (0, 0)
    return pl.pallas_call(
        _moe_kernel,
        grid=(t // tm, N_GROUPS),
        in_specs=[
            pl.BlockSpec((tm, D_MODEL), lambda i, e: (i, 0)),
            pl.BlockSpec((1, D_MODEL), const2),
            pl.BlockSpec((D_MODEL, 2 * ROUTER_LANES), const2),
            pl.BlockSpec((1, ROUTER_LANES), const2),
            pl.BlockSpec((EXPERTS_PER_GROUP, D_MODEL, 2 * D_EXPERT), lambda i, e: (e, 0, 0)),
            pl.BlockSpec((EXPERTS_PER_GROUP, D_EXPERT, D_MODEL), lambda i, e: (e, 0, 0)),
            pl.BlockSpec((1, D_MODEL), const2),
        ],
        out_specs=pl.BlockSpec((tm, D_MODEL), lambda i, e: (i, 0)),
        out_shape=jax.ShapeDtypeStruct((t, D_MODEL), F32),
        scratch_shapes=[pltpu.VMEM((tm, D_MODEL), BF16),
                        pltpu.VMEM((tm, ROUTER_LANES), F32),
                        pltpu.VMEM((tm, ROUTER_LANES), F32),
                        pltpu.VMEM((tm, D_MODEL), F32),
                        pltpu.SMEM((SUBLANES,), jnp.int32)],
        compiler_params=pltpu.CompilerParams(
            dimension_semantics=("parallel", "arbitrary"),
            vmem_limit_bytes=MOE_VMEM_LIMIT),
        name="hier_moe",
    )(h1, g, wr2, br, wgu_bf, wd_bf, gf)


def _split_bf16(w):
    hi = w.astype(BF16)
    lo = (w - hi.astype(F32)).astype(BF16)
    return jnp.concatenate([hi, lo], axis=-1)


def kernel(x, meta_tokens, norm_mix_g, w_in, conv_dw_w, conv_dw_b, conv_ln_g, conv_ln_b,
           conv_pw_w, ret_gn_g, ret_w_o, w_out, norm_ffn_g, w_group_router, b_group_router,
           w_expert_router, b_expert_router, w_expert_gate, w_expert_up, w_expert_down,
           norm_final_g):
    batch, seq, d = x.shape
    assert d == D_MODEL and seq % RET_BLOCK == 0 and w_in.shape[0] == 1
    t = batch * seq
    x2d = x.reshape(t, d)
    row = lambda v: v.reshape(1, -1)

    proj_meta, w_in_bf = _meta_proj(meta_tokens, row(norm_mix_g[0]), w_in[0], IN_PROJ_COLS)
    proj, y_conv = _proj_conv(x2d, row(norm_mix_g[0]), w_in_bf, proj_meta, conv_dw_w[0],
                              row(conv_dw_b[0]), row(conv_ln_g[0]), row(conv_ln_b[0]),
                              conv_pw_w[0].astype(BF16), seq, PROJ_TILE)
    h1, w_gu, w_dn = _retention(proj, proj_meta, y_conv, x2d, _retention_tables(seq),
                                row(ret_gn_g[0]), ret_w_o[0].astype(BF16), w_out[0].astype(BF16),
                                w_expert_gate[0], w_expert_up[0], w_expert_down[0], batch, seq)
    w_gu = w_gu.reshape(N_EXPERTS, D_MODEL, 2 * D_EXPERT)
    w_dn = w_dn.reshape(N_EXPERTS, D_EXPERT, D_MODEL)

    w_router = jnp.concatenate([w_group_router[0], w_expert_router[0]], axis=1)
    w_router = jnp.pad(w_router, ((0, 0), (0, ROUTER_LANES - w_router.shape[1])))
    b_router = jnp.concatenate([b_group_router[0], b_expert_router[0]])
    b_router = jnp.pad(b_router, (0, ROUTER_LANES - b_router.shape[0])).reshape(1, -1)
    out = _moe(h1, row(norm_ffn_g[0]), _split_bf16(w_router), b_router, w_gu, w_dn,
               row(norm_final_g), min(1024, t))
    return out.reshape(batch, seq, d)
```

```python
import functools
import math

import jax
import jax.numpy as jnp
from jax import lax
from jax.experimental import pallas as pl
from jax.experimental.pallas import tpu as pltpu

D_MODEL = 1024
CHUNK = 64
N_META = 16
CONV_DIM = 1024
CONV_WIDTH = 31
RET_HEADS = 4
RET_QK_DIM = 256
RET_V_DIM = 512
ROPE_BASE = 10000.0
N_GROUPS = 4
EXPERTS_PER_GROUP = 4
N_EXPERTS = N_GROUPS * EXPERTS_PER_GROUP
D_EXPERT = 512
EPS = 1e-6
D_IN = 2 * CONV_DIM + 2 * RET_HEADS * RET_QK_DIM + 2 * RET_HEADS * RET_V_DIM + 2 * D_MODEL

LANES = 128
SUBLANES = 8
CONV_HALO = 32
IN_PROJ_COLS = 2048
PROJ_TILE = 1024
CONV_TILE = 256
CONV_ROWS_PER_ITER = 4
RET_BLOCK = 256
ROUTER_LANES = LANES
MOE_BLOCK = 128
VMEM_LIMIT = 48 * 1024 * 1024
PROJ_VMEM_LIMIT = 56 * 1024 * 1024
MOE_VMEM_LIMIT = 60 * 1024 * 1024

F32 = jnp.float32
BF16 = jnp.bfloat16


def _sigmoid(x):
    return 1.0 / (1.0 + jnp.exp(-x))


def _dot(a, b):
    return jnp.dot(a, b, preferred_element_type=F32)


def _meta_proj_kernel(x_ref, g_ref, w_ref, o_ref, wbf_ref, u_ref):
    @pl.when(pl.program_id(0) == 0)
    def _():
        x = x_ref[...]
        ms = jnp.mean(x * x, axis=-1, keepdims=True)
        u_ref[...] = (x * lax.rsqrt(ms + EPS) * g_ref[...]).astype(BF16)

    w = w_ref[...].astype(BF16)
    wbf_ref[...] = w
    o_ref[...] = _dot(u_ref[...], w).astype(o_ref.dtype)


def _meta_proj(meta, g, w, tn):
    t, d = meta.shape
    n = w.shape[1]
    return pl.pallas_call(
        _meta_proj_kernel,
        grid=(n // tn,),
        in_specs=[
            pl.BlockSpec((t, d), lambda j: (0, 0)),
            pl.BlockSpec((1, d), lambda j: (0, 0)),
            pl.BlockSpec((d, tn), lambda j: (0, j)),
        ],
        out_specs=[pl.BlockSpec((t, tn), lambda j: (0, j)),
                   pl.BlockSpec((d, tn), lambda j: (0, j))],
        out_shape=[jax.ShapeDtypeStruct((t, n), BF16),
                   jax.ShapeDtypeStruct((d, n), BF16)],
        scratch_shapes=[pltpu.VMEM((t, d), BF16)],
        compiler_params=pltpu.CompilerParams(
            dimension_semantics=("arbitrary",),
            vmem_limit_bytes=VMEM_LIMIT),
        name="meta_proj",
    )(meta, g, w)


def _conv_shifts(win_ref, shift_ref):
    span = CONV_TILE + CONV_HALO - SUBLANES
    for s in range(1, SUBLANES):
        shift_ref[s - 1, 0:span, :] = win_ref[s:s + span, :]


def _conv_taps(cb, r0, win_ref, shift_ref, cbuf_ref, wdw_ref, bdw_ref):
    def group(m):
        return slice(r0 + m * SUBLANES, r0 + (m + 1) * SUBLANES)

    first = CONV_HALO - (CONV_WIDTH - 1)
    cols = pl.ds(pl.multiple_of(cb * LANES, LANES), LANES)
    bias = jnp.broadcast_to(bdw_ref[:, cols], (SUBLANES, LANES))
    accs = [bias] * CONV_ROWS_PER_ITER
    for shift in range(SUBLANES):
        js = [j for j in range(CONV_WIDTH) if (first + j) % SUBLANES == shift]
        tiles = [(first + j) // SUBLANES for j in js]
        taps = [jnp.broadcast_to(wdw_ref[j:j + 1, cols], (SUBLANES, LANES)) for j in js]
        wins = {}
        for m in range(min(tiles), max(tiles) + CONV_ROWS_PER_ITER):
            wins[m] = (win_ref[group(m), cols] if shift == 0
                       else shift_ref[shift - 1, group(m), cols])
        for c in range(CONV_ROWS_PER_ITER):
            for tap, m in zip(taps, tiles):
                accs[c] = accs[c] + tap * wins[m + c]
    for c in range(CONV_ROWS_PER_ITER):
        cbuf_ref[group(c), cols] = accs[c]


def _proj_conv_kernel(x_ref, g_ref, w_ref, ma_ref, mg_ref, wdw_ref, bdw_ref, lng_ref, lnb_ref,
                      wpw_ref, proj_ref, yconv_ref, u_ref, hbuf_ref, win_ref, shift_ref,
                      cbuf_ref, *, tiles_per_seq):
    i, j = pl.program_id(0), pl.program_id(1)
    tm = x_ref.shape[0]

    @pl.when(j == 0)
    def _():
        x = x_ref[...]
        ms = jnp.mean(x * x, axis=-1, keepdims=True)
        u = (x * lax.rsqrt(ms + EPS) * g_ref[...]).astype(BF16)
        u_ref[...] = u

        @pl.when(i % tiles_per_seq == 0)
        def _():
            hbuf_ref[0:CONV_HALO - N_META, :] = jnp.zeros((CONV_HALO - N_META, CONV_DIM), F32)
            ma = ma_ref[...].astype(F32)
            mg = mg_ref[...].astype(F32)
            hbuf_ref[CONV_HALO - N_META:CONV_HALO, :] = ma * _sigmoid(mg)

        @pl.when(i % tiles_per_seq != 0)
        def _():
            hbuf_ref[0:CONV_HALO, :] = hbuf_ref[tm:tm + CONV_HALO, :]

        glu = _dot(u, w_ref[...])
        hbuf_ref[CONV_HALO:CONV_HALO + tm, :] = glu[:, :CONV_DIM] * _sigmoid(glu[:, CONV_DIM:])

    @pl.when(j > 0)
    def _():
        base = pl.multiple_of((j - 1) * CONV_TILE, CONV_TILE)
        win_ref[...] = hbuf_ref[pl.ds(base, CONV_HALO + CONV_TILE), :]
        _conv_shifts(win_ref, shift_ref)
        proj_ref[0] = _dot(u_ref[...], w_ref[...]).astype(proj_ref.dtype)

        def block_body(cb, carry):
            for r0 in range(0, CONV_TILE, SUBLANES * CONV_ROWS_PER_ITER):
                _conv_taps(cb, r0, win_ref, shift_ref, cbuf_ref, wdw_ref, bdw_ref)
            return carry

        lax.fori_loop(0, CONV_DIM // LANES, block_body, 0)
        c = cbuf_ref[...]
        mu = jnp.mean(c, axis=-1, keepdims=True)
        cc = c - mu
        var = jnp.mean(cc * cc, axis=-1, keepdims=True)
        y = cc * lax.rsqrt(var + EPS) * lng_ref[...] + lnb_ref[...]
        y = y * _sigmoid(y)
        yconv_ref[pl.ds(base, CONV_TILE), :] = _dot(y.astype(BF16),
                                                    wpw_ref[...]).astype(yconv_ref.dtype)


def _proj_conv(x2d, g, w_bf, proj_meta, wdw, bdw, lng, lnb, wpw_bf, seq, tm):
    t, d = x2d.shape
    n_steps = D_IN // IN_PROJ_COLS
    assert IN_PROJ_COLS == 2 * CONV_DIM and (n_steps - 1) * CONV_TILE == tm and seq % tm == 0
    const = lambda i, j: (0, 0)
    return pl.pallas_call(
        functools.partial(_proj_conv_kernel, tiles_per_seq=seq // tm),
        grid=(t // tm, n_steps),
        in_specs=[
            pl.BlockSpec((tm, d), lambda i, j: (i, 0)),
            pl.BlockSpec((1, d), const),
            pl.BlockSpec((d, IN_PROJ_COLS), lambda i, j: (0, j)),
            pl.BlockSpec((N_META, CONV_DIM), lambda i, j: (0, 0)),
            pl.BlockSpec((N_META, CONV_DIM), lambda i, j: (0, 1)),
            pl.BlockSpec((CONV_WIDTH, CONV_DIM), const),
            pl.BlockSpec((1, CONV_DIM), const),
            pl.BlockSpec((1, CONV_DIM), const),
            pl.BlockSpec((1, CONV_DIM), const),
            pl.BlockSpec((CONV_DIM, D_MODEL), const),
        ],
        out_specs=[
            pl.BlockSpec((1, tm, IN_PROJ_COLS), lambda i, j: (jnp.maximum(j - 1, 0), i, 0)),
            pl.BlockSpec((tm, D_MODEL), lambda i, j: (i, 0)),
        ],
        out_shape=[jax.ShapeDtypeStruct((n_steps - 1, t, IN_PROJ_COLS), BF16),
                   jax.ShapeDtypeStruct((t, D_MODEL), BF16)],
        scratch_shapes=[pltpu.VMEM((tm, d), BF16),
                        pltpu.VMEM((CONV_HALO + tm, CONV_DIM), F32),
                        pltpu.VMEM((CONV_HALO + CONV_TILE, CONV_DIM), F32),
                        pltpu.VMEM((SUBLANES - 1, CONV_HALO + CONV_TILE, CONV_DIM), F32),
                        pltpu.VMEM((CONV_TILE, CONV_DIM), F32)],
        compiler_params=pltpu.CompilerParams(
            dimension_semantics=("arbitrary", "arbitrary"),
            vmem_limit_bytes=PROJ_VMEM_LIMIT),
        name="proj_conv",
    )(x2d, g, w_bf, proj_meta, proj_meta, wdw, bdw, lng, lnb, wpw_bf)


def _rotary(x, cos, sin):
    half = x.shape[-1] // 2
    x1, x2 = x[:, :half], x[:, half:]
    return jnp.concatenate([x1 * cos - x2 * sin, x2 * cos + x1 * sin], axis=-1)


def _ret_kernel(qk_ref, v_ref, gret_ref, gm_ref, yconv_ref, x_ref, cos_ref, sin_ref,
                mk_ref, mv_ref, mcos_ref, msin_ref, dmat_ref, qdec_ref, kdec_ref, mkdec_ref,
                bdec_ref, gn_ref, wo_ref, wout_ref, eg_ref, eu_ref, ed_ref,
                o_ref, egu_ref, edn_ref, state_ref):
    i = pl.program_id(1)
    k_scale = RET_QK_DIM ** -0.5

    egu_ref[:, :D_EXPERT] = eg_ref[...].astype(BF16)
    egu_ref[:, D_EXPERT:] = eu_ref[...].astype(BF16)
    edn_ref[...] = ed_ref[...].astype(BF16)

    @pl.when(i == 0)
    def _():
        mcos, msin = mcos_ref[...], msin_ref[...]
        for h in range(RET_HEADS):
            mk = mk_ref[:, h * RET_QK_DIM:(h + 1) * RET_QK_DIM].astype(F32)
            mk = _rotary(mk, mcos, msin) * k_scale * mkdec_ref[h]
            mv = mv_ref[:, h * RET_V_DIM:(h + 1) * RET_V_DIM]
            state_ref[h] = lax.dot_general(mk.astype(BF16), mv, (((0,), (0,)), ((), ())),
                                           preferred_element_type=F32)

    cos, sin = cos_ref[...], sin_ref[...]
    y_ret = jnp.zeros((RET_BLOCK, D_MODEL), F32)
    for h in range(RET_HEADS):
        qq = slice(h * RET_QK_DIM, (h + 1) * RET_QK_DIM)
        kk = slice((RET_HEADS + h) * RET_QK_DIM, (RET_HEADS + h + 1) * RET_QK_DIM)
        vv = slice(h * RET_V_DIM, (h + 1) * RET_V_DIM)
        q = _rotary(qk_ref[0, :, qq].astype(F32), cos, sin)
        k = _rotary(qk_ref[0, :, kk].astype(F32), cos, sin) * k_scale
        v = v_ref[0, :, vv]
        q_bf = q.astype(BF16)
        scores = lax.dot_general(q_bf, k.astype(BF16), (((1,), (1,)), ((), ())),
                                 preferred_element_type=F32) * dmat_ref[h]
        state = state_ref[h]
        o = _dot(scores.astype(BF16), v) + _dot(q_bf, state.astype(BF16)) * qdec_ref[h]
        state_ref[h] = state * bdec_ref[h] + lax.dot_general(
            (k * kdec_ref[h]).astype(BF16), v, (((0,), (0,)), ((), ())),
            preferred_element_type=F32)
        mu = jnp.mean(o, axis=-1, keepdims=True)
        oc = o - mu
        var = jnp.mean(oc * oc, axis=-1, keepdims=True)
        on = oc * lax.rsqrt(var + EPS) * gn_ref[:, vv]
        gr = gret_ref[0, :, vv]
        gated = gr * _sigmoid(gr) * on.astype(BF16)
        y_ret = y_ret + _dot(gated, wo_ref[vv, :])

    ga = gm_ref[0, :, :D_MODEL]
    gb = gm_ref[0, :, D_MODEL:]
    merged = _sigmoid(ga) * yconv_ref[...] + _sigmoid(gb) * y_ret.astype(BF16)
    o_ref[...] = x_ref[...] + _dot(merged, wout_ref[...])


def _retention(proj, proj_meta, y_conv, x2d, tables, gn, wo_bf, wout_bf, w_gate, w_up, w_down,
               batch, seq):
    t = proj.shape[1]
    nb = seq // RET_BLOCK
    n_steps = batch * nb
    gu_rows, dn_rows = N_EXPERTS * D_MODEL // n_steps, N_EXPERTS * D_EXPERT // n_steps
    assert gu_rows * n_steps == N_EXPERTS * D_MODEL and dn_rows * n_steps == N_EXPERTS * D_EXPERT
    assert gu_rows % (2 * SUBLANES) == 0 and dn_rows % (2 * SUBLANES) == 0
    hq = RET_HEADS * RET_QK_DIM
    hv = RET_HEADS * RET_V_DIM
    assert 2 * hq == hv == 2 * D_MODEL == proj.shape[2]
    slab = lambda s: pl.BlockSpec((1, RET_BLOCK, hv), lambda b, i: (s, row(b, i), 0))
    mk_col, mv_col = 2 * CONV_DIM // hq + 1, (2 * CONV_DIM + 2 * hq) // hv
    row = lambda b, i: b * nb + i
    const2 = lambda b, i: (0, 0)
    const3 = lambda b, i: (0, 0, 0)
    cos, sin, mcos, msin, dmat, qdec, kdec, mkdec, bdec = tables
    return pl.pallas_call(
        _ret_kernel,
        grid=(batch, nb),
        in_specs=[
            slab(0), slab(1), slab(2), slab(3),
            pl.BlockSpec((RET_BLOCK, D_MODEL), lambda b, i: (row(b, i), 0)),
            pl.BlockSpec((RET_BLOCK, D_MODEL), lambda b, i: (row(b, i), 0)),
            pl.BlockSpec((RET_BLOCK, RET_QK_DIM // 2), lambda b, i: (i, 0)),
            pl.BlockSpec((RET_BLOCK, RET_QK_DIM // 2), lambda b, i: (i, 0)),
            pl.BlockSpec((N_META, hq), lambda b, i: (0, mk_col)),
            pl.BlockSpec((N_META, hv), lambda b, i: (0, mv_col)),
            pl.BlockSpec((N_META, RET_QK_DIM // 2), const2),
            pl.BlockSpec((N_META, RET_QK_DIM // 2), const2),
            pl.BlockSpec((RET_HEADS, RET_BLOCK, RET_BLOCK), const3),
            pl.BlockSpec((RET_HEADS, RET_BLOCK, 1), const3),
            pl.BlockSpec((RET_HEADS, RET_BLOCK, 1), const3),
            pl.BlockSpec((RET_HEADS, N_META, 1), const3),
            pl.BlockSpec((RET_HEADS, 1, 1), const3),
            pl.BlockSpec((1, hv), const2),
            pl.BlockSpec((hv, D_MODEL), const2),
            pl.BlockSpec((D_MODEL, D_MODEL), const2),
            pl.BlockSpec((gu_rows, D_EXPERT), lambda b, i: (row(b, i), 0)),
            pl.BlockSpec((gu_rows, D_EXPERT), lambda b, i: (row(b, i), 0)),
            pl.BlockSpec((dn_rows, D_MODEL), lambda b, i: (row(b, i), 0)),
        ],
        out_specs=[
            pl.BlockSpec((RET_BLOCK, D_MODEL), lambda b, i: (row(b, i), 0)),
            pl.BlockSpec((gu_rows, 2 * D_EXPERT), lambda b, i: (row(b, i), 0)),
            pl.BlockSpec((dn_rows, D_MODEL), lambda b, i: (row(b, i), 0)),
        ],
        out_shape=[jax.ShapeDtypeStruct((t, D_MODEL), F32),
                   jax.ShapeDtypeStruct((N_EXPERTS * D_MODEL, 2 * D_EXPERT), BF16),
                   jax.ShapeDtypeStruct((N_EXPERTS * D_EXPERT, D_MODEL), BF16)],
        scratch_shapes=[pltpu.VMEM((RET_HEADS, RET_QK_DIM, RET_V_DIM), F32)],
        compiler_params=pltpu.CompilerParams(
            dimension_semantics=("parallel", "arbitrary"),
            vmem_limit_bytes=VMEM_LIMIT),
        name="retention_mix",
    )(proj, proj, proj, proj, y_conv, x2d, cos, sin, proj_meta, proj_meta, mcos, msin,
      dmat, qdec, kdec, mkdec, bdec, gn, wo_bf, wout_bf,
      w_gate.reshape(N_EXPERTS * D_MODEL, D_EXPERT), w_up.reshape(N_EXPERTS * D_MODEL, D_EXPERT),
      w_down.reshape(N_EXPERTS * D_EXPERT, D_MODEL))


def _retention_tables(seq):
    half = RET_QK_DIM // 2
    inv = ROPE_BASE ** (-jnp.arange(half, dtype=F32) / half)
    pos = jnp.arange(N_META + seq, dtype=F32)
    ang = pos[:, None] * inv[None, :]
    cos_all, sin_all = jnp.cos(ang), jnp.sin(ang)
    log_gamma = jnp.log(1.0 - 2.0 ** (-5.0 - jnp.arange(RET_HEADS, dtype=F32)))
    idx = jnp.arange(RET_BLOCK, dtype=F32)
    chunk = jnp.arange(RET_BLOCK, dtype=jnp.int32) // CHUNK
    visible = chunk[None, :] <= chunk[:, None]
    dmat = jnp.where(visible[None],
                     jnp.exp(log_gamma[:, None, None] * jnp.abs(idx[:, None] - idx[None, :])),
                     0.0)
    qdec = jnp.exp(log_gamma[:, None] * (idx + 1.0))[:, :, None]
    kdec = jnp.exp(log_gamma[:, None] * (RET_BLOCK - 1.0 - idx))[:, :, None]
    midx = jnp.arange(N_META, dtype=F32)
    mkdec = jnp.exp(log_gamma[:, None] * (N_META - 1.0 - midx))[:, :, None]
    bdec = jnp.exp(log_gamma * RET_BLOCK)[:, None, None]
    return (cos_all[N_META:], sin_all[N_META:], cos_all[:N_META], sin_all[:N_META],
            dmat, qdec, kdec, mkdec, bdec)


def _route(logits):
    lane = lax.broadcasted_iota(jnp.int32, logits.shape, 1)
    neg = jnp.float32(-jnp.inf)
    big = jnp.int32(ROUTER_LANES)

    def first_max(masked):
        val = jnp.max(masked, axis=-1, keepdims=True)
        idx = jnp.min(jnp.where(masked == val, lane, big), axis=-1, keepdims=True)
        return val, idx

    gmask = lane < N_GROUPS
    gmax, gidx = first_max(jnp.where(gmask, logits, neg))
    denom = jnp.sum(jnp.where(gmask, jnp.exp(logits - gmax), 0.0), axis=-1, keepdims=True)
    p_group = 1.0 / denom
    assert EXPERTS_PER_GROUP & (EXPERTS_PER_GROUP - 1) == 0
    shift = EXPERTS_PER_GROUP.bit_length() - 1
    lane_group = (lane - N_GROUPS) >> shift
    in_group = jnp.where(lane_group == gidx, logits, neg)
    v1, i1 = first_max(in_group)
    v2, i2 = first_max(jnp.where(lane == i1, neg, in_group))
    e2 = jnp.exp(v2 - v1)
    w1 = p_group / (1.0 + e2)
    w2 = p_group * e2 / (1.0 + e2)
    return jnp.where(lane == i1, w1, 0.0) + jnp.where(lane == i2, w2, 0.0), gidx


def _moe_kernel(h_ref, g_ref, wr_ref, br_ref, wgu_ref, wd_ref, gf_ref, o_ref,
                xs_ref, cs_ref, pos_ref, y_ref, seg_ref):
    grp = pl.program_id(1)
    tm = h_ref.shape[0]

    @pl.when(grp == 0)
    def _():
        h = h_ref[...]
        ms = jnp.mean(h * h, axis=-1, keepdims=True)
        u = h * lax.rsqrt(ms + EPS) * g_ref[...]
        u_hi = u.astype(BF16)
        u_lo = (u - u_hi.astype(F32)).astype(BF16)
        hi_part = _dot(u_hi, wr_ref[...])
        lo_part = _dot(u_lo, wr_ref[:, :ROUTER_LANES])
        logits = (hi_part[:, :ROUTER_LANES] + (hi_part[:, ROUTER_LANES:] + lo_part)
                  + br_ref[...])
        comb, gidx = _route(logits)

        lane = lax.broadcasted_iota(jnp.int32, (tm, ROUTER_LANES), 1)
        onehot = jnp.where(lane == gidx, 1.0, 0.0)
        counts = jnp.sum(onehot, axis=0, keepdims=True)
        row = lax.broadcasted_iota(jnp.int32, (tm, tm), 0)
        col = lax.broadcasted_iota(jnp.int32, (tm, tm), 1)
        earlier = jnp.where(col < row, 1.0, 0.0).astype(BF16)
        prefix = _dot(earlier, onehot.astype(BF16))
        lane_row = lax.broadcasted_iota(jnp.int32, (1, ROUTER_LANES), 1)
        start = jnp.int32(0)
        starts = jnp.zeros((1, ROUTER_LANES), F32)
        for gg in range(N_GROUPS):
            seg_ref[gg] = start
            starts = starts + jnp.where(lane_row == gg, start.astype(F32), 0.0)
            start = start + jnp.sum(jnp.where(lane_row == gg, counts, 0.0)).astype(jnp.int32)
        seg_ref[N_GROUPS] = start
        pos = jnp.sum(onehot * (prefix + starts), axis=-1, keepdims=True)
        pos_lanes = jnp.broadcast_to(pos, (tm, ROUTER_LANES))
        pos_ref[...] = pos_lanes
        pos_row = pos_lanes.T[0:1, :].astype(jnp.int32)
        perm = jnp.where(row == pos_row, 1.0, 0.0).astype(BF16)
        c_hi = comb.astype(BF16)
        c_lo = (comb - c_hi.astype(F32)).astype(BF16)
        moved = _dot(perm, jnp.concatenate([u_hi, c_hi, c_lo], axis=-1))
        xs_ref[...] = moved[:, :D_MODEL].astype(BF16)
        cs_ref[...] = (moved[:, D_MODEL:D_MODEL + ROUTER_LANES]
                       + moved[:, D_MODEL + ROUTER_LANES:])
        y_ref[...] = jnp.zeros_like(y_ref)

    seg_lo, seg_hi = seg_ref[grp], seg_ref[grp + 1]
    blk_lo = seg_lo // MOE_BLOCK
    blk_hi = jnp.where(seg_hi > seg_lo, (seg_hi + MOE_BLOCK - 1) // MOE_BLOCK, blk_lo)

    def experts(blk, n_blocks):
        n_rows = n_blocks * MOE_BLOCK
        rows = pl.ds(pl.multiple_of(blk * MOE_BLOCK, MOE_BLOCK), n_rows)
        xb = xs_ref[rows, :]
        cb = cs_ref[rows, :]
        lane = lax.broadcasted_iota(jnp.int32, cb.shape, 1)
        acc = jnp.zeros((n_rows, D_MODEL), F32)
        for e in range(EXPERTS_PER_GROUP):
            expert_lane = N_GROUPS + grp * EXPERTS_PER_GROUP + e
            w_e = jnp.sum(jnp.where(lane == expert_lane, cb, 0.0), axis=-1, keepdims=True)
            gu = _dot(xb, wgu_ref[e])
            gate, up = gu[:, :D_EXPERT], gu[:, D_EXPERT:]
            hid = gate * _sigmoid(gate) * up * w_e
            acc = acc + _dot(hid.astype(BF16), wd_ref[e])
        y_ref[rows, :] += acc

    n_blk = blk_hi - blk_lo

    def pair_body(p, carry):
        experts(blk_lo + 2 * p, 2)
        return carry

    lax.fori_loop(0, n_blk // 2, pair_body, 0)

    @pl.when(n_blk % 2 == 1)
    def _():
        experts(blk_hi - 1, 1)

    @pl.when(grp == N_GROUPS - 1)
    def _():
        pos = pos_ref[:, 0:1].astype(jnp.int32)
        col = lax.broadcasted_iota(jnp.int32, (tm, tm), 1)
        unperm = jnp.where(col == pos, 1.0, 0.0).astype(BF16)
        h2 = h_ref[...] + _dot(unperm, y_ref[...].astype(BF16))
        ms = jnp.mean(h2 * h2, axis=-1, keepdims=True)
        o_ref[...] = h2 * lax.rsqrt(ms + EPS) * gf_ref[...]


def _moe(h1, g, wr2, br, wgu_bf, wd_bf, gf, tm):
    t = h1.shape[0]
    const2 = lambda i, e: (0, 0)
    return pl.pallas_call(
        _moe_kernel,
        grid=(t // tm, N_GROUPS),
        in_specs=[
            pl.BlockSpec((tm, D_MODEL), lambda i, e: (i, 0)),
            pl.BlockSpec((1, D_MODEL), const2),
            pl.BlockSpec((D_MODEL, 2 * ROUTER_LANES), const2),
            pl.BlockSpec((1, ROUTER_LANES), const2),
            pl.BlockSpec((EXPERTS_PER_GROUP, D_MODEL, 2 * D_EXPERT), lambda i, e: (e, 0, 0)),
            pl.BlockSpec((EXPERTS_PER_GROUP, D_EXPERT, D_MODEL), lambda i, e: (e, 0, 0)),
            pl.BlockSpec((1, D_MODEL), const2),
        ],
        out_specs=pl.BlockSpec((tm, D_MODEL), lambda i, e: (i, 0)),
        out_shape=jax.ShapeDtypeStruct((t, D_MODEL), F32),
        scratch_shapes=[pltpu.VMEM((tm, D_MODEL), BF16),
                        pltpu.VMEM((tm, ROUTER_LANES), F32),
                        pltpu.VMEM((tm, ROUTER_LANES), F32),
                        pltpu.VMEM((tm, D_MODEL), F32),
                        pltpu.SMEM((SUBLANES,), jnp.int32)],
        compiler_params=pltpu.CompilerParams(
            dimension_semantics=("parallel", "arbitrary"),
            vmem_limit_bytes=MOE_VMEM_LIMIT),
        name="hier_moe",
    )(h1, g, wr2, br, wgu_bf, wd_bf, gf)


def _split_bf16(w):
    hi = w.astype(BF16)
    lo = (w - hi.astype(F32)).astype(BF16)
    return jnp.concatenate([hi, lo], axis=-1)


def kernel(x, meta_tokens, norm_mix_g, w_in, conv_dw_w, conv_dw_b, conv_ln_g, conv_ln_b,
           conv_pw_w, ret_gn_g, ret_w_o, w_out, norm_ffn_g, w_group_router, b_group_router,
           w_expert_router, b_expert_router, w_expert_gate, w_expert_up, w_expert_down,
           norm_final_g):
    batch, seq, d = x.shape
    assert d == D_MODEL and seq % RET_BLOCK == 0 and w_in.shape[0] == 1
    t = batch * seq
    x2d = x.reshape(t, d)
    row = lambda v: v.reshape(1, -1)

    proj_meta, w_in_bf = _meta_proj(meta_tokens, row(norm_mix_g[0]), w_in[0], IN_PROJ_COLS)
    proj, y_conv = _proj_conv(x2d, row(norm_mix_g[0]), w_in_bf, proj_meta, conv_dw_w[0],
                              row(conv_dw_b[0]), row(conv_ln_g[0]), row(conv_ln_b[0]),
                              conv_pw_w[0].astype(BF16), seq, PROJ_TILE)
    h1, w_gu, w_dn = _retention(proj, proj_meta, y_conv, x2d, _retention_tables(seq),
                                row(ret_gn_g[0]), ret_w_o[0].astype(BF16), w_out[0].astype(BF16),
                                w_expert_gate[0], w_expert_up[0], w_expert_down[0], batch, seq)
    w_gu = w_gu.reshape(N_EXPERTS, D_MODEL, 2 * D_EXPERT)
    w_dn = w_dn.reshape(N_EXPERTS, D_EXPERT, D_MODEL)

    w_router = jnp.concatenate([w_group_router[0], w_expert_router[0]], axis=1)
    w_router = jnp.pad(w_router, ((0, 0), (0, ROUTER_LANES - w_router.shape[1])))
    b_router = jnp.concatenate([b_group_router[0], b_expert_router[0]])
    b_router = jnp.pad(b_router, (0, ROUTER_LANES - b_router.shape[0])).reshape(1, -1)
    out = _moe(h1, row(norm_ffn_g[0]), _split_bf16(w_router), b_router, w_gu, w_dn,
               row(norm_final_g), min(1024, t))
    return out.reshape(batch, seq, d)
```

```python
import functools
import math

import jax
import jax.numpy as jnp
from jax import lax
from jax.experimental import pallas as pl
from jax.experimental.pallas import tpu as pltpu

D_MODEL = 1024
CHUNK = 64
N_META = 16
CONV_DIM = 1024
CONV_WIDTH = 31
RET_HEADS = 4
RET_QK_DIM = 256
RET_V_DIM = 512
ROPE_BASE = 10000.0
N_GROUPS = 4
EXPERTS_PER_GROUP = 4
N_EXPERTS = N_GROUPS * EXPERTS_PER_GROUP
D_EXPERT = 512
EPS = 1e-6
D_IN = 2 * CONV_DIM + 2 * RET_HEADS * RET_QK_DIM + 2 * RET_HEADS * RET_V_DIM + 2 * D_MODEL

LANES = 128
SUBLANES = 8
CONV_HALO = 32
IN_PROJ_COLS = 2048
PROJ_TILE = 1024
CONV_TILE = 256
CONV_ROWS_PER_ITER = 4
RET_BLOCK = 256
RET_STEP = 512
ROUTER_LANES = LANES
MOE_BLOCK = 128
VMEM_LIMIT = 48 * 1024 * 1024
PROJ_VMEM_LIMIT = 56 * 1024 * 1024
MOE_VMEM_LIMIT = 60 * 1024 * 1024

F32 = jnp.float32
BF16 = jnp.bfloat16


def _sigmoid(x):
    return 1.0 / (1.0 + jnp.exp(-x))


def _dot(a, b):
    return jnp.dot(a, b, preferred_element_type=F32)


def _meta_proj_kernel(x_ref, g_ref, w_ref, o_ref, wbf_ref, u_ref):
    @pl.when(pl.program_id(0) == 0)
    def _():
        x = x_ref[...]
        ms = jnp.mean(x * x, axis=-1, keepdims=True)
        u_ref[...] = (x * lax.rsqrt(ms + EPS) * g_ref[...]).astype(BF16)

    w = w_ref[...].astype(BF16)
    wbf_ref[...] = w
    o_ref[...] = _dot(u_ref[...], w).astype(o_ref.dtype)


def _meta_proj(meta, g, w, tn):
    t, d = meta.shape
    n = w.shape[1]
    return pl.pallas_call(
        _meta_proj_kernel,
        grid=(n // tn,),
        in_specs=[
            pl.BlockSpec((t, d), lambda j: (0, 0)),
            pl.BlockSpec((1, d), lambda j: (0, 0)),
            pl.BlockSpec((d, tn), lambda j: (0, j)),
        ],
        out_specs=[pl.BlockSpec((t, tn), lambda j: (0, j)),
                   pl.BlockSpec((d, tn), lambda j: (0, j))],
        out_shape=[jax.ShapeDtypeStruct((t, n), BF16),
                   jax.ShapeDtypeStruct((d, n), BF16)],
        scratch_shapes=[pltpu.VMEM((t, d), BF16)],
        compiler_params=pltpu.CompilerParams(
            dimension_semantics=("arbitrary",),
            vmem_limit_bytes=VMEM_LIMIT),
        name="meta_proj",
    )(meta, g, w)


def _conv_shifts(win_ref, shift_ref):
    span = CONV_TILE + CONV_HALO - SUBLANES
    for s in range(1, SUBLANES):
        shift_ref[s - 1, 0:span, :] = win_ref[s:s + span, :]


def _conv_taps(cb, r0, win_ref, shift_ref, cbuf_ref, wdw_ref, bdw_ref):
    def group(m):
        return slice(r0 + m * SUBLANES, r0 + (m + 1) * SUBLANES)

    first = CONV_HALO - (CONV_WIDTH - 1)
    cols = pl.ds(pl.multiple_of(cb * LANES, LANES), LANES)
    bias = jnp.broadcast_to(bdw_ref[:, cols], (SUBLANES, LANES))
    accs = [bias] * CONV_ROWS_PER_ITER
    for shift in range(SUBLANES):
        js = [j for j in range(CONV_WIDTH) if (first + j) % SUBLANES == shift]
        tiles = [(first + j) // SUBLANES for j in js]
        taps = [jnp.broadcast_to(wdw_ref[j:j + 1, cols], (SUBLANES, LANES)) for j in js]
        wins = {}
        for m in range(min(tiles), max(tiles) + CONV_ROWS_PER_ITER):
            wins[m] = (win_ref[group(m), cols] if shift == 0
                       else shift_ref[shift - 1, group(m), cols])
        for c in range(CONV_ROWS_PER_ITER):
            for tap, m in zip(taps, tiles):
                accs[c] = accs[c] + tap * wins[m + c]
    for c in range(CONV_ROWS_PER_ITER):
        cbuf_ref[group(c), cols] = accs[c]


def _proj_conv_kernel(x_ref, g_ref, w_ref, ma_ref, mg_ref, wdw_ref, bdw_ref, lng_ref, lnb_ref,
                      wpw_ref, proj_ref, yconv_ref, u_ref, hbuf_ref, win_ref, shift_ref,
                      cbuf_ref, *, tiles_per_seq):
    i, j = pl.program_id(0), pl.program_id(1)
    tm = x_ref.shape[0]

    @pl.when(j == 0)
    def _():
        x = x_ref[...]
        ms = jnp.mean(x * x, axis=-1, keepdims=True)
        u = (x * lax.rsqrt(ms + EPS) * g_ref[...]).astype(BF16)
        u_ref[...] = u

        @pl.when(i % tiles_per_seq == 0)
        def _():
            hbuf_ref[0:CONV_HALO - N_META, :] = jnp.zeros((CONV_HALO - N_META, CONV_DIM), F32)
            ma = ma_ref[...].astype(F32)
            mg = mg_ref[...].astype(F32)
            hbuf_ref[CONV_HALO - N_META:CONV_HALO, :] = ma * _sigmoid(mg)

        @pl.when(i % tiles_per_seq != 0)
        def _():
            hbuf_ref[0:CONV_HALO, :] = hbuf_ref[tm:tm + CONV_HALO, :]

        glu = _dot(u, w_ref[...])
        hbuf_ref[CONV_HALO:CONV_HALO + tm, :] = glu[:, :CONV_DIM] * _sigmoid(glu[:, CONV_DIM:])

    @pl.when(j > 0)
    def _():
        base = pl.multiple_of((j - 1) * CONV_TILE, CONV_TILE)
        win_ref[...] = hbuf_ref[pl.ds(base, CONV_HALO + CONV_TILE), :]
        _conv_shifts(win_ref, shift_ref)
        proj_ref[0] = _dot(u_ref[...], w_ref[...]).astype(proj_ref.dtype)

        def block_body(cb, carry):
            for r0 in range(0, CONV_TILE, SUBLANES * CONV_ROWS_PER_ITER):
                _conv_taps(cb, r0, win_ref, shift_ref, cbuf_ref, wdw_ref, bdw_ref)
            return carry

        lax.fori_loop(0, CONV_DIM // LANES, block_body, 0)
        c = cbuf_ref[...]
        mu = jnp.mean(c, axis=-1, keepdims=True)
        cc = c - mu
        var = jnp.mean(cc * cc, axis=-1, keepdims=True)
        y = cc * lax.rsqrt(var + EPS) * lng_ref[...] + lnb_ref[...]
        y = y * _sigmoid(y)
        yconv_ref[pl.ds(base, CONV_TILE), :] = _dot(y.astype(BF16),
                                                    wpw_ref[...]).astype(yconv_ref.dtype)


def _proj_conv(x2d, g, w_bf, proj_meta, wdw, bdw, lng, lnb, wpw_bf, seq, tm):
    t, d = x2d.shape
    n_steps = D_IN // IN_PROJ_COLS
    assert IN_PROJ_COLS == 2 * CONV_DIM and (n_steps - 1) * CONV_TILE == tm and seq % tm == 0
    const = lambda i, j: (0, 0)
    return pl.pallas_call(
        functools.partial(_proj_conv_kernel, tiles_per_seq=seq // tm),
        grid=(t // tm, n_steps),
        in_specs=[
            pl.BlockSpec((tm, d), lambda i, j: (i, 0)),
            pl.BlockSpec((1, d), const),
            pl.BlockSpec((d, IN_PROJ_COLS), lambda i, j: (0, j)),
            pl.BlockSpec((N_META, CONV_DIM), lambda i, j: (0, 0)),
            pl.BlockSpec((N_META, CONV_DIM), lambda i, j: (0, 1)),
            pl.BlockSpec((CONV_WIDTH, CONV_DIM), const),
            pl.BlockSpec((1, CONV_DIM), const),
            pl.BlockSpec((1, CONV_DIM), const),
            pl.BlockSpec((1, CONV_DIM), const),
            pl.BlockSpec((CONV_DIM, D_MODEL), const),
        ],
        out_specs=[
            pl.BlockSpec((1, tm, IN_PROJ_COLS), lambda i, j: (jnp.maximum(j - 1, 0), i, 0)),
            pl.BlockSpec((tm, D_MODEL), lambda i, j: (i, 0)),
        ],
        out_shape=[jax.ShapeDtypeStruct((n_steps - 1, t, IN_PROJ_COLS), BF16),
                   jax.ShapeDtypeStruct((t, D_MODEL), BF16)],
        scratch_shapes=[pltpu.VMEM((tm, d), BF16),
                        pltpu.VMEM((CONV_HALO + tm, CONV_DIM), F32),
                        pltpu.VMEM((CONV_HALO + CONV_TILE, CONV_DIM), F32),
                        pltpu.VMEM((SUBLANES - 1, CONV_HALO + CONV_TILE, CONV_DIM), F32),
                        pltpu.VMEM((CONV_TILE, CONV_DIM), F32)],
        compiler_params=pltpu.CompilerParams(
            dimension_semantics=("arbitrary", "arbitrary"),
            vmem_limit_bytes=PROJ_VMEM_LIMIT),
        name="proj_conv",
    )(x2d, g, w_bf, proj_meta, proj_meta, wdw, bdw, lng, lnb, wpw_bf)


def _rotary(x, cos, sin):
    half = x.shape[-1] // 2
    x1, x2 = x[:, :half], x[:, half:]
    return jnp.concatenate([x1 * cos - x2 * sin, x2 * cos + x1 * sin], axis=-1)


def _ret_kernel(qk_ref, v_ref, gret_ref, gm_ref, yconv_ref, x_ref, cos_ref, sin_ref,
                mk_ref, mv_ref, mcos_ref, msin_ref, dmat_ref, qdec_ref, kdec_ref, mkdec_ref,
                bdec_ref, gn_ref, wo_ref, wout_ref, eg_ref, eu_ref, ed_ref,
                o_ref, egu_ref, edn_ref, state_ref, gated_ref):
    i = pl.program_id(1)
    k_scale = RET_QK_DIM ** -0.5

    egu_ref[:, :D_EXPERT] = eg_ref[...].astype(BF16)
    egu_ref[:, D_EXPERT:] = eu_ref[...].astype(BF16)
    edn_ref[...] = ed_ref[...].astype(BF16)

    @pl.when(i == 0)
    def _():
        mcos, msin = mcos_ref[...], msin_ref[...]
        for h in range(RET_HEADS):
            mk = mk_ref[:, h * RET_QK_DIM:(h + 1) * RET_QK_DIM].astype(F32)
            mk = _rotary(mk, mcos, msin) * k_scale * mkdec_ref[h]
            mv = mv_ref[:, h * RET_V_DIM:(h + 1) * RET_V_DIM]
            state_ref[h] = lax.dot_general(mk.astype(BF16), mv, (((0,), (0,)), ((), ())),
                                           preferred_element_type=F32)

    for sb in range(RET_STEP // RET_BLOCK):
        rows = slice(sb * RET_BLOCK, (sb + 1) * RET_BLOCK)
        cos, sin = cos_ref[rows, :], sin_ref[rows, :]
        for h in range(RET_HEADS):
            qq = slice(h * RET_QK_DIM, (h + 1) * RET_QK_DIM)
            kk = slice((RET_HEADS + h) * RET_QK_DIM, (RET_HEADS + h + 1) * RET_QK_DIM)
            vv = slice(h * RET_V_DIM, (h + 1) * RET_V_DIM)
            q = _rotary(qk_ref[0, rows, qq].astype(F32), cos, sin)
            k = _rotary(qk_ref[0, rows, kk].astype(F32), cos, sin) * k_scale
            v = v_ref[0, rows, vv]
            q_bf = q.astype(BF16)
            scores = lax.dot_general(q_bf, k.astype(BF16), (((1,), (1,)), ((), ())),
                                     preferred_element_type=F32) * dmat_ref[h]
            state = state_ref[h]
            o = _dot(scores.astype(BF16), v) + _dot(q_bf, state.astype(BF16)) * qdec_ref[h]
            state_ref[h] = state * bdec_ref[h] + lax.dot_general(
                (k * kdec_ref[h]).astype(BF16), v, (((0,), (0,)), ((), ())),
                preferred_element_type=F32)
            mu = jnp.mean(o, axis=-1, keepdims=True)
            oc = o - mu
            var = jnp.mean(oc * oc, axis=-1, keepdims=True)
            on = oc * lax.rsqrt(var + EPS) * gn_ref[:, vv]
            gr = gret_ref[0, rows, vv]
            gated_ref[rows, vv] = gr * _sigmoid(gr) * on.astype(BF16)

    y_ret = _dot(gated_ref[...], wo_ref[...])
    ga = gm_ref[0, :, :D_MODEL]
    gb = gm_ref[0, :, D_MODEL:]
    merged = _sigmoid(ga) * yconv_ref[...] + _sigmoid(gb) * y_ret.astype(BF16)
    o_ref[...] = x_ref[...] + _dot(merged, wout_ref[...])


def _retention(proj, proj_meta, y_conv, x2d, tables, gn, wo_bf, wout_bf, w_gate, w_up, w_down,
               batch, seq):
    t = proj.shape[1]
    nb = seq // RET_STEP
    n_steps = batch * nb
    gu_rows, dn_rows = N_EXPERTS * D_MODEL // n_steps, N_EXPERTS * D_EXPERT // n_steps
    assert gu_rows * n_steps == N_EXPERTS * D_MODEL and dn_rows * n_steps == N_EXPERTS * D_EXPERT
    assert gu_rows % (2 * SUBLANES) == 0 and dn_rows % (2 * SUBLANES) == 0
    hq = RET_HEADS * RET_QK_DIM
    hv = RET_HEADS * RET_V_DIM
    assert 2 * hq == hv == 2 * D_MODEL == proj.shape[2]
    slab = lambda s: pl.BlockSpec((1, RET_STEP, hv), lambda b, i: (s, row(b, i), 0))
    mk_col, mv_col = 2 * CONV_DIM // hq + 1, (2 * CONV_DIM + 2 * hq) // hv
    row = lambda b, i: b * nb + i
    const2 = lambda b, i: (0, 0)
    const3 = lambda b, i: (0, 0, 0)
    cos, sin, mcos, msin, dmat, qdec, kdec, mkdec, bdec = tables
    return pl.pallas_call(
        _ret_kernel,
        grid=(batch, nb),
        in_specs=[
            slab(0), slab(1), slab(2), slab(3),
            pl.BlockSpec((RET_STEP, D_MODEL), lambda b, i: (row(b, i), 0)),
            pl.BlockSpec((RET_STEP, D_MODEL), lambda b, i: (row(b, i), 0)),
            pl.BlockSpec((RET_STEP, RET_QK_DIM // 2), lambda b, i: (i, 0)),
            pl.BlockSpec((RET_STEP, RET_QK_DIM // 2), lambda b, i: (i, 0)),
            pl.BlockSpec((N_META, hq), lambda b, i: (0, mk_col)),
            pl.BlockSpec((N_META, hv), lambda b, i: (0, mv_col)),
            pl.BlockSpec((N_META, RET_QK_DIM // 2), const2),
            pl.BlockSpec((N_META, RET_QK_DIM // 2), const2),
            pl.BlockSpec((RET_HEADS, RET_BLOCK, RET_BLOCK), const3),
            pl.BlockSpec((RET_HEADS, RET_BLOCK, 1), const3),
            pl.BlockSpec((RET_HEADS, RET_BLOCK, 1), const3),
            pl.BlockSpec((RET_HEADS, N_META, 1), const3),
            pl.BlockSpec((RET_HEADS, 1, 1), const3),
            pl.BlockSpec((1, hv), const2),
            pl.BlockSpec((hv, D_MODEL), const2),
            pl.BlockSpec((D_MODEL, D_MODEL), const2),
            pl.BlockSpec((gu_rows, D_EXPERT), lambda b, i: (row(b, i), 0)),
            pl.BlockSpec((gu_rows, D_EXPERT), lambda b, i: (row(b, i), 0)),
            pl.BlockSpec((dn_rows, D_MODEL), lambda b, i: (row(b, i), 0)),
        ],
        out_specs=[
            pl.BlockSpec((RET_STEP, D_MODEL), lambda b, i: (row(b, i), 0)),
            pl.BlockSpec((gu_rows, 2 * D_EXPERT), lambda b, i: (row(b, i), 0)),
            pl.BlockSpec((dn_rows, D_MODEL), lambda b, i: (row(b, i), 0)),
        ],
        out_shape=[jax.ShapeDtypeStruct((t, D_MODEL), F32),
                   jax.ShapeDtypeStruct((N_EXPERTS * D_MODEL, 2 * D_EXPERT), BF16),
                   jax.ShapeDtypeStruct((N_EXPERTS * D_EXPERT, D_MODEL), BF16)],
        scratch_shapes=[pltpu.VMEM((RET_HEADS, RET_QK_DIM, RET_V_DIM), F32),
                        pltpu.VMEM((RET_STEP, hv), BF16)],
        compiler_params=pltpu.CompilerParams(
            dimension_semantics=("parallel", "arbitrary"),
            vmem_limit_bytes=PROJ_VMEM_LIMIT),
        name="retention_mix",
    )(proj, proj, proj, proj, y_conv, x2d, cos, sin, proj_meta, proj_meta, mcos, msin,
      dmat, qdec, kdec, mkdec, bdec, gn, wo_bf, wout_bf,
      w_gate.reshape(N_EXPERTS * D_MODEL, D_EXPERT), w_up.reshape(N_EXPERTS * D_MODEL, D_EXPERT),
      w_down.reshape(N_EXPERTS * D_EXPERT, D_MODEL))


def _retention_tables(seq):
    half = RET_QK_DIM // 2
    inv = ROPE_BASE ** (-jnp.arange(half, dtype=F32) / half)
    pos = jnp.arange(N_META + seq, dtype=F32)
    ang = pos[:, None] * inv[None, :]
    cos_all, sin_all = jnp.cos(ang), jnp.sin(ang)
    log_gamma = jnp.log(1.0 - 2.0 ** (-5.0 - jnp.arange(RET_HEADS, dtype=F32)))
    idx = jnp.arange(RET_BLOCK, dtype=F32)
    chunk = jnp.arange(RET_BLOCK, dtype=jnp.int32) // CHUNK
    visible = chunk[None, :] <= chunk[:, None]
    dmat = jnp.where(visible[None],
                     jnp.exp(log_gamma[:, None, None] * jnp.abs(idx[:, None] - idx[None, :])),
                     0.0)
    qdec = jnp.exp(log_gamma[:, None] * (idx + 1.0))[:, :, None]
    kdec = jnp.exp(log_gamma[:, None] * (RET_BLOCK - 1.0 - idx))[:, :, None]
    midx = jnp.arange(N_META, dtype=F32)
    mkdec = jnp.exp(log_gamma[:, None] * (N_META - 1.0 - midx))[:, :, None]
    bdec = jnp.exp(log_gamma * RET_BLOCK)[:, None, None]
    return (cos_all[N_META:], sin_all[N_META:], cos_all[:N_META], sin_all[:N_META],
            dmat, qdec, kdec, mkdec, bdec)


def _route(logits):
    lane = lax.broadcasted_iota(jnp.int32, logits.shape, 1)
    neg = jnp.float32(-jnp.inf)
    big = jnp.int32(ROUTER_LANES)

    def first_max(masked):
        val = jnp.max(masked, axis=-1, keepdims=True)
        idx = jnp.min(jnp.where(masked == val, lane, big), axis=-1, keepdims=True)
        return val, idx

    gmask = lane < N_GROUPS
    gmax, gidx = first_max(jnp.where(gmask, logits, neg))
    denom = jnp.sum(jnp.where(gmask, jnp.exp(logits - gmax), 0.0), axis=-1, keepdims=True)
    p_group = 1.0 / denom
    assert EXPERTS_PER_GROUP & (EXPERTS_PER_GROUP - 1) == 0
    shift = EXPERTS_PER_GROUP.bit_length() - 1
    lane_group = (lane - N_GROUPS) >> shift
    in_group = jnp.where(lane_group == gidx, logits, neg)
    v1, i1 = first_max(in_group)
    v2, i2 = first_max(jnp.where(lane == i1, neg, in_group))
    e2 = jnp.exp(v2 - v1)
    w1 = p_group / (1.0 + e2)
    w2 = p_group * e2 / (1.0 + e2)
    return jnp.where(lane == i1, w1, 0.0) + jnp.where(lane == i2, w2, 0.0), gidx


def _moe_kernel(h_ref, g_ref, wr_ref, br_ref, wgu_ref, wd_ref, gf_ref, o_ref,
                xs_ref, cs_ref, pos_ref, y_ref, seg_ref):
    grp = pl.program_id(1)
    tm = h_ref.shape[0]

    @pl.when(grp == 0)
    def _():
        h = h_ref[...]
        ms = jnp.mean(h * h, axis=-1, keepdims=True)
        u = h * lax.rsqrt(ms + EPS) * g_ref[...]
        u_hi = u.astype(BF16)
        u_lo = (u - u_hi.astype(F32)).astype(BF16)
        hi_part = _dot(u_hi, wr_ref[...])
        lo_part = _dot(u_lo, wr_ref[:, :ROUTER_LANES])
        logits = (hi_part[:, :ROUTER_LANES] + (hi_part[:, ROUTER_LANES:] + lo_part)
                  + br_ref[...])
        comb, gidx = _route(logits)

        lane = lax.broadcasted_iota(jnp.int32, (tm, ROUTER_LANES), 1)
        onehot = jnp.where(lane == gidx, 1.0, 0.0)
        counts = jnp.sum(onehot, axis=0, keepdims=True)
        row = lax.broadcasted_iota(jnp.int32, (tm, tm), 0)
        col = lax.broadcasted_iota(jnp.int32, (tm, tm), 1)
        earlier = jnp.where(col < row, 1.0, 0.0).astype(BF16)
        prefix = _dot(earlier, onehot.astype(BF16))
        lane_row = lax.broadcasted_iota(jnp.int32, (1, ROUTER_LANES), 1)
        start = jnp.int32(0)
        starts = jnp.zeros((1, ROUTER_LANES), F32)
        for gg in range(N_GROUPS):
            seg_ref[gg] = start
            starts = starts + jnp.where(lane_row == gg, start.astype(F32), 0.0)
            start = start + jnp.sum(jnp.where(lane_row == gg, counts, 0.0)).astype(jnp.int32)
        seg_ref[N_GROUPS] = start
        pos = jnp.sum(onehot * (prefix + starts), axis=-1, keepdims=True)
        pos_lanes = jnp.broadcast_to(pos, (tm, ROUTER_LANES))
        pos_ref[...] = pos_lanes
        pos_row = pos_lanes.T[0:1, :].astype(jnp.int32)
        perm = jnp.where(row == pos_row, 1.0, 0.0).astype(BF16)
        c_hi = comb.astype(BF16)
        c_lo = (comb - c_hi.astype(F32)).astype(BF16)
        moved = _dot(perm, jnp.concatenate([u_hi, c_hi, c_lo], axis=-1))
        xs_ref[...] = moved[:, :D_MODEL].astype(BF16)
        cs_ref[...] = (moved[:, D_MODEL:D_MODEL + ROUTER_LANES]
                       + moved[:, D_MODEL + ROUTER_LANES:])
        y_ref[...] = jnp.zeros_like(y_ref)

    seg_lo, seg_hi = seg_ref[grp], seg_ref[grp + 1]
    blk_lo = seg_lo // MOE_BLOCK
    blk_hi = jnp.where(seg_hi > seg_lo, (seg_hi + MOE_BLOCK - 1) // MOE_BLOCK, blk_lo)

    def experts(blk, n_blocks):
        n_rows = n_blocks * MOE_BLOCK
        rows = pl.ds(pl.multiple_of(blk * MOE_BLOCK, MOE_BLOCK), n_rows)
        xb = xs_ref[rows, :]
        cb = cs_ref[rows, :]
        lane = lax.broadcasted_iota(jnp.int32, cb.shape, 1)
        acc = jnp.zeros((n_rows, D_MODEL), F32)
        for e in range(EXPERTS_PER_GROUP):
            expert_lane = N_GROUPS + grp * EXPERTS_PER_GROUP + e
            w_e = jnp.sum(jnp.where(lane == expert_lane, cb, 0.0), axis=-1, keepdims=True)
            gu = _dot(xb, wgu_ref[e])
            gate, up = gu[:, :D_EXPERT], gu[:, D_EXPERT:]
            hid = gate * _sigmoid(gate) * up * w_e
            acc = acc + _dot(hid.astype(BF16), wd_ref[e])
        y_ref[rows, :] += acc

    n_blk = blk_hi - blk_lo

    def pair_body(p, carry):
        experts(blk_lo + 2 * p, 2)
        return carry

    lax.fori_loop(0, n_blk // 2, pair_body, 0)

    @pl.when(n_blk % 2 == 1)
    def _():
        experts(blk_hi - 1, 1)

    @pl.when(grp == N_GROUPS - 1)
    def _():
        pos = pos_ref[:, 0:1].astype(jnp.int32)
        col = lax.broadcasted_iota(jnp.int32, (tm, tm), 1)
        unperm = jnp.where(col == pos, 1.0, 0.0).astype(BF16)
        h2 = h_ref[...] + _dot(unperm, y_ref[...].astype(BF16))
        ms = jnp.mean(h2 * h2, axis=-1, keepdims=True)
        o_ref[...] = h2 * lax.rsqrt(ms + EPS) * gf_ref[...]


def _moe(h1, g, wr2, br, wgu_bf, wd_bf, gf, tm):
    t = h1.shape[0]
    const2 = lambda i, e: (0, 0)
    return pl.pallas_call(
        _moe_kernel,
        grid=(t // tm, N_GROUPS),
        in_specs=[
            pl.BlockSpec((tm, D_MODEL), lambda i, e: (i, 0)),
            pl.BlockSpec((1, D_MODEL), const2),
            pl.BlockSpec((D_MODEL, 2 * ROUTER_LANES), const2),
            pl.BlockSpec((1, ROUTER_LANES), const2),
            pl.BlockSpec((EXPERTS_PER_GROUP, D_MODEL, 2 * D_EXPERT), lambda i, e: (e, 0, 0)),
            pl.BlockSpec((EXPERTS_PER_GROUP, D_EXPERT, D_MODEL), lambda i, e: (e, 0, 0)),
            pl.BlockSpec((1, D_MODEL), const2),
        ],
        out_specs=pl.BlockSpec((tm, D_MODEL), lambda i, e: (i, 0)),
        out_shape=jax.ShapeDtypeStruct((t, D_MODEL), F32),
        scratch_shapes=[pltpu.VMEM((tm, D_MODEL), BF16),
                        pltpu.VMEM((tm, ROUTER_LANES), F32),
                        pltpu.VMEM((tm, ROUTER_LANES), F32),
                        pltpu.VMEM((tm, D_MODEL), F32),
                        pltpu.SMEM((SUBLANES,), jnp.int32)],
        compiler_params=pltpu.CompilerParams(
            dimension_semantics=("parallel", "arbitrary"),
            vmem_limit_bytes=MOE_VMEM_LIMIT),
        name="hier_moe",
    )(h1, g, wr2, br, wgu_bf, wd_bf, gf)


def _split_bf16(w):
    hi = w.astype(BF16)
    lo = (w - hi.astype(F32)).astype(BF16)
    return jnp.concatenate([hi, lo], axis=-1)


def kernel(x, meta_tokens, norm_mix_g, w_in, conv_dw_w, conv_dw_b, conv_ln_g, conv_ln_b,
           conv_pw_w, ret_gn_g, ret_w_o, w_out, norm_ffn_g, w_group_router, b_group_router,
           w_expert_router, b_expert_router, w_expert_gate, w_expert_up, w_expert_down,
           norm_final_g):
    batch, seq, d = x.shape
    assert d == D_MODEL and seq % RET_STEP == 0 and w_in.shape[0] == 1
    t = batch * seq
    x2d = x.reshape(t, d)
    row = lambda v: v.reshape(1, -1)

    proj_meta, w_in_bf = _meta_proj(meta_tokens, row(norm_mix_g[0]), w_in[0], IN_PROJ_COLS)
    proj, y_conv = _proj_conv(x2d, row(norm_mix_g[0]), w_in_bf, proj_meta, conv_dw_w[0],
                              row(conv_dw_b[0]), row(conv_ln_g[0]), row(conv_ln_b[0]),
                              conv_pw_w[0].astype(BF16), seq, PROJ_TILE)
    h1, w_gu, w_dn = _retention(proj, proj_meta, y_conv, x2d, _retention_tables(seq),
                                row(ret_gn_g[0]), ret_w_o[0].astype(BF16), w_out[0].astype(BF16),
                                w_expert_gate[0], w_expert_up[0], w_expert_down[0], batch, seq)
    w_gu = w_gu.reshape(N_EXPERTS, D_MODEL, 2 * D_EXPERT)
    w_dn = w_dn.reshape(N_EXPERTS, D_EXPERT, D_MODEL)

    w_router = jnp.concatenate([w_group_router[0], w_expert_router[0]], axis=1)
    w_router = jnp.pad(w_router, ((0, 0), (0, ROUTER_LANES - w_router.shape[1])))
    b_router = jnp.concatenate([b_group_router[0], b_expert_router[0]])
    b_router = jnp.pad(b_router, (0, ROUTER_LANES - b_router.shape[0])).reshape(1, -1)
    out = _moe(h1, row(norm_ffn_g[0]), _split_bf16(w_router), b_router, w_gu, w_dn,
               row(norm_final_g), min(1024, t))
    return out.reshape(batch, seq, d)
```

```python
import functools
import math

import jax
import jax.numpy as jnp
from jax import lax
from jax.experimental import pallas as pl
from jax.experimental.pallas import tpu as pltpu

D_MODEL = 1024
CHUNK = 64
N_META = 16
CONV_DIM = 1024
CONV_WIDTH = 31
RET_HEADS = 4
RET_QK_DIM = 256
RET_V_DIM = 512
ROPE_BASE = 10000.0
N_GROUPS = 4
EXPERTS_PER_GROUP = 4
N_EXPERTS = N_GROUPS * EXPERTS_PER_GROUP
D_EXPERT = 512
EPS = 1e-6
D_IN = 2 * CONV_DIM + 2 * RET_HEADS * RET_QK_DIM + 2 * RET_HEADS * RET_V_DIM + 2 * D_MODEL

LANES = 128
SUBLANES = 8
CONV_HALO = 32
IN_PROJ_COLS = 2048
PROJ_TILE = 1024
CONV_TILE = 256
CONV_ROWS_PER_ITER = 4
RET_BLOCK = 256
RET_STEP = 512
ROUTER_LANES = LANES
MOE_BLOCK = 128
VMEM_LIMIT = 48 * 1024 * 1024
PROJ_VMEM_LIMIT = 56 * 1024 * 1024
MOE_VMEM_LIMIT = 60 * 1024 * 1024

F32 = jnp.float32
BF16 = jnp.bfloat16


def _sigmoid(x):
    return 1.0 / (1.0 + jnp.exp(-x))


def _dot(a, b):
    return jnp.dot(a, b, preferred_element_type=F32)


def _meta_proj_kernel(x_ref, g_ref, w_ref, o_ref, wbf_ref, u_ref):
    @pl.when(pl.program_id(0) == 0)
    def _():
        x = x_ref[...]
        ms = jnp.mean(x * x, axis=-1, keepdims=True)
        u_ref[...] = (x * lax.rsqrt(ms + EPS) * g_ref[...]).astype(BF16)

    w = w_ref[...].astype(BF16)
    wbf_ref[...] = w
    o_ref[...] = _dot(u_ref[...], w).astype(o_ref.dtype)


def _meta_proj(meta, g, w, tn):
    t, d = meta.shape
    n = w.shape[1]
    return pl.pallas_call(
        _meta_proj_kernel,
        grid=(n // tn,),
        in_specs=[
            pl.BlockSpec((t, d), lambda j: (0, 0)),
            pl.BlockSpec((1, d), lambda j: (0, 0)),
            pl.BlockSpec((d, tn), lambda j: (0, j)),
        ],
        out_specs=[pl.BlockSpec((t, tn), lambda j: (0, j)),
                   pl.BlockSpec((d, tn), lambda j: (0, j))],
        out_shape=[jax.ShapeDtypeStruct((t, n), BF16),
                   jax.ShapeDtypeStruct((d, n), BF16)],
        scratch_shapes=[pltpu.VMEM((t, d), BF16)],
        compiler_params=pltpu.CompilerParams(
            dimension_semantics=("arbitrary",),
            vmem_limit_bytes=VMEM_LIMIT),
        name="meta_proj",
    )(meta, g, w)


def _conv_shifts(win_ref, shift_ref):
    span = CONV_TILE + CONV_HALO - SUBLANES
    for s in range(1, SUBLANES):
        shift_ref[s - 1, 0:span, :] = win_ref[s:s + span, :]


def _conv_taps(cb, r0, win_ref, shift_ref, cbuf_ref, wdw_ref, bdw_ref):
    def group(m):
        return slice(r0 + m * SUBLANES, r0 + (m + 1) * SUBLANES)

    first = CONV_HALO - (CONV_WIDTH - 1)
    cols = pl.ds(pl.multiple_of(cb * LANES, LANES), LANES)
    bias = jnp.broadcast_to(bdw_ref[:, cols], (SUBLANES, LANES))
    accs = [bias] * CONV_ROWS_PER_ITER
    for shift in range(SUBLANES):
        js = [j for j in range(CONV_WIDTH) if (first + j) % SUBLANES == shift]
        tiles = [(first + j) // SUBLANES for j in js]
        taps = [jnp.broadcast_to(wdw_ref[j:j + 1, cols], (SUBLANES, LANES)) for j in js]
        wins = {}
        for m in range(min(tiles), max(tiles) + CONV_ROWS_PER_ITER):
            wins[m] = (win_ref[group(m), cols] if shift == 0
                       else shift_ref[shift - 1, group(m), cols])
        for c in range(CONV_ROWS_PER_ITER):
            for tap, m in zip(taps, tiles):
                accs[c] = accs[c] + tap * wins[m + c]
    for c in range(CONV_ROWS_PER_ITER):
        cbuf_ref[group(c), cols] = accs[c]


def _proj_conv_kernel(x_ref, g_ref, w_ref, ma_ref, mg_ref, wdw_ref, bdw_ref, lng_ref, lnb_ref,
                      wpw_ref, proj_ref, yconv_ref, u_ref, hbuf_ref, win_ref, shift_ref,
                      cbuf_ref, *, tiles_per_seq):
    i, j = pl.program_id(0), pl.program_id(1)
    tm = x_ref.shape[0]

    @pl.when(j == 0)
    def _():
        x = x_ref[...]
        ms = jnp.mean(x * x, axis=-1, keepdims=True)
        u = (x * lax.rsqrt(ms + EPS) * g_ref[...]).astype(BF16)
        u_ref[...] = u

        @pl.when(i % tiles_per_seq == 0)
        def _():
            hbuf_ref[0:CONV_HALO - N_META, :] = jnp.zeros((CONV_HALO - N_META, CONV_DIM), F32)
            ma = ma_ref[...].astype(F32)
            mg = mg_ref[...].astype(F32)
            hbuf_ref[CONV_HALO - N_META:CONV_HALO, :] = ma * _sigmoid(mg)

        @pl.when(i % tiles_per_seq != 0)
        def _():
            hbuf_ref[0:CONV_HALO, :] = hbuf_ref[tm:tm + CONV_HALO, :]

        glu = _dot(u, w_ref[...])
        hbuf_ref[CONV_HALO:CONV_HALO + tm, :] = glu[:, :CONV_DIM] * _sigmoid(glu[:, CONV_DIM:])

    @pl.when(j > 0)
    def _():
        base = pl.multiple_of((j - 1) * CONV_TILE, CONV_TILE)
        win_ref[...] = hbuf_ref[pl.ds(base, CONV_HALO + CONV_TILE), :]
        _conv_shifts(win_ref, shift_ref)
        proj_ref[0] = _dot(u_ref[...], w_ref[...]).astype(proj_ref.dtype)

        def block_body(cb, carry):
            for r0 in range(0, CONV_TILE, SUBLANES * CONV_ROWS_PER_ITER):
                _conv_taps(cb, r0, win_ref, shift_ref, cbuf_ref, wdw_ref, bdw_ref)
            return carry

        lax.fori_loop(0, CONV_DIM // LANES, block_body, 0)
        c = cbuf_ref[...]
        mu = jnp.mean(c, axis=-1, keepdims=True)
        cc = c - mu
        var = jnp.mean(cc * cc, axis=-1, keepdims=True)
        y = cc * lax.rsqrt(var + EPS) * lng_ref[...] + lnb_ref[...]
        y = y * _sigmoid(y)
        yconv_ref[pl.ds(base, CONV_TILE), :] = _dot(y.astype(BF16),
                                                    wpw_ref[...]).astype(yconv_ref.dtype)


def _proj_conv(x2d, g, w_bf, proj_meta, wdw, bdw, lng, lnb, wpw_bf, seq, tm):
    t, d = x2d.shape
    n_steps = D_IN // IN_PROJ_COLS
    assert IN_PROJ_COLS == 2 * CONV_DIM and (n_steps - 1) * CONV_TILE == tm and seq % tm == 0
    const = lambda i, j: (0, 0)
    return pl.pallas_call(
        functools.partial(_proj_conv_kernel, tiles_per_seq=seq // tm),
        grid=(t // tm, n_steps),
        in_specs=[
            pl.BlockSpec((tm, d), lambda i, j: (i, 0)),
            pl.BlockSpec((1, d), const),
            pl.BlockSpec((d, IN_PROJ_COLS), lambda i, j: (0, j)),
            pl.BlockSpec((N_META, CONV_DIM), lambda i, j: (0, 0)),
            pl.BlockSpec((N_META, CONV_DIM), lambda i, j: (0, 1)),
            pl.BlockSpec((CONV_WIDTH, CONV_DIM), const),
            pl.BlockSpec((1, CONV_DIM), const),
            pl.BlockSpec((1, CONV_DIM), const),
            pl.BlockSpec((1, CONV_DIM), const),
            pl.BlockSpec((CONV_DIM, D_MODEL), const),
        ],
        out_specs=[
            pl.BlockSpec((1, tm, IN_PROJ_COLS), lambda i, j: (jnp.maximum(j - 1, 0), i, 0)),
            pl.BlockSpec((tm, D_MODEL), lambda i, j: (i, 0)),
        ],
        out_shape=[jax.ShapeDtypeStruct((n_steps - 1, t, IN_PROJ_COLS), BF16),
                   jax.ShapeDtypeStruct((t, D_MODEL), BF16)],
        scratch_shapes=[pltpu.VMEM((tm, d), BF16),
                        pltpu.VMEM((CONV_HALO + tm, CONV_DIM), F32),
                        pltpu.VMEM((CONV_HALO + CONV_TILE, CONV_DIM), F32),
                        pltpu.VMEM((SUBLANES - 1, CONV_HALO + CONV_TILE, CONV_DIM), F32),
                        pltpu.VMEM((CONV_TILE, CONV_DIM), F32)],
        compiler_params=pltpu.CompilerParams(
            dimension_semantics=("arbitrary", "arbitrary"),
            vmem_limit_bytes=PROJ_VMEM_LIMIT),
        name="proj_conv",
    )(x2d, g, w_bf, proj_meta, proj_meta, wdw, bdw, lng, lnb, wpw_bf)


def _rotary(x, cos, sin):
    half = x.shape[-1] // 2
    x1, x2 = x[:, :half], x[:, half:]
    return jnp.concatenate([x1 * cos - x2 * sin, x2 * cos + x1 * sin], axis=-1)


def _ret_kernel(qk_ref, v_ref, gret_ref, gm_ref, yconv_ref, x_ref, cos_ref, sin_ref,
                mk_ref, mv_ref, mcos_ref, msin_ref, dmat_ref, qdec_ref, kdec_ref, mkdec_ref,
                bdec_ref, gn_ref, wo_ref, wout_ref, eg_ref, eu_ref, ed_ref,
                o_ref, egu_ref, edn_ref, state_ref, gated_ref):
    i = pl.program_id(1)
    k_scale = RET_QK_DIM ** -0.5

    egu_ref[:, :D_EXPERT] = eg_ref[...].astype(BF16)
    egu_ref[:, D_EXPERT:] = eu_ref[...].astype(BF16)
    edn_ref[...] = ed_ref[...].astype(BF16)

    @pl.when(i == 0)
    def _():
        mcos, msin = mcos_ref[...], msin_ref[...]
        for h in range(RET_HEADS):
            mk = mk_ref[:, h * RET_QK_DIM:(h + 1) * RET_QK_DIM].astype(F32)
            mk = _rotary(mk, mcos, msin) * k_scale * mkdec_ref[h]
            mv = mv_ref[:, h * RET_V_DIM:(h + 1) * RET_V_DIM]
            state_ref[h] = lax.dot_general(mk.astype(BF16), mv, (((0,), (0,)), ((), ())),
                                           preferred_element_type=F32)

    for sb in range(RET_STEP // RET_BLOCK):
        rows = slice(sb * RET_BLOCK, (sb + 1) * RET_BLOCK)
        cos, sin = cos_ref[rows, :], sin_ref[rows, :]
        for h in range(RET_HEADS):
            qq = slice(h * RET_QK_DIM, (h + 1) * RET_QK_DIM)
            kk = slice((RET_HEADS + h) * RET_QK_DIM, (RET_HEADS + h + 1) * RET_QK_DIM)
            vv = slice(h * RET_V_DIM, (h + 1) * RET_V_DIM)
            q = _rotary(qk_ref[0, rows, qq].astype(F32), cos, sin)
            k = _rotary(qk_ref[0, rows, kk].astype(F32), cos, sin) * k_scale
            v = v_ref[0, rows, vv]
            q_bf = q.astype(BF16)
            scores = lax.dot_general(q_bf, k.astype(BF16), (((1,), (1,)), ((), ())),
                                     preferred_element_type=F32) * dmat_ref[h]
            state = state_ref[h]
            o = _dot(scores.astype(BF16), v) + _dot(q_bf, state.astype(BF16)) * qdec_ref[h]
            state_ref[h] = state * bdec_ref[h] + lax.dot_general(
                (k * kdec_ref[h]).astype(BF16), v, (((0,), (0,)), ((), ())),
                preferred_element_type=F32)
            mu = jnp.mean(o, axis=-1, keepdims=True)
            oc = o - mu
            var = jnp.mean(oc * oc, axis=-1, keepdims=True)
            on = oc * lax.rsqrt(var + EPS) * gn_ref[:, vv]
            gr = gret_ref[0, rows, vv]
            gated_ref[rows, vv] = gr * _sigmoid(gr) * on.astype(BF16)

    y_ret = _dot(gated_ref[...], wo_ref[...])
    ga = gm_ref[0, :, :D_MODEL]
    gb = gm_ref[0, :, D_MODEL:]
    merged = _sigmoid(ga) * yconv_ref[...] + _sigmoid(gb) * y_ret.astype(BF16)
    o_ref[...] = x_ref[...] + _dot(merged, wout_ref[...])


def _retention(proj, proj_meta, y_conv, x2d, tables, gn, wo_bf, wout_bf, w_gate, w_up, w_down,
               batch, seq):
    t = proj.shape[1]
    nb = seq // RET_STEP
    n_steps = batch * nb
    gu_rows, dn_rows = N_EXPERTS * D_MODEL // n_steps, N_EXPERTS * D_EXPERT // n_steps
    assert gu_rows * n_steps == N_EXPERTS * D_MODEL and dn_rows * n_steps == N_EXPERTS * D_EXPERT
    assert gu_rows % (2 * SUBLANES) == 0 and dn_rows % (2 * SUBLANES) == 0
    hq = RET_HEADS * RET_QK_DIM
    hv = RET_HEADS * RET_V_DIM
    assert 2 * hq == hv == 2 * D_MODEL == proj.shape[2]
    slab = lambda s: pl.BlockSpec((1, RET_STEP, hv), lambda b, i: (s, row(b, i), 0))
    mk_col, mv_col = 2 * CONV_DIM // hq + 1, (2 * CONV_DIM + 2 * hq) // hv
    row = lambda b, i: b * nb + i
    const2 = lambda b, i: (0, 0)
    const3 = lambda b, i: (0, 0, 0)
    cos, sin, mcos, msin, dmat, qdec, kdec, mkdec, bdec = tables
    return pl.pallas_call(
        _ret_kernel,
        grid=(batch, nb),
        in_specs=[
            slab(0), slab(1), slab(2), slab(3),
            pl.BlockSpec((RET_STEP, D_MODEL), lambda b, i: (row(b, i), 0)),
            pl.BlockSpec((RET_STEP, D_MODEL), lambda b, i: (row(b, i), 0)),
            pl.BlockSpec((RET_STEP, RET_QK_DIM // 2), lambda b, i: (i, 0)),
            pl.BlockSpec((RET_STEP, RET_QK_DIM // 2), lambda b, i: (i, 0)),
            pl.BlockSpec((N_META, hq), lambda b, i: (0, mk_col)),
            pl.BlockSpec((N_META, hv), lambda b, i: (0, mv_col)),
            pl.BlockSpec((N_META, RET_QK_DIM // 2), const2),
            pl.BlockSpec((N_META, RET_QK_DIM // 2), const2),
            pl.BlockSpec((RET_HEADS, RET_BLOCK, RET_BLOCK), const3),
            pl.BlockSpec((RET_HEADS, RET_BLOCK, 1), const3),
            pl.BlockSpec((RET_HEADS, RET_BLOCK, 1), const3),
            pl.BlockSpec((RET_HEADS, N_META, 1), const3),
            pl.BlockSpec((RET_HEADS, 1, 1), const3),
            pl.BlockSpec((1, hv), const2),
            pl.BlockSpec((hv, D_MODEL), const2),
            pl.BlockSpec((D_MODEL, D_MODEL), const2),
            pl.BlockSpec((gu_rows, D_EXPERT), lambda b, i: (row(b, i), 0)),
            pl.BlockSpec((gu_rows, D_EXPERT), lambda b, i: (row(b, i), 0)),
            pl.BlockSpec((dn_rows, D_MODEL), lambda b, i: (row(b, i), 0)),
        ],
        out_specs=[
            pl.BlockSpec((RET_STEP, D_MODEL), lambda b, i: (row(b, i), 0)),
            pl.BlockSpec((gu_rows, 2 * D_EXPERT), lambda b, i: (row(b, i), 0)),
            pl.BlockSpec((dn_rows, D_MODEL), lambda b, i: (row(b, i), 0)),
        ],
        out_shape=[jax.ShapeDtypeStruct((t, D_MODEL), F32),
                   jax.ShapeDtypeStruct((N_EXPERTS * D_MODEL, 2 * D_EXPERT), BF16),
                   jax.ShapeDtypeStruct((N_EXPERTS * D_EXPERT, D_MODEL), BF16)],
        scratch_shapes=[pltpu.VMEM((RET_HEADS, RET_QK_DIM, RET_V_DIM), F32),
                        pltpu.VMEM((RET_STEP, hv), BF16)],
        compiler_params=pltpu.CompilerParams(
            dimension_semantics=("parallel", "arbitrary"),
            vmem_limit_bytes=PROJ_VMEM_LIMIT),
        name="retention_mix",
    )(proj, proj, proj, proj, y_conv, x2d, cos, sin, proj_meta, proj_meta, mcos, msin,
      dmat, qdec, kdec, mkdec, bdec, gn, wo_bf, wout_bf,
      w_gate.reshape(N_EXPERTS * D_MODEL, D_EXPERT), w_up.reshape(N_EXPERTS * D_MODEL, D_EXPERT),
      w_down.reshape(N_EXPERTS * D_EXPERT, D_MODEL))


def _retention_tables(seq):
    half = RET_QK_DIM // 2
    inv = ROPE_BASE ** (-jnp.arange(half, dtype=F32) / half)
    pos = jnp.arange(N_META + seq, dtype=F32)
    ang = pos[:, None] * inv[None, :]
    cos_all, sin_all = jnp.cos(ang), jnp.sin(ang)
    log_gamma = jnp.log(1.0 - 2.0 ** (-5.0 - jnp.arange(RET_HEADS, dtype=F32)))
    idx = jnp.arange(RET_BLOCK, dtype=F32)
    chunk = jnp.arange(RET_BLOCK, dtype=jnp.int32) // CHUNK
    visible = chunk[None, :] <= chunk[:, None]
    dmat = jnp.where(visible[None],
                     jnp.exp(log_gamma[:, None, None] * jnp.abs(idx[:, None] - idx[None, :])),
                     0.0)
    qdec = jnp.exp(log_gamma[:, None] * (idx + 1.0))[:, :, None]
    kdec = jnp.exp(log_gamma[:, None] * (RET_BLOCK - 1.0 - idx))[:, :, None]
    midx = jnp.arange(N_META, dtype=F32)
    mkdec = jnp.exp(log_gamma[:, None] * (N_META - 1.0 - midx))[:, :, None]
    bdec = jnp.exp(log_gamma * RET_BLOCK)[:, None, None]
    return (cos_all[N_META:], sin_all[N_META:], cos_all[:N_META], sin_all[:N_META],
            dmat, qdec, kdec, mkdec, bdec)


def _route(logits):
    lane = lax.broadcasted_iota(jnp.int32, logits.shape, 1)
    neg = jnp.float32(-jnp.inf)
    big = jnp.int32(ROUTER_LANES)

    def first_max(masked):
        val = jnp.max(masked, axis=-1, keepdims=True)
        idx = jnp.min(jnp.where(masked == val, lane, big), axis=-1, keepdims=True)
        return val, idx

    gmask = lane < N_GROUPS
    gmax, gidx = first_max(jnp.where(gmask, logits, neg))
    denom = jnp.sum(jnp.where(gmask, jnp.exp(logits - gmax), 0.0), axis=-1, keepdims=True)
    p_group = 1.0 / denom
    assert EXPERTS_PER_GROUP & (EXPERTS_PER_GROUP - 1) == 0
    shift = EXPERTS_PER_GROUP.bit_length() - 1
    lane_group = (lane - N_GROUPS) >> shift
    in_group = jnp.where(lane_group == gidx, logits, neg)
    v1, i1 = first_max(in_group)
    v2, i2 = first_max(jnp.where(lane == i1, neg, in_group))
    e2 = jnp.exp(v2 - v1)
    w1 = p_group / (1.0 + e2)
    w2 = p_group * e2 / (1.0 + e2)
    return jnp.where(lane == i1, w1, 0.0) + jnp.where(lane == i2, w2, 0.0), gidx


def _moe_kernel(h_ref, g_ref, wr_ref, br_ref, wgu_ref, wd_ref, gf_ref, o_ref,
                xs_ref, cs_ref, pos_ref, y_ref, seg_ref):
    grp = pl.program_id(1)
    tm = h_ref.shape[0]

    @pl.when(grp == 0)
    def _():
        h = h_ref[...]
        ms = jnp.mean(h * h, axis=-1, keepdims=True)
        u = h * lax.rsqrt(ms + EPS) * g_ref[...]
        u_hi = u.astype(BF16)
        u_lo = (u - u_hi.astype(F32)).astype(BF16)
        hi_part = _dot(u_hi, wr_ref[...])
        lo_part = _dot(u_lo, wr_ref[:, :ROUTER_LANES])
        logits = (hi_part[:, :ROUTER_LANES] + (hi_part[:, ROUTER_LANES:] + lo_part)
                  + br_ref[...])
        comb, gidx = _route(logits)

        lane = lax.broadcasted_iota(jnp.int32, (tm, ROUTER_LANES), 1)
        onehot = jnp.where(lane == gidx, 1.0, 0.0)
        counts = jnp.sum(onehot, axis=0, keepdims=True)
        row = lax.broadcasted_iota(jnp.int32, (tm, tm), 0)
        col = lax.broadcasted_iota(jnp.int32, (tm, tm), 1)
        earlier = jnp.where(col < row, 1.0, 0.0).astype(BF16)
        prefix = _dot(earlier, onehot.astype(BF16))
        lane_row = lax.broadcasted_iota(jnp.int32, (1, ROUTER_LANES), 1)
        start = jnp.int32(0)
        starts = jnp.zeros((1, ROUTER_LANES), F32)
        for gg in range(N_GROUPS):
            count = jnp.sum(jnp.where(lane_row == gg, counts, 0.0)).astype(jnp.int32)
            n_blocks = (count + MOE_BLOCK - 1) // MOE_BLOCK
            seg_ref[gg] = start // MOE_BLOCK
            seg_ref[N_GROUPS + gg] = n_blocks
            starts = starts + jnp.where(lane_row == gg, start.astype(F32), 0.0)
            start = start + n_blocks * MOE_BLOCK
        pos = jnp.sum(onehot * (prefix + starts), axis=-1, keepdims=True)
        pos_lanes = jnp.broadcast_to(pos, (tm, ROUTER_LANES))
        pos_ref[...] = pos_lanes
        pos_row = pos_lanes.T[0:1, :].astype(jnp.int32)
        srow = lax.broadcasted_iota(jnp.int32, (xs_ref.shape[0], tm), 0)
        perm = jnp.where(srow == pos_row, 1.0, 0.0).astype(BF16)
        c_hi = comb.astype(BF16)
        c_lo = (comb - c_hi.astype(F32)).astype(BF16)
        moved = _dot(perm, jnp.concatenate([u_hi, c_hi, c_lo], axis=-1))
        xs_ref[...] = moved[:, :D_MODEL].astype(BF16)
        cs_ref[...] = (moved[:, D_MODEL:D_MODEL + ROUTER_LANES]
                       + moved[:, D_MODEL + ROUTER_LANES:])
        y_ref[...] = jnp.zeros_like(y_ref)

    blk_lo, n_blk = seg_ref[grp], seg_ref[N_GROUPS + grp]

    def experts(blk, n_blocks):
        n_rows = n_blocks * MOE_BLOCK
        rows = pl.ds(pl.multiple_of(blk * MOE_BLOCK, MOE_BLOCK), n_rows)
        xb = xs_ref[rows, :]
        cb = cs_ref[rows, :]
        lane = lax.broadcasted_iota(jnp.int32, cb.shape, 1)
        acc = jnp.zeros((n_rows, D_MODEL), F32)
        for e in range(EXPERTS_PER_GROUP):
            expert_lane = N_GROUPS + grp * EXPERTS_PER_GROUP + e
            w_e = jnp.sum(jnp.where(lane == expert_lane, cb, 0.0), axis=-1, keepdims=True)
            gu = _dot(xb, wgu_ref[e])
            gate, up = gu[:, :D_EXPERT], gu[:, D_EXPERT:]
            hid = gate * _sigmoid(gate) * up * w_e
            acc = acc + _dot(hid.astype(BF16), wd_ref[e])
        y_ref[rows, :] += acc

    def pair_body(p, carry):
        experts(blk_lo + 2 * p, 2)
        return carry

    lax.fori_loop(0, n_blk // 2, pair_body, 0)

    @pl.when(n_blk % 2 == 1)
    def _():
        experts(blk_lo + n_blk - 1, 1)

    @pl.when(grp == N_GROUPS - 1)
    def _():
        pos = pos_ref[:, 0:1].astype(jnp.int32)
        scol = lax.broadcasted_iota(jnp.int32, (tm, y_ref.shape[0]), 1)
        unperm = jnp.where(scol == pos, 1.0, 0.0).astype(BF16)
        h2 = h_ref[...] + _dot(unperm, y_ref[...].astype(BF16))
        ms = jnp.mean(h2 * h2, axis=-1, keepdims=True)
        o_ref[...] = h2 * lax.rsqrt(ms + EPS) * gf_ref[...]


def _moe(h1, g, wr2, br, wgu_bf, wd_bf, gf, tm):
    t = h1.shape[0]
    assert tm % MOE_BLOCK == 0
    sorted_rows = tm + (N_GROUPS - 1) * MOE_BLOCK
    const2 = lambda i, e: (0, 0)
    return pl.pallas_call(
        _moe_kernel,
        grid=(t // tm, N_GROUPS),
        in_specs=[
            pl.BlockSpec((tm, D_MODEL), lambda i, e: (i, 0)),
            pl.BlockSpec((1, D_MODEL), const2),
            pl.BlockSpec((D_MODEL, 2 * ROUTER_LANES), const2),
            pl.BlockSpec((1, ROUTER_LANES), const2),
            pl.BlockSpec((EXPERTS_PER_GROUP, D_MODEL, 2 * D_EXPERT), lambda i, e: (e, 0, 0)),
            pl.BlockSpec((EXPERTS_PER_GROUP, D_EXPERT, D_MODEL), lambda i, e: (e, 0, 0)),
            pl.BlockSpec((1, D_MODEL), const2),
        ],
        out_specs=pl.BlockSpec((tm, D_MODEL), lambda i, e: (i, 0)),
        out_shape=jax.ShapeDtypeStruct((t, D_MODEL), F32),
        scratch_shapes=[pltpu.VMEM((sorted_rows, D_MODEL), BF16),
                        pltpu.VMEM((sorted_rows, ROUTER_LANES), F32),
                        pltpu.VMEM((tm, ROUTER_LANES), F32),
                        pltpu.VMEM((sorted_rows, D_MODEL), F32),
                        pltpu.SMEM((2 * N_GROUPS,), jnp.int32)],
        compiler_params=pltpu.CompilerParams(
            dimension_semantics=("parallel", "arbitrary"),
            vmem_limit_bytes=MOE_VMEM_LIMIT),
        name="hier_moe",
    )(h1, g, wr2, br, wgu_bf, wd_bf, gf)


def _split_bf16(w):
    hi = w.astype(BF16)
    lo = (w - hi.astype(F32)).astype(BF16)
    return jnp.concatenate([hi, lo], axis=-1)


def kernel(x, meta_tokens, norm_mix_g, w_in, conv_dw_w, conv_dw_b, conv_ln_g, conv_ln_b,
           conv_pw_w, ret_gn_g, ret_w_o, w_out, norm_ffn_g, w_group_router, b_group_router,
           w_expert_router, b_expert_router, w_expert_gate, w_expert_up, w_expert_down,
           norm_final_g):
    batch, seq, d = x.shape
    assert d == D_MODEL and seq % RET_STEP == 0 and w_in.shape[0] == 1
    t = batch * seq
    x2d = x.reshape(t, d)
    row = lambda v: v.reshape(1, -1)

    proj_meta, w_in_bf = _meta_proj(meta_tokens, row(norm_mix_g[0]), w_in[0], IN_PROJ_COLS)
    proj, y_conv = _proj_conv(x2d, row(norm_mix_g[0]), w_in_bf, proj_meta, conv_dw_w[0],
                              row(conv_dw_b[0]), row(conv_ln_g[0]), row(conv_ln_b[0]),
                              conv_pw_w[0].astype(BF16), seq, PROJ_TILE)
    h1, w_gu, w_dn = _retention(proj, proj_meta, y_conv, x2d, _retention_tables(seq),
                                row(ret_gn_g[0]), ret_w_o[0].astype(BF16), w_out[0].astype(BF16),
                                w_expert_gate[0], w_expert_up[0], w_expert_down[0], batch, seq)
    w_gu = w_gu.reshape(N_EXPERTS, D_MODEL, 2 * D_EXPERT)
    w_dn = w_dn.reshape(N_EXPERTS, D_EXPERT, D_MODEL)

    w_router = jnp.concatenate([w_group_router[0], w_expert_router[0]], axis=1)
    w_router = jnp.pad(w_router, ((0, 0), (0, ROUTER_LANES - w_router.shape[1])))
    b_router = jnp.concatenate([b_group_router[0], b_expert_router[0]])
    b_router = jnp.pad(b_router, (0, ROUTER_LANES - b_router.shape[0])).reshape(1, -1)
    out = _moe(h1, row(norm_ffn_g[0]), _split_bf16(w_router), b_router, w_gu, w_dn,
               row(norm_final_g), min(1024, t))
    return out.reshape(batch, seq, d)
```

```python
import functools

import jax
import jax.numpy as jnp
from jax import lax
from jax.experimental import pallas as pl
from jax.experimental.pallas import tpu as pltpu

D_MODEL = 1024
CHUNK = 64
N_META = 16
CONV_DIM = 1024
CONV_WIDTH = 31
RET_HEADS = 4
RET_QK_DIM = 256
RET_V_DIM = 512
ROPE_BASE = 10000.0
N_GROUPS = 4
EXPERTS_PER_GROUP = 4
N_EXPERTS = N_GROUPS * EXPERTS_PER_GROUP
D_EXPERT = 512
EPS = 1e-6
D_IN = 2 * CONV_DIM + 2 * RET_HEADS * RET_QK_DIM + 2 * RET_HEADS * RET_V_DIM + 2 * D_MODEL

LANES = 128
SUBLANES = 8
CONV_HALO = 32
IN_PROJ_COLS = 2048
PROJ_TILE = 1024
CONV_TILE = 256
CONV_ROWS_PER_ITER = 4
RET_BLOCK = 256
RET_STEP = 512
ROUTER_LANES = LANES
MOE_BLOCK = 128
VMEM_LIMIT = 48 * 1024 * 1024
PROJ_VMEM_LIMIT = 56 * 1024 * 1024
MOE_VMEM_LIMIT = 60 * 1024 * 1024

F32 = jnp.float32
BF16 = jnp.bfloat16


def _sigmoid(x):
    return 1.0 / (1.0 + jnp.exp(-x))


def _dot(a, b):
    return jnp.dot(a, b, preferred_element_type=F32)


def _meta_proj_kernel(x_ref, g_ref, w_ref, o_ref, wbf_ref, u_ref):
    @pl.when(pl.program_id(0) == 0)
    def _():
        x = x_ref[...]
        ms = jnp.mean(x * x, axis=-1, keepdims=True)
        u_ref[...] = (x * lax.rsqrt(ms + EPS) * g_ref[...]).astype(BF16)

    w = w_ref[...].astype(BF16)
    wbf_ref[...] = w
    o_ref[...] = _dot(u_ref[...], w).astype(o_ref.dtype)


def _meta_proj(meta, g, w, tn):
    t, d = meta.shape
    n = w.shape[1]
    return pl.pallas_call(
        _meta_proj_kernel,
        grid=(n // tn,),
        in_specs=[
            pl.BlockSpec((t, d), lambda j: (0, 0)),
            pl.BlockSpec((1, d), lambda j: (0, 0)),
            pl.BlockSpec((d, tn), lambda j: (0, j)),
        ],
        out_specs=[pl.BlockSpec((t, tn), lambda j: (0, j)),
                   pl.BlockSpec((d, tn), lambda j: (0, j))],
        out_shape=[jax.ShapeDtypeStruct((t, n), BF16),
                   jax.ShapeDtypeStruct((d, n), BF16)],
        scratch_shapes=[pltpu.VMEM((t, d), BF16)],
        compiler_params=pltpu.CompilerParams(
            dimension_semantics=("arbitrary",),
            vmem_limit_bytes=VMEM_LIMIT),
        name="meta_proj",
    )(meta, g, w)


def _conv_shifts(win_ref, shift_ref):
    span = CONV_TILE + CONV_HALO - SUBLANES
    for s in range(1, SUBLANES):
        shift_ref[s - 1, 0:span, :] = win_ref[s:s + span, :]


def _conv_taps(cb, r0, win_ref, shift_ref, cbuf_ref, wdw_ref, bdw_ref):
    def group(m):
        return slice(r0 + m * SUBLANES, r0 + (m + 1) * SUBLANES)

    first = CONV_HALO - (CONV_WIDTH - 1)
    cols = pl.ds(pl.multiple_of(cb * LANES, LANES), LANES)
    bias = jnp.broadcast_to(bdw_ref[:, cols], (SUBLANES, LANES))
    accs = [bias] * CONV_ROWS_PER_ITER
    for shift in range(SUBLANES):
        js = [j for j in range(CONV_WIDTH) if (first + j) % SUBLANES == shift]
        tiles = [(first + j) // SUBLANES for j in js]
        taps = [jnp.broadcast_to(wdw_ref[j:j + 1, cols], (SUBLANES, LANES)) for j in js]
        wins = {}
        for m in range(min(tiles), max(tiles) + CONV_ROWS_PER_ITER):
            wins[m] = (win_ref[group(m), cols] if shift == 0
                       else shift_ref[shift - 1, group(m), cols])
        for c in range(CONV_ROWS_PER_ITER):
            for tap, m in zip(taps, tiles):
                accs[c] = accs[c] + tap * wins[m + c]
    for c in range(CONV_ROWS_PER_ITER):
        cbuf_ref[group(c), cols] = accs[c]


def _proj_conv_kernel(x_ref, g_ref, w_ref, ma_ref, mg_ref, wdw_ref, bdw_ref, lng_ref, lnb_ref,
                      wpw_ref, proj_ref, yconv_ref, u_ref, hbuf_ref, win_ref, shift_ref,
                      cbuf_ref, *, tiles_per_seq):
    i, j = pl.program_id(0), pl.program_id(1)
    tm = x_ref.shape[0]

    @pl.when(j == 0)
    def _():
        x = x_ref[...]
        ms = jnp.mean(x * x, axis=-1, keepdims=True)
        u = (x * lax.rsqrt(ms + EPS) * g_ref[...]).astype(BF16)
        u_ref[...] = u

        @pl.when(i % tiles_per_seq == 0)
        def _():
            hbuf_ref[0:CONV_HALO - N_META, :] = jnp.zeros((CONV_HALO - N_META, CONV_DIM), F32)
            ma = ma_ref[...].astype(F32)
            mg = mg_ref[...].astype(F32)
            hbuf_ref[CONV_HALO - N_META:CONV_HALO, :] = ma * _sigmoid(mg)

        @pl.when(i % tiles_per_seq != 0)
        def _():
            hbuf_ref[0:CONV_HALO, :] = hbuf_ref[tm:tm + CONV_HALO, :]

        glu = _dot(u, w_ref[...])
        hbuf_ref[CONV_HALO:CONV_HALO + tm, :] = glu[:, :CONV_DIM] * _sigmoid(glu[:, CONV_DIM:])

    @pl.when(j > 0)
    def _():
        base = pl.multiple_of((j - 1) * CONV_TILE, CONV_TILE)
        win_ref[...] = hbuf_ref[pl.ds(base, CONV_HALO + CONV_TILE), :]
        _conv_shifts(win_ref, shift_ref)
        proj_ref[0] = _dot(u_ref[...], w_ref[...]).astype(proj_ref.dtype)

        def block_body(cb, carry):
            for r0 in range(0, CONV_TILE, SUBLANES * CONV_ROWS_PER_ITER):
                _conv_taps(cb, r0, win_ref, shift_ref, cbuf_ref, wdw_ref, bdw_ref)
            return carry

        lax.fori_loop(0, CONV_DIM // LANES, block_body, 0)
        c = cbuf_ref[...]
        mu = jnp.mean(c, axis=-1, keepdims=True)
        cc = c - mu
        var = jnp.mean(cc * cc, axis=-1, keepdims=True)
        y = (cc * lax.rsqrt(var + EPS) * lng_ref[...] + lnb_ref[...]).astype(BF16)
        y = y * _sigmoid(y)
        yconv_ref[pl.ds(base, CONV_TILE), :] = _dot(y, wpw_ref[...]).astype(yconv_ref.dtype)


def _proj_conv(x2d, g, w_bf, proj_meta, wdw, bdw, lng, lnb, wpw_bf, seq, tm):
    t, d = x2d.shape
    n_steps = D_IN // IN_PROJ_COLS
    assert IN_PROJ_COLS == 2 * CONV_DIM and (n_steps - 1) * CONV_TILE == tm and seq % tm == 0
    const = lambda i, j: (0, 0)
    return pl.pallas_call(
        functools.partial(_proj_conv_kernel, tiles_per_seq=seq // tm),
        grid=(t // tm, n_steps),
        in_specs=[
            pl.BlockSpec((tm, d), lambda i, j: (i, 0)),
            pl.BlockSpec((1, d), const),
            pl.BlockSpec((d, IN_PROJ_COLS), lambda i, j: (0, j)),
            pl.BlockSpec((N_META, CONV_DIM), lambda i, j: (0, 0)),
            pl.BlockSpec((N_META, CONV_DIM), lambda i, j: (0, 1)),
            pl.BlockSpec((CONV_WIDTH, CONV_DIM), const),
            pl.BlockSpec((1, CONV_DIM), const),
            pl.BlockSpec((1, CONV_DIM), const),
            pl.BlockSpec((1, CONV_DIM), const),
            pl.BlockSpec((CONV_DIM, D_MODEL), const),
        ],
        out_specs=[
            pl.BlockSpec((1, tm, IN_PROJ_COLS), lambda i, j: (jnp.maximum(j - 1, 0), i, 0)),
            pl.BlockSpec((tm, D_MODEL), lambda i, j: (i, 0)),
        ],
        out_shape=[jax.ShapeDtypeStruct((n_steps - 1, t, IN_PROJ_COLS), BF16),
                   jax.ShapeDtypeStruct((t, D_MODEL), BF16)],
        scratch_shapes=[pltpu.VMEM((tm, d), BF16),
                        pltpu.VMEM((CONV_HALO + tm, CONV_DIM), F32),
                        pltpu.VMEM((CONV_HALO + CONV_TILE, CONV_DIM), F32),
                        pltpu.VMEM((SUBLANES - 1, CONV_HALO + CONV_TILE, CONV_DIM), F32),
                        pltpu.VMEM((CONV_TILE, CONV_DIM), F32)],
        compiler_params=pltpu.CompilerParams(
            dimension_semantics=("arbitrary", "arbitrary"),
            vmem_limit_bytes=PROJ_VMEM_LIMIT),
        name="proj_conv",
    )(x2d, g, w_bf, proj_meta, proj_meta, wdw, bdw, lng, lnb, wpw_bf)


def _rotary(x, cos, sin):
    half = x.shape[-1] // 2
    x1, x2 = x[:, :half], x[:, half:]
    return jnp.concatenate([x1 * cos - x2 * sin, x2 * cos + x1 * sin], axis=-1)


def _ret_kernel(qk_ref, v_ref, gret_ref, gm_ref, yconv_ref, x_ref, cos_ref, sin_ref,
                mk_ref, mv_ref, mcos_ref, msin_ref, dmat_ref, qdec_ref, kdec_ref, mkdec_ref,
                bdec_ref, gn_ref, wo_ref, wout_ref, eg_ref, eu_ref, ed_ref,
                o_ref, egu_ref, edn_ref, state_ref, gated_ref):
    i = pl.program_id(1)
    k_scale = RET_QK_DIM ** -0.5

    egu_ref[:, :D_EXPERT] = eg_ref[...].astype(BF16)
    egu_ref[:, D_EXPERT:] = eu_ref[...].astype(BF16)
    edn_ref[...] = ed_ref[...].astype(BF16)

    @pl.when(i == 0)
    def _():
        mcos, msin = mcos_ref[...], msin_ref[...]
        for h in range(RET_HEADS):
            mk = mk_ref[:, h * RET_QK_DIM:(h + 1) * RET_QK_DIM].astype(F32)
            mk = _rotary(mk, mcos, msin) * k_scale * mkdec_ref[h]
            mv = mv_ref[:, h * RET_V_DIM:(h + 1) * RET_V_DIM]
            state_ref[h] = lax.dot_general(mk.astype(BF16), mv, (((0,), (0,)), ((), ())),
                                           preferred_element_type=F32)

    for sb in range(RET_STEP // RET_BLOCK):
        rows = slice(sb * RET_BLOCK, (sb + 1) * RET_BLOCK)
        cos, sin = cos_ref[rows, :], sin_ref[rows, :]
        for h in range(RET_HEADS):
            qq = slice(h * RET_QK_DIM, (h + 1) * RET_QK_DIM)
            kk = slice((RET_HEADS + h) * RET_QK_DIM, (RET_HEADS + h + 1) * RET_QK_DIM)
            vv = slice(h * RET_V_DIM, (h + 1) * RET_V_DIM)
            q = _rotary(qk_ref[0, rows, qq].astype(F32), cos, sin)
            k = _rotary(qk_ref[0, rows, kk].astype(F32), cos, sin) * k_scale
            v = v_ref[0, rows, vv]
            q_bf = q.astype(BF16)
            scores = lax.dot_general(q_bf, k.astype(BF16), (((1,), (1,)), ((), ())),
                                     preferred_element_type=F32) * dmat_ref[h]
            state = state_ref[h]
            o = _dot(scores.astype(BF16), v) + _dot(q_bf, state.astype(BF16)) * qdec_ref[h]
            state_ref[h] = state * bdec_ref[h] + lax.dot_general(
                (k * kdec_ref[h]).astype(BF16), v, (((0,), (0,)), ((), ())),
                preferred_element_type=F32)
            mu = jnp.mean(o, axis=-1, keepdims=True)
            oc = o - mu
            var = jnp.mean(oc * oc, axis=-1, keepdims=True)
            on = oc * lax.rsqrt(var + EPS) * gn_ref[:, vv]
            gr = gret_ref[0, rows, vv]
            gated_ref[rows, vv] = gr * _sigmoid(gr) * on.astype(BF16)

    y_ret = _dot(gated_ref[...], wo_ref[...])
    ga = gm_ref[0, :, :D_MODEL]
    gb = gm_ref[0, :, D_MODEL:]
    merged = _sigmoid(ga) * yconv_ref[...] + _sigmoid(gb) * y_ret.astype(BF16)
    o_ref[...] = x_ref[...] + _dot(merged, wout_ref[...])


def _retention(proj, proj_meta, y_conv, x2d, tables, gn, wo_bf, wout_bf, w_gate, w_up, w_down,
               batch, seq):
    t = proj.shape[1]
    nb = seq // RET_STEP
    n_steps = batch * nb
    gu_rows, dn_rows = N_EXPERTS * D_MODEL // n_steps, N_EXPERTS * D_EXPERT // n_steps
    assert gu_rows * n_steps == N_EXPERTS * D_MODEL and dn_rows * n_steps == N_EXPERTS * D_EXPERT
    assert gu_rows % (2 * SUBLANES) == 0 and dn_rows % (2 * SUBLANES) == 0
    hq = RET_HEADS * RET_QK_DIM
    hv = RET_HEADS * RET_V_DIM
    assert 2 * hq == hv == 2 * D_MODEL == proj.shape[2]
    slab = lambda s: pl.BlockSpec((1, RET_STEP, hv), lambda b, i: (s, row(b, i), 0))
    mk_col, mv_col = 2 * CONV_DIM // hq + 1, (2 * CONV_DIM + 2 * hq) // hv
    row = lambda b, i: b * nb + i
    const2 = lambda b, i: (0, 0)
    const3 = lambda b, i: (0, 0, 0)
    cos, sin, mcos, msin, dmat, qdec, kdec, mkdec, bdec = tables
    return pl.pallas_call(
        _ret_kernel,
        grid=(batch, nb),
        in_specs=[
            slab(0), slab(1), slab(2), slab(3),
            pl.BlockSpec((RET_STEP, D_MODEL), lambda b, i: (row(b, i), 0)),
            pl.BlockSpec((RET_STEP, D_MODEL), lambda b, i: (row(b, i), 0)),
            pl.BlockSpec((RET_STEP, RET_QK_DIM // 2), lambda b, i: (i, 0)),
            pl.BlockSpec((RET_STEP, RET_QK_DIM // 2), lambda b, i: (i, 0)),
            pl.BlockSpec((N_META, hq), lambda b, i: (0, mk_col)),
            pl.BlockSpec((N_META, hv), lambda b, i: (0, mv_col)),
            pl.BlockSpec((N_META, RET_QK_DIM // 2), const2),
            pl.BlockSpec((N_META, RET_QK_DIM // 2), const2),
            pl.BlockSpec((RET_HEADS, RET_BLOCK, RET_BLOCK), const3),
            pl.BlockSpec((RET_HEADS, RET_BLOCK, 1), const3),
            pl.BlockSpec((RET_HEADS, RET_BLOCK, 1), const3),
            pl.BlockSpec((RET_HEADS, N_META, 1), const3),
            pl.BlockSpec((RET_HEADS, 1, 1), const3),
            pl.BlockSpec((1, hv), const2),
            pl.BlockSpec((hv, D_MODEL), const2),
            pl.BlockSpec((D_MODEL, D_MODEL), const2),
            pl.BlockSpec((gu_rows, D_EXPERT), lambda b, i: (row(b, i), 0)),
            pl.BlockSpec((gu_rows, D_EXPERT), lambda b, i: (row(b, i), 0)),
            pl.BlockSpec((dn_rows, D_MODEL), lambda b, i: (row(b, i), 0)),
        ],
        out_specs=[
            pl.BlockSpec((RET_STEP, D_MODEL), lambda b, i: (row(b, i), 0)),
            pl.BlockSpec((gu_rows, 2 * D_EXPERT), lambda b, i: (row(b, i), 0)),
            pl.BlockSpec((dn_rows, D_MODEL), lambda b, i: (row(b, i), 0)),
        ],
        out_shape=[jax.ShapeDtypeStruct((t, D_MODEL), F32),
                   jax.ShapeDtypeStruct((N_EXPERTS * D_MODEL, 2 * D_EXPERT), BF16),
                   jax.ShapeDtypeStruct((N_EXPERTS * D_EXPERT, D_MODEL), BF16)],
        scratch_shapes=[pltpu.VMEM((RET_HEADS, RET_QK_DIM, RET_V_DIM), F32),
                        pltpu.VMEM((RET_STEP, hv), BF16)],
        compiler_params=pltpu.CompilerParams(
            dimension_semantics=("parallel", "arbitrary"),
            vmem_limit_bytes=PROJ_VMEM_LIMIT),
        name="retention_mix",
    )(proj, proj, proj, proj, y_conv, x2d, cos, sin, proj_meta, proj_meta, mcos, msin,
      dmat, qdec, kdec, mkdec, bdec, gn, wo_bf, wout_bf,
      w_gate.reshape(N_EXPERTS * D_MODEL, D_EXPERT), w_up.reshape(N_EXPERTS * D_MODEL, D_EXPERT),
      w_down.reshape(N_EXPERTS * D_EXPERT, D_MODEL))


def _retention_tables(seq):
    half = RET_QK_DIM // 2
    inv = ROPE_BASE ** (-jnp.arange(half, dtype=F32) / half)
    pos = jnp.arange(N_META + seq, dtype=F32)
    ang = pos[:, None] * inv[None, :]
    cos_all, sin_all = jnp.cos(ang), jnp.sin(ang)
    log_gamma = jnp.log(1.0 - 2.0 ** (-5.0 - jnp.arange(RET_HEADS, dtype=F32)))
    idx = jnp.arange(RET_BLOCK, dtype=F32)
    chunk = jnp.arange(RET_BLOCK, dtype=jnp.int32) // CHUNK
    visible = chunk[None, :] <= chunk[:, None]
    dmat = jnp.where(visible[None],
                     jnp.exp(log_gamma[:, None, None] * jnp.abs(idx[:, None] - idx[None, :])),
                     0.0)
    qdec = jnp.exp(log_gamma[:, None] * (idx + 1.0))[:, :, None]
    kdec = jnp.exp(log_gamma[:, None] * (RET_BLOCK - 1.0 - idx))[:, :, None]
    midx = jnp.arange(N_META, dtype=F32)
    mkdec = jnp.exp(log_gamma[:, None] * (N_META - 1.0 - midx))[:, :, None]
    bdec = jnp.exp(log_gamma * RET_BLOCK)[:, None, None]
    return (cos_all[N_META:], sin_all[N_META:], cos_all[:N_META], sin_all[:N_META],
            dmat, qdec, kdec, mkdec, bdec)


def _route(logits):
    lane = lax.broadcasted_iota(jnp.int32, logits.shape, 1)
    neg = jnp.float32(-jnp.inf)
    big = jnp.int32(ROUTER_LANES)

    def first_max(masked):
        val = jnp.max(masked, axis=-1, keepdims=True)
        idx = jnp.min(jnp.where(masked == val, lane, big), axis=-1, keepdims=True)
        return val, idx

    gmask = lane < N_GROUPS
    gmax, gidx = first_max(jnp.where(gmask, logits, neg))
    denom = jnp.sum(jnp.where(gmask, jnp.exp(logits - gmax), 0.0), axis=-1, keepdims=True)
    p_group = 1.0 / denom
    assert EXPERTS_PER_GROUP & (EXPERTS_PER_GROUP - 1) == 0
    shift = EXPERTS_PER_GROUP.bit_length() - 1
    lane_group = (lane - N_GROUPS) >> shift
    in_group = jnp.where(lane_group == gidx, logits, neg)
    v1, i1 = first_max(in_group)
    v2, i2 = first_max(jnp.where(lane == i1, neg, in_group))
    e2 = jnp.exp(v2 - v1)
    w1 = p_group / (1.0 + e2)
    w2 = p_group * e2 / (1.0 + e2)
    return jnp.where(lane == i1, w1, 0.0) + jnp.where(lane == i2, w2, 0.0), gidx


def _moe_kernel(h_ref, g_ref, wr_ref, br_ref, tri_ref, wgu_ref, wd_ref, gf_ref, o_ref,
                xs_ref, cs_ref, pos_ref, y_ref, seg_ref):
    grp = pl.program_id(1)
    tm = h_ref.shape[0]

    @pl.when(grp == 0)
    def _():
        h = h_ref[...]
        ms = jnp.mean(h * h, axis=-1, keepdims=True)
        u = h * lax.rsqrt(ms + EPS) * g_ref[...]
        u_hi = u.astype(BF16)
        u_lo = (u - u_hi.astype(F32)).astype(BF16)
        hi_part = _dot(u_hi, wr_ref[...])
        lo_part = _dot(u_lo, wr_ref[:, :ROUTER_LANES])
        logits = (hi_part[:, :ROUTER_LANES] + (hi_part[:, ROUTER_LANES:] + lo_part)
                  + br_ref[...])
        comb, gidx = _route(logits)

        lane = lax.broadcasted_iota(jnp.int32, (tm, ROUTER_LANES), 1)
        onehot = jnp.where(lane == gidx, 1.0, 0.0)
        counts = jnp.sum(onehot, axis=0, keepdims=True)
        row = lax.broadcasted_iota(jnp.int32, (tm, tm), 0)
        prefix = _dot(tri_ref[...], onehot.astype(BF16))
        lane_row = lax.broadcasted_iota(jnp.int32, (1, ROUTER_LANES), 1)
        start = jnp.int32(0)
        starts = jnp.zeros((1, ROUTER_LANES), F32)
        for gg in range(N_GROUPS):
            seg_ref[gg] = start
            starts = starts + jnp.where(lane_row == gg, start.astype(F32), 0.0)
            start = start + jnp.sum(jnp.where(lane_row == gg, counts, 0.0)).astype(jnp.int32)
        seg_ref[N_GROUPS] = start
        pos = jnp.sum(onehot * (prefix + starts), axis=-1, keepdims=True)
        pos_lanes = jnp.broadcast_to(pos, (tm, ROUTER_LANES))
        pos_ref[...] = pos_lanes
        pos_row = pos_lanes.T[0:1, :].astype(jnp.int32)
        perm = jnp.where(row == pos_row, 1.0, 0.0).astype(BF16)
        c_hi = comb.astype(BF16)
        c_lo = (comb - c_hi.astype(F32)).astype(BF16)
        moved = _dot(perm, jnp.concatenate([u_hi, c_hi, c_lo], axis=-1))
        xs_ref[...] = moved[:, :D_MODEL].astype(BF16)
        cs_ref[...] = (moved[:, D_MODEL:D_MODEL + ROUTER_LANES]
                       + moved[:, D_MODEL + ROUTER_LANES:])
        y_ref[...] = jnp.zeros_like(y_ref)

    seg_lo, seg_hi = seg_ref[grp], seg_ref[grp + 1]
    blk_lo = seg_lo // MOE_BLOCK
    blk_hi = jnp.where(seg_hi > seg_lo, (seg_hi + MOE_BLOCK - 1) // MOE_BLOCK, blk_lo)

    def experts(blk, n_blocks):
        n_rows = n_blocks * MOE_BLOCK
        rows = pl.ds(pl.multiple_of(blk * MOE_BLOCK, MOE_BLOCK), n_rows)
        xb = xs_ref[rows, :]
        cb = cs_ref[rows, :]
        lane = lax.broadcasted_iota(jnp.int32, cb.shape, 1)
        acc = jnp.zeros((n_rows, D_MODEL), F32)
        for e in range(EXPERTS_PER_GROUP):
            expert_lane = N_GROUPS + grp * EXPERTS_PER_GROUP + e
            w_e = jnp.sum(jnp.where(lane == expert_lane, cb, 0.0), axis=-1, keepdims=True)
            gu = _dot(xb, wgu_ref[e])
            gate, up = gu[:, :D_EXPERT], gu[:, D_EXPERT:]
            hid = gate * _sigmoid(gate) * up * w_e
            acc = acc + _dot(hid.astype(BF16), wd_ref[e])
        y_ref[rows, :] += acc

    n_blk = blk_hi - blk_lo

    def pair_body(p, carry):
        experts(blk_lo + 2 * p, 2)
        return carry

    lax.fori_loop(0, n_blk // 2, pair_body, 0)

    @pl.when(n_blk % 2 == 1)
    def _():
        experts(blk_hi - 1, 1)

    @pl.when(grp == N_GROUPS - 1)
    def _():
        pos = pos_ref[:, 0:1].astype(jnp.int32)
        col = lax.broadcasted_iota(jnp.int32, (tm, tm), 1)
        unperm = jnp.where(col == pos, 1.0, 0.0).astype(BF16)
        h2 = h_ref[...] + _dot(unperm, y_ref[...].astype(BF16))
        ms = jnp.mean(h2 * h2, axis=-1, keepdims=True)
        o_ref[...] = h2 * lax.rsqrt(ms + EPS) * gf_ref[...]


def _moe(h1, g, wr2, br, wgu_bf, wd_bf, gf, tm):
    t = h1.shape[0]
    const2 = lambda i, e: (0, 0)
    token = jnp.arange(tm, dtype=jnp.int32)
    earlier = (token[None, :] < token[:, None]).astype(BF16)
    return pl.pallas_call(
        _moe_kernel,
        grid=(t // tm, N_GROUPS),
        in_specs=[
            pl.BlockSpec((tm, D_MODEL), lambda i, e: (i, 0)),
            pl.BlockSpec((1, D_MODEL), const2),
            pl.BlockSpec((D_MODEL, 2 * ROUTER_LANES), const2),
            pl.BlockSpec((1, ROUTER_LANES), const2),
            pl.BlockSpec((tm, tm), const2),
            pl.BlockSpec((EXPERTS_PER_GROUP, D_MODEL, 2 * D_EXPERT), lambda i, e: (e, 0, 0)),
            pl.BlockSpec((EXPERTS_PER_GROUP, D_EXPERT, D_MODEL), lambda i, e: (e, 0, 0)),
            pl.BlockSpec((1, D_MODEL), const2),
        ],
        out_specs=pl.BlockSpec((tm, D_MODEL), lambda i, e: (i, 0)),
        out_shape=jax.ShapeDtypeStruct((t, D_MODEL), F32),
        scratch_shapes=[pltpu.VMEM((tm, D_MODEL), BF16),
                        pltpu.VMEM((tm, ROUTER_LANES), F32),
                        pltpu.VMEM((tm, ROUTER_LANES), F32),
                        pltpu.VMEM((tm, D_MODEL), F32),
                        pltpu.SMEM((SUBLANES,), jnp.int32)],
        compiler_params=pltpu.CompilerParams(
            dimension_semantics=("parallel", "arbitrary"),
            vmem_limit_bytes=MOE_VMEM_LIMIT),
        name="hier_moe",
    )(h1, g, wr2, br, earlier, wgu_bf, wd_bf, gf)


def _split_bf16(w):
    hi = w.astype(BF16)
    lo = (w - hi.astype(F32)).astype(BF16)
    return jnp.concatenate([hi, lo], axis=-1)


def kernel(x, meta_tokens, norm_mix_g, w_in, conv_dw_w, conv_dw_b, conv_ln_g, conv_ln_b,
           conv_pw_w, ret_gn_g, ret_w_o, w_out, norm_ffn_g, w_group_router, b_group_router,
           w_expert_router, b_expert_router, w_expert_gate, w_expert_up, w_expert_down,
           norm_final_g):
    batch, seq, d = x.shape
    assert d == D_MODEL and seq % RET_STEP == 0 and w_in.shape[0] == 1
    t = batch * seq
    x2d = x.reshape(t, d)
    row = lambda v: v.reshape(1, -1)

    proj_meta, w_in_bf = _meta_proj(meta_tokens, row(norm_mix_g[0]), w_in[0], IN_PROJ_COLS)
    proj, y_conv = _proj_conv(x2d, row(norm_mix_g[0]), w_in_bf, proj_meta, conv_dw_w[0],
                              row(conv_dw_b[0]), row(conv_ln_g[0]), row(conv_ln_b[0]),
                              conv_pw_w[0].astype(BF16), seq, PROJ_TILE)
    h1, w_gu, w_dn = _retention(proj, proj_meta, y_conv, x2d, _retention_tables(seq),
                                row(ret_gn_g[0]), ret_w_o[0].astype(BF16), w_out[0].astype(BF16),
                                w_expert_gate[0], w_expert_up[0], w_expert_down[0], batch, seq)
    w_gu = w_gu.reshape(N_EXPERTS, D_MODEL, 2 * D_EXPERT)
    w_dn = w_dn.reshape(N_EXPERTS, D_EXPERT, D_MODEL)

    w_router = jnp.concatenate([w_group_router[0], w_expert_router[0]], axis=1)
    w_router = jnp.pad(w_router, ((0, 0), (0, ROUTER_LANES - w_router.shape[1])))
    b_router = jnp.concatenate([b_group_router[0], b_expert_router[0]])
    b_router = jnp.pad(b_router, (0, ROUTER_LANES - b_router.shape[0])).reshape(1, -1)
    out = _moe(h1, row(norm_ffn_g[0]), _split_bf16(w_router), b_router, w_gu, w_dn,
               row(norm_final_g), min(1024, t))
    return out.reshape(batch, seq, d)
```

```python
import functools

import jax
import jax.numpy as jnp
from jax import lax
from jax.experimental import pallas as pl
from jax.experimental.pallas import tpu as pltpu

D_MODEL = 1024
CHUNK = 64
N_META = 16
CONV_DIM = 1024
CONV_WIDTH = 31
RET_HEADS = 4
RET_QK_DIM = 256
RET_V_DIM = 512
ROPE_BASE = 10000.0
N_GROUPS = 4
EXPERTS_PER_GROUP = 4
N_EXPERTS = N_GROUPS * EXPERTS_PER_GROUP
D_EXPERT = 512
EPS = 1e-6
D_IN = 2 * CONV_DIM + 2 * RET_HEADS * RET_QK_DIM + 2 * RET_HEADS * RET_V_DIM + 2 * D_MODEL

LANES = 128
SUBLANES = 8
CONV_HALO = 32
IN_PROJ_COLS = 2048
PROJ_TILE = 1024
CONV_TILE = 256
CONV_ROWS_PER_ITER = 4
RET_BLOCK = 256
RET_STEP = 512
ROUTER_LANES = LANES
MOE_BLOCK = 128
VMEM_LIMIT = 48 * 1024 * 1024
PROJ_VMEM_LIMIT = 56 * 1024 * 1024
MOE_VMEM_LIMIT = 60 * 1024 * 1024

F32 = jnp.float32
BF16 = jnp.bfloat16


def _sigmoid(x):
    return 1.0 / (1.0 + jnp.exp(-x))


def _dot(a, b):
    return jnp.dot(a, b, preferred_element_type=F32)


def _meta_proj_kernel(x_ref, g_ref, w_ref, o_ref, wbf_ref, u_ref):
    @pl.when(pl.program_id(0) == 0)
    def _():
        x = x_ref[...]
        ms = jnp.mean(x * x, axis=-1, keepdims=True)
        u_ref[...] = (x * lax.rsqrt(ms + EPS) * g_ref[...]).astype(BF16)

    w = w_ref[...].astype(BF16)
    wbf_ref[...] = w
    o_ref[...] = _dot(u_ref[...], w).astype(o_ref.dtype)


def _meta_proj(meta, g, w, tn):
    t, d = meta.shape
    n = w.shape[1]
    return pl.pallas_call(
        _meta_proj_kernel,
        grid=(n // tn,),
        in_specs=[
            pl.BlockSpec((t, d), lambda j: (0, 0)),
            pl.BlockSpec((1, d), lambda j: (0, 0)),
            pl.BlockSpec((d, tn), lambda j: (0, j)),
        ],
        out_specs=[pl.BlockSpec((t, tn), lambda j: (0, j)),
                   pl.BlockSpec((d, tn), lambda j: (0, j))],
        out_shape=[jax.ShapeDtypeStruct((t, n), BF16),
                   jax.ShapeDtypeStruct((d, n), BF16)],
        scratch_shapes=[pltpu.VMEM((t, d), BF16)],
        compiler_params=pltpu.CompilerParams(
            dimension_semantics=("arbitrary",),
            vmem_limit_bytes=VMEM_LIMIT),
        name="meta_proj",
    )(meta, g, w)


def _conv_shifts(win_ref, shift_ref):
    span = CONV_TILE + CONV_HALO - SUBLANES
    for s in range(1, SUBLANES):
        shift_ref[s - 1, 0:span, :] = win_ref[s:s + span, :]


def _conv_taps(cb, r0, win_ref, shift_ref, cbuf_ref, wdw_ref, bdw_ref):
    def group(m):
        return slice(r0 + m * SUBLANES, r0 + (m + 1) * SUBLANES)

    first = CONV_HALO - (CONV_WIDTH - 1)
    cols = pl.ds(pl.multiple_of(cb * LANES, LANES), LANES)
    bias = jnp.broadcast_to(bdw_ref[:, cols], (SUBLANES, LANES))
    accs = [bias] * CONV_ROWS_PER_ITER
    for shift in range(SUBLANES):
        js = [j for j in range(CONV_WIDTH) if (first + j) % SUBLANES == shift]
        tiles = [(first + j) // SUBLANES for j in js]
        taps = [jnp.broadcast_to(wdw_ref[j:j + 1, cols], (SUBLANES, LANES)) for j in js]
        wins = {}
        for m in range(min(tiles), max(tiles) + CONV_ROWS_PER_ITER):
            wins[m] = (win_ref[group(m), cols] if shift == 0
                       else shift_ref[shift - 1, group(m), cols])
        for c in range(CONV_ROWS_PER_ITER):
            for tap, m in zip(taps, tiles):
                accs[c] = accs[c] + tap * wins[m + c]
    for c in range(CONV_ROWS_PER_ITER):
        cbuf_ref[group(c), cols] = accs[c]


def _proj_conv_kernel(x_ref, g_ref, w_ref, ma_ref, mg_ref, wdw_ref, bdw_ref, lng_ref, lnb_ref,
                      wpw_ref, proj_ref, yconv_ref, u_ref, hbuf_ref, win_ref, shift_ref,
                      cbuf_ref, *, tiles_per_seq):
    i, j = pl.program_id(0), pl.program_id(1)
    tm = x_ref.shape[0]

    @pl.when(j == 0)
    def _():
        x = x_ref[...]
        ms = jnp.mean(x * x, axis=-1, keepdims=True)
        u = (x * lax.rsqrt(ms + EPS) * g_ref[...]).astype(BF16)
        u_ref[...] = u

        @pl.when(i % tiles_per_seq == 0)
        def _():
            hbuf_ref[0:CONV_HALO - N_META, :] = jnp.zeros((CONV_HALO - N_META, CONV_DIM), F32)
            ma = ma_ref[...].astype(F32)
            mg = mg_ref[...].astype(F32)
            hbuf_ref[CONV_HALO - N_META:CONV_HALO, :] = ma * _sigmoid(mg)

        @pl.when(i % tiles_per_seq != 0)
        def _():
            hbuf_ref[0:CONV_HALO, :] = hbuf_ref[tm:tm + CONV_HALO, :]

        glu = _dot(u, w_ref[...])
        hbuf_ref[CONV_HALO:CONV_HALO + tm, :] = glu[:, :CONV_DIM] * _sigmoid(glu[:, CONV_DIM:])

    @pl.when(j > 0)
    def _():
        base = pl.multiple_of((j - 1) * CONV_TILE, CONV_TILE)
        win_ref[...] = hbuf_ref[pl.ds(base, CONV_HALO + CONV_TILE), :]
        _conv_shifts(win_ref, shift_ref)
        proj_ref[0] = _dot(u_ref[...], w_ref[...]).astype(proj_ref.dtype)

        def block_body(cb, carry):
            for r0 in range(0, CONV_TILE, SUBLANES * CONV_ROWS_PER_ITER):
                _conv_taps(cb, r0, win_ref, shift_ref, cbuf_ref, wdw_ref, bdw_ref)
            return carry

        lax.fori_loop(0, CONV_DIM // LANES, block_body, 0)
        c = cbuf_ref[...]
        mu = jnp.mean(c, axis=-1, keepdims=True)
        cc = c - mu
        var = jnp.mean(cc * cc, axis=-1, keepdims=True)
        y = (cc * lax.rsqrt(var + EPS) * lng_ref[...] + lnb_ref[...]).astype(BF16)
        y = y * _sigmoid(y)
        yconv_ref[pl.ds(base, CONV_TILE), :] = _dot(y, wpw_ref[...]).astype(yconv_ref.dtype)


def _proj_conv(x2d, g, w_bf, proj_meta, wdw, bdw, lng, lnb, wpw_bf, seq, tm):
    t, d = x2d.shape
    n_steps = D_IN // IN_PROJ_COLS
    assert IN_PROJ_COLS == 2 * CONV_DIM and (n_steps - 1) * CONV_TILE == tm and seq % tm == 0
    const = lambda i, j: (0, 0)
    return pl.pallas_call(
        functools.partial(_proj_conv_kernel, tiles_per_seq=seq // tm),
        grid=(t // tm, n_steps),
        in_specs=[
            pl.BlockSpec((tm, d), lambda i, j: (i, 0)),
            pl.BlockSpec((1, d), const),
            pl.BlockSpec((d, IN_PROJ_COLS), lambda i, j: (0, j)),
            pl.BlockSpec((N_META, CONV_DIM), lambda i, j: (0, 0)),
            pl.BlockSpec((N_META, CONV_DIM), lambda i, j: (0, 1)),
            pl.BlockSpec((CONV_WIDTH, CONV_DIM), const),
            pl.BlockSpec((1, CONV_DIM), const),
            pl.BlockSpec((1, CONV_DIM), const),
            pl.BlockSpec((1, CONV_DIM), const),
            pl.BlockSpec((CONV_DIM, D_MODEL), const),
        ],
        out_specs=[
            pl.BlockSpec((1, tm, IN_PROJ_COLS), lambda i, j: (jnp.maximum(j - 1, 0), i, 0)),
            pl.BlockSpec((tm, D_MODEL), lambda i, j: (i, 0)),
        ],
        out_shape=[jax.ShapeDtypeStruct((n_steps - 1, t, IN_PROJ_COLS), BF16),
                   jax.ShapeDtypeStruct((t, D_MODEL), BF16)],
        scratch_shapes=[pltpu.VMEM((tm, d), BF16),
                        pltpu.VMEM((CONV_HALO + tm, CONV_DIM), F32),
                        pltpu.VMEM((CONV_HALO + CONV_TILE, CONV_DIM), F32),
                        pltpu.VMEM((SUBLANES - 1, CONV_HALO + CONV_TILE, CONV_DIM), F32),
                        pltpu.VMEM((CONV_TILE, CONV_DIM), F32)],
        compiler_params=pltpu.CompilerParams(
            dimension_semantics=("arbitrary", "arbitrary"),
            vmem_limit_bytes=PROJ_VMEM_LIMIT),
        name="proj_conv",
    )(x2d, g, w_bf, proj_meta, proj_meta, wdw, bdw, lng, lnb, wpw_bf)


def _rotary(x, cos, sin):
    half = x.shape[-1] // 2
    x1, x2 = x[:, :half], x[:, half:]
    return jnp.concatenate([x1 * cos - x2 * sin, x2 * cos + x1 * sin], axis=-1)


def _ret_kernel(qk_ref, v_ref, gret_ref, gm_ref, yconv_ref, x_ref, cos_ref, sin_ref,
                mk_ref, mv_ref, mcos_ref, msin_ref, dmat_ref, qdec_ref, kdec_ref, mkdec_ref,
                bdec_ref, gn_ref, wo_ref, wout_ref, eg_ref, eu_ref, ed_ref,
                o_ref, egu_ref, edn_ref, state_ref, gated_ref):
    i = pl.program_id(1)
    k_scale = RET_QK_DIM ** -0.5

    egu_ref[:, :D_EXPERT] = eg_ref[...].astype(BF16)
    egu_ref[:, D_EXPERT:] = eu_ref[...].astype(BF16)
    edn_ref[...] = ed_ref[...].astype(BF16)

    @pl.when(i == 0)
    def _():
        mcos, msin = mcos_ref[...], msin_ref[...]
        for h in range(RET_HEADS):
            mk = mk_ref[:, h * RET_QK_DIM:(h + 1) * RET_QK_DIM].astype(F32)
            mk = _rotary(mk, mcos, msin) * k_scale * mkdec_ref[h]
            mv = mv_ref[:, h * RET_V_DIM:(h + 1) * RET_V_DIM]
            state_ref[h] = lax.dot_general(mk.astype(BF16), mv, (((0,), (0,)), ((), ())),
                                           preferred_element_type=F32)

    for sb in range(RET_STEP // RET_BLOCK):
        rows = slice(sb * RET_BLOCK, (sb + 1) * RET_BLOCK)
        cos, sin = cos_ref[rows, :], sin_ref[rows, :]
        for h in range(RET_HEADS):
            qq = slice(h * RET_QK_DIM, (h + 1) * RET_QK_DIM)
            kk = slice((RET_HEADS + h) * RET_QK_DIM, (RET_HEADS + h + 1) * RET_QK_DIM)
            vv = slice(h * RET_V_DIM, (h + 1) * RET_V_DIM)
            q = _rotary(qk_ref[0, rows, qq].astype(F32), cos, sin)
            k = _rotary(qk_ref[0, rows, kk].astype(F32), cos, sin) * k_scale
            v = v_ref[0, rows, vv]
            q_bf = q.astype(BF16)
            scores = lax.dot_general(q_bf, k.astype(BF16), (((1,), (1,)), ((), ())),
                                     preferred_element_type=F32) * dmat_ref[h]
            state = state_ref[h]
            o = _dot(scores.astype(BF16), v) + _dot(q_bf, state.astype(BF16)) * qdec_ref[h]
            state_ref[h] = state * bdec_ref[h] + lax.dot_general(
                (k * kdec_ref[h]).astype(BF16), v, (((0,), (0,)), ((), ())),
                preferred_element_type=F32)
            mu = jnp.mean(o, axis=-1, keepdims=True)
            oc = o - mu
            var = jnp.mean(oc * oc, axis=-1, keepdims=True)
            on = oc * lax.rsqrt(var + EPS) * gn_ref[:, vv]
            gr = gret_ref[0, rows, vv]
            gated_ref[rows, vv] = gr * _sigmoid(gr) * on.astype(BF16)

    y_ret = _dot(gated_ref[...], wo_ref[...])
    ga = gm_ref[0, :, :D_MODEL]
    gb = gm_ref[0, :, D_MODEL:]
    merged = _sigmoid(ga) * yconv_ref[...] + _sigmoid(gb) * y_ret.astype(BF16)
    o_ref[...] = x_ref[...] + _dot(merged, wout_ref[...])


def _retention(proj, proj_meta, y_conv, x2d, tables, gn, wo_bf, wout_bf, w_gate, w_up, w_down,
               batch, seq):
    t = proj.shape[1]
    nb = seq // RET_STEP
    n_steps = batch * nb
    gu_rows, dn_rows = N_EXPERTS * D_MODEL // n_steps, N_EXPERTS * D_EXPERT // n_steps
    assert gu_rows * n_steps == N_EXPERTS * D_MODEL and dn_rows * n_steps == N_EXPERTS * D_EXPERT
    assert gu_rows % (2 * SUBLANES) == 0 and dn_rows % (2 * SUBLANES) == 0
    hq = RET_HEADS * RET_QK_DIM
    hv = RET_HEADS * RET_V_DIM
    assert 2 * hq == hv == 2 * D_MODEL == proj.shape[2]
    slab = lambda s: pl.BlockSpec((1, RET_STEP, hv), lambda b, i: (s, row(b, i), 0))
    mk_col, mv_col = 2 * CONV_DIM // hq + 1, (2 * CONV_DIM + 2 * hq) // hv
    row = lambda b, i: b * nb + i
    const2 = lambda b, i: (0, 0)
    const3 = lambda b, i: (0, 0, 0)
    cos, sin, mcos, msin, dmat, qdec, kdec, mkdec, bdec = tables
    return pl.pallas_call(
        _ret_kernel,
        grid=(batch, nb),
        in_specs=[
            slab(0), slab(1), slab(2), slab(3),
            pl.BlockSpec((RET_STEP, D_MODEL), lambda b, i: (row(b, i), 0)),
            pl.BlockSpec((RET_STEP, D_MODEL), lambda b, i: (row(b, i), 0)),
            pl.BlockSpec((RET_STEP, RET_QK_DIM // 2), lambda b, i: (i, 0)),
            pl.BlockSpec((RET_STEP, RET_QK_DIM // 2), lambda b, i: (i, 0)),
            pl.BlockSpec((N_META, hq), lambda b, i: (0, mk_col)),
            pl.BlockSpec((N_META, hv), lambda b, i: (0, mv_col)),
            pl.BlockSpec((N_META, RET_QK_DIM // 2), const2),
            pl.BlockSpec((N_META, RET_QK_DIM // 2), const2),
            pl.BlockSpec((RET_HEADS, RET_BLOCK, RET_BLOCK), const3),
            pl.BlockSpec((RET_HEADS, RET_BLOCK, 1), const3),
            pl.BlockSpec((RET_HEADS, RET_BLOCK, 1), const3),
            pl.BlockSpec((RET_HEADS, N_META, 1), const3),
            pl.BlockSpec((RET_HEADS, 1, 1), const3),
            pl.BlockSpec((1, hv), const2),
            pl.BlockSpec((hv, D_MODEL), const2),
            pl.BlockSpec((D_MODEL, D_MODEL), const2),
            pl.BlockSpec((gu_rows, D_EXPERT), lambda b, i: (row(b, i), 0)),
            pl.BlockSpec((gu_rows, D_EXPERT), lambda b, i: (row(b, i), 0)),
            pl.BlockSpec((dn_rows, D_MODEL), lambda b, i: (row(b, i), 0)),
        ],
        out_specs=[
            pl.BlockSpec((RET_STEP, D_MODEL), lambda b, i: (row(b, i), 0)),
            pl.BlockSpec((gu_rows, 2 * D_EXPERT), lambda b, i: (row(b, i), 0)),
            pl.BlockSpec((dn_rows, D_MODEL), lambda b, i: (row(b, i), 0)),
        ],
        out_shape=[jax.ShapeDtypeStruct((t, D_MODEL), F32),
                   jax.ShapeDtypeStruct((N_EXPERTS * D_MODEL, 2 * D_EXPERT), BF16),
                   jax.ShapeDtypeStruct((N_EXPERTS * D_EXPERT, D_MODEL), BF16)],
        scratch_shapes=[pltpu.VMEM((RET_HEADS, RET_QK_DIM, RET_V_DIM), F32),
                        pltpu.VMEM((RET_STEP, hv), BF16)],
        compiler_params=pltpu.CompilerParams(
            dimension_semantics=("parallel", "arbitrary"),
            vmem_limit_bytes=PROJ_VMEM_LIMIT),
        name="retention_mix",
    )(proj, proj, proj, proj, y_conv, x2d, cos, sin, proj_meta, proj_meta, mcos, msin,
      dmat, qdec, kdec, mkdec, bdec, gn, wo_bf, wout_bf,
      w_gate.reshape(N_EXPERTS * D_MODEL, D_EXPERT), w_up.reshape(N_EXPERTS * D_MODEL, D_EXPERT),
      w_down.reshape(N_EXPERTS * D_EXPERT, D_MODEL))


def _retention_tables(seq):
    half = RET_QK_DIM // 2
    inv = ROPE_BASE ** (-jnp.arange(half, dtype=F32) / half)
    pos = jnp.arange(N_META + seq, dtype=F32)
    ang = pos[:, None] * inv[None, :]
    cos_all, sin_all = jnp.cos(ang), jnp.sin(ang)
    log_gamma = jnp.log(1.0 - 2.0 ** (-5.0 - jnp.arange(RET_HEADS, dtype=F32)))
    idx = jnp.arange(RET_BLOCK, dtype=F32)
    chunk = jnp.arange(RET_BLOCK, dtype=jnp.int32) // CHUNK
    visible = chunk[None, :] <= chunk[:, None]
    dmat = jnp.where(visible[None],
                     jnp.exp(log_gamma[:, None, None] * jnp.abs(idx[:, None] - idx[None, :])),
                     0.0)
    qdec = jnp.exp(log_gamma[:, None] * (idx + 1.0))[:, :, None]
    kdec = jnp.exp(log_gamma[:, None] * (RET_BLOCK - 1.0 - idx))[:, :, None]
    midx = jnp.arange(N_META, dtype=F32)
    mkdec = jnp.exp(log_gamma[:, None] * (N_META - 1.0 - midx))[:, :, None]
    bdec = jnp.exp(log_gamma * RET_BLOCK)[:, None, None]
    return (cos_all[N_META:], sin_all[N_META:], cos_all[:N_META], sin_all[:N_META],
            dmat, qdec, kdec, mkdec, bdec)


def _route(logits):
    lane = lax.broadcasted_iota(jnp.int32, logits.shape, 1)
    neg = jnp.float32(-jnp.inf)
    big = jnp.int32(ROUTER_LANES)

    def first_max(masked):
        val = jnp.max(masked, axis=-1, keepdims=True)
        idx = jnp.min(jnp.where(masked == val, lane, big), axis=-1, keepdims=True)
        return val, idx

    gmask = lane < N_GROUPS
    gmax, gidx = first_max(jnp.where(gmask, logits, neg))
    denom = jnp.sum(jnp.where(gmask, jnp.exp(logits - gmax), 0.0), axis=-1, keepdims=True)
    p_group = 1.0 / denom
    assert EXPERTS_PER_GROUP & (EXPERTS_PER_GROUP - 1) == 0
    shift = EXPERTS_PER_GROUP.bit_length() - 1
    lane_group = (lane - N_GROUPS) >> shift
    in_group = jnp.where(lane_group == gidx, logits, neg)
    v1, i1 = first_max(in_group)
    v2, i2 = first_max(jnp.where(lane == i1, neg, in_group))
    e2 = jnp.exp(v2 - v1)
    w1 = p_group / (1.0 + e2)
    w2 = p_group * e2 / (1.0 + e2)
    return jnp.where(lane == i1, w1, 0.0) + jnp.where(lane == i2, w2, 0.0), gidx


def _moe_kernel(h_ref, g_ref, wr_ref, br_ref, tri_ref, wgu_ref, wd_ref, gf_ref, o_ref,
                xs_ref, cs_ref, pos_ref, y_ref, seg_ref):
    grp = pl.program_id(1)
    tm = h_ref.shape[0]

    @pl.when(grp == 0)
    def _():
        h = h_ref[...]
        ms = jnp.mean(h * h, axis=-1, keepdims=True)
        u = h * lax.rsqrt(ms + EPS) * g_ref[...]
        u_hi = u.astype(BF16)
        u_lo = (u - u_hi.astype(F32)).astype(BF16)
        hi_part = _dot(u_hi, wr_ref[...])
        lo_part = _dot(u_lo, wr_ref[:, :ROUTER_LANES])
        logits = (hi_part[:, :ROUTER_LANES] + (hi_part[:, ROUTER_LANES:] + lo_part)
                  + br_ref[...])
        comb, gidx = _route(logits)

        lane = lax.broadcasted_iota(jnp.int32, (tm, ROUTER_LANES), 1)
        onehot = jnp.where(lane == gidx, 1.0, 0.0)
        counts = jnp.sum(onehot, axis=0, keepdims=True)
        row = lax.broadcasted_iota(jnp.int32, (tm, tm), 0)
        prefix = _dot(tri_ref[...], onehot.astype(BF16))
        lane_row = lax.broadcasted_iota(jnp.int32, (1, ROUTER_LANES), 1)
        start = jnp.int32(0)
        starts = jnp.zeros((1, ROUTER_LANES), F32)
        for gg in range(N_GROUPS):
            seg_ref[gg] = start
            starts = starts + jnp.where(lane_row == gg, start.astype(F32), 0.0)
            start = start + jnp.sum(jnp.where(lane_row == gg, counts, 0.0)).astype(jnp.int32)
        seg_ref[N_GROUPS] = start
        pos = jnp.sum(onehot * (prefix + starts), axis=-1, keepdims=True)
        pos_lanes = jnp.broadcast_to(pos, (tm, ROUTER_LANES))
        pos_ref[...] = pos_lanes
        pos_row = pos_lanes.T[0:1, :].astype(jnp.int32)
        perm = jnp.where(row == pos_row, 1.0, 0.0).astype(BF16)
        c_hi = comb.astype(BF16)
        c_lo = (comb - c_hi.astype(F32)).astype(BF16)
        moved = _dot(perm, jnp.concatenate([u_hi, c_hi, c_lo], axis=-1))
        xs_ref[...] = moved[:, :D_MODEL].astype(BF16)
        cs_ref[...] = (moved[:, D_MODEL:D_MODEL + ROUTER_LANES]
                       + moved[:, D_MODEL + ROUTER_LANES:])
        y_ref[...] = jnp.zeros_like(y_ref)

    seg_lo, seg_hi = seg_ref[grp], seg_ref[grp + 1]
    blk_lo = seg_lo // MOE_BLOCK
    blk_hi = jnp.where(seg_hi > seg_lo, (seg_hi + MOE_BLOCK - 1) // MOE_BLOCK, blk_lo)

    def experts(blk, n_blocks):
        n_rows = n_blocks * MOE_BLOCK
        rows = pl.ds(pl.multiple_of(blk * MOE_BLOCK, MOE_BLOCK), n_rows)
        xb = xs_ref[rows, :]
        cb = cs_ref[rows, :]
        lane = lax.broadcasted_iota(jnp.int32, cb.shape, 1)
        acc = jnp.zeros((n_rows, D_MODEL), F32)
        for e in range(EXPERTS_PER_GROUP):
            expert_lane = N_GROUPS + grp * EXPERTS_PER_GROUP + e
            w_e = jnp.sum(jnp.where(lane == expert_lane, cb, 0.0), axis=-1, keepdims=True)
            gu = _dot(xb, wgu_ref[e])
            gate, up = gu[:, :D_EXPERT], gu[:, D_EXPERT:]
            hid = gate * _sigmoid(gate) * up * w_e
            acc = acc + _dot(hid.astype(BF16), wd_ref[e])
        y_ref[rows, :] += acc

    n_blk = blk_hi - blk_lo
    odd = n_blk % 2 == 1
    n_pairs = jnp.where(odd, jnp.maximum(n_blk - 3, 0), n_blk) // 2

    def pair_body(p, carry):
        experts(blk_lo + 2 * p, 2)
        return carry

    lax.fori_loop(0, n_pairs, pair_body, 0)

    @pl.when(jnp.logical_and(odd, n_blk >= 3))
    def _():
        experts(blk_hi - 3, 3)

    @pl.when(n_blk == 1)
    def _():
        experts(blk_lo, 1)

    @pl.when(grp == N_GROUPS - 1)
    def _():
        pos = pos_ref[:, 0:1].astype(jnp.int32)
        col = lax.broadcasted_iota(jnp.int32, (tm, tm), 1)
        unperm = jnp.where(col == pos, 1.0, 0.0).astype(BF16)
        h2 = h_ref[...] + _dot(unperm, y_ref[...].astype(BF16))
        ms = jnp.mean(h2 * h2, axis=-1, keepdims=True)
        o_ref[...] = h2 * lax.rsqrt(ms + EPS) * gf_ref[...]


def _moe(h1, g, wr2, br, wgu_bf, wd_bf, gf, tm):
    t = h1.shape[0]
    const2 = lambda i, e: (0, 0)
    token = jnp.arange(tm, dtype=jnp.int32)
    earlier = (token[None, :] < token[:, None]).astype(BF16)
    return pl.pallas_call(
        _moe_kernel,
        grid=(t // tm, N_GROUPS),
        in_specs=[
            pl.BlockSpec((tm, D_MODEL), lambda i, e: (i, 0)),
            pl.BlockSpec((1, D_MODEL), const2),
            pl.BlockSpec((D_MODEL, 2 * ROUTER_LANES), const2),
            pl.BlockSpec((1, ROUTER_LANES), const2),
            pl.BlockSpec((tm, tm), const2),
            pl.BlockSpec((EXPERTS_PER_GROUP, D_MODEL, 2 * D_EXPERT), lambda i, e: (e, 0, 0)),
            pl.BlockSpec((EXPERTS_PER_GROUP, D_EXPERT, D_MODEL), lambda i, e: (e, 0, 0)),
            pl.BlockSpec((1, D_MODEL), const2),
        ],
        out_specs=pl.BlockSpec((tm, D_MODEL), lambda i, e: (i, 0)),
        out_shape=jax.ShapeDtypeStruct((t, D_MODEL), F32),
        scratch_shapes=[pltpu.VMEM((tm, D_MODEL), BF16),
                        pltpu.VMEM((tm, ROUTER_LANES), F32),
                        pltpu.VMEM((tm, ROUTER_LANES), F32),
                        pltpu.VMEM((tm, D_MODEL), F32),
                        pltpu.SMEM((SUBLANES,), jnp.int32)],
        compiler_params=pltpu.CompilerParams(
            dimension_semantics=("parallel", "arbitrary"),
            vmem_limit_bytes=MOE_VMEM_LIMIT),
        name="hier_moe",
    )(h1, g, wr2, br, earlier, wgu_bf, wd_bf, gf)


def _split_bf16(w):
    hi = w.astype(BF16)
    lo = (w - hi.astype(F32)).astype(BF16)
    return jnp.concatenate([hi, lo], axis=-1)


def kernel(x, meta_tokens, norm_mix_g, w_in, conv_dw_w, conv_dw_b, conv_ln_g, conv_ln_b,
           conv_pw_w, ret_gn_g, ret_w_o, w_out, norm_ffn_g, w_group_router, b_group_router,
           w_expert_router, b_expert_router, w_expert_gate, w_expert_up, w_expert_down,
           norm_final_g):
    batch, seq, d = x.shape
    assert d == D_MODEL and seq % RET_STEP == 0 and w_in.shape[0] == 1
    t = batch * seq
    x2d = x.reshape(t, d)
    row = lambda v: v.reshape(1, -1)

    proj_meta, w_in_bf = _meta_proj(meta_tokens, row(norm_mix_g[0]), w_in[0], IN_PROJ_COLS)
    proj, y_conv = _proj_conv(x2d, row(norm_mix_g[0]), w_in_bf, proj_meta, conv_dw_w[0],
                              row(conv_dw_b[0]), row(conv_ln_g[0]), row(conv_ln_b[0]),
                              conv_pw_w[0].astype(BF16), seq, PROJ_TILE)
    h1, w_gu, w_dn = _retention(proj, proj_meta, y_conv, x2d, _retention_tables(seq),
                                row(ret_gn_g[0]), ret_w_o[0].astype(BF16), w_out[0].astype(BF16),
                                w_expert_gate[0], w_expert_up[0], w_expert_down[0], batch, seq)
    w_gu = w_gu.reshape(N_EXPERTS, D_MODEL, 2 * D_EXPERT)
    w_dn = w_dn.reshape(N_EXPERTS, D_EXPERT, D_MODEL)

    w_router = jnp.concatenate([w_group_router[0], w_expert_router[0]], axis=1)
    w_router = jnp.pad(w_router, ((0, 0), (0, ROUTER_LANES - w_router.shape[1])))
    b_router = jnp.concatenate([b_group_router[0], b_expert_router[0]])
    b_router = jnp.pad(b_router, (0, ROUTER_LANES - b_router.shape[0])).reshape(1, -1)
    out = _moe(h1, row(norm_ffn_g[0]), _split_bf16(w_router), b_router, w_gu, w_dn,
               row(norm_final_g), min(1024, t))
    return out.reshape(batch, seq, d)
```

```python
import functools

import jax
import jax.numpy as jnp
from jax import lax
from jax.experimental import pallas as pl
from jax.experimental.pallas import tpu as pltpu

D_MODEL = 1024
CHUNK = 64
N_META = 16
CONV_DIM = 1024
CONV_WIDTH = 31
RET_HEADS = 4
RET_QK_DIM = 256
RET_V_DIM = 512
ROPE_BASE = 10000.0
N_GROUPS = 4
EXPERTS_PER_GROUP = 4
N_EXPERTS = N_GROUPS * EXPERTS_PER_GROUP
D_EXPERT = 512
EPS = 1e-6
D_IN = 2 * CONV_DIM + 2 * RET_HEADS * RET_QK_DIM + 2 * RET_HEADS * RET_V_DIM + 2 * D_MODEL

LANES = 128
SUBLANES = 8
CONV_HALO = 32
IN_PROJ_COLS = 2048
PROJ_TILE = 1024
CONV_TILE = 256
CONV_ROWS_PER_ITER = 4
RET_BLOCK = 256
RET_STEP = 512
ROUTER_LANES = LANES
MOE_BLOCK = 128
MOE_ALIGN = 64
MOE_SPANS = (256, 320, 384)
VMEM_LIMIT = 48 * 1024 * 1024
PROJ_VMEM_LIMIT = 56 * 1024 * 1024
MOE_VMEM_LIMIT = 60 * 1024 * 1024

F32 = jnp.float32
BF16 = jnp.bfloat16


def _sigmoid(x):
    return 1.0 / (1.0 + jnp.exp(-x))


def _dot(a, b):
    return jnp.dot(a, b, preferred_element_type=F32)


def _meta_proj_kernel(x_ref, g_ref, w_ref, o_ref, wbf_ref, u_ref):
    @pl.when(pl.program_id(0) == 0)
    def _():
        x = x_ref[...]
        ms = jnp.mean(x * x, axis=-1, keepdims=True)
        u_ref[...] = (x * lax.rsqrt(ms + EPS) * g_ref[...]).astype(BF16)

    w = w_ref[...].astype(BF16)
    wbf_ref[...] = w
    o_ref[...] = _dot(u_ref[...], w).astype(o_ref.dtype)


def _meta_proj(meta, g, w, tn):
    t, d = meta.shape
    n = w.shape[1]
    return pl.pallas_call(
        _meta_proj_kernel,
        grid=(n // tn,),
        in_specs=[
            pl.BlockSpec((t, d), lambda j: (0, 0)),
            pl.BlockSpec((1, d), lambda j: (0, 0)),
            pl.BlockSpec((d, tn), lambda j: (0, j)),
        ],
        out_specs=[pl.BlockSpec((t, tn), lambda j: (0, j)),
                   pl.BlockSpec((d, tn), lambda j: (0, j))],
        out_shape=[jax.ShapeDtypeStruct((t, n), BF16),
                   jax.ShapeDtypeStruct((d, n), BF16)],
        scratch_shapes=[pltpu.VMEM((t, d), BF16)],
        compiler_params=pltpu.CompilerParams(
            dimension_semantics=("arbitrary",),
            vmem_limit_bytes=VMEM_LIMIT),
        name="meta_proj",
    )(meta, g, w)


def _conv_shifts(win_ref, shift_ref):
    span = CONV_TILE + CONV_HALO - SUBLANES
    for s in range(1, SUBLANES):
        shift_ref[s - 1, 0:span, :] = win_ref[s:s + span, :]


def _conv_taps(cb, r0, win_ref, shift_ref, cbuf_ref, wdw_ref, bdw_ref):
    def group(m):
        return slice(r0 + m * SUBLANES, r0 + (m + 1) * SUBLANES)

    first = CONV_HALO - (CONV_WIDTH - 1)
    cols = pl.ds(pl.multiple_of(cb * LANES, LANES), LANES)
    bias = jnp.broadcast_to(bdw_ref[:, cols], (SUBLANES, LANES))
    accs = [bias] * CONV_ROWS_PER_ITER
    for shift in range(SUBLANES):
        js = [j for j in range(CONV_WIDTH) if (first + j) % SUBLANES == shift]
        tiles = [(first + j) // SUBLANES for j in js]
        taps = [jnp.broadcast_to(wdw_ref[j:j + 1, cols], (SUBLANES, LANES)) for j in js]
        wins = {}
        for m in range(min(tiles), max(tiles) + CONV_ROWS_PER_ITER):
            wins[m] = (win_ref[group(m), cols] if shift == 0
                       else shift_ref[shift - 1, group(m), cols])
        for c in range(CONV_ROWS_PER_ITER):
            for tap, m in zip(taps, tiles):
                accs[c] = accs[c] + tap * wins[m + c]
    for c in range(CONV_ROWS_PER_ITER):
        cbuf_ref[group(c), cols] = accs[c]


def _proj_conv_kernel(x_ref, g_ref, w_ref, ma_ref, mg_ref, wdw_ref, bdw_ref, lng_ref, lnb_ref,
                      wpw_ref, proj_ref, yconv_ref, u_ref, hbuf_ref, win_ref, shift_ref,
                      cbuf_ref, *, tiles_per_seq):
    i, j = pl.program_id(0), pl.program_id(1)
    tm = x_ref.shape[0]

    @pl.when(j == 0)
    def _():
        x = x_ref[...]
        ms = jnp.mean(x * x, axis=-1, keepdims=True)
        u = (x * lax.rsqrt(ms + EPS) * g_ref[...]).astype(BF16)
        u_ref[...] = u

        @pl.when(i % tiles_per_seq == 0)
        def _():
            hbuf_ref[0:CONV_HALO - N_META, :] = jnp.zeros((CONV_HALO - N_META, CONV_DIM), F32)
            ma = ma_ref[...].astype(F32)
            mg = mg_ref[...].astype(F32)
            hbuf_ref[CONV_HALO - N_META:CONV_HALO, :] = ma * _sigmoid(mg)

        @pl.when(i % tiles_per_seq != 0)
        def _():
            hbuf_ref[0:CONV_HALO, :] = hbuf_ref[tm:tm + CONV_HALO, :]

        glu = _dot(u, w_ref[...])
        hbuf_ref[CONV_HALO:CONV_HALO + tm, :] = glu[:, :CONV_DIM] * _sigmoid(glu[:, CONV_DIM:])

    @pl.when(j > 0)
    def _():
        base = pl.multiple_of((j - 1) * CONV_TILE, CONV_TILE)
        win_ref[...] = hbuf_ref[pl.ds(base, CONV_HALO + CONV_TILE), :]
        _conv_shifts(win_ref, shift_ref)
        proj_ref[0] = _dot(u_ref[...], w_ref[...]).astype(proj_ref.dtype)

        def block_body(cb, carry):
            for r0 in range(0, CONV_TILE, SUBLANES * CONV_ROWS_PER_ITER):
                _conv_taps(cb, r0, win_ref, shift_ref, cbuf_ref, wdw_ref, bdw_ref)
            return carry

        lax.fori_loop(0, CONV_DIM // LANES, block_body, 0)
        c = cbuf_ref[...]
        mu = jnp.mean(c, axis=-1, keepdims=True)
        cc = c - mu
        var = jnp.mean(cc * cc, axis=-1, keepdims=True)
        y = (cc * lax.rsqrt(var + EPS) * lng_ref[...] + lnb_ref[...]).astype(BF16)
        y = y * _sigmoid(y)
        yconv_ref[pl.ds(base, CONV_TILE), :] = _dot(y, wpw_ref[...]).astype(yconv_ref.dtype)


def _proj_conv(x2d, g, w_bf, proj_meta, wdw, bdw, lng, lnb, wpw_bf, seq, tm):
    t, d = x2d.shape
    n_steps = D_IN // IN_PROJ_COLS
    assert IN_PROJ_COLS == 2 * CONV_DIM and (n_steps - 1) * CONV_TILE == tm and seq % tm == 0
    const = lambda i, j: (0, 0)
    return pl.pallas_call(
        functools.partial(_proj_conv_kernel, tiles_per_seq=seq // tm),
        grid=(t // tm, n_steps),
        in_specs=[
            pl.BlockSpec((tm, d), lambda i, j: (i, 0)),
            pl.BlockSpec((1, d), const),
            pl.BlockSpec((d, IN_PROJ_COLS), lambda i, j: (0, j)),
            pl.BlockSpec((N_META, CONV_DIM), lambda i, j: (0, 0)),
            pl.BlockSpec((N_META, CONV_DIM), lambda i, j: (0, 1)),
            pl.BlockSpec((CONV_WIDTH, CONV_DIM), const),
            pl.BlockSpec((1, CONV_DIM), const),
            pl.BlockSpec((1, CONV_DIM), const),
            pl.BlockSpec((1, CONV_DIM), const),
            pl.BlockSpec((CONV_DIM, D_MODEL), const),
        ],
        out_specs=[
            pl.BlockSpec((1, tm, IN_PROJ_COLS), lambda i, j: (jnp.maximum(j - 1, 0), i, 0)),
            pl.BlockSpec((tm, D_MODEL), lambda i, j: (i, 0)),
        ],
        out_shape=[jax.ShapeDtypeStruct((n_steps - 1, t, IN_PROJ_COLS), BF16),
                   jax.ShapeDtypeStruct((t, D_MODEL), BF16)],
        scratch_shapes=[pltpu.VMEM((tm, d), BF16),
                        pltpu.VMEM((CONV_HALO + tm, CONV_DIM), F32),
                        pltpu.VMEM((CONV_HALO + CONV_TILE, CONV_DIM), F32),
                        pltpu.VMEM((SUBLANES - 1, CONV_HALO + CONV_TILE, CONV_DIM), F32),
                        pltpu.VMEM((CONV_TILE, CONV_DIM), F32)],
        compiler_params=pltpu.CompilerParams(
            dimension_semantics=("arbitrary", "arbitrary"),
            vmem_limit_bytes=PROJ_VMEM_LIMIT),
        name="proj_conv",
    )(x2d, g, w_bf, proj_meta, proj_meta, wdw, bdw, lng, lnb, wpw_bf)


def _rotary(x, cos, sin):
    half = x.shape[-1] // 2
    x1, x2 = x[:, :half], x[:, half:]
    return jnp.concatenate([x1 * cos - x2 * sin, x2 * cos + x1 * sin], axis=-1)


def _ret_kernel(qk_ref, v_ref, gret_ref, gm_ref, yconv_ref, x_ref, cos_ref, sin_ref,
                mk_ref, mv_ref, mcos_ref, msin_ref, dmat_ref, qdec_ref, kdec_ref, mkdec_ref,
                bdec_ref, gn_ref, wo_ref, wout_ref, eg_ref, eu_ref, ed_ref,
                o_ref, egu_ref, edn_ref, state_ref, gated_ref):
    i = pl.program_id(1)
    k_scale = RET_QK_DIM ** -0.5

    egu_ref[:, :D_EXPERT] = eg_ref[...].astype(BF16)
    egu_ref[:, D_EXPERT:] = eu_ref[...].astype(BF16)
    edn_ref[...] = ed_ref[...].astype(BF16)

    @pl.when(i == 0)
    def _():
        mcos, msin = mcos_ref[...], msin_ref[...]
        for h in range(RET_HEADS):
            mk = mk_ref[:, h * RET_QK_DIM:(h + 1) * RET_QK_DIM].astype(F32)
            mk = _rotary(mk, mcos, msin) * k_scale * mkdec_ref[h]
            mv = mv_ref[:, h * RET_V_DIM:(h + 1) * RET_V_DIM]
            state_ref[h] = lax.dot_general(mk.astype(BF16), mv, (((0,), (0,)), ((), ())),
                                           preferred_element_type=F32)

    for sb in range(RET_STEP // RET_BLOCK):
        rows = slice(sb * RET_BLOCK, (sb + 1) * RET_BLOCK)
        cos, sin = cos_ref[rows, :], sin_ref[rows, :]
        for h in range(RET_HEADS):
            qq = slice(h * RET_QK_DIM, (h + 1) * RET_QK_DIM)
            kk = slice((RET_HEADS + h) * RET_QK_DIM, (RET_HEADS + h + 1) * RET_QK_DIM)
            vv = slice(h * RET_V_DIM, (h + 1) * RET_V_DIM)
            q = _rotary(qk_ref[0, rows, qq].astype(F32), cos, sin)
            k = _rotary(qk_ref[0, rows, kk].astype(F32), cos, sin) * k_scale
            v = v_ref[0, rows, vv]
            q_bf = q.astype(BF16)
            scores = lax.dot_general(q_bf, k.astype(BF16), (((1,), (1,)), ((), ())),
                                     preferred_element_type=F32) * dmat_ref[h]
            state = state_ref[h]
            o = _dot(scores.astype(BF16), v) + _dot(q_bf, state.astype(BF16)) * qdec_ref[h]
            state_ref[h] = state * bdec_ref[h] + lax.dot_general(
                (k * kdec_ref[h]).astype(BF16), v, (((0,), (0,)), ((), ())),
                preferred_element_type=F32)
            mu = jnp.mean(o, axis=-1, keepdims=True)
            oc = o - mu
            var = jnp.mean(oc * oc, axis=-1, keepdims=True)
            on = oc * lax.rsqrt(var + EPS) * gn_ref[:, vv]
            gr = gret_ref[0, rows, vv]
            gated_ref[rows, vv] = gr * _sigmoid(gr) * on.astype(BF16)

    y_ret = _dot(gated_ref[...], wo_ref[...])
    ga = gm_ref[0, :, :D_MODEL]
    gb = gm_ref[0, :, D_MODEL:]
    merged = _sigmoid(ga) * yconv_ref[...] + _sigmoid(gb) * y_ret.astype(BF16)
    o_ref[...] = x_ref[...] + _dot(merged, wout_ref[...])


def _retention(proj, proj_meta, y_conv, x2d, tables, gn, wo_bf, wout_bf, w_gate, w_up, w_down,
               batch, seq):
    t = proj.shape[1]
    nb = seq // RET_STEP
    n_steps = batch * nb
    gu_rows, dn_rows = N_EXPERTS * D_MODEL // n_steps, N_EXPERTS * D_EXPERT // n_steps
    assert gu_rows * n_steps == N_EXPERTS * D_MODEL and dn_rows * n_steps == N_EXPERTS * D_EXPERT
    assert gu_rows % (2 * SUBLANES) == 0 and dn_rows % (2 * SUBLANES) == 0
    hq = RET_HEADS * RET_QK_DIM
    hv = RET_HEADS * RET_V_DIM
    assert 2 * hq == hv == 2 * D_MODEL == proj.shape[2]
    slab = lambda s: pl.BlockSpec((1, RET_STEP, hv), lambda b, i: (s, row(b, i), 0))
    mk_col, mv_col = 2 * CONV_DIM // hq + 1, (2 * CONV_DIM + 2 * hq) // hv
    row = lambda b, i: b * nb + i
    const2 = lambda b, i: (0, 0)
    const3 = lambda b, i: (0, 0, 0)
    cos, sin, mcos, msin, dmat, qdec, kdec, mkdec, bdec = tables
    return pl.pallas_call(
        _ret_kernel,
        grid=(batch, nb),
        in_specs=[
            slab(0), slab(1), slab(2), slab(3),
            pl.BlockSpec((RET_STEP, D_MODEL), lambda b, i: (row(b, i), 0)),
            pl.BlockSpec((RET_STEP, D_MODEL), lambda b, i: (row(b, i), 0)),
            pl.BlockSpec((RET_STEP, RET_QK_DIM // 2), lambda b, i: (i, 0)),
            pl.BlockSpec((RET_STEP, RET_QK_DIM // 2), lambda b, i: (i, 0)),
            pl.BlockSpec((N_META, hq), lambda b, i: (0, mk_col)),
            pl.BlockSpec((N_META, hv), lambda b, i: (0, mv_col)),
            pl.BlockSpec((N_META, RET_QK_DIM // 2), const2),
            pl.BlockSpec((N_META, RET_QK_DIM // 2), const2),
            pl.BlockSpec((RET_HEADS, RET_BLOCK, RET_BLOCK), const3),
            pl.BlockSpec((RET_HEADS, RET_BLOCK, 1), const3),
            pl.BlockSpec((RET_HEADS, RET_BLOCK, 1), const3),
            pl.BlockSpec((RET_HEADS, N_META, 1), const3),
            pl.BlockSpec((RET_HEADS, 1, 1), const3),
            pl.BlockSpec((1, hv), const2),
            pl.BlockSpec((hv, D_MODEL), const2),
            pl.BlockSpec((D_MODEL, D_MODEL), const2),
            pl.BlockSpec((gu_rows, D_EXPERT), lambda b, i: (row(b, i), 0)),
            pl.BlockSpec((gu_rows, D_EXPERT), lambda b, i: (row(b, i), 0)),
            pl.BlockSpec((dn_rows, D_MODEL), lambda b, i: (row(b, i), 0)),
        ],
        out_specs=[
            pl.BlockSpec((RET_STEP, D_MODEL), lambda b, i: (row(b, i), 0)),
            pl.BlockSpec((gu_rows, 2 * D_EXPERT), lambda b, i: (row(b, i), 0)),
            pl.BlockSpec((dn_rows, D_MODEL), lambda b, i: (row(b, i), 0)),
        ],
        out_shape=[jax.ShapeDtypeStruct((t, D_MODEL), F32),
                   jax.ShapeDtypeStruct((N_EXPERTS * D_MODEL, 2 * D_EXPERT), BF16),
                   jax.ShapeDtypeStruct((N_EXPERTS * D_EXPERT, D_MODEL), BF16)],
        scratch_shapes=[pltpu.VMEM((RET_HEADS, RET_QK_DIM, RET_V_DIM), F32),
                        pltpu.VMEM((RET_STEP, hv), BF16)],
        compiler_params=pltpu.CompilerParams(
            dimension_semantics=("parallel", "arbitrary"),
            vmem_limit_bytes=PROJ_VMEM_LIMIT),
        name="retention_mix",
    )(proj, proj, proj, proj, y_conv, x2d, cos, sin, proj_meta, proj_meta, mcos, msin,
      dmat, qdec, kdec, mkdec, bdec, gn, wo_bf, wout_bf,
      w_gate.reshape(N_EXPERTS * D_MODEL, D_EXPERT), w_up.reshape(N_EXPERTS * D_MODEL, D_EXPERT),
      w_down.reshape(N_EXPERTS * D_EXPERT, D_MODEL))


def _retention_tables(seq):
    half = RET_QK_DIM // 2
    inv = ROPE_BASE ** (-jnp.arange(half, dtype=F32) / half)
    pos = jnp.arange(N_META + seq, dtype=F32)
    ang = pos[:, None] * inv[None, :]
    cos_all, sin_all = jnp.cos(ang), jnp.sin(ang)
    log_gamma = jnp.log(1.0 - 2.0 ** (-5.0 - jnp.arange(RET_HEADS, dtype=F32)))
    idx = jnp.arange(RET_BLOCK, dtype=F32)
    chunk = jnp.arange(RET_BLOCK, dtype=jnp.int32) // CHUNK
    visible = chunk[None, :] <= chunk[:, None]
    dmat = jnp.where(visible[None],
                     jnp.exp(log_gamma[:, None, None] * jnp.abs(idx[:, None] - idx[None, :])),
                     0.0)
    qdec = jnp.exp(log_gamma[:, None] * (idx + 1.0))[:, :, None]
    kdec = jnp.exp(log_gamma[:, None] * (RET_BLOCK - 1.0 - idx))[:, :, None]
    midx = jnp.arange(N_META, dtype=F32)
    mkdec = jnp.exp(log_gamma[:, None] * (N_META - 1.0 - midx))[:, :, None]
    bdec = jnp.exp(log_gamma * RET_BLOCK)[:, None, None]
    return (cos_all[N_META:], sin_all[N_META:], cos_all[:N_META], sin_all[:N_META],
            dmat, qdec, kdec, mkdec, bdec)


def _route(logits):
    lane = lax.broadcasted_iota(jnp.int32, logits.shape, 1)
    neg = jnp.float32(-jnp.inf)
    big = jnp.int32(ROUTER_LANES)

    def first_max(masked):
        val = jnp.max(masked, axis=-1, keepdims=True)
        idx = jnp.min(jnp.where(masked == val, lane, big), axis=-1, keepdims=True)
        return val, idx

    gmask = lane < N_GROUPS
    gmax, gidx = first_max(jnp.where(gmask, logits, neg))
    denom = jnp.sum(jnp.where(gmask, jnp.exp(logits - gmax), 0.0), axis=-1, keepdims=True)
    p_group = 1.0 / denom
    assert EXPERTS_PER_GROUP & (EXPERTS_PER_GROUP - 1) == 0
    shift = EXPERTS_PER_GROUP.bit_length() - 1
    lane_group = (lane - N_GROUPS) >> shift
    in_group = jnp.where(lane_group == gidx, logits, neg)
    v1, i1 = first_max(in_group)
    v2, i2 = first_max(jnp.where(lane == i1, neg, in_group))
    e2 = jnp.exp(v2 - v1)
    w1 = p_group / (1.0 + e2)
    w2 = p_group * e2 / (1.0 + e2)
    return jnp.where(lane == i1, w1, 0.0) + jnp.where(lane == i2, w2, 0.0), gidx


def _moe_kernel(h_ref, g_ref, wr_ref, br_ref, tri_ref, wgu_ref, wd_ref, gf_ref, o_ref,
                xs_ref, cs_ref, pos_ref, y_ref, seg_ref):
    grp = pl.program_id(1)
    tm = h_ref.shape[0]

    @pl.when(grp == 0)
    def _():
        h = h_ref[...]
        ms = jnp.mean(h * h, axis=-1, keepdims=True)
        u = h * lax.rsqrt(ms + EPS) * g_ref[...]
        u_hi = u.astype(BF16)
        u_lo = (u - u_hi.astype(F32)).astype(BF16)
        hi_part = _dot(u_hi, wr_ref[...])
        lo_part = _dot(u_lo, wr_ref[:, :ROUTER_LANES])
        logits = (hi_part[:, :ROUTER_LANES] + (hi_part[:, ROUTER_LANES:] + lo_part)
                  + br_ref[...])
        comb, gidx = _route(logits)

        lane = lax.broadcasted_iota(jnp.int32, (tm, ROUTER_LANES), 1)
        onehot = jnp.where(lane == gidx, 1.0, 0.0)
        counts = jnp.sum(onehot, axis=0, keepdims=True)
        row = lax.broadcasted_iota(jnp.int32, (tm, tm), 0)
        prefix = _dot(tri_ref[...], onehot.astype(BF16))
        lane_row = lax.broadcasted_iota(jnp.int32, (1, ROUTER_LANES), 1)
        start = jnp.int32(0)
        starts = jnp.zeros((1, ROUTER_LANES), F32)
        for gg in range(N_GROUPS):
            seg_ref[gg] = start
            starts = starts + jnp.where(lane_row == gg, start.astype(F32), 0.0)
            start = start + jnp.sum(jnp.where(lane_row == gg, counts, 0.0)).astype(jnp.int32)
        seg_ref[N_GROUPS] = start
        pos = jnp.sum(onehot * (prefix + starts), axis=-1, keepdims=True)
        pos_lanes = jnp.broadcast_to(pos, (tm, ROUTER_LANES))
        pos_ref[...] = pos_lanes
        pos_row = pos_lanes.T[0:1, :].astype(jnp.int32)
        perm = jnp.where(row == pos_row, 1.0, 0.0).astype(BF16)
        c_hi = comb.astype(BF16)
        c_lo = (comb - c_hi.astype(F32)).astype(BF16)
        moved = _dot(perm, jnp.concatenate([u_hi, c_hi, c_lo], axis=-1))
        xs_ref[...] = moved[:, :D_MODEL].astype(BF16)
        cs_ref[...] = (moved[:, D_MODEL:D_MODEL + ROUTER_LANES]
                       + moved[:, D_MODEL + ROUTER_LANES:])
        y_ref[...] = jnp.zeros_like(y_ref)

    seg_lo, seg_hi = seg_ref[grp], seg_ref[grp + 1]

    def experts(row0, n_rows):
        rows = pl.ds(pl.multiple_of(row0, MOE_ALIGN), n_rows)
        xb = xs_ref[rows, :]
        cb = cs_ref[rows, :]
        lane = lax.broadcasted_iota(jnp.int32, cb.shape, 1)
        acc = jnp.zeros((n_rows, D_MODEL), F32)
        for e in range(EXPERTS_PER_GROUP):
            expert_lane = N_GROUPS + grp * EXPERTS_PER_GROUP + e
            w_e = jnp.sum(jnp.where(lane == expert_lane, cb, 0.0), axis=-1, keepdims=True)
            gu = _dot(xb, wgu_ref[e])
            gate, up = gu[:, :D_EXPERT], gu[:, D_EXPERT:]
            hid = gate * _sigmoid(gate) * up * w_e
            acc = acc + _dot(hid.astype(BF16), wd_ref[e])
        y_ref[rows, :] += acc

    lo = seg_lo // MOE_ALIGN
    span = (seg_hi + MOE_ALIGN - 1) // MOE_ALIGN - lo
    nonempty = seg_hi > seg_lo
    for k, n_rows in enumerate(MOE_SPANS):
        fits = span <= n_rows // MOE_ALIGN
        if k > 0:
            fits = jnp.logical_and(fits, span > MOE_SPANS[k - 1] // MOE_ALIGN)

        @pl.when(jnp.logical_and(nonempty, fits))
        def _(n_rows=n_rows):
            experts(jnp.minimum(lo * MOE_ALIGN, tm - n_rows), n_rows)

    @pl.when(span > MOE_SPANS[-1] // MOE_ALIGN)
    def _():
        blk_lo = seg_lo // MOE_BLOCK
        n_blk = (seg_hi + MOE_BLOCK - 1) // MOE_BLOCK - blk_lo
        odd = n_blk % 2 == 1
        n_pairs = jnp.where(odd, n_blk - 3, n_blk) // 2

        def pair_body(p, carry):
            experts((blk_lo + 2 * p) * MOE_BLOCK, 2 * MOE_BLOCK)
            return carry

        lax.fori_loop(0, n_pairs, pair_body, 0)

        @pl.when(odd)
        def _():
            experts((blk_lo + n_blk - 3) * MOE_BLOCK, 3 * MOE_BLOCK)

    @pl.when(grp == N_GROUPS - 1)
    def _():
        pos = pos_ref[:, 0:1].astype(jnp.int32)
        col = lax.broadcasted_iota(jnp.int32, (tm, tm), 1)
        unperm = jnp.where(col == pos, 1.0, 0.0).astype(BF16)
        h2 = h_ref[...] + _dot(unperm, y_ref[...].astype(BF16))
        ms = jnp.mean(h2 * h2, axis=-1, keepdims=True)
        o_ref[...] = h2 * lax.rsqrt(ms + EPS) * gf_ref[...]


def _moe(h1, g, wr2, br, wgu_bf, wd_bf, gf, tm):
    t = h1.shape[0]
    const2 = lambda i, e: (0, 0)
    token = jnp.arange(tm, dtype=jnp.int32)
    earlier = (token[None, :] < token[:, None]).astype(BF16)
    return pl.pallas_call(
        _moe_kernel,
        grid=(t // tm, N_GROUPS),
        in_specs=[
            pl.BlockSpec((tm, D_MODEL), lambda i, e: (i, 0)),
            pl.BlockSpec((1, D_MODEL), const2),
            pl.BlockSpec((D_MODEL, 2 * ROUTER_LANES), const2),
            pl.BlockSpec((1, ROUTER_LANES), const2),
            pl.BlockSpec((tm, tm), const2),
            pl.BlockSpec((EXPERTS_PER_GROUP, D_MODEL, 2 * D_EXPERT), lambda i, e: (e, 0, 0)),
            pl.BlockSpec((EXPERTS_PER_GROUP, D_EXPERT, D_MODEL), lambda i, e: (e, 0, 0)),
            pl.BlockSpec((1, D_MODEL), const2),
        ],
        out_specs=pl.BlockSpec((tm, D_MODEL), lambda i, e: (i, 0)),
        out_shape=jax.ShapeDtypeStruct((t, D_MODEL), F32),
        scratch_shapes=[pltpu.VMEM((tm, D_MODEL), BF16),
                        pltpu.VMEM((tm, ROUTER_LANES), F32),
                        pltpu.VMEM((tm, ROUTER_LANES), F32),
                        pltpu.VMEM((tm, D_MODEL), F32),
                        pltpu.SMEM((SUBLANES,), jnp.int32)],
        compiler_params=pltpu.CompilerParams(
            dimension_semantics=("parallel", "arbitrary"),
            vmem_limit_bytes=MOE_VMEM_LIMIT),
        name="hier_moe",
    )(h1, g, wr2, br, earlier, wgu_bf, wd_bf, gf)


def _split_bf16(w):
    hi = w.astype(BF16)
    lo = (w - hi.astype(F32)).astype(BF16)
    return jnp.concatenate([hi, lo], axis=-1)


def kernel(x, meta_tokens, norm_mix_g, w_in, conv_dw_w, conv_dw_b, conv_ln_g, conv_ln_b,
           conv_pw_w, ret_gn_g, ret_w_o, w_out, norm_ffn_g, w_group_router, b_group_router,
           w_expert_router, b_expert_router, w_expert_gate, w_expert_up, w_expert_down,
           norm_final_g):
    batch, seq, d = x.shape
    assert d == D_MODEL and seq % RET_STEP == 0 and w_in.shape[0] == 1
    t = batch * seq
    x2d = x.reshape(t, d)
    row = lambda v: v.reshape(1, -1)

    proj_meta, w_in_bf = _meta_proj(meta_tokens, row(norm_mix_g[0]), w_in[0], IN_PROJ_COLS)
    proj, y_conv = _proj_conv(x2d, row(norm_mix_g[0]), w_in_bf, proj_meta, conv_dw_w[0],
                              row(conv_dw_b[0]), row(conv_ln_g[0]), row(conv_ln_b[0]),
                              conv_pw_w[0].astype(BF16), seq, PROJ_TILE)
    h1, w_gu, w_dn = _retention(proj, proj_meta, y_conv, x2d, _retention_tables(seq),
                                row(ret_gn_g[0]), ret_w_o[0].astype(BF16), w_out[0].astype(BF16),
                                w_expert_gate[0], w_expert_up[0], w_expert_down[0], batch, seq)
    w_gu = w_gu.reshape(N_EXPERTS, D_MODEL, 2 * D_EXPERT)
    w_dn = w_dn.reshape(N_EXPERTS, D_EXPERT, D_MODEL)

    w_router = jnp.concatenate([w_group_router[0], w_expert_router[0]], axis=1)
    w_router = jnp.pad(w_router, ((0, 0), (0, ROUTER_LANES - w_router.shape[1])))
    b_router = jnp.concatenate([b_group_router[0], b_expert_router[0]])
    b_router = jnp.pad(b_router, (0, ROUTER_LANES - b_router.shape[0])).reshape(1, -1)
    out = _moe(h1, row(norm_ffn_g[0]), _split_bf16(w_router), b_router, w_gu, w_dn,
               row(norm_final_g), min(1024, t))
    return out.reshape(batch, seq, d)
```

```python
import functools

import jax
import jax.numpy as jnp
from jax import lax
from jax.experimental import pallas as pl
from jax.experimental.pallas import tpu as pltpu

D_MODEL = 1024
CHUNK = 64
N_META = 16
CONV_DIM = 1024
CONV_WIDTH = 31
RET_HEADS = 4
RET_QK_DIM = 256
RET_V_DIM = 512
ROPE_BASE = 10000.0
N_GROUPS = 4
EXPERTS_PER_GROUP = 4
N_EXPERTS = N_GROUPS * EXPERTS_PER_GROUP
D_EXPERT = 512
EPS = 1e-6
D_IN = 2 * CONV_DIM + 2 * RET_HEADS * RET_QK_DIM + 2 * RET_HEADS * RET_V_DIM + 2 * D_MODEL

LANES = 128
SUBLANES = 8
CONV_HALO = 32
IN_PROJ_COLS = 2048
PROJ_TILE = 1024
CONV_TILE = 256
CONV_ROWS_PER_ITER = 4
RET_BLOCK = 256
RET_STEP = 512
ROUTER_LANES = LANES
MOE_BLOCK = 128
MOE_ALIGN = 32
MOE_SPANS = (256, 288, 320, 352, 384)
VMEM_LIMIT = 48 * 1024 * 1024
PROJ_VMEM_LIMIT = 56 * 1024 * 1024
MOE_VMEM_LIMIT = 60 * 1024 * 1024

F32 = jnp.float32
BF16 = jnp.bfloat16


def _sigmoid(x):
    return 1.0 / (1.0 + jnp.exp(-x))


def _dot(a, b):
    return jnp.dot(a, b, preferred_element_type=F32)


def _meta_proj_kernel(x_ref, g_ref, w_ref, o_ref, wbf_ref, u_ref):
    @pl.when(pl.program_id(0) == 0)
    def _():
        x = x_ref[...]
        ms = jnp.mean(x * x, axis=-1, keepdims=True)
        u_ref[...] = (x * lax.rsqrt(ms + EPS) * g_ref[...]).astype(BF16)

    w = w_ref[...].astype(BF16)
    wbf_ref[...] = w
    o_ref[...] = _dot(u_ref[...], w).astype(o_ref.dtype)


def _meta_proj(meta, g, w, tn):
    t, d = meta.shape
    n = w.shape[1]
    return pl.pallas_call(
        _meta_proj_kernel,
        grid=(n // tn,),
        in_specs=[
            pl.BlockSpec((t, d), lambda j: (0, 0)),
            pl.BlockSpec((1, d), lambda j: (0, 0)),
            pl.BlockSpec((d, tn), lambda j: (0, j)),
        ],
        out_specs=[pl.BlockSpec((t, tn), lambda j: (0, j)),
                   pl.BlockSpec((d, tn), lambda j: (0, j))],
        out_shape=[jax.ShapeDtypeStruct((t, n), BF16),
                   jax.ShapeDtypeStruct((d, n), BF16)],
        scratch_shapes=[pltpu.VMEM((t, d), BF16)],
        compiler_params=pltpu.CompilerParams(
            dimension_semantics=("arbitrary",),
            vmem_limit_bytes=VMEM_LIMIT),
        name="meta_proj",
    )(meta, g, w)


def _conv_shifts(win_ref, shift_ref):
    span = CONV_TILE + CONV_HALO - SUBLANES
    for s in range(1, SUBLANES):
        shift_ref[s - 1, 0:span, :] = win_ref[s:s + span, :]


def _conv_taps(cb, r0, win_ref, shift_ref, cbuf_ref, wdw_ref, bdw_ref):
    def group(m):
        return slice(r0 + m * SUBLANES, r0 + (m + 1) * SUBLANES)

    first = CONV_HALO - (CONV_WIDTH - 1)
    cols = pl.ds(pl.multiple_of(cb * LANES, LANES), LANES)
    bias = jnp.broadcast_to(bdw_ref[:, cols], (SUBLANES, LANES))
    accs = [bias] * CONV_ROWS_PER_ITER
    for shift in range(SUBLANES):
        js = [j for j in range(CONV_WIDTH) if (first + j) % SUBLANES == shift]
        tiles = [(first + j) // SUBLANES for j in js]
        taps = [jnp.broadcast_to(wdw_ref[j:j + 1, cols], (SUBLANES, LANES)) for j in js]
        wins = {}
        for m in range(min(tiles), max(tiles) + CONV_ROWS_PER_ITER):
            wins[m] = (win_ref[group(m), cols] if shift == 0
                       else shift_ref[shift - 1, group(m), cols])
        for c in range(CONV_ROWS_PER_ITER):
            for tap, m in zip(taps, tiles):
                accs[c] = accs[c] + tap * wins[m + c]
    for c in range(CONV_ROWS_PER_ITER):
        cbuf_ref[group(c), cols] = accs[c]


def _proj_conv_kernel(x_ref, g_ref, w_ref, ma_ref, mg_ref, wdw_ref, bdw_ref, lng_ref, lnb_ref,
                      wpw_ref, proj_ref, yconv_ref, u_ref, hbuf_ref, win_ref, shift_ref,
                      cbuf_ref, *, tiles_per_seq):
    i, j = pl.program_id(0), pl.program_id(1)
    tm = x_ref.shape[0]

    @pl.when(j == 0)
    def _():
        x = x_ref[...]
        ms = jnp.mean(x * x, axis=-1, keepdims=True)
        u = (x * lax.rsqrt(ms + EPS) * g_ref[...]).astype(BF16)
        u_ref[...] = u

        @pl.when(i % tiles_per_seq == 0)
        def _():
            hbuf_ref[0:CONV_HALO - N_META, :] = jnp.zeros((CONV_HALO - N_META, CONV_DIM), F32)
            ma = ma_ref[...].astype(F32)
            mg = mg_ref[...].astype(F32)
            hbuf_ref[CONV_HALO - N_META:CONV_HALO, :] = ma * _sigmoid(mg)

        @pl.when(i % tiles_per_seq != 0)
        def _():
            hbuf_ref[0:CONV_HALO, :] = hbuf_ref[tm:tm + CONV_HALO, :]

        glu = _dot(u, w_ref[...])
        hbuf_ref[CONV_HALO:CONV_HALO + tm, :] = glu[:, :CONV_DIM] * _sigmoid(glu[:, CONV_DIM:])

    @pl.when(j > 0)
    def _():
        base = pl.multiple_of((j - 1) * CONV_TILE, CONV_TILE)
        win_ref[...] = hbuf_ref[pl.ds(base, CONV_HALO + CONV_TILE), :]
        _conv_shifts(win_ref, shift_ref)
        proj_ref[0] = _dot(u_ref[...], w_ref[...]).astype(proj_ref.dtype)

        def block_body(cb, carry):
            for r0 in range(0, CONV_TILE, SUBLANES * CONV_ROWS_PER_ITER):
                _conv_taps(cb, r0, win_ref, shift_ref, cbuf_ref, wdw_ref, bdw_ref)
            return carry

        lax.fori_loop(0, CONV_DIM // LANES, block_body, 0)
        c = cbuf_ref[...]
        mu = jnp.mean(c, axis=-1, keepdims=True)
        cc = c - mu
        var = jnp.mean(cc * cc, axis=-1, keepdims=True)
        y = (cc * lax.rsqrt(var + EPS) * lng_ref[...] + lnb_ref[...]).astype(BF16)
        y = y * _sigmoid(y)
        yconv_ref[pl.ds(base, CONV_TILE), :] = _dot(y, wpw_ref[...]).astype(yconv_ref.dtype)


def _proj_conv(x2d, g, w_bf, proj_meta, wdw, bdw, lng, lnb, wpw_bf, seq, tm):
    t, d = x2d.shape
    n_steps = D_IN // IN_PROJ_COLS
    assert IN_PROJ_COLS == 2 * CONV_DIM and (n_steps - 1) * CONV_TILE == tm and seq % tm == 0
    const = lambda i, j: (0, 0)
    return pl.pallas_call(
        functools.partial(_proj_conv_kernel, tiles_per_seq=seq // tm),
        grid=(t // tm, n_steps),
        in_specs=[
            pl.BlockSpec((tm, d), lambda i, j: (i, 0)),
            pl.BlockSpec((1, d), const),
            pl.BlockSpec((d, IN_PROJ_COLS), lambda i, j: (0, j)),
            pl.BlockSpec((N_META, CONV_DIM), lambda i, j: (0, 0)),
            pl.BlockSpec((N_META, CONV_DIM), lambda i, j: (0, 1)),
            pl.BlockSpec((CONV_WIDTH, CONV_DIM), const),
            pl.BlockSpec((1, CONV_DIM), const),
            pl.BlockSpec((1, CONV_DIM), const),
            pl.BlockSpec((1, CONV_DIM), const),
            pl.BlockSpec((CONV_DIM, D_MODEL), const),
        ],
        out_specs=[
            pl.BlockSpec((1, tm, IN_PROJ_COLS), lambda i, j: (jnp.maximum(j - 1, 0), i, 0)),
            pl.BlockSpec((tm, D_MODEL), lambda i, j: (i, 0)),
        ],
        out_shape=[jax.ShapeDtypeStruct((n_steps - 1, t, IN_PROJ_COLS), BF16),
                   jax.ShapeDtypeStruct((t, D_MODEL), BF16)],
        scratch_shapes=[pltpu.VMEM((tm, d), BF16),
                        pltpu.VMEM((CONV_HALO + tm, CONV_DIM), F32),
                        pltpu.VMEM((CONV_HALO + CONV_TILE, CONV_DIM), F32),
                        pltpu.VMEM((SUBLANES - 1, CONV_HALO + CONV_TILE, CONV_DIM), F32),
                        pltpu.VMEM((CONV_TILE, CONV_DIM), F32)],
        compiler_params=pltpu.CompilerParams(
            dimension_semantics=("arbitrary", "arbitrary"),
            vmem_limit_bytes=PROJ_VMEM_LIMIT),
        name="proj_conv",
    )(x2d, g, w_bf, proj_meta, proj_meta, wdw, bdw, lng, lnb, wpw_bf)


def _rotary(x, cos, sin):
    half = x.shape[-1] // 2
    x1, x2 = x[:, :half], x[:, half:]
    return jnp.concatenate([x1 * cos - x2 * sin, x2 * cos + x1 * sin], axis=-1)


def _ret_kernel(qk_ref, v_ref, gret_ref, gm_ref, yconv_ref, x_ref, cos_ref, sin_ref,
                mk_ref, mv_ref, mcos_ref, msin_ref, dmat_ref, qdec_ref, kdec_ref, mkdec_ref,
                bdec_ref, gn_ref, wo_ref, wout_ref, eg_ref, eu_ref, ed_ref,
                o_ref, egu_ref, edn_ref, state_ref, gated_ref):
    i = pl.program_id(1)
    k_scale = RET_QK_DIM ** -0.5

    egu_ref[:, :D_EXPERT] = eg_ref[...].astype(BF16)
    egu_ref[:, D_EXPERT:] = eu_ref[...].astype(BF16)
    edn_ref[...] = ed_ref[...].astype(BF16)

    @pl.when(i == 0)
    def _():
        mcos, msin = mcos_ref[...], msin_ref[...]
        for h in range(RET_HEADS):
            mk = mk_ref[:, h * RET_QK_DIM:(h + 1) * RET_QK_DIM].astype(F32)
            mk = _rotary(mk, mcos, msin) * k_scale * mkdec_ref[h]
            mv = mv_ref[:, h * RET_V_DIM:(h + 1) * RET_V_DIM]
            state_ref[h] = lax.dot_general(mk.astype(BF16), mv, (((0,), (0,)), ((), ())),
                                           preferred_element_type=F32)

    for sb in range(RET_STEP // RET_BLOCK):
        rows = slice(sb * RET_BLOCK, (sb + 1) * RET_BLOCK)
        cos, sin = cos_ref[rows, :], sin_ref[rows, :]
        for h in range(RET_HEADS):
            qq = slice(h * RET_QK_DIM, (h + 1) * RET_QK_DIM)
            kk = slice((RET_HEADS + h) * RET_QK_DIM, (RET_HEADS + h + 1) * RET_QK_DIM)
            vv = slice(h * RET_V_DIM, (h + 1) * RET_V_DIM)
            q = _rotary(qk_ref[0, rows, qq].astype(F32), cos, sin)
            k = _rotary(qk_ref[0, rows, kk].astype(F32), cos, sin) * k_scale
            v = v_ref[0, rows, vv]
            q_bf = q.astype(BF16)
            scores = lax.dot_general(q_bf, k.astype(BF16), (((1,), (1,)), ((), ())),
                                     preferred_element_type=F32) * dmat_ref[h]
            state = state_ref[h]
            o = _dot(scores.astype(BF16), v) + _dot(q_bf, state.astype(BF16)) * qdec_ref[h]
            state_ref[h] = state * bdec_ref[h] + lax.dot_general(
                (k * kdec_ref[h]).astype(BF16), v, (((0,), (0,)), ((), ())),
                preferred_element_type=F32)
            mu = jnp.mean(o, axis=-1, keepdims=True)
            oc = o - mu
            var = jnp.mean(oc * oc, axis=-1, keepdims=True)
            on = oc * lax.rsqrt(var + EPS) * gn_ref[:, vv]
            gr = gret_ref[0, rows, vv]
            gated_ref[rows, vv] = gr * _sigmoid(gr) * on.astype(BF16)

    y_ret = _dot(gated_ref[...], wo_ref[...])
    ga = gm_ref[0, :, :D_MODEL]
    gb = gm_ref[0, :, D_MODEL:]
    merged = _sigmoid(ga) * yconv_ref[...] + _sigmoid(gb) * y_ret.astype(BF16)
    o_ref[...] = x_ref[...] + _dot(merged, wout_ref[...])


def _retention(proj, proj_meta, y_conv, x2d, tables, gn, wo_bf, wout_bf, w_gate, w_up, w_down,
               batch, seq):
    t = proj.shape[1]
    nb = seq // RET_STEP
    n_steps = batch * nb
    gu_rows, dn_rows = N_EXPERTS * D_MODEL // n_steps, N_EXPERTS * D_EXPERT // n_steps
    assert gu_rows * n_steps == N_EXPERTS * D_MODEL and dn_rows * n_steps == N_EXPERTS * D_EXPERT
    assert gu_rows % (2 * SUBLANES) == 0 and dn_rows % (2 * SUBLANES) == 0
    hq = RET_HEADS * RET_QK_DIM
    hv = RET_HEADS * RET_V_DIM
    assert 2 * hq == hv == 2 * D_MODEL == proj.shape[2]
    slab = lambda s: pl.BlockSpec((1, RET_STEP, hv), lambda b, i: (s, row(b, i), 0))
    mk_col, mv_col = 2 * CONV_DIM // hq + 1, (2 * CONV_DIM + 2 * hq) // hv
    row = lambda b, i: b * nb + i
    const2 = lambda b, i: (0, 0)
    const3 = lambda b, i: (0, 0, 0)
    cos, sin, mcos, msin, dmat, qdec, kdec, mkdec, bdec = tables
    return pl.pallas_call(
        _ret_kernel,
        grid=(batch, nb),
        in_specs=[
            slab(0), slab(1), slab(2), slab(3),
            pl.BlockSpec((RET_STEP, D_MODEL), lambda b, i: (row(b, i), 0)),
            pl.BlockSpec((RET_STEP, D_MODEL), lambda b, i: (row(b, i), 0)),
            pl.BlockSpec((RET_STEP, RET_QK_DIM // 2), lambda b, i: (i, 0)),
            pl.BlockSpec((RET_STEP, RET_QK_DIM // 2), lambda b, i: (i, 0)),
            pl.BlockSpec((N_META, hq), lambda b, i: (0, mk_col)),
            pl.BlockSpec((N_META, hv), lambda b, i: (0, mv_col)),
            pl.BlockSpec((N_META, RET_QK_DIM // 2), const2),
            pl.BlockSpec((N_META, RET_QK_DIM // 2), const2),
            pl.BlockSpec((RET_HEADS, RET_BLOCK, RET_BLOCK), const3),
            pl.BlockSpec((RET_HEADS, RET_BLOCK, 1), const3),
            pl.BlockSpec((RET_HEADS, RET_BLOCK, 1), const3),
            pl.BlockSpec((RET_HEADS, N_META, 1), const3),
            pl.BlockSpec((RET_HEADS, 1, 1), const3),
            pl.BlockSpec((1, hv), const2),
            pl.BlockSpec((hv, D_MODEL), const2),
            pl.BlockSpec((D_MODEL, D_MODEL), const2),
            pl.BlockSpec((gu_rows, D_EXPERT), lambda b, i: (row(b, i), 0)),
            pl.BlockSpec((gu_rows, D_EXPERT), lambda b, i: (row(b, i), 0)),
            pl.BlockSpec((dn_rows, D_MODEL), lambda b, i: (row(b, i), 0)),
        ],
        out_specs=[
            pl.BlockSpec((RET_STEP, D_MODEL), lambda b, i: (row(b, i), 0)),
            pl.BlockSpec((gu_rows, 2 * D_EXPERT), lambda b, i: (row(b, i), 0)),
            pl.BlockSpec((dn_rows, D_MODEL), lambda b, i: (row(b, i), 0)),
        ],
        out_shape=[jax.ShapeDtypeStruct((t, D_MODEL), F32),
                   jax.ShapeDtypeStruct((N_EXPERTS * D_MODEL, 2 * D_EXPERT), BF16),
                   jax.ShapeDtypeStruct((N_EXPERTS * D_EXPERT, D_MODEL), BF16)],
        scratch_shapes=[pltpu.VMEM((RET_HEADS, RET_QK_DIM, RET_V_DIM), F32),
                        pltpu.VMEM((RET_STEP, hv), BF16)],
        compiler_params=pltpu.CompilerParams(
            dimension_semantics=("parallel", "arbitrary"),
            vmem_limit_bytes=PROJ_VMEM_LIMIT),
        name="retention_mix",
    )(proj, proj, proj, proj, y_conv, x2d, cos, sin, proj_meta, proj_meta, mcos, msin,
      dmat, qdec, kdec, mkdec, bdec, gn, wo_bf, wout_bf,
      w_gate.reshape(N_EXPERTS * D_MODEL, D_EXPERT), w_up.reshape(N_EXPERTS * D_MODEL, D_EXPERT),
      w_down.reshape(N_EXPERTS * D_EXPERT, D_MODEL))


def _retention_tables(seq):
    half = RET_QK_DIM // 2
    inv = ROPE_BASE ** (-jnp.arange(half, dtype=F32) / half)
    pos = jnp.arange(N_META + seq, dtype=F32)
    ang = pos[:, None] * inv[None, :]
    cos_all, sin_all = jnp.cos(ang), jnp.sin(ang)
    log_gamma = jnp.log(1.0 - 2.0 ** (-5.0 - jnp.arange(RET_HEADS, dtype=F32)))
    idx = jnp.arange(RET_BLOCK, dtype=F32)
    chunk = jnp.arange(RET_BLOCK, dtype=jnp.int32) // CHUNK
    visible = chunk[None, :] <= chunk[:, None]
    dmat = jnp.where(visible[None],
                     jnp.exp(log_gamma[:, None, None] * jnp.abs(idx[:, None] - idx[None, :])),
                     0.0)
    qdec = jnp.exp(log_gamma[:, None] * (idx + 1.0))[:, :, None]
    kdec = jnp.exp(log_gamma[:, None] * (RET_BLOCK - 1.0 - idx))[:, :, None]
    midx = jnp.arange(N_META, dtype=F32)
    mkdec = jnp.exp(log_gamma[:, None] * (N_META - 1.0 - midx))[:, :, None]
    bdec = jnp.exp(log_gamma * RET_BLOCK)[:, None, None]
    return (cos_all[N_META:], sin_all[N_META:], cos_all[:N_META], sin_all[:N_META],
            dmat, qdec, kdec, mkdec, bdec)


def _route(logits):
    lane = lax.broadcasted_iota(jnp.int32, logits.shape, 1)
    neg = jnp.float32(-jnp.inf)
    big = jnp.int32(ROUTER_LANES)

    def first_max(masked):
        val = jnp.max(masked, axis=-1, keepdims=True)
        idx = jnp.min(jnp.where(masked == val, lane, big), axis=-1, keepdims=True)
        return val, idx

    gmask = lane < N_GROUPS
    gmax, gidx = first_max(jnp.where(gmask, logits, neg))
    denom = jnp.sum(jnp.where(gmask, jnp.exp(logits - gmax), 0.0), axis=-1, keepdims=True)
    p_group = 1.0 / denom
    assert EXPERTS_PER_GROUP & (EXPERTS_PER_GROUP - 1) == 0
    shift = EXPERTS_PER_GROUP.bit_length() - 1
    lane_group = (lane - N_GROUPS) >> shift
    in_group = jnp.where(lane_group == gidx, logits, neg)
    v1, i1 = first_max(in_group)
    v2, i2 = first_max(jnp.where(lane == i1, neg, in_group))
    e2 = jnp.exp(v2 - v1)
    w1 = p_group / (1.0 + e2)
    w2 = p_group * e2 / (1.0 + e2)
    return jnp.where(lane == i1, w1, 0.0) + jnp.where(lane == i2, w2, 0.0), gidx


def _moe_kernel(h_ref, g_ref, wr_ref, br_ref, tri_ref, wgu_ref, wd_ref, gf_ref, o_ref,
                xs_ref, cs_ref, pos_ref, y_ref, seg_ref):
    grp = pl.program_id(1)
    tm = h_ref.shape[0]

    @pl.when(grp == 0)
    def _():
        h = h_ref[...]
        ms = jnp.mean(h * h, axis=-1, keepdims=True)
        u = h * lax.rsqrt(ms + EPS) * g_ref[...]
        u_hi = u.astype(BF16)
        u_lo = (u - u_hi.astype(F32)).astype(BF16)
        hi_part = _dot(u_hi, wr_ref[...])
        lo_part = _dot(u_lo, wr_ref[:, :ROUTER_LANES])
        logits = (hi_part[:, :ROUTER_LANES] + (hi_part[:, ROUTER_LANES:] + lo_part)
                  + br_ref[...])
        comb, gidx = _route(logits)

        lane = lax.broadcasted_iota(jnp.int32, (tm, ROUTER_LANES), 1)
        onehot = jnp.where(lane == gidx, 1.0, 0.0)
        counts = jnp.sum(onehot, axis=0, keepdims=True)
        row = lax.broadcasted_iota(jnp.int32, (tm, tm), 0)
        prefix = _dot(tri_ref[...], onehot.astype(BF16))
        lane_row = lax.broadcasted_iota(jnp.int32, (1, ROUTER_LANES), 1)
        start = jnp.int32(0)
        starts = jnp.zeros((1, ROUTER_LANES), F32)
        for gg in range(N_GROUPS):
            seg_ref[gg] = start
            starts = starts + jnp.where(lane_row == gg, start.astype(F32), 0.0)
            start = start + jnp.sum(jnp.where(lane_row == gg, counts, 0.0)).astype(jnp.int32)
        seg_ref[N_GROUPS] = start
        pos = jnp.sum(onehot * (prefix + starts), axis=-1, keepdims=True)
        pos_lanes = jnp.broadcast_to(pos, (tm, ROUTER_LANES))
        pos_ref[...] = pos_lanes
        pos_row = pos_lanes.T[0:1, :].astype(jnp.int32)
        perm = jnp.where(row == pos_row, 1.0, 0.0).astype(BF16)
        c_hi = comb.astype(BF16)
        c_lo = (comb - c_hi.astype(F32)).astype(BF16)
        moved = _dot(perm, jnp.concatenate([u_hi, c_hi, c_lo], axis=-1))
        xs_ref[...] = moved[:, :D_MODEL].astype(BF16)
        cs_ref[...] = (moved[:, D_MODEL:D_MODEL + ROUTER_LANES]
                       + moved[:, D_MODEL + ROUTER_LANES:])
        y_ref[...] = jnp.zeros_like(y_ref)

    seg_lo, seg_hi = seg_ref[grp], seg_ref[grp + 1]

    def experts(row0, n_rows):
        rows = pl.ds(pl.multiple_of(row0, MOE_ALIGN), n_rows)
        xb = xs_ref[rows, :]
        cb = cs_ref[rows, :]
        lane = lax.broadcasted_iota(jnp.int32, cb.shape, 1)
        acc = jnp.zeros((n_rows, D_MODEL), F32)
        for e in range(EXPERTS_PER_GROUP):
            expert_lane = N_GROUPS + grp * EXPERTS_PER_GROUP + e
            w_e = jnp.sum(jnp.where(lane == expert_lane, cb, 0.0), axis=-1, keepdims=True)
            gu = _dot(xb, wgu_ref[e])
            gate, up = gu[:, :D_EXPERT], gu[:, D_EXPERT:]
            hid = gate * _sigmoid(gate) * up * w_e
            acc = acc + _dot(hid.astype(BF16), wd_ref[e])
        y_ref[rows, :] += acc

    lo = seg_lo // MOE_ALIGN
    span = (seg_hi + MOE_ALIGN - 1) // MOE_ALIGN - lo
    nonempty = seg_hi > seg_lo
    for k, n_rows in enumerate(MOE_SPANS):
        fits = span <= n_rows // MOE_ALIGN
        if k > 0:
            fits = jnp.logical_and(fits, span > MOE_SPANS[k - 1] // MOE_ALIGN)

        @pl.when(jnp.logical_and(nonempty, fits))
        def _(n_rows=n_rows):
            experts(jnp.minimum(lo * MOE_ALIGN, tm - n_rows), n_rows)

    @pl.when(span > MOE_SPANS[-1] // MOE_ALIGN)
    def _():
        blk_lo = seg_lo // MOE_BLOCK
        n_blk = (seg_hi + MOE_BLOCK - 1) // MOE_BLOCK - blk_lo
        odd = n_blk % 2 == 1
        n_pairs = jnp.where(odd, n_blk - 3, n_blk) // 2

        def pair_body(p, carry):
            experts((blk_lo + 2 * p) * MOE_BLOCK, 2 * MOE_BLOCK)
            return carry

        lax.fori_loop(0, n_pairs, pair_body, 0)

        @pl.when(odd)
        def _():
            experts((blk_lo + n_blk - 3) * MOE_BLOCK, 3 * MOE_BLOCK)

    @pl.when(grp == N_GROUPS - 1)
    def _():
        pos = pos_ref[:, 0:1].astype(jnp.int32)
        col = lax.broadcasted_iota(jnp.int32, (tm, tm), 1)
        unperm = jnp.where(col == pos, 1.0, 0.0).astype(BF16)
        h2 = h_ref[...] + _dot(unperm, y_ref[...].astype(BF16))
        ms = jnp.mean(h2 * h2, axis=-1, keepdims=True)
        o_ref[...] = h2 * lax.rsqrt(ms + EPS) * gf_ref[...]


def _moe(h1, g, wr2, br, wgu_bf, wd_bf, gf, tm):
    t = h1.shape[0]
    const2 = lambda i, e: (0, 0)
    token = jnp.arange(tm, dtype=jnp.int32)
    earlier = (token[None, :] < token[:, None]).astype(BF16)
    return pl.pallas_call(
        _moe_kernel,
        grid=(t // tm, N_GROUPS),
        in_specs=[
            pl.BlockSpec((tm, D_MODEL), lambda i, e: (i, 0)),
            pl.BlockSpec((1, D_MODEL), const2),
            pl.BlockSpec((D_MODEL, 2 * ROUTER_LANES), const2),
            pl.BlockSpec((1, ROUTER_LANES), const2),
            pl.BlockSpec((tm, tm), const2),
            pl.BlockSpec((EXPERTS_PER_GROUP, D_MODEL, 2 * D_EXPERT), lambda i, e: (e, 0, 0)),
            pl.BlockSpec((EXPERTS_PER_GROUP, D_EXPERT, D_MODEL), lambda i, e: (e, 0, 0)),
            pl.BlockSpec((1, D_MODEL), const2),
        ],
        out_specs=pl.BlockSpec((tm, D_MODEL), lambda i, e: (i, 0)),
        out_shape=jax.ShapeDtypeStruct((t, D_MODEL), F32),
        scratch_shapes=[pltpu.VMEM((tm, D_MODEL), BF16),
                        pltpu.VMEM((tm, ROUTER_LANES), F32),
                        pltpu.VMEM((tm, ROUTER_LANES), F32),
                        pltpu.VMEM((tm, D_MODEL), F32),
                        pltpu.SMEM((SUBLANES,), jnp.int32)],
        compiler_params=pltpu.CompilerParams(
            dimension_semantics=("parallel", "arbitrary"),
            vmem_limit_bytes=MOE_VMEM_LIMIT),
        name="hier_moe",
    )(h1, g, wr2, br, earlier, wgu_bf, wd_bf, gf)


def _split_bf16(w):
    hi = w.astype(BF16)
    lo = (w - hi.astype(F32)).astype(BF16)
    return jnp.concatenate([hi, lo], axis=-1)


def kernel(x, meta_tokens, norm_mix_g, w_in, conv_dw_w, conv_dw_b, conv_ln_g, conv_ln_b,
           conv_pw_w, ret_gn_g, ret_w_o, w_out, norm_ffn_g, w_group_router, b_group_router,
           w_expert_router, b_expert_router, w_expert_gate, w_expert_up, w_expert_down,
           norm_final_g):
    batch, seq, d = x.shape
    assert d == D_MODEL and seq % RET_STEP == 0 and w_in.shape[0] == 1
    t = batch * seq
    x2d = x.reshape(t, d)
    row = lambda v: v.reshape(1, -1)

    proj_meta, w_in_bf = _meta_proj(meta_tokens, row(norm_mix_g[0]), w_in[0], IN_PROJ_COLS)
    proj, y_conv = _proj_conv(x2d, row(norm_mix_g[0]), w_in_bf, proj_meta, conv_dw_w[0],
                              row(conv_dw_b[0]), row(conv_ln_g[0]), row(conv_ln_b[0]),
                              conv_pw_w[0].astype(BF16), seq, PROJ_TILE)
    h1, w_gu, w_dn = _retention(proj, proj_meta, y_conv, x2d, _retention_tables(seq),
                                row(ret_gn_g[0]), ret_w_o[0].astype(BF16), w_out[0].astype(BF16),
                                w_expert_gate[0], w_expert_up[0], w_expert_down[0], batch, seq)
    w_gu = w_gu.reshape(N_EXPERTS, D_MODEL, 2 * D_EXPERT)
    w_dn = w_dn.reshape(N_EXPERTS, D_EXPERT, D_MODEL)

    w_router = jnp.concatenate([w_group_router[0], w_expert_router[0]], axis=1)
    w_router = jnp.pad(w_router, ((0, 0), (0, ROUTER_LANES - w_router.shape[1])))
    b_router = jnp.concatenate([b_group_router[0], b_expert_router[0]])
    b_router = jnp.pad(b_router, (0, ROUTER_LANES - b_router.shape[0])).reshape(1, -1)
    out = _moe(h1, row(norm_ffn_g[0]), _split_bf16(w_router), b_router, w_gu, w_dn,
               row(norm_final_g), min(1024, t))
    return out.reshape(batch, seq, d)
```

```python
import functools

import jax
import jax.numpy as jnp
from jax import lax
from jax.experimental import pallas as pl
from jax.experimental.pallas import tpu as pltpu

D_MODEL = 1024
CHUNK = 64
N_META = 16
CONV_DIM = 1024
CONV_WIDTH = 31
RET_HEADS = 4
RET_QK_DIM = 256
RET_V_DIM = 512
ROPE_BASE = 10000.0
N_GROUPS = 4
EXPERTS_PER_GROUP = 4
N_EXPERTS = N_GROUPS * EXPERTS_PER_GROUP
D_EXPERT = 512
EPS = 1e-6
D_IN = 2 * CONV_DIM + 2 * RET_HEADS * RET_QK_DIM + 2 * RET_HEADS * RET_V_DIM + 2 * D_MODEL

LANES = 128
SUBLANES = 8
CONV_HALO = 32
IN_PROJ_COLS = 2048
PROJ_TILE = 1024
CONV_TILE = 256
CONV_ROWS_PER_ITER = 4
RET_BLOCK = 256
RET_STEP = 512
ROUTER_LANES = LANES
MOE_BLOCK = 128
MOE_ALIGN = 64
MOE_SPANS = (256, 320, 384)
VMEM_LIMIT = 48 * 1024 * 1024
PROJ_VMEM_LIMIT = 56 * 1024 * 1024
MOE_VMEM_LIMIT = 60 * 1024 * 1024

F32 = jnp.float32
BF16 = jnp.bfloat16


def _sigmoid(x):
    return 1.0 / (1.0 + jnp.exp(-x))


def _dot(a, b):
    return jnp.dot(a, b, preferred_element_type=F32)


def _meta_proj_kernel(x_ref, g_ref, w_ref, o_ref, wbf_ref, u_ref):
    @pl.when(pl.program_id(0) == 0)
    def _():
        x = x_ref[...]
        ms = jnp.mean(x * x, axis=-1, keepdims=True)
        u_ref[...] = (x * lax.rsqrt(ms + EPS) * g_ref[...]).astype(BF16)

    w = w_ref[...].astype(BF16)
    wbf_ref[...] = w
    o_ref[...] = _dot(u_ref[...], w).astype(o_ref.dtype)


def _meta_proj(meta, g, w, tn):
    t, d = meta.shape
    n = w.shape[1]
    return pl.pallas_call(
        _meta_proj_kernel,
        grid=(n // tn,),
        in_specs=[
            pl.BlockSpec((t, d), lambda j: (0, 0)),
            pl.BlockSpec((1, d), lambda j: (0, 0)),
            pl.BlockSpec((d, tn), lambda j: (0, j)),
        ],
        out_specs=[pl.BlockSpec((t, tn), lambda j: (0, j)),
                   pl.BlockSpec((d, tn), lambda j: (0, j))],
        out_shape=[jax.ShapeDtypeStruct((t, n), BF16),
                   jax.ShapeDtypeStruct((d, n), BF16)],
        scratch_shapes=[pltpu.VMEM((t, d), BF16)],
        compiler_params=pltpu.CompilerParams(
            dimension_semantics=("arbitrary",),
            vmem_limit_bytes=VMEM_LIMIT),
        name="meta_proj",
    )(meta, g, w)


def _conv_shifts(win_ref, shift_ref):
    span = CONV_TILE + CONV_HALO - SUBLANES
    for s in range(1, SUBLANES):
        shift_ref[s - 1, 0:span, :] = win_ref[s:s + span, :]


def _conv_taps(cb, r0, win_ref, shift_ref, cbuf_ref, wdw_ref, bdw_ref):
    def group(m):
        return slice(r0 + m * SUBLANES, r0 + (m + 1) * SUBLANES)

    first = CONV_HALO - (CONV_WIDTH - 1)
    cols = pl.ds(pl.multiple_of(cb * LANES, LANES), LANES)
    bias = jnp.broadcast_to(bdw_ref[:, cols], (SUBLANES, LANES))
    accs = [bias] * CONV_ROWS_PER_ITER
    for shift in range(SUBLANES):
        js = [j for j in range(CONV_WIDTH) if (first + j) % SUBLANES == shift]
        tiles = [(first + j) // SUBLANES for j in js]
        taps = [jnp.broadcast_to(wdw_ref[j:j + 1, cols], (SUBLANES, LANES)) for j in js]
        wins = {}
        for m in range(min(tiles), max(tiles) + CONV_ROWS_PER_ITER):
            wins[m] = (win_ref[group(m), cols] if shift == 0
                       else shift_ref[shift - 1, group(m), cols])
        for c in range(CONV_ROWS_PER_ITER):
            for tap, m in zip(taps, tiles):
                accs[c] = accs[c] + tap * wins[m + c]
    for c in range(CONV_ROWS_PER_ITER):
        cbuf_ref[group(c), cols] = accs[c]


def _proj_conv_kernel(x_ref, g_ref, w_ref, ma_ref, mg_ref, wdw_ref, bdw_ref, lng_ref, lnb_ref,
                      wpw_ref, proj_ref, yconv_ref, u_ref, hbuf_ref, win_ref, shift_ref,
                      cbuf_ref, act_ref, *, tiles_per_seq):
    i, j = pl.program_id(0), pl.program_id(1)
    tm = x_ref.shape[0]

    @pl.when(j == 0)
    def _():
        x = x_ref[...]
        ms = jnp.mean(x * x, axis=-1, keepdims=True)
        u = (x * lax.rsqrt(ms + EPS) * g_ref[...]).astype(BF16)
        u_ref[...] = u

        @pl.when(i % tiles_per_seq == 0)
        def _():
            hbuf_ref[0:CONV_HALO - N_META, :] = jnp.zeros((CONV_HALO - N_META, CONV_DIM), F32)
            ma = ma_ref[...].astype(F32)
            mg = mg_ref[...].astype(F32)
            hbuf_ref[CONV_HALO - N_META:CONV_HALO, :] = ma * _sigmoid(mg)

        @pl.when(i % tiles_per_seq != 0)
        def _():
            hbuf_ref[0:CONV_HALO, :] = hbuf_ref[tm:tm + CONV_HALO, :]

        glu = _dot(u, w_ref[...])
        hbuf_ref[CONV_HALO:CONV_HALO + tm, :] = glu[:, :CONV_DIM] * _sigmoid(glu[:, CONV_DIM:])

    @pl.when(j > 0)
    def _():
        base = pl.multiple_of((j - 1) * CONV_TILE, CONV_TILE)
        win_ref[...] = hbuf_ref[pl.ds(base, CONV_HALO + CONV_TILE), :]
        _conv_shifts(win_ref, shift_ref)
        proj_ref[0] = _dot(u_ref[...], w_ref[...]).astype(proj_ref.dtype)

        def block_body(cb, carry):
            for r0 in range(0, CONV_TILE, SUBLANES * CONV_ROWS_PER_ITER):
                _conv_taps(cb, r0, win_ref, shift_ref, cbuf_ref, wdw_ref, bdw_ref)
            return carry

        lax.fori_loop(0, CONV_DIM // LANES, block_body, 0)
        c = cbuf_ref[...]
        mu = jnp.mean(c, axis=-1, keepdims=True)
        cc = c - mu
        var = jnp.mean(cc * cc, axis=-1, keepdims=True)
        y = (cc * lax.rsqrt(var + EPS) * lng_ref[...] + lnb_ref[...]).astype(BF16)
        act_ref[pl.ds(base, CONV_TILE), :] = y * _sigmoid(y)

    @pl.when(j == pl.num_programs(1) - 1)
    def _():
        yconv_ref[...] = _dot(act_ref[...], wpw_ref[...]).astype(yconv_ref.dtype)


def _proj_conv(x2d, g, w_bf, proj_meta, wdw, bdw, lng, lnb, wpw_bf, seq, tm):
    t, d = x2d.shape
    n_steps = D_IN // IN_PROJ_COLS
    assert IN_PROJ_COLS == 2 * CONV_DIM and (n_steps - 1) * CONV_TILE == tm and seq % tm == 0
    const = lambda i, j: (0, 0)
    return pl.pallas_call(
        functools.partial(_proj_conv_kernel, tiles_per_seq=seq // tm),
        grid=(t // tm, n_steps),
        in_specs=[
            pl.BlockSpec((tm, d), lambda i, j: (i, 0)),
            pl.BlockSpec((1, d), const),
            pl.BlockSpec((d, IN_PROJ_COLS), lambda i, j: (0, j)),
            pl.BlockSpec((N_META, CONV_DIM), lambda i, j: (0, 0)),
            pl.BlockSpec((N_META, CONV_DIM), lambda i, j: (0, 1)),
            pl.BlockSpec((CONV_WIDTH, CONV_DIM), const),
            pl.BlockSpec((1, CONV_DIM), const),
            pl.BlockSpec((1, CONV_DIM), const),
            pl.BlockSpec((1, CONV_DIM), const),
            pl.BlockSpec((CONV_DIM, D_MODEL), const),
        ],
        out_specs=[
            pl.BlockSpec((1, tm, IN_PROJ_COLS), lambda i, j: (jnp.maximum(j - 1, 0), i, 0)),
            pl.BlockSpec((tm, D_MODEL), lambda i, j: (i, 0)),
        ],
        out_shape=[jax.ShapeDtypeStruct((n_steps - 1, t, IN_PROJ_COLS), BF16),
                   jax.ShapeDtypeStruct((t, D_MODEL), BF16)],
        scratch_shapes=[pltpu.VMEM((tm, d), BF16),
                        pltpu.VMEM((CONV_HALO + tm, CONV_DIM), F32),
                        pltpu.VMEM((CONV_HALO + CONV_TILE, CONV_DIM), F32),
                        pltpu.VMEM((SUBLANES - 1, CONV_HALO + CONV_TILE, CONV_DIM), F32),
                        pltpu.VMEM((CONV_TILE, CONV_DIM), F32),
                        pltpu.VMEM((tm, CONV_DIM), BF16)],
        compiler_params=pltpu.CompilerParams(
            dimension_semantics=("arbitrary", "arbitrary"),
            vmem_limit_bytes=PROJ_VMEM_LIMIT),
        name="proj_conv",
    )(x2d, g, w_bf, proj_meta, proj_meta, wdw, bdw, lng, lnb, wpw_bf)


def _rotary(x, cos, sin):
    half = x.shape[-1] // 2
    x1, x2 = x[:, :half], x[:, half:]
    return jnp.concatenate([x1 * cos - x2 * sin, x2 * cos + x1 * sin], axis=-1)


def _ret_kernel(qk_ref, v_ref, gret_ref, gm_ref, yconv_ref, x_ref, cos_ref, sin_ref,
                mk_ref, mv_ref, mcos_ref, msin_ref, dmat_ref, qdec_ref, kdec_ref, mkdec_ref,
                bdec_ref, gn_ref, wo_ref, wout_ref, eg_ref, eu_ref, ed_ref,
                o_ref, egu_ref, edn_ref, state_ref, gated_ref):
    i = pl.program_id(1)
    k_scale = RET_QK_DIM ** -0.5

    egu_ref[:, :D_EXPERT] = eg_ref[...].astype(BF16)
    egu_ref[:, D_EXPERT:] = eu_ref[...].astype(BF16)
    edn_ref[...] = ed_ref[...].astype(BF16)

    @pl.when(i == 0)
    def _():
        mcos, msin = mcos_ref[...], msin_ref[...]
        for h in range(RET_HEADS):
            mk = mk_ref[:, h * RET_QK_DIM:(h + 1) * RET_QK_DIM].astype(F32)
            mk = _rotary(mk, mcos, msin) * k_scale * mkdec_ref[h]
            mv = mv_ref[:, h * RET_V_DIM:(h + 1) * RET_V_DIM]
            state_ref[h] = lax.dot_general(mk.astype(BF16), mv, (((0,), (0,)), ((), ())),
                                           preferred_element_type=F32)

    for sb in range(RET_STEP // RET_BLOCK):
        rows = slice(sb * RET_BLOCK, (sb + 1) * RET_BLOCK)
        cos, sin = cos_ref[rows, :], sin_ref[rows, :]
        for h in range(RET_HEADS):
            qq = slice(h * RET_QK_DIM, (h + 1) * RET_QK_DIM)
            kk = slice((RET_HEADS + h) * RET_QK_DIM, (RET_HEADS + h + 1) * RET_QK_DIM)
            vv = slice(h * RET_V_DIM, (h + 1) * RET_V_DIM)
            q = _rotary(qk_ref[0, rows, qq].astype(F32), cos, sin)
            k = _rotary(qk_ref[0, rows, kk].astype(F32), cos, sin) * k_scale
            v = v_ref[0, rows, vv]
            q_bf = q.astype(BF16)
            scores = lax.dot_general(q_bf, k.astype(BF16), (((1,), (1,)), ((), ())),
                                     preferred_element_type=F32) * dmat_ref[h]
            state = state_ref[h]
            o = _dot(scores.astype(BF16), v) + _dot(q_bf, state.astype(BF16)) * qdec_ref[h]
            state_ref[h] = state * bdec_ref[h] + lax.dot_general(
                (k * kdec_ref[h]).astype(BF16), v, (((0,), (0,)), ((), ())),
                preferred_element_type=F32)
            mu = jnp.mean(o, axis=-1, keepdims=True)
            oc = o - mu
            var = jnp.mean(oc * oc, axis=-1, keepdims=True)
            on = oc * lax.rsqrt(var + EPS) * gn_ref[:, vv]
            gr = gret_ref[0, rows, vv]
            gated_ref[rows, vv] = gr * _sigmoid(gr) * on.astype(BF16)

    y_ret = _dot(gated_ref[...], wo_ref[...])
    ga = gm_ref[0, :, :D_MODEL]
    gb = gm_ref[0, :, D_MODEL:]
    merged = _sigmoid(ga) * yconv_ref[...] + _sigmoid(gb) * y_ret.astype(BF16)
    o_ref[...] = x_ref[...] + _dot(merged, wout_ref[...])


def _retention(proj, proj_meta, y_conv, x2d, tables, gn, wo_bf, wout_bf, w_gate, w_up, w_down,
               batch, seq):
    t = proj.shape[1]
    nb = seq // RET_STEP
    n_steps = batch * nb
    gu_rows, dn_rows = N_EXPERTS * D_MODEL // n_steps, N_EXPERTS * D_EXPERT // n_steps
    assert gu_rows * n_steps == N_EXPERTS * D_MODEL and dn_rows * n_steps == N_EXPERTS * D_EXPERT
    assert gu_rows % (2 * SUBLANES) == 0 and dn_rows % (2 * SUBLANES) == 0
    hq = RET_HEADS * RET_QK_DIM
    hv = RET_HEADS * RET_V_DIM
    assert 2 * hq == hv == 2 * D_MODEL == proj.shape[2]
    slab = lambda s: pl.BlockSpec((1, RET_STEP, hv), lambda b, i: (s, row(b, i), 0))
    mk_col, mv_col = 2 * CONV_DIM // hq + 1, (2 * CONV_DIM + 2 * hq) // hv
    row = lambda b, i: b * nb + i
    const2 = lambda b, i: (0, 0)
    const3 = lambda b, i: (0, 0, 0)
    cos, sin, mcos, msin, dmat, qdec, kdec, mkdec, bdec = tables
    return pl.pallas_call(
        _ret_kernel,
        grid=(batch, nb),
        in_specs=[
            slab(0), slab(1), slab(2), slab(3),
            pl.BlockSpec((RET_STEP, D_MODEL), lambda b, i: (row(b, i), 0)),
            pl.BlockSpec((RET_STEP, D_MODEL), lambda b, i: (row(b, i), 0)),
            pl.BlockSpec((RET_STEP, RET_QK_DIM // 2), lambda b, i: (i, 0)),
            pl.BlockSpec((RET_STEP, RET_QK_DIM // 2), lambda b, i: (i, 0)),
            pl.BlockSpec((N_META, hq), lambda b, i: (0, mk_col)),
            pl.BlockSpec((N_META, hv), lambda b, i: (0, mv_col)),
            pl.BlockSpec((N_META, RET_QK_DIM // 2), const2),
            pl.BlockSpec((N_META, RET_QK_DIM // 2), const2),
            pl.BlockSpec((RET_HEADS, RET_BLOCK, RET_BLOCK), const3),
            pl.BlockSpec((RET_HEADS, RET_BLOCK, 1), const3),
            pl.BlockSpec((RET_HEADS, RET_BLOCK, 1), const3),
            pl.BlockSpec((RET_HEADS, N_META, 1), const3),
            pl.BlockSpec((RET_HEADS, 1, 1), const3),
            pl.BlockSpec((1, hv), const2),
            pl.BlockSpec((hv, D_MODEL), const2),
            pl.BlockSpec((D_MODEL, D_MODEL), const2),
            pl.BlockSpec((gu_rows, D_EXPERT), lambda b, i: (row(b, i), 0)),
            pl.BlockSpec((gu_rows, D_EXPERT), lambda b, i: (row(b, i), 0)),
            pl.BlockSpec((dn_rows, D_MODEL), lambda b, i: (row(b, i), 0)),
        ],
        out_specs=[
            pl.BlockSpec((RET_STEP, D_MODEL), lambda b, i: (row(b, i), 0)),
            pl.BlockSpec((gu_rows, 2 * D_EXPERT), lambda b, i: (row(b, i), 0)),
            pl.BlockSpec((dn_rows, D_MODEL), lambda b, i: (row(b, i), 0)),
        ],
        out_shape=[jax.ShapeDtypeStruct((t, D_MODEL), F32),
                   jax.ShapeDtypeStruct((N_EXPERTS * D_MODEL, 2 * D_EXPERT), BF16),
                   jax.ShapeDtypeStruct((N_EXPERTS * D_EXPERT, D_MODEL), BF16)],
        scratch_shapes=[pltpu.VMEM((RET_HEADS, RET_QK_DIM, RET_V_DIM), F32),
                        pltpu.VMEM((RET_STEP, hv), BF16)],
        compiler_params=pltpu.CompilerParams(
            dimension_semantics=("parallel", "arbitrary"),
            vmem_limit_bytes=PROJ_VMEM_LIMIT),
        name="retention_mix",
    )(proj, proj, proj, proj, y_conv, x2d, cos, sin, proj_meta, proj_meta, mcos, msin,
      dmat, qdec, kdec, mkdec, bdec, gn, wo_bf, wout_bf,
      w_gate.reshape(N_EXPERTS * D_MODEL, D_EXPERT), w_up.reshape(N_EXPERTS * D_MODEL, D_EXPERT),
      w_down.reshape(N_EXPERTS * D_EXPERT, D_MODEL))


def _retention_tables(seq):
    half = RET_QK_DIM // 2
    inv = ROPE_BASE ** (-jnp.arange(half, dtype=F32) / half)
    pos = jnp.arange(N_META + seq, dtype=F32)
    ang = pos[:, None] * inv[None, :]
    cos_all, sin_all = jnp.cos(ang), jnp.sin(ang)
    log_gamma = jnp.log(1.0 - 2.0 ** (-5.0 - jnp.arange(RET_HEADS, dtype=F32)))
    idx = jnp.arange(RET_BLOCK, dtype=F32)
    chunk = jnp.arange(RET_BLOCK, dtype=jnp.int32) // CHUNK
    visible = chunk[None, :] <= chunk[:, None]
    dmat = jnp.where(visible[None],
                     jnp.exp(log_gamma[:, None, None] * jnp.abs(idx[:, None] - idx[None, :])),
                     0.0)
    qdec = jnp.exp(log_gamma[:, None] * (idx + 1.0))[:, :, None]
    kdec = jnp.exp(log_gamma[:, None] * (RET_BLOCK - 1.0 - idx))[:, :, None]
    midx = jnp.arange(N_META, dtype=F32)
    mkdec = jnp.exp(log_gamma[:, None] * (N_META - 1.0 - midx))[:, :, None]
    bdec = jnp.exp(log_gamma * RET_BLOCK)[:, None, None]
    return (cos_all[N_META:], sin_all[N_META:], cos_all[:N_META], sin_all[:N_META],
            dmat, qdec, kdec, mkdec, bdec)


def _route(logits):
    lane = lax.broadcasted_iota(jnp.int32, logits.shape, 1)
    neg = jnp.float32(-jnp.inf)
    big = jnp.int32(ROUTER_LANES)

    def first_max(masked):
        val = jnp.max(masked, axis=-1, keepdims=True)
        idx = jnp.min(jnp.where(masked == val, lane, big), axis=-1, keepdims=True)
        return val, idx

    gmask = lane < N_GROUPS
    gmax, gidx = first_max(jnp.where(gmask, logits, neg))
    denom = jnp.sum(jnp.where(gmask, jnp.exp(logits - gmax), 0.0), axis=-1, keepdims=True)
    p_group = 1.0 / denom
    assert EXPERTS_PER_GROUP & (EXPERTS_PER_GROUP - 1) == 0
    shift = EXPERTS_PER_GROUP.bit_length() - 1
    lane_group = (lane - N_GROUPS) >> shift
    in_group = jnp.where(lane_group == gidx, logits, neg)
    v1, i1 = first_max(in_group)
    v2, i2 = first_max(jnp.where(lane == i1, neg, in_group))
    e2 = jnp.exp(v2 - v1)
    w1 = p_group / (1.0 + e2)
    w2 = p_group * e2 / (1.0 + e2)
    return jnp.where(lane == i1, w1, 0.0) + jnp.where(lane == i2, w2, 0.0), gidx


def _moe_kernel(h_ref, g_ref, wr_ref, br_ref, tri_ref, wgu_ref, wd_ref, gf_ref, o_ref,
                xs_ref, cs_ref, pos_ref, y_ref, seg_ref):
    grp = pl.program_id(1)
    tm = h_ref.shape[0]

    @pl.when(grp == 0)
    def _():
        h = h_ref[...]
        ms = jnp.mean(h * h, axis=-1, keepdims=True)
        u = h * lax.rsqrt(ms + EPS) * g_ref[...]
        u_hi = u.astype(BF16)
        u_lo = (u - u_hi.astype(F32)).astype(BF16)
        hi_part = _dot(u_hi, wr_ref[...])
        lo_part = _dot(u_lo, wr_ref[:, :ROUTER_LANES])
        logits = (hi_part[:, :ROUTER_LANES] + (hi_part[:, ROUTER_LANES:] + lo_part)
                  + br_ref[...])
        comb, gidx = _route(logits)

        lane = lax.broadcasted_iota(jnp.int32, (tm, ROUTER_LANES), 1)
        onehot = jnp.where(lane == gidx, 1.0, 0.0)
        counts = jnp.sum(onehot, axis=0, keepdims=True)
        row = lax.broadcasted_iota(jnp.int32, (tm, tm), 0)
        prefix = _dot(tri_ref[...], onehot.astype(BF16))
        lane_row = lax.broadcasted_iota(jnp.int32, (1, ROUTER_LANES), 1)
        start = jnp.int32(0)
        starts = jnp.zeros((1, ROUTER_LANES), F32)
        for gg in range(N_GROUPS):
            seg_ref[gg] = start
            starts = starts + jnp.where(lane_row == gg, start.astype(F32), 0.0)
            start = start + jnp.sum(jnp.where(lane_row == gg, counts, 0.0)).astype(jnp.int32)
        seg_ref[N_GROUPS] = start
        pos = jnp.sum(onehot * (prefix + starts), axis=-1, keepdims=True)
        pos_lanes = jnp.broadcast_to(pos, (tm, ROUTER_LANES))
        pos_ref[...] = pos_lanes
        pos_row = pos_lanes.T[0:1, :].astype(jnp.int32)
        perm = jnp.where(row == pos_row, 1.0, 0.0).astype(BF16)
        c_hi = comb.astype(BF16)
        c_lo = (comb - c_hi.astype(F32)).astype(BF16)
        moved = _dot(perm, jnp.concatenate([u_hi, c_hi, c_lo], axis=-1))
        xs_ref[...] = moved[:, :D_MODEL].astype(BF16)
        cs_ref[...] = (moved[:, D_MODEL:D_MODEL + ROUTER_LANES]
                       + moved[:, D_MODEL + ROUTER_LANES:])
        y_ref[...] = jnp.zeros_like(y_ref)

    seg_lo, seg_hi = seg_ref[grp], seg_ref[grp + 1]

    def experts(row0, n_rows):
        rows = pl.ds(pl.multiple_of(row0, MOE_ALIGN), n_rows)
        xb = xs_ref[rows, :]
        cb = cs_ref[rows, :]
        lane = lax.broadcasted_iota(jnp.int32, cb.shape, 1)
        acc = jnp.zeros((n_rows, D_MODEL), F32)
        for e in range(EXPERTS_PER_GROUP):
            expert_lane = N_GROUPS + grp * EXPERTS_PER_GROUP + e
            w_e = jnp.sum(jnp.where(lane == expert_lane, cb, 0.0), axis=-1, keepdims=True)
            gu = _dot(xb, wgu_ref[e])
            gate, up = gu[:, :D_EXPERT], gu[:, D_EXPERT:]
            hid = gate * _sigmoid(gate) * up * w_e
            acc = acc + _dot(hid.astype(BF16), wd_ref[e])
        y_ref[rows, :] += acc

    lo = seg_lo // MOE_ALIGN
    span = (seg_hi + MOE_ALIGN - 1) // MOE_ALIGN - lo
    nonempty = seg_hi > seg_lo
    for k, n_rows in enumerate(MOE_SPANS):
        fits = span <= n_rows // MOE_ALIGN
        if k > 0:
            fits = jnp.logical_and(fits, span > MOE_SPANS[k - 1] // MOE_ALIGN)

        @pl.when(jnp.logical_and(nonempty, fits))
        def _(n_rows=n_rows):
            experts(jnp.minimum(lo * MOE_ALIGN, tm - n_rows), n_rows)

    @pl.when(span > MOE_SPANS[-1] // MOE_ALIGN)
    def _():
        blk_lo = seg_lo // MOE_BLOCK
        n_blk = (seg_hi + MOE_BLOCK - 1) // MOE_BLOCK - blk_lo
        odd = n_blk % 2 == 1
        n_pairs = jnp.where(odd, n_blk - 3, n_blk) // 2

        def pair_body(p, carry):
            experts((blk_lo + 2 * p) * MOE_BLOCK, 2 * MOE_BLOCK)
            return carry

        lax.fori_loop(0, n_pairs, pair_body, 0)

        @pl.when(odd)
        def _():
            experts((blk_lo + n_blk - 3) * MOE_BLOCK, 3 * MOE_BLOCK)

    @pl.when(grp == N_GROUPS - 1)
    def _():
        pos = pos_ref[:, 0:1].astype(jnp.int32)
        col = lax.broadcasted_iota(jnp.int32, (tm, tm), 1)
        unperm = jnp.where(col == pos, 1.0, 0.0).astype(BF16)
        h2 = h_ref[...] + _dot(unperm, y_ref[...].astype(BF16))
        ms = jnp.mean(h2 * h2, axis=-1, keepdims=True)
        o_ref[...] = h2 * lax.rsqrt(ms + EPS) * gf_ref[...]


def _moe(h1, g, wr2, br, wgu_bf, wd_bf, gf, tm):
    t = h1.shape[0]
    const2 = lambda i, e: (0, 0)
    token = jnp.arange(tm, dtype=jnp.int32)
    earlier = (token[None, :] < token[:, None]).astype(BF16)
    return pl.pallas_call(
        _moe_kernel,
        grid=(t // tm, N_GROUPS),
        in_specs=[
            pl.BlockSpec((tm, D_MODEL), lambda i, e: (i, 0)),
            pl.BlockSpec((1, D_MODEL), const2),
            pl.BlockSpec((D_MODEL, 2 * ROUTER_LANES), const2),
            pl.BlockSpec((1, ROUTER_LANES), const2),
            pl.BlockSpec((tm, tm), const2),
            pl.BlockSpec((EXPERTS_PER_GROUP, D_MODEL, 2 * D_EXPERT), lambda i, e: (e, 0, 0)),
            pl.BlockSpec((EXPERTS_PER_GROUP, D_EXPERT, D_MODEL), lambda i, e: (e, 0, 0)),
            pl.BlockSpec((1, D_MODEL), const2),
        ],
        out_specs=pl.BlockSpec((tm, D_MODEL), lambda i, e: (i, 0)),
        out_shape=jax.ShapeDtypeStruct((t, D_MODEL), F32),
        scratch_shapes=[pltpu.VMEM((tm, D_MODEL), BF16),
                        pltpu.VMEM((tm, ROUTER_LANES), F32),
                        pltpu.VMEM((tm, ROUTER_LANES), F32),
                        pltpu.VMEM((tm, D_MODEL), F32),
                        pltpu.SMEM((SUBLANES,), jnp.int32)],
        compiler_params=pltpu.CompilerParams(
            dimension_semantics=("parallel", "arbitrary"),
            vmem_limit_bytes=MOE_VMEM_LIMIT),
        name="hier_moe",
    )(h1, g, wr2, br, earlier, wgu_bf, wd_bf, gf)


def _split_bf16(w):
    hi = w.astype(BF16)
    lo = (w - hi.astype(F32)).astype(BF16)
    return jnp.concatenate([hi, lo], axis=-1)


def kernel(x, meta_tokens, norm_mix_g, w_in, conv_dw_w, conv_dw_b, conv_ln_g, conv_ln_b,
           conv_pw_w, ret_gn_g, ret_w_o, w_out, norm_ffn_g, w_group_router, b_group_router,
           w_expert_router, b_expert_router, w_expert_gate, w_expert_up, w_expert_down,
           norm_final_g):
    batch, seq, d = x.shape
    assert d == D_MODEL and seq % RET_STEP == 0 and w_in.shape[0] == 1
    t = batch * seq
    x2d = x.reshape(t, d)
    row = lambda v: v.reshape(1, -1)

    proj_meta, w_in_bf = _meta_proj(meta_tokens, row(norm_mix_g[0]), w_in[0], IN_PROJ_COLS)
    proj, y_conv = _proj_conv(x2d, row(norm_mix_g[0]), w_in_bf, proj_meta, conv_dw_w[0],
                              row(conv_dw_b[0]), row(conv_ln_g[0]), row(conv_ln_b[0]),
                              conv_pw_w[0].astype(BF16), seq, PROJ_TILE)
    h1, w_gu, w_dn = _retention(proj, proj_meta, y_conv, x2d, _retention_tables(seq),
                                row(ret_gn_g[0]), ret_w_o[0].astype(BF16), w_out[0].astype(BF16),
                                w_expert_gate[0], w_expert_up[0], w_expert_down[0], batch, seq)
    w_gu = w_gu.reshape(N_EXPERTS, D_MODEL, 2 * D_EXPERT)
    w_dn = w_dn.reshape(N_EXPERTS, D_EXPERT, D_MODEL)

    w_router = jnp.concatenate([w_group_router[0], w_expert_router[0]], axis=1)
    w_router = jnp.pad(w_router, ((0, 0), (0, ROUTER_LANES - w_router.shape[1])))
    b_router = jnp.concatenate([b_group_router[0], b_expert_router[0]])
    b_router = jnp.pad(b_router, (0, ROUTER_LANES - b_router.shape[0])).reshape(1, -1)
    out = _moe(h1, row(norm_ffn_g[0]), _split_bf16(w_router), b_router, w_gu, w_dn,
               row(norm_final_g), min(1024, t))
    return out.reshape(batch, seq, d)
```

```python
import functools

import jax
import jax.numpy as jnp
from jax import lax
from jax.experimental import pallas as pl
from jax.experimental.pallas import tpu as pltpu

D_MODEL = 1024
CHUNK = 64
N_META = 16
CONV_DIM = 1024
CONV_WIDTH = 31
RET_HEADS = 4
RET_QK_DIM = 256
RET_V_DIM = 512
ROPE_BASE = 10000.0
N_GROUPS = 4
EXPERTS_PER_GROUP = 4
N_EXPERTS = N_GROUPS * EXPERTS_PER_GROUP
D_EXPERT = 512
EPS = 1e-6
D_IN = 2 * CONV_DIM + 2 * RET_HEADS * RET_QK_DIM + 2 * RET_HEADS * RET_V_DIM + 2 * D_MODEL

LANES = 128
SUBLANES = 8
CONV_HALO = 32
IN_PROJ_COLS = 2048
PROJ_TILE = 1024
CONV_TILE = 256
CONV_ROWS_PER_ITER = 4
RET_BLOCK = 256
RET_STEP = 512
ROUTER_LANES = LANES
MOE_BLOCK = 128
MOE_ALIGN = 64
MOE_SPANS = (256, 320, 384)
VMEM_LIMIT = 48 * 1024 * 1024
PROJ_VMEM_LIMIT = 56 * 1024 * 1024
MOE_VMEM_LIMIT = 60 * 1024 * 1024

F32 = jnp.float32
BF16 = jnp.bfloat16


def _sigmoid(x):
    return 1.0 / (1.0 + jnp.exp(-x))


def _dot(a, b):
    return jnp.dot(a, b, preferred_element_type=F32)


def _meta_proj_kernel(x_ref, g_ref, w_ref, o_ref, wbf_ref, u_ref):
    @pl.when(pl.program_id(0) == 0)
    def _():
        x = x_ref[...]
        ms = jnp.mean(x * x, axis=-1, keepdims=True)
        u_ref[...] = (x * lax.rsqrt(ms + EPS) * g_ref[...]).astype(BF16)

    w = w_ref[...].astype(BF16)
    wbf_ref[...] = w
    o_ref[...] = _dot(u_ref[...], w).astype(o_ref.dtype)


def _meta_proj(meta, g, w, tn):
    t, d = meta.shape
    n = w.shape[1]
    return pl.pallas_call(
        _meta_proj_kernel,
        grid=(n // tn,),
        in_specs=[
            pl.BlockSpec((t, d), lambda j: (0, 0)),
            pl.BlockSpec((1, d), lambda j: (0, 0)),
            pl.BlockSpec((d, tn), lambda j: (0, j)),
        ],
        out_specs=[pl.BlockSpec((t, tn), lambda j: (0, j)),
                   pl.BlockSpec((d, tn), lambda j: (0, j))],
        out_shape=[jax.ShapeDtypeStruct((t, n), BF16),
                   jax.ShapeDtypeStruct((d, n), BF16)],
        scratch_shapes=[pltpu.VMEM((t, d), BF16)],
        compiler_params=pltpu.CompilerParams(
            dimension_semantics=("arbitrary",),
            vmem_limit_bytes=VMEM_LIMIT),
        name="meta_proj",
    )(meta, g, w)


def _conv_shifts(win_ref, shift_ref):
    span = CONV_TILE + CONV_HALO - SUBLANES
    for s in range(1, SUBLANES):
        shift_ref[s - 1, 0:span, :] = win_ref[s:s + span, :]


def _conv_taps(cb, r0, win_ref, shift_ref, cbuf_ref, wdw_ref, bdw_ref):
    def group(m):
        return slice(r0 + m * SUBLANES, r0 + (m + 1) * SUBLANES)

    first = CONV_HALO - (CONV_WIDTH - 1)
    cols = pl.ds(pl.multiple_of(cb * LANES, LANES), LANES)
    bias = jnp.broadcast_to(bdw_ref[:, cols], (SUBLANES, LANES))
    accs = [bias] * CONV_ROWS_PER_ITER
    for shift in range(SUBLANES):
        js = [j for j in range(CONV_WIDTH) if (first + j) % SUBLANES == shift]
        tiles = [(first + j) // SUBLANES for j in js]
        taps = [jnp.broadcast_to(wdw_ref[j:j + 1, cols], (SUBLANES, LANES)) for j in js]
        wins = {}
        for m in range(min(tiles), max(tiles) + CONV_ROWS_PER_ITER):
            wins[m] = (win_ref[group(m), cols] if shift == 0
                       else shift_ref[shift - 1, group(m), cols])
        for c in range(CONV_ROWS_PER_ITER):
            for tap, m in zip(taps, tiles):
                accs[c] = accs[c] + tap * wins[m + c]
    for c in range(CONV_ROWS_PER_ITER):
        cbuf_ref[group(c), cols] = accs[c]


def _proj_conv_kernel(x_ref, g_ref, w_ref, ma_ref, mg_ref, wdw_ref, bdw_ref, lng_ref, lnb_ref,
                      wpw_ref, proj_ref, yconv_ref, u_ref, hbuf_ref, win_ref, shift_ref,
                      cbuf_ref, *, tiles_per_seq):
    i, j = pl.program_id(0), pl.program_id(1)
    tm = x_ref.shape[0]

    @pl.when(j == 0)
    def _():
        x = x_ref[...]
        ms = jnp.mean(x * x, axis=-1, keepdims=True)
        u = (x * lax.rsqrt(ms + EPS) * g_ref[...]).astype(BF16)
        u_ref[...] = u

        @pl.when(i % tiles_per_seq == 0)
        def _():
            hbuf_ref[0:CONV_HALO - N_META, :] = jnp.zeros((CONV_HALO - N_META, CONV_DIM), F32)
            ma = ma_ref[...].astype(F32)
            mg = mg_ref[...].astype(F32)
            hbuf_ref[CONV_HALO - N_META:CONV_HALO, :] = ma * _sigmoid(mg)

        @pl.when(i % tiles_per_seq != 0)
        def _():
            hbuf_ref[0:CONV_HALO, :] = hbuf_ref[tm:tm + CONV_HALO, :]

        glu = _dot(u, w_ref[...])
        hbuf_ref[CONV_HALO:CONV_HALO + tm, :] = glu[:, :CONV_DIM] * _sigmoid(glu[:, CONV_DIM:])

    @pl.when(j > 0)
    def _():
        base = pl.multiple_of((j - 1) * CONV_TILE, CONV_TILE)
        win_ref[...] = hbuf_ref[pl.ds(base, CONV_HALO + CONV_TILE), :]
        _conv_shifts(win_ref, shift_ref)
        proj_ref[0] = _dot(u_ref[...], w_ref[...]).astype(proj_ref.dtype)

        def block_body(cb, carry):
            for r0 in range(0, CONV_TILE, SUBLANES * CONV_ROWS_PER_ITER):
                _conv_taps(cb, r0, win_ref, shift_ref, cbuf_ref, wdw_ref, bdw_ref)
            return carry

        lax.fori_loop(0, CONV_DIM // LANES, block_body, 0)
        c = cbuf_ref[...]
        mu = jnp.mean(c, axis=-1, keepdims=True)
        cc = c - mu
        var = jnp.mean(cc * cc, axis=-1, keepdims=True)
        y = (cc * lax.rsqrt(var + EPS) * lng_ref[...] + lnb_ref[...]).astype(BF16)
        y = y * _sigmoid(y)
        yconv_ref[pl.ds(base, CONV_TILE), :] = _dot(y, wpw_ref[...]).astype(yconv_ref.dtype)


def _proj_conv(x2d, g, w_bf, proj_meta, wdw, bdw, lng, lnb, wpw_bf, seq, tm):
    t, d = x2d.shape
    n_steps = D_IN // IN_PROJ_COLS
    assert IN_PROJ_COLS == 2 * CONV_DIM and (n_steps - 1) * CONV_TILE == tm and seq % tm == 0
    const = lambda i, j: (0, 0)
    return pl.pallas_call(
        functools.partial(_proj_conv_kernel, tiles_per_seq=seq // tm),
        grid=(t // tm, n_steps),
        in_specs=[
            pl.BlockSpec((tm, d), lambda i, j: (i, 0)),
            pl.BlockSpec((1, d), const),
            pl.BlockSpec((d, IN_PROJ_COLS), lambda i, j: (0, j)),
            pl.BlockSpec((N_META, CONV_DIM), lambda i, j: (0, 0)),
            pl.BlockSpec((N_META, CONV_DIM), lambda i, j: (0, 1)),
            pl.BlockSpec((CONV_WIDTH, CONV_DIM), const),
            pl.BlockSpec((1, CONV_DIM), const),
            pl.BlockSpec((1, CONV_DIM), const),
            pl.BlockSpec((1, CONV_DIM), const),
            pl.BlockSpec((CONV_DIM, D_MODEL), const),
        ],
        out_specs=[
            pl.BlockSpec((1, tm, IN_PROJ_COLS), lambda i, j: (jnp.maximum(j - 1, 0), i, 0)),
            pl.BlockSpec((tm, D_MODEL), lambda i, j: (i, 0)),
        ],
        out_shape=[jax.ShapeDtypeStruct((n_steps - 1, t, IN_PROJ_COLS), BF16),
                   jax.ShapeDtypeStruct((t, D_MODEL), BF16)],
        scratch_shapes=[pltpu.VMEM((tm, d), BF16),
                        pltpu.VMEM((CONV_HALO + tm, CONV_DIM), F32),
                        pltpu.VMEM((CONV_HALO + CONV_TILE, CONV_DIM), F32),
                        pltpu.VMEM((SUBLANES - 1, CONV_HALO + CONV_TILE, CONV_DIM), F32),
                        pltpu.VMEM((CONV_TILE, CONV_DIM), F32)],
        compiler_params=pltpu.CompilerParams(
            dimension_semantics=("arbitrary", "arbitrary"),
            vmem_limit_bytes=PROJ_VMEM_LIMIT),
        name="proj_conv",
    )(x2d, g, w_bf, proj_meta, proj_meta, wdw, bdw, lng, lnb, wpw_bf)


def _rotary(x, cos, sin):
    half = x.shape[-1] // 2
    x1, x2 = x[:, :half], x[:, half:]
    return jnp.concatenate([x1 * cos - x2 * sin, x2 * cos + x1 * sin], axis=-1)


def _ret_kernel(qk_ref, v_ref, gret_ref, gm_ref, yconv_ref, x_ref, cos_ref, sin_ref,
                mk_ref, mv_ref, mcos_ref, msin_ref, dmat_ref, qdec_ref, kdec_ref, mkdec_ref,
                bdec_ref, gn_ref, wo_ref, wout_ref, eg_ref, eu_ref, ed_ref,
                o_ref, egu_ref, edn_ref, state_ref, gated_ref):
    i = pl.program_id(1)
    k_scale = RET_QK_DIM ** -0.5

    egu_ref[:, :D_EXPERT] = eg_ref[...].astype(BF16)
    egu_ref[:, D_EXPERT:] = eu_ref[...].astype(BF16)
    edn_ref[...] = ed_ref[...].astype(BF16)

    @pl.when(i == 0)
    def _():
        mcos, msin = mcos_ref[...], msin_ref[...]
        for h in range(RET_HEADS):
            mk = mk_ref[:, h * RET_QK_DIM:(h + 1) * RET_QK_DIM].astype(F32)
            mk = _rotary(mk, mcos, msin) * k_scale * mkdec_ref[h]
            mv = mv_ref[:, h * RET_V_DIM:(h + 1) * RET_V_DIM]
            state_ref[h] = lax.dot_general(mk.astype(BF16), mv, (((0,), (0,)), ((), ())),
                                           preferred_element_type=F32)

    for sb in range(RET_STEP // RET_BLOCK):
        rows = slice(sb * RET_BLOCK, (sb + 1) * RET_BLOCK)
        cos, sin = cos_ref[rows, :], sin_ref[rows, :]
        for h in range(RET_HEADS):
            qq = slice(h * RET_QK_DIM, (h + 1) * RET_QK_DIM)
            kk = slice((RET_HEADS + h) * RET_QK_DIM, (RET_HEADS + h + 1) * RET_QK_DIM)
            vv = slice(h * RET_V_DIM, (h + 1) * RET_V_DIM)
            q = _rotary(qk_ref[0, rows, qq].astype(F32), cos, sin)
            k = _rotary(qk_ref[0, rows, kk].astype(F32), cos, sin) * k_scale
            v = v_ref[0, rows, vv]
            q_bf = q.astype(BF16)
            scores = lax.dot_general(q_bf, k.astype(BF16), (((1,), (1,)), ((), ())),
                                     preferred_element_type=F32) * dmat_ref[h]
            state = state_ref[h]
            o = _dot(scores.astype(BF16), v) + _dot(q_bf, state.astype(BF16)) * qdec_ref[h]
            state_ref[h] = state * bdec_ref[h] + lax.dot_general(
                (k * kdec_ref[h]).astype(BF16), v, (((0,), (0,)), ((), ())),
                preferred_element_type=F32)
            mu = jnp.mean(o, axis=-1, keepdims=True)
            oc = o - mu
            var = jnp.mean(oc * oc, axis=-1, keepdims=True)
            on = oc * lax.rsqrt(var + EPS) * gn_ref[:, vv]
            gr = gret_ref[0, rows, vv]
            gated_ref[rows, vv] = gr * _sigmoid(gr) * on.astype(BF16)

    y_ret = _dot(gated_ref[...], wo_ref[...])
    ga = gm_ref[0, :, :D_MODEL]
    gb = gm_ref[0, :, D_MODEL:]
    merged = _sigmoid(ga) * yconv_ref[...] + _sigmoid(gb) * y_ret.astype(BF16)
    o_ref[...] = x_ref[...] + _dot(merged, wout_ref[...])


def _retention(proj, proj_meta, y_conv, x2d, tables, gn, wo_bf, wout_bf, w_gate, w_up, w_down,
               batch, seq):
    t = proj.shape[1]
    nb = seq // RET_STEP
    n_steps = batch * nb
    gu_rows, dn_rows = N_EXPERTS * D_MODEL // n_steps, N_EXPERTS * D_EXPERT // n_steps
    assert gu_rows * n_steps == N_EXPERTS * D_MODEL and dn_rows * n_steps == N_EXPERTS * D_EXPERT
    assert gu_rows % (2 * SUBLANES) == 0 and dn_rows % (2 * SUBLANES) == 0
    hq = RET_HEADS * RET_QK_DIM
    hv = RET_HEADS * RET_V_DIM
    assert 2 * hq == hv == 2 * D_MODEL == proj.shape[2]
    slab = lambda s: pl.BlockSpec((1, RET_STEP, hv), lambda b, i: (s, row(b, i), 0))
    mk_col, mv_col = 2 * CONV_DIM // hq + 1, (2 * CONV_DIM + 2 * hq) // hv
    row = lambda b, i: b * nb + i
    const2 = lambda b, i: (0, 0)
    const3 = lambda b, i: (0, 0, 0)
    cos, sin, mcos, msin, dmat, qdec, kdec, mkdec, bdec = tables
    return pl.pallas_call(
        _ret_kernel,
        grid=(batch, nb),
        in_specs=[
            slab(0), slab(1), slab(2), slab(3),
            pl.BlockSpec((RET_STEP, D_MODEL), lambda b, i: (row(b, i), 0)),
            pl.BlockSpec((RET_STEP, D_MODEL), lambda b, i: (row(b, i), 0)),
            pl.BlockSpec((RET_STEP, RET_QK_DIM // 2), lambda b, i: (i, 0)),
            pl.BlockSpec((RET_STEP, RET_QK_DIM // 2), lambda b, i: (i, 0)),
            pl.BlockSpec((N_META, hq), lambda b, i: (0, mk_col)),
            pl.BlockSpec((N_META, hv), lambda b, i: (0, mv_col)),
            pl.BlockSpec((N_META, RET_QK_DIM // 2), const2),
            pl.BlockSpec((N_META, RET_QK_DIM // 2), const2),
            pl.BlockSpec((RET_HEADS, RET_BLOCK, RET_BLOCK), const3),
            pl.BlockSpec((RET_HEADS, RET_BLOCK, 1), const3),
            pl.BlockSpec((RET_HEADS, RET_BLOCK, 1), const3),
            pl.BlockSpec((RET_HEADS, N_META, 1), const3),
            pl.BlockSpec((RET_HEADS, 1, 1), const3),
            pl.BlockSpec((1, hv), const2),
            pl.BlockSpec((hv, D_MODEL), const2),
            pl.BlockSpec((D_MODEL, D_MODEL), const2),
            pl.BlockSpec((gu_rows, D_EXPERT), lambda b, i: (row(b, i), 0)),
            pl.BlockSpec((gu_rows, D_EXPERT), lambda b, i: (row(b, i), 0)),
            pl.BlockSpec((dn_rows, D_MODEL), lambda b, i: (row(b, i), 0)),
        ],
        out_specs=[
            pl.BlockSpec((RET_STEP, D_MODEL), lambda b, i: (row(b, i), 0)),
            pl.BlockSpec((gu_rows, 2 * D_EXPERT), lambda b, i: (row(b, i), 0)),
            pl.BlockSpec((dn_rows, D_MODEL), lambda b, i: (row(b, i), 0)),
        ],
        out_shape=[jax.ShapeDtypeStruct((t, D_MODEL), F32),
                   jax.ShapeDtypeStruct((N_EXPERTS * D_MODEL, 2 * D_EXPERT), BF16),
                   jax.ShapeDtypeStruct((N_EXPERTS * D_EXPERT, D_MODEL), BF16)],
        scratch_shapes=[pltpu.VMEM((RET_HEADS, RET_QK_DIM, RET_V_DIM), F32),
                        pltpu.VMEM((RET_STEP, hv), BF16)],
        compiler_params=pltpu.CompilerParams(
            dimension_semantics=("parallel", "arbitrary"),
            vmem_limit_bytes=PROJ_VMEM_LIMIT),
        name="retention_mix",
    )(proj, proj, proj, proj, y_conv, x2d, cos, sin, proj_meta, proj_meta, mcos, msin,
      dmat, qdec, kdec, mkdec, bdec, gn, wo_bf, wout_bf,
      w_gate.reshape(N_EXPERTS * D_MODEL, D_EXPERT), w_up.reshape(N_EXPERTS * D_MODEL, D_EXPERT),
      w_down.reshape(N_EXPERTS * D_EXPERT, D_MODEL))


def _retention_tables(seq):
    half = RET_QK_DIM // 2
    inv = ROPE_BASE ** (-jnp.arange(half, dtype=F32) / half)
    pos = jnp.arange(N_META + seq, dtype=F32)
    ang = pos[:, None] * inv[None, :]
    cos_all, sin_all = jnp.cos(ang), jnp.sin(ang)
    log_gamma = jnp.log(1.0 - 2.0 ** (-5.0 - jnp.arange(RET_HEADS, dtype=F32)))
    idx = jnp.arange(RET_BLOCK, dtype=F32)
    chunk = jnp.arange(RET_BLOCK, dtype=jnp.int32) // CHUNK
    visible = chunk[None, :] <= chunk[:, None]
    dmat = jnp.where(visible[None],
                     jnp.exp(log_gamma[:, None, None] * jnp.abs(idx[:, None] - idx[None, :])),
                     0.0)
    qdec = jnp.exp(log_gamma[:, None] * (idx + 1.0))[:, :, None]
    kdec = jnp.exp(log_gamma[:, None] * (RET_BLOCK - 1.0 - idx))[:, :, None]
    midx = jnp.arange(N_META, dtype=F32)
    mkdec = jnp.exp(log_gamma[:, None] * (N_META - 1.0 - midx))[:, :, None]
    bdec = jnp.exp(log_gamma * RET_BLOCK)[:, None, None]
    return (cos_all[N_META:], sin_all[N_META:], cos_all[:N_META], sin_all[:N_META],
            dmat, qdec, kdec, mkdec, bdec)


def _route(logits):
    lane = lax.broadcasted_iota(jnp.int32, logits.shape, 1)
    neg = jnp.float32(-jnp.inf)
    big = jnp.int32(ROUTER_LANES)

    def first_max(masked):
        val = jnp.max(masked, axis=-1, keepdims=True)
        idx = jnp.min(jnp.where(masked == val, lane, big), axis=-1, keepdims=True)
        return val, idx

    gmask = lane < N_GROUPS
    gmax, gidx = first_max(jnp.where(gmask, logits, neg))
    denom = jnp.sum(jnp.where(gmask, jnp.exp(logits - gmax), 0.0), axis=-1, keepdims=True)
    p_group = 1.0 / denom
    assert EXPERTS_PER_GROUP & (EXPERTS_PER_GROUP - 1) == 0
    shift = EXPERTS_PER_GROUP.bit_length() - 1
    lane_group = (lane - N_GROUPS) >> shift
    in_group = jnp.where(lane_group == gidx, logits, neg)
    v1, i1 = first_max(in_group)
    v2, i2 = first_max(jnp.where(lane == i1, neg, in_group))
    e2 = jnp.exp(v2 - v1)
    w1 = p_group / (1.0 + e2)
    w2 = p_group * e2 / (1.0 + e2)
    return jnp.where(lane == i1, w1, 0.0) + jnp.where(lane == i2, w2, 0.0), gidx


def _moe_kernel(h_ref, g_ref, wr_ref, br_ref, tri_ref, wgu_ref, wd_ref, gf_ref, o_ref,
                xs_ref, cs_ref, pos_ref, y_ref, seg_ref):
    grp = pl.program_id(1)
    tm = h_ref.shape[0]

    @pl.when(grp == 0)
    def _():
        h = h_ref[...]
        ms = jnp.mean(h * h, axis=-1, keepdims=True)
        u = h * lax.rsqrt(ms + EPS) * g_ref[...]
        u_hi = u.astype(BF16)
        u_lo = (u - u_hi.astype(F32)).astype(BF16)
        hi_part = _dot(u_hi, wr_ref[...])
        lo_part = _dot(u_lo, wr_ref[:, :ROUTER_LANES])
        logits = (hi_part[:, :ROUTER_LANES] + (hi_part[:, ROUTER_LANES:] + lo_part)
                  + br_ref[...])
        comb, gidx = _route(logits)

        lane = lax.broadcasted_iota(jnp.int32, (tm, ROUTER_LANES), 1)
        onehot = jnp.where(lane == gidx, 1.0, 0.0)
        counts = jnp.sum(onehot, axis=0, keepdims=True)
        row = lax.broadcasted_iota(jnp.int32, (tm, tm), 0)
        prefix = _dot(tri_ref[...], onehot.astype(BF16))
        lane_row = lax.broadcasted_iota(jnp.int32, (1, ROUTER_LANES), 1)
        start = jnp.int32(0)
        starts = jnp.zeros((1, ROUTER_LANES), F32)
        for gg in range(N_GROUPS):
            seg_ref[gg] = start
            starts = starts + jnp.where(lane_row == gg, start.astype(F32), 0.0)
            start = start + jnp.sum(jnp.where(lane_row == gg, counts, 0.0)).astype(jnp.int32)
        seg_ref[N_GROUPS] = start
        pos = jnp.sum(onehot * (prefix + starts), axis=-1, keepdims=True)
        pos_lanes = jnp.broadcast_to(pos, (tm, ROUTER_LANES))
        pos_ref[...] = pos_lanes
        pos_row = pos_lanes.T[0:1, :].astype(jnp.int32)
        perm = jnp.where(row == pos_row, 1.0, 0.0).astype(BF16)
        c_hi = comb.astype(BF16)
        c_lo = (comb - c_hi.astype(F32)).astype(BF16)
        moved = _dot(perm, jnp.concatenate([u_hi, c_hi, c_lo], axis=-1))
        xs_ref[...] = moved[:, :D_MODEL].astype(BF16)
        cs_ref[...] = (moved[:, D_MODEL:D_MODEL + ROUTER_LANES]
                       + moved[:, D_MODEL + ROUTER_LANES:])
        y_ref[...] = jnp.zeros_like(y_ref)

    seg_lo, seg_hi = seg_ref[grp], seg_ref[grp + 1]

    def experts(row0, n_rows):
        rows = pl.ds(pl.multiple_of(row0, MOE_ALIGN), n_rows)
        xb = xs_ref[rows, :]
        cb = cs_ref[rows, :]
        lane = lax.broadcasted_iota(jnp.int32, cb.shape, 1)
        acc = jnp.zeros((n_rows, D_MODEL), F32)
        for e in range(EXPERTS_PER_GROUP):
            expert_lane = N_GROUPS + grp * EXPERTS_PER_GROUP + e
            w_e = jnp.sum(jnp.where(lane == expert_lane, cb, 0.0), axis=-1, keepdims=True)
            gu = _dot(xb, wgu_ref[e])
            gate, up = gu[:, :D_EXPERT], gu[:, D_EXPERT:]
            hid = gate * _sigmoid(gate) * up * w_e
            acc = acc + _dot(hid.astype(BF16), wd_ref[e])
        y_ref[rows, :] += acc

    lo = seg_lo // MOE_ALIGN
    span = (seg_hi + MOE_ALIGN - 1) // MOE_ALIGN - lo
    nonempty = seg_hi > seg_lo
    for k, n_rows in enumerate(MOE_SPANS):
        fits = span <= n_rows // MOE_ALIGN
        if k > 0:
            fits = jnp.logical_and(fits, span > MOE_SPANS[k - 1] // MOE_ALIGN)

        @pl.when(jnp.logical_and(nonempty, fits))
        def _(n_rows=n_rows):
            experts(jnp.minimum(lo * MOE_ALIGN, tm - n_rows), n_rows)

    @pl.when(span > MOE_SPANS[-1] // MOE_ALIGN)
    def _():
        blk_lo = seg_lo // MOE_BLOCK
        n_blk = (seg_hi + MOE_BLOCK - 1) // MOE_BLOCK - blk_lo
        odd = n_blk % 2 == 1
        n_pairs = jnp.where(odd, n_blk - 3, n_blk) // 2

        def pair_body(p, carry):
            experts((blk_lo + 2 * p) * MOE_BLOCK, 2 * MOE_BLOCK)
            return carry

        lax.fori_loop(0, n_pairs, pair_body, 0)

        @pl.when(odd)
        def _():
            experts((blk_lo + n_blk - 3) * MOE_BLOCK, 3 * MOE_BLOCK)

    @pl.when(grp == N_GROUPS - 1)
    def _():
        pos = pos_ref[:, 0:1].astype(jnp.int32)
        col = lax.broadcasted_iota(jnp.int32, (tm, tm), 1)
        unperm = jnp.where(col == pos, 1.0, 0.0).astype(BF16)
        h2 = h_ref[...] + _dot(unperm, y_ref[...].astype(BF16))
        ms = jnp.mean(h2 * h2, axis=-1, keepdims=True)
        o_ref[...] = h2 * lax.rsqrt(ms + EPS) * gf_ref[...]


def _moe(h1, g, wr2, br, wgu_bf, wd_bf, gf, tm):
    t = h1.shape[0]
    const2 = lambda i, e: (0, 0)
    token = jnp.arange(tm, dtype=jnp.int32)
    earlier = (token[None, :] < token[:, None]).astype(BF16)
    return pl.pallas_call(
        _moe_kernel,
        grid=(t // tm, N_GROUPS),
        in_specs=[
            pl.BlockSpec((tm, D_MODEL), lambda i, e: (i, 0)),
            pl.BlockSpec((1, D_MODEL), const2),
            pl.BlockSpec((D_MODEL, 2 * ROUTER_LANES), const2),
            pl.BlockSpec((1, ROUTER_LANES), const2),
            pl.BlockSpec((tm, tm), const2),
            pl.BlockSpec((EXPERTS_PER_GROUP, D_MODEL, 2 * D_EXPERT), lambda i, e: (e, 0, 0)),
            pl.BlockSpec((EXPERTS_PER_GROUP, D_EXPERT, D_MODEL), lambda i, e: (e, 0, 0)),
            pl.BlockSpec((1, D_MODEL), const2),
        ],
        out_specs=pl.BlockSpec((tm, D_MODEL), lambda i, e: (i, 0)),
        out_shape=jax.ShapeDtypeStruct((t, D_MODEL), F32),
        scratch_shapes=[pltpu.VMEM((tm, D_MODEL), BF16),
                        pltpu.VMEM((tm, ROUTER_LANES), F32),
                        pltpu.VMEM((tm, ROUTER_LANES), F32),
                        pltpu.VMEM((tm, D_MODEL), F32),
                        pltpu.SMEM((SUBLANES,), jnp.int32)],
        compiler_params=pltpu.CompilerParams(
            dimension_semantics=("parallel", "arbitrary"),
            vmem_limit_bytes=MOE_VMEM_LIMIT),
        name="hier_moe",
    )(h1, g, wr2, br, earlier, wgu_bf, wd_bf, gf)


def _split_bf16(w):
    hi = w.astype(BF16)
    lo = (w - hi.astype(F32)).astype(BF16)
    return jnp.concatenate([hi, lo], axis=-1)


def kernel(x, meta_tokens, norm_mix_g, w_in, conv_dw_w, conv_dw_b, conv_ln_g, conv_ln_b,
           conv_pw_w, ret_gn_g, ret_w_o, w_out, norm_ffn_g, w_group_router, b_group_router,
           w_expert_router, b_expert_router, w_expert_gate, w_expert_up, w_expert_down,
           norm_final_g):
    batch, seq, d = x.shape
    assert d == D_MODEL and seq % RET_STEP == 0 and w_in.shape[0] == 1
    t = batch * seq
    x2d = x.reshape(t, d)
    row = lambda v: v.reshape(1, -1)

    proj_meta, w_in_bf = _meta_proj(meta_tokens, row(norm_mix_g[0]), w_in[0], IN_PROJ_COLS)
    proj, y_conv = _proj_conv(x2d, row(norm_mix_g[0]), w_in_bf, proj_meta, conv_dw_w[0],
                              row(conv_dw_b[0]), row(conv_ln_g[0]), row(conv_ln_b[0]),
                              conv_pw_w[0].astype(BF16), seq, PROJ_TILE)
    h1, w_gu, w_dn = _retention(proj, proj_meta, y_conv, x2d, _retention_tables(seq),
                                row(ret_gn_g[0]), ret_w_o[0].astype(BF16), w_out[0].astype(BF16),
                                w_expert_gate[0], w_expert_up[0], w_expert_down[0], batch, seq)
    w_gu = w_gu.reshape(N_EXPERTS, D_MODEL, 2 * D_EXPERT)
    w_dn = w_dn.reshape(N_EXPERTS, D_EXPERT, D_MODEL)

    w_router = jnp.concatenate([w_group_router[0], w_expert_router[0]], axis=1)
    w_router = jnp.pad(w_router, ((0, 0), (0, ROUTER_LANES - w_router.shape[1])))
    b_router = jnp.concatenate([b_group_router[0], b_expert_router[0]])
    b_router = jnp.pad(b_router, (0, ROUTER_LANES - b_router.shape[0])).reshape(1, -1)
    out = _moe(h1, row(norm_ffn_g[0]), _split_bf16(w_router), b_router, w_gu, w_dn,
               row(norm_final_g), min(1024, t))
    return out.reshape(batch, seq, d)
```

```python
import functools

import jax
import jax.numpy as jnp
from jax import lax
from jax.experimental import pallas as pl
from jax.experimental.pallas import tpu as pltpu

D_MODEL = 1024
CHUNK = 64
N_META = 16
CONV_DIM = 1024
CONV_WIDTH = 31
RET_HEADS = 4
RET_QK_DIM = 256
RET_V_DIM = 512
ROPE_BASE = 10000.0
N_GROUPS = 4
EXPERTS_PER_GROUP = 4
N_EXPERTS = N_GROUPS * EXPERTS_PER_GROUP
D_EXPERT = 512
EPS = 1e-6
D_IN = 2 * CONV_DIM + 2 * RET_HEADS * RET_QK_DIM + 2 * RET_HEADS * RET_V_DIM + 2 * D_MODEL

LANES = 128
SUBLANES = 8
CONV_HALO = 32
IN_PROJ_COLS = 2048
PROJ_TILE = 1024
CONV_TILE = 256
CONV_ROWS_PER_ITER = 4
RET_BLOCK = 256
RET_STEP = 512
ROUTER_LANES = LANES
MOE_BLOCK = 128
MOE_ALIGN = 32
MOE_SPANS = (256, 320, 384)
VMEM_LIMIT = 48 * 1024 * 1024
PROJ_VMEM_LIMIT = 56 * 1024 * 1024
MOE_VMEM_LIMIT = 60 * 1024 * 1024

F32 = jnp.float32
BF16 = jnp.bfloat16


def _sigmoid(x):
    return 1.0 / (1.0 + jnp.exp(-x))


def _dot(a, b):
    return jnp.dot(a, b, preferred_element_type=F32)


def _meta_proj_kernel(x_ref, g_ref, w_ref, o_ref, wbf_ref, u_ref):
    @pl.when(pl.program_id(0) == 0)
    def _():
        x = x_ref[...]
        ms = jnp.mean(x * x, axis=-1, keepdims=True)
        u_ref[...] = (x * lax.rsqrt(ms + EPS) * g_ref[...]).astype(BF16)

    w = w_ref[...].astype(BF16)
    wbf_ref[...] = w
    o_ref[...] = _dot(u_ref[...], w).astype(o_ref.dtype)


def _meta_proj(meta, g, w, tn):
    t, d = meta.shape
    n = w.shape[1]
    return pl.pallas_call(
        _meta_proj_kernel,
        grid=(n // tn,),
        in_specs=[
            pl.BlockSpec((t, d), lambda j: (0, 0)),
            pl.BlockSpec((1, d), lambda j: (0, 0)),
            pl.BlockSpec((d, tn), lambda j: (0, j)),
        ],
        out_specs=[pl.BlockSpec((t, tn), lambda j: (0, j)),
                   pl.BlockSpec((d, tn), lambda j: (0, j))],
        out_shape=[jax.ShapeDtypeStruct((t, n), BF16),
                   jax.ShapeDtypeStruct((d, n), BF16)],
        scratch_shapes=[pltpu.VMEM((t, d), BF16)],
        compiler_params=pltpu.CompilerParams(
            dimension_semantics=("arbitrary",),
            vmem_limit_bytes=VMEM_LIMIT),
        name="meta_proj",
    )(meta, g, w)


def _conv_shifts(win_ref, shift_ref):
    span = CONV_TILE + CONV_HALO - SUBLANES
    for s in range(1, SUBLANES):
        shift_ref[s - 1, 0:span, :] = win_ref[s:s + span, :]


def _conv_taps(cb, r0, win_ref, shift_ref, cbuf_ref, wdw_ref, bdw_ref):
    def group(m):
        return slice(r0 + m * SUBLANES, r0 + (m + 1) * SUBLANES)

    first = CONV_HALO - (CONV_WIDTH - 1)
    cols = pl.ds(pl.multiple_of(cb * LANES, LANES), LANES)
    bias = jnp.broadcast_to(bdw_ref[:, cols], (SUBLANES, LANES))
    accs = [bias] * CONV_ROWS_PER_ITER
    for shift in range(SUBLANES):
        js = [j for j in range(CONV_WIDTH) if (first + j) % SUBLANES == shift]
        tiles = [(first + j) // SUBLANES for j in js]
        taps = [jnp.broadcast_to(wdw_ref[j:j + 1, cols], (SUBLANES, LANES)) for j in js]
        wins = {}
        for m in range(min(tiles), max(tiles) + CONV_ROWS_PER_ITER):
            wins[m] = (win_ref[group(m), cols] if shift == 0
                       else shift_ref[shift - 1, group(m), cols])
        for c in range(CONV_ROWS_PER_ITER):
            for tap, m in zip(taps, tiles):
                accs[c] = accs[c] + tap * wins[m + c]
    for c in range(CONV_ROWS_PER_ITER):
        cbuf_ref[group(c), cols] = accs[c]


def _proj_conv_kernel(x_ref, g_ref, w_ref, ma_ref, mg_ref, wdw_ref, bdw_ref, lng_ref, lnb_ref,
                      wpw_ref, proj_ref, yconv_ref, u_ref, hbuf_ref, win_ref, shift_ref,
                      cbuf_ref, *, tiles_per_seq):
    i, j = pl.program_id(0), pl.program_id(1)
    tm = x_ref.shape[0]

    @pl.when(j == 0)
    def _():
        x = x_ref[...]
        ms = jnp.mean(x * x, axis=-1, keepdims=True)
        u = (x * lax.rsqrt(ms + EPS) * g_ref[...]).astype(BF16)
        u_ref[...] = u

        @pl.when(i % tiles_per_seq == 0)
        def _():
            hbuf_ref[0:CONV_HALO - N_META, :] = jnp.zeros((CONV_HALO - N_META, CONV_DIM), F32)
            ma = ma_ref[...].astype(F32)
            mg = mg_ref[...].astype(F32)
            hbuf_ref[CONV_HALO - N_META:CONV_HALO, :] = ma * _sigmoid(mg)

        @pl.when(i % tiles_per_seq != 0)
        def _():
            hbuf_ref[0:CONV_HALO, :] = hbuf_ref[tm:tm + CONV_HALO, :]

        glu = _dot(u, w_ref[...])
        hbuf_ref[CONV_HALO:CONV_HALO + tm, :] = glu[:, :CONV_DIM] * _sigmoid(glu[:, CONV_DIM:])

    @pl.when(j > 0)
    def _():
        base = pl.multiple_of((j - 1) * CONV_TILE, CONV_TILE)
        win_ref[...] = hbuf_ref[pl.ds(base, CONV_HALO + CONV_TILE), :]
        _conv_shifts(win_ref, shift_ref)
        proj_ref[0] = _dot(u_ref[...], w_ref[...]).astype(proj_ref.dtype)

        def block_body(cb, carry):
            for r0 in range(0, CONV_TILE, SUBLANES * CONV_ROWS_PER_ITER):
                _conv_taps(cb, r0, win_ref, shift_ref, cbuf_ref, wdw_ref, bdw_ref)
            return carry

        lax.fori_loop(0, CONV_DIM // LANES, block_body, 0)
        c = cbuf_ref[...]
        mu = jnp.mean(c, axis=-1, keepdims=True)
        cc = c - mu
        var = jnp.mean(cc * cc, axis=-1, keepdims=True)
        y = (cc * lax.rsqrt(var + EPS) * lng_ref[...] + lnb_ref[...]).astype(BF16)
        y = y * _sigmoid(y)
        yconv_ref[pl.ds(base, CONV_TILE), :] = _dot(y, wpw_ref[...]).astype(yconv_ref.dtype)


def _proj_conv(x2d, g, w_bf, proj_meta, wdw, bdw, lng, lnb, wpw_bf, seq, tm):
    t, d = x2d.shape
    n_steps = D_IN // IN_PROJ_COLS
    assert IN_PROJ_COLS == 2 * CONV_DIM and (n_steps - 1) * CONV_TILE == tm and seq % tm == 0
    const = lambda i, j: (0, 0)
    return pl.pallas_call(
        functools.partial(_proj_conv_kernel, tiles_per_seq=seq // tm),
        grid=(t // tm, n_steps),
        in_specs=[
            pl.BlockSpec((tm, d), lambda i, j: (i, 0)),
            pl.BlockSpec((1, d), const),
            pl.BlockSpec((d, IN_PROJ_COLS), lambda i, j: (0, j)),
            pl.BlockSpec((N_META, CONV_DIM), lambda i, j: (0, 0)),
            pl.BlockSpec((N_META, CONV_DIM), lambda i, j: (0, 1)),
            pl.BlockSpec((CONV_WIDTH, CONV_DIM), const),
            pl.BlockSpec((1, CONV_DIM), const),
            pl.BlockSpec((1, CONV_DIM), const),
            pl.BlockSpec((1, CONV_DIM), const),
            pl.BlockSpec((CONV_DIM, D_MODEL), const),
        ],
        out_specs=[
            pl.BlockSpec((1, tm, IN_PROJ_COLS), lambda i, j: (jnp.maximum(j - 1, 0), i, 0)),
            pl.BlockSpec((tm, D_MODEL), lambda i, j: (i, 0)),
        ],
        out_shape=[jax.ShapeDtypeStruct((n_steps - 1, t, IN_PROJ_COLS), BF16),
                   jax.ShapeDtypeStruct((t, D_MODEL), BF16)],
        scratch_shapes=[pltpu.VMEM((tm, d), BF16),
                        pltpu.VMEM((CONV_HALO + tm, CONV_DIM), F32),
                        pltpu.VMEM((CONV_HALO + CONV_TILE, CONV_DIM), F32),
                        pltpu.VMEM((SUBLANES - 1, CONV_HALO + CONV_TILE, CONV_DIM), F32),
                        pltpu.VMEM((CONV_TILE, CONV_DIM), F32)],
        compiler_params=pltpu.CompilerParams(
            dimension_semantics=("arbitrary", "arbitrary"),
            vmem_limit_bytes=PROJ_VMEM_LIMIT),
        name="proj_conv",
    )(x2d, g, w_bf, proj_meta, proj_meta, wdw, bdw, lng, lnb, wpw_bf)


def _rotary(x, cos, sin):
    half = x.shape[-1] // 2
    x1, x2 = x[:, :half], x[:, half:]
    return jnp.concatenate([x1 * cos - x2 * sin, x2 * cos + x1 * sin], axis=-1)


def _ret_kernel(qk_ref, v_ref, gret_ref, gm_ref, yconv_ref, x_ref, cos_ref, sin_ref,
                mk_ref, mv_ref, mcos_ref, msin_ref, dmat_ref, qdec_ref, kdec_ref, mkdec_ref,
                bdec_ref, gn_ref, wo_ref, wout_ref, eg_ref, eu_ref, ed_ref,
                o_ref, egu_ref, edn_ref, state_ref, gated_ref):
    i = pl.program_id(1)
    k_scale = RET_QK_DIM ** -0.5

    egu_ref[:, :D_EXPERT] = eg_ref[...].astype(BF16)
    egu_ref[:, D_EXPERT:] = eu_ref[...].astype(BF16)
    edn_ref[...] = ed_ref[...].astype(BF16)

    @pl.when(i == 0)
    def _():
        mcos, msin = mcos_ref[...], msin_ref[...]
        for h in range(RET_HEADS):
            mk = mk_ref[:, h * RET_QK_DIM:(h + 1) * RET_QK_DIM].astype(F32)
            mk = _rotary(mk, mcos, msin) * k_scale * mkdec_ref[h]
            mv = mv_ref[:, h * RET_V_DIM:(h + 1) * RET_V_DIM]
            state_ref[h] = lax.dot_general(mk.astype(BF16), mv, (((0,), (0,)), ((), ())),
                                           preferred_element_type=F32)

    for sb in range(RET_STEP // RET_BLOCK):
        rows = slice(sb * RET_BLOCK, (sb + 1) * RET_BLOCK)
        cos, sin = cos_ref[rows, :], sin_ref[rows, :]
        for h in range(RET_HEADS):
            qq = slice(h * RET_QK_DIM, (h + 1) * RET_QK_DIM)
            kk = slice((RET_HEADS + h) * RET_QK_DIM, (RET_HEADS + h + 1) * RET_QK_DIM)
            vv = slice(h * RET_V_DIM, (h + 1) * RET_V_DIM)
            q = _rotary(qk_ref[0, rows, qq].astype(F32), cos, sin)
            k = _rotary(qk_ref[0, rows, kk].astype(F32), cos, sin) * k_scale
            v = v_ref[0, rows, vv]
            q_bf = q.astype(BF16)
            scores = lax.dot_general(q_bf, k.astype(BF16), (((1,), (1,)), ((), ())),
                                     preferred_element_type=F32) * dmat_ref[h]
            state = state_ref[h]
            o = _dot(scores.astype(BF16), v) + _dot(q_bf, state.astype(BF16)) * qdec_ref[h]
            state_ref[h] = state * bdec_ref[h] + lax.dot_general(
                (k * kdec_ref[h]).astype(BF16), v, (((0,), (0,)), ((), ())),
                preferred_element_type=F32)
            mu = jnp.mean(o, axis=-1, keepdims=True)
            oc = o - mu
            var = jnp.mean(oc * oc, axis=-1, keepdims=True)
            on = oc * lax.rsqrt(var + EPS) * gn_ref[:, vv]
            gr = gret_ref[0, rows, vv]
            gated_ref[rows, vv] = gr * _sigmoid(gr) * on.astype(BF16)

    y_ret = _dot(gated_ref[...], wo_ref[...])
    ga = gm_ref[0, :, :D_MODEL]
    gb = gm_ref[0, :, D_MODEL:]
    merged = _sigmoid(ga) * yconv_ref[...] + _sigmoid(gb) * y_ret.astype(BF16)
    o_ref[...] = x_ref[...] + _dot(merged, wout_ref[...])


def _retention(proj, proj_meta, y_conv, x2d, tables, gn, wo_bf, wout_bf, w_gate, w_up, w_down,
               batch, seq):
    t = proj.shape[1]
    nb = seq // RET_STEP
    n_steps = batch * nb
    gu_rows, dn_rows = N_EXPERTS * D_MODEL // n_steps, N_EXPERTS * D_EXPERT // n_steps
    assert gu_rows * n_steps == N_EXPERTS * D_MODEL and dn_rows * n_steps == N_EXPERTS * D_EXPERT
    assert gu_rows % (2 * SUBLANES) == 0 and dn_rows % (2 * SUBLANES) == 0
    hq = RET_HEADS * RET_QK_DIM
    hv = RET_HEADS * RET_V_DIM
    assert 2 * hq == hv == 2 * D_MODEL == proj.shape[2]
    slab = lambda s: pl.BlockSpec((1, RET_STEP, hv), lambda b, i: (s, row(b, i), 0))
    mk_col, mv_col = 2 * CONV_DIM // hq + 1, (2 * CONV_DIM + 2 * hq) // hv
    row = lambda b, i: b * nb + i
    const2 = lambda b, i: (0, 0)
    const3 = lambda b, i: (0, 0, 0)
    cos, sin, mcos, msin, dmat, qdec, kdec, mkdec, bdec = tables
    return pl.pallas_call(
        _ret_kernel,
        grid=(batch, nb),
        in_specs=[
            slab(0), slab(1), slab(2), slab(3),
            pl.BlockSpec((RET_STEP, D_MODEL), lambda b, i: (row(b, i), 0)),
            pl.BlockSpec((RET_STEP, D_MODEL), lambda b, i: (row(b, i), 0)),
            pl.BlockSpec((RET_STEP, RET_QK_DIM // 2), lambda b, i: (i, 0)),
            pl.BlockSpec((RET_STEP, RET_QK_DIM // 2), lambda b, i: (i, 0)),
            pl.BlockSpec((N_META, hq), lambda b, i: (0, mk_col)),
            pl.BlockSpec((N_META, hv), lambda b, i: (0, mv_col)),
            pl.BlockSpec((N_META, RET_QK_DIM // 2), const2),
            pl.BlockSpec((N_META, RET_QK_DIM // 2), const2),
            pl.BlockSpec((RET_HEADS, RET_BLOCK, RET_BLOCK), const3),
            pl.BlockSpec((RET_HEADS, RET_BLOCK, 1), const3),
            pl.BlockSpec((RET_HEADS, RET_BLOCK, 1), const3),
            pl.BlockSpec((RET_HEADS, N_META, 1), const3),
            pl.BlockSpec((RET_HEADS, 1, 1), const3),
            pl.BlockSpec((1, hv), const2),
            pl.BlockSpec((hv, D_MODEL), const2),
            pl.BlockSpec((D_MODEL, D_MODEL), const2),
            pl.BlockSpec((gu_rows, D_EXPERT), lambda b, i: (row(b, i), 0)),
            pl.BlockSpec((gu_rows, D_EXPERT), lambda b, i: (row(b, i), 0)),
            pl.BlockSpec((dn_rows, D_MODEL), lambda b, i: (row(b, i), 0)),
        ],
        out_specs=[
            pl.BlockSpec((RET_STEP, D_MODEL), lambda b, i: (row(b, i), 0)),
            pl.BlockSpec((gu_rows, 2 * D_EXPERT), lambda b, i: (row(b, i), 0)),
            pl.BlockSpec((dn_rows, D_MODEL), lambda b, i: (row(b, i), 0)),
        ],
        out_shape=[jax.ShapeDtypeStruct((t, D_MODEL), F32),
                   jax.ShapeDtypeStruct((N_EXPERTS * D_MODEL, 2 * D_EXPERT), BF16),
                   jax.ShapeDtypeStruct((N_EXPERTS * D_EXPERT, D_MODEL), BF16)],
        scratch_shapes=[pltpu.VMEM((RET_HEADS, RET_QK_DIM, RET_V_DIM), F32),
                        pltpu.VMEM((RET_STEP, hv), BF16)],
        compiler_params=pltpu.CompilerParams(
            dimension_semantics=("parallel", "arbitrary"),
            vmem_limit_bytes=PROJ_VMEM_LIMIT),
        name="retention_mix",
    )(proj, proj, proj, proj, y_conv, x2d, cos, sin, proj_meta, proj_meta, mcos, msin,
      dmat, qdec, kdec, mkdec, bdec, gn, wo_bf, wout_bf,
      w_gate.reshape(N_EXPERTS * D_MODEL, D_EXPERT), w_up.reshape(N_EXPERTS * D_MODEL, D_EXPERT),
      w_down.reshape(N_EXPERTS * D_EXPERT, D_MODEL))


def _retention_tables(seq):
    half = RET_QK_DIM // 2
    inv = ROPE_BASE ** (-jnp.arange(half, dtype=F32) / half)
    pos = jnp.arange(N_META + seq, dtype=F32)
    ang = pos[:, None] * inv[None, :]
    cos_all, sin_all = jnp.cos(ang), jnp.sin(ang)
    log_gamma = jnp.log(1.0 - 2.0 ** (-5.0 - jnp.arange(RET_HEADS, dtype=F32)))
    idx = jnp.arange(RET_BLOCK, dtype=F32)
    chunk = jnp.arange(RET_BLOCK, dtype=jnp.int32) // CHUNK
    visible = chunk[None, :] <= chunk[:, None]
    dmat = jnp.where(visible[None],
                     jnp.exp(log_gamma[:, None, None] * jnp.abs(idx[:, None] - idx[None, :])),
                     0.0)
    qdec = jnp.exp(log_gamma[:, None] * (idx + 1.0))[:, :, None]
    kdec = jnp.exp(log_gamma[:, None] * (RET_BLOCK - 1.0 - idx))[:, :, None]
    midx = jnp.arange(N_META, dtype=F32)
    mkdec = jnp.exp(log_gamma[:, None] * (N_META - 1.0 - midx))[:, :, None]
    bdec = jnp.exp(log_gamma * RET_BLOCK)[:, None, None]
    return (cos_all[N_META:], sin_all[N_META:], cos_all[:N_META], sin_all[:N_META],
            dmat, qdec, kdec, mkdec, bdec)


def _route(logits):
    lane = lax.broadcasted_iota(jnp.int32, logits.shape, 1)
    neg = jnp.float32(-jnp.inf)
    big = jnp.int32(ROUTER_LANES)

    def first_max(masked):
        val = jnp.max(masked, axis=-1, keepdims=True)
        idx = jnp.min(jnp.where(masked == val, lane, big), axis=-1, keepdims=True)
        return val, idx

    gmask = lane < N_GROUPS
    gmax, gidx = first_max(jnp.where(gmask, logits, neg))
    denom = jnp.sum(jnp.where(gmask, jnp.exp(logits - gmax), 0.0), axis=-1, keepdims=True)
    p_group = 1.0 / denom
    assert EXPERTS_PER_GROUP & (EXPERTS_PER_GROUP - 1) == 0
    shift = EXPERTS_PER_GROUP.bit_length() - 1
    lane_group = (lane - N_GROUPS) >> shift
    in_group = jnp.where(lane_group == gidx, logits, neg)
    v1, i1 = first_max(in_group)
    v2, i2 = first_max(jnp.where(lane == i1, neg, in_group))
    e2 = jnp.exp(v2 - v1)
    w1 = p_group / (1.0 + e2)
    w2 = p_group * e2 / (1.0 + e2)
    return jnp.where(lane == i1, w1, 0.0) + jnp.where(lane == i2, w2, 0.0), gidx


def _moe_kernel(h_ref, g_ref, wr_ref, br_ref, tri_ref, wgu_ref, wd_ref, gf_ref, o_ref,
                xs_ref, cs_ref, pos_ref, y_ref, seg_ref):
    grp = pl.program_id(1)
    tm = h_ref.shape[0]

    @pl.when(grp == 0)
    def _():
        h = h_ref[...]
        ms = jnp.mean(h * h, axis=-1, keepdims=True)
        u = h * lax.rsqrt(ms + EPS) * g_ref[...]
        u_hi = u.astype(BF16)
        u_lo = (u - u_hi.astype(F32)).astype(BF16)
        hi_part = _dot(u_hi, wr_ref[...])
        lo_part = _dot(u_lo, wr_ref[:, :ROUTER_LANES])
        logits = (hi_part[:, :ROUTER_LANES] + (hi_part[:, ROUTER_LANES:] + lo_part)
                  + br_ref[...])
        comb, gidx = _route(logits)

        lane = lax.broadcasted_iota(jnp.int32, (tm, ROUTER_LANES), 1)
        onehot = jnp.where(lane == gidx, 1.0, 0.0)
        counts = jnp.sum(onehot, axis=0, keepdims=True)
        row = lax.broadcasted_iota(jnp.int32, (tm, tm), 0)
        prefix = _dot(tri_ref[...], onehot.astype(BF16))
        lane_row = lax.broadcasted_iota(jnp.int32, (1, ROUTER_LANES), 1)
        start = jnp.int32(0)
        starts = jnp.zeros((1, ROUTER_LANES), F32)
        for gg in range(N_GROUPS):
            seg_ref[gg] = start
            starts = starts + jnp.where(lane_row == gg, start.astype(F32), 0.0)
            start = start + jnp.sum(jnp.where(lane_row == gg, counts, 0.0)).astype(jnp.int32)
        seg_ref[N_GROUPS] = start
        pos = jnp.sum(onehot * (prefix + starts), axis=-1, keepdims=True)
        pos_lanes = jnp.broadcast_to(pos, (tm, ROUTER_LANES))
        pos_ref[...] = pos_lanes
        pos_row = pos_lanes.T[0:1, :].astype(jnp.int32)
        perm = jnp.where(row == pos_row, 1.0, 0.0).astype(BF16)
        c_hi = comb.astype(BF16)
        c_lo = (comb - c_hi.astype(F32)).astype(BF16)
        moved = _dot(perm, jnp.concatenate([u_hi, c_hi, c_lo], axis=-1))
        xs_ref[...] = moved[:, :D_MODEL].astype(BF16)
        cs_ref[...] = (moved[:, D_MODEL:D_MODEL + ROUTER_LANES]
                       + moved[:, D_MODEL + ROUTER_LANES:])
        y_ref[...] = jnp.zeros_like(y_ref)

    seg_lo, seg_hi = seg_ref[grp], seg_ref[grp + 1]

    def experts(row0, n_rows):
        rows = pl.ds(pl.multiple_of(row0, MOE_ALIGN), n_rows)
        xb = xs_ref[rows, :]
        cb = cs_ref[rows, :]
        lane = lax.broadcasted_iota(jnp.int32, cb.shape, 1)
        acc = jnp.zeros((n_rows, D_MODEL), F32)
        for e in range(EXPERTS_PER_GROUP):
            expert_lane = N_GROUPS + grp * EXPERTS_PER_GROUP + e
            w_e = jnp.sum(jnp.where(lane == expert_lane, cb, 0.0), axis=-1, keepdims=True)
            gu = _dot(xb, wgu_ref[e])
            gate, up = gu[:, :D_EXPERT], gu[:, D_EXPERT:]
            hid = gate * _sigmoid(gate) * up * w_e
            acc = acc + _dot(hid.astype(BF16), wd_ref[e])
        y_ref[rows, :] += acc

    lo = seg_lo // MOE_ALIGN
    span = (seg_hi + MOE_ALIGN - 1) // MOE_ALIGN - lo
    nonempty = seg_hi > seg_lo
    for k, n_rows in enumerate(MOE_SPANS):
        fits = span <= n_rows // MOE_ALIGN
        if k > 0:
            fits = jnp.logical_and(fits, span > MOE_SPANS[k - 1] // MOE_ALIGN)

        @pl.when(jnp.logical_and(nonempty, fits))
        def _(n_rows=n_rows):
            experts(jnp.minimum(lo * MOE_ALIGN, tm - n_rows), n_rows)

    @pl.when(span > MOE_SPANS[-1] // MOE_ALIGN)
    def _():
        blk_lo = seg_lo // MOE_BLOCK
        n_blk = (seg_hi + MOE_BLOCK - 1) // MOE_BLOCK - blk_lo
        odd = n_blk % 2 == 1
        n_pairs = jnp.where(odd, n_blk - 3, n_blk) // 2

        def pair_body(p, carry):
            experts((blk_lo + 2 * p) * MOE_BLOCK, 2 * MOE_BLOCK)
            return carry

        lax.fori_loop(0, n_pairs, pair_body, 0)

        @pl.when(odd)
        def _():
            experts((blk_lo + n_blk - 3) * MOE_BLOCK, 3 * MOE_BLOCK)

    @pl.when(grp == N_GROUPS - 1)
    def _():
        pos = pos_ref[:, 0:1].astype(jnp.int32)
        col = lax.broadcasted_iota(jnp.int32, (tm, tm), 1)
        unperm = jnp.where(col == pos, 1.0, 0.0).astype(BF16)
        h2 = h_ref[...] + _dot(unperm, y_ref[...].astype(BF16))
        ms = jnp.mean(h2 * h2, axis=-1, keepdims=True)
        o_ref[...] = h2 * lax.rsqrt(ms + EPS) * gf_ref[...]


def _moe(h1, g, wr2, br, wgu_bf, wd_bf, gf, tm):
    t = h1.shape[0]
    const2 = lambda i, e: (0, 0)
    token = jnp.arange(tm, dtype=jnp.int32)
    earlier = (token[None, :] < token[:, None]).astype(BF16)
    return pl.pallas_call(
        _moe_kernel,
        grid=(t // tm, N_GROUPS),
        in_specs=[
            pl.BlockSpec((tm, D_MODEL), lambda i, e: (i, 0)),
            pl.BlockSpec((1, D_MODEL), const2),
            pl.BlockSpec((D_MODEL, 2 * ROUTER_LANES), const2),
            pl.BlockSpec((1, ROUTER_LANES), const2),
            pl.BlockSpec((tm, tm), const2),
            pl.BlockSpec((EXPERTS_PER_GROUP, D_MODEL, 2 * D_EXPERT), lambda i, e: (e, 0, 0)),
            pl.BlockSpec((EXPERTS_PER_GROUP, D_EXPERT, D_MODEL), lambda i, e: (e, 0, 0)),
            pl.BlockSpec((1, D_MODEL), const2),
        ],
        out_specs=pl.BlockSpec((tm, D_MODEL), lambda i, e: (i, 0)),
        out_shape=jax.ShapeDtypeStruct((t, D_MODEL), F32),
        scratch_shapes=[pltpu.VMEM((tm, D_MODEL), BF16),
                        pltpu.VMEM((tm, ROUTER_LANES), F32),
                        pltpu.VMEM((tm, ROUTER_LANES), F32),
                        pltpu.VMEM((tm, D_MODEL), F32),
                        pltpu.SMEM((SUBLANES,), jnp.int32)],
        compiler_params=pltpu.CompilerParams(
            dimension_semantics=("parallel", "arbitrary"),
            vmem_limit_bytes=MOE_VMEM_LIMIT),
        name="hier_moe",
    )(h1, g, wr2, br, earlier, wgu_bf, wd_bf, gf)


def _split_bf16(w):
    hi = w.astype(BF16)
    lo = (w - hi.astype(F32)).astype(BF16)
    return jnp.concatenate([hi, lo], axis=-1)


def kernel(x, meta_tokens, norm_mix_g, w_in, conv_dw_w, conv_dw_b, conv_ln_g, conv_ln_b,
           conv_pw_w, ret_gn_g, ret_w_o, w_out, norm_ffn_g, w_group_router, b_group_router,
           w_expert_router, b_expert_router, w_expert_gate, w_expert_up, w_expert_down,
           norm_final_g):
    batch, seq, d = x.shape
    assert d == D_MODEL and seq % RET_STEP == 0 and w_in.shape[0] == 1
    t = batch * seq
    x2d = x.reshape(t, d)
    row = lambda v: v.reshape(1, -1)

    proj_meta, w_in_bf = _meta_proj(meta_tokens, row(norm_mix_g[0]), w_in[0], IN_PROJ_COLS)
    proj, y_conv = _proj_conv(x2d, row(norm_mix_g[0]), w_in_bf, proj_meta, conv_dw_w[0],
                              row(conv_dw_b[0]), row(conv_ln_g[0]), row(conv_ln_b[0]),
                              conv_pw_w[0].astype(BF16), seq, PROJ_TILE)
    h1, w_gu, w_dn = _retention(proj, proj_meta, y_conv, x2d, _retention_tables(seq),
                                row(ret_gn_g[0]), ret_w_o[0].astype(BF16), w_out[0].astype(BF16),
                                w_expert_gate[0], w_expert_up[0], w_expert_down[0], batch, seq)
    w_gu = w_gu.reshape(N_EXPERTS, D_MODEL, 2 * D_EXPERT)
    w_dn = w_dn.reshape(N_EXPERTS, D_EXPERT, D_MODEL)

    w_router = jnp.concatenate([w_group_router[0], w_expert_router[0]], axis=1)
    w_router = jnp.pad(w_router, ((0, 0), (0, ROUTER_LANES - w_router.shape[1])))
    b_router = jnp.concatenate([b_group_router[0], b_expert_router[0]])
    b_router = jnp.pad(b_router, (0, ROUTER_LANES - b_router.shape[0])).reshape(1, -1)
    out = _moe(h1, row(norm_ffn_g[0]), _split_bf16(w_router), b_router, w_gu, w_dn,
               row(norm_final_g), min(1024, t))
    return out.reshape(batch, seq, d)
```

```python
import functools

import jax
import jax.numpy as jnp
from jax import lax
from jax.experimental import pallas as pl
from jax.experimental.pallas import tpu as pltpu

D_MODEL = 1024
CHUNK = 64
N_META = 16
CONV_DIM = 1024
CONV_WIDTH = 31
RET_HEADS = 4
RET_QK_DIM = 256
RET_V_DIM = 512
ROPE_BASE = 10000.0
N_GROUPS = 4
EXPERTS_PER_GROUP = 4
N_EXPERTS = N_GROUPS * EXPERTS_PER_GROUP
D_EXPERT = 512
EPS = 1e-6
D_IN = 2 * CONV_DIM + 2 * RET_HEADS * RET_QK_DIM + 2 * RET_HEADS * RET_V_DIM + 2 * D_MODEL

LANES = 128
SUBLANES = 8
CONV_HALO = 32
IN_PROJ_COLS = 2048
PROJ_TILE = 1024
CONV_TILE = 256
CONV_ROWS_PER_ITER = 4
RET_BLOCK = 256
RET_STEP = 512
ROUTER_LANES = LANES
MOE_BLOCK = 128
MOE_ALIGN = 16
MOE_SPANS = (256, 320, 384)
VMEM_LIMIT = 48 * 1024 * 1024
PROJ_VMEM_LIMIT = 56 * 1024 * 1024
MOE_VMEM_LIMIT = 60 * 1024 * 1024

F32 = jnp.float32
BF16 = jnp.bfloat16


def _sigmoid(x):
    return 1.0 / (1.0 + jnp.exp(-x))


def _dot(a, b):
    return jnp.dot(a, b, preferred_element_type=F32)


def _meta_proj_kernel(x_ref, g_ref, w_ref, o_ref, wbf_ref, u_ref):
    @pl.when(pl.program_id(0) == 0)
    def _():
        x = x_ref[...]
        ms = jnp.mean(x * x, axis=-1, keepdims=True)
        u_ref[...] = (x * lax.rsqrt(ms + EPS) * g_ref[...]).astype(BF16)

    w = w_ref[...].astype(BF16)
    wbf_ref[...] = w
    o_ref[...] = _dot(u_ref[...], w).astype(o_ref.dtype)


def _meta_proj(meta, g, w, tn):
    t, d = meta.shape
    n = w.shape[1]
    return pl.pallas_call(
        _meta_proj_kernel,
        grid=(n // tn,),
        in_specs=[
            pl.BlockSpec((t, d), lambda j: (0, 0)),
            pl.BlockSpec((1, d), lambda j: (0, 0)),
            pl.BlockSpec((d, tn), lambda j: (0, j)),
        ],
        out_specs=[pl.BlockSpec((t, tn), lambda j: (0, j)),
                   pl.BlockSpec((d, tn), lambda j: (0, j))],
        out_shape=[jax.ShapeDtypeStruct((t, n), BF16),
                   jax.ShapeDtypeStruct((d, n), BF16)],
        scratch_shapes=[pltpu.VMEM((t, d), BF16)],
        compiler_params=pltpu.CompilerParams(
            dimension_semantics=("arbitrary",),
            vmem_limit_bytes=VMEM_LIMIT),
        name="meta_proj",
    )(meta, g, w)


def _conv_shifts(win_ref, shift_ref):
    span = CONV_TILE + CONV_HALO - SUBLANES
    for s in range(1, SUBLANES):
        shift_ref[s - 1, 0:span, :] = win_ref[s:s + span, :]


def _conv_taps(cb, r0, win_ref, shift_ref, cbuf_ref, wdw_ref, bdw_ref):
    def group(m):
        return slice(r0 + m * SUBLANES, r0 + (m + 1) * SUBLANES)

    first = CONV_HALO - (CONV_WIDTH - 1)
    cols = pl.ds(pl.multiple_of(cb * LANES, LANES), LANES)
    bias = jnp.broadcast_to(bdw_ref[:, cols], (SUBLANES, LANES))
    accs = [bias] * CONV_ROWS_PER_ITER
    for shift in range(SUBLANES):
        js = [j for j in range(CONV_WIDTH) if (first + j) % SUBLANES == shift]
        tiles = [(first + j) // SUBLANES for j in js]
        taps = [jnp.broadcast_to(wdw_ref[j:j + 1, cols], (SUBLANES, LANES)) for j in js]
        wins = {}
        for m in range(min(tiles), max(tiles) + CONV_ROWS_PER_ITER):
            wins[m] = (win_ref[group(m), cols] if shift == 0
                       else shift_ref[shift - 1, group(m), cols])
        for c in range(CONV_ROWS_PER_ITER):
            for tap, m in zip(taps, tiles):
                accs[c] = accs[c] + tap * wins[m + c]
    for c in range(CONV_ROWS_PER_ITER):
        cbuf_ref[group(c), cols] = accs[c]


def _proj_conv_kernel(x_ref, g_ref, w_ref, ma_ref, mg_ref, wdw_ref, bdw_ref, lng_ref, lnb_ref,
                      wpw_ref, proj_ref, yconv_ref, u_ref, hbuf_ref, win_ref, shift_ref,
                      cbuf_ref, *, tiles_per_seq):
    i, j = pl.program_id(0), pl.program_id(1)
    tm = x_ref.shape[0]

    @pl.when(j == 0)
    def _():
        x = x_ref[...]
        ms = jnp.mean(x * x, axis=-1, keepdims=True)
        u = (x * lax.rsqrt(ms + EPS) * g_ref[...]).astype(BF16)
        u_ref[...] = u

        @pl.when(i % tiles_per_seq == 0)
        def _():
            hbuf_ref[0:CONV_HALO - N_META, :] = jnp.zeros((CONV_HALO - N_META, CONV_DIM), F32)
            ma = ma_ref[...].astype(F32)
            mg = mg_ref[...].astype(F32)
            hbuf_ref[CONV_HALO - N_META:CONV_HALO, :] = ma * _sigmoid(mg)

        @pl.when(i % tiles_per_seq != 0)
        def _():
            hbuf_ref[0:CONV_HALO, :] = hbuf_ref[tm:tm + CONV_HALO, :]

        glu = _dot(u, w_ref[...])
        hbuf_ref[CONV_HALO:CONV_HALO + tm, :] = glu[:, :CONV_DIM] * _sigmoid(glu[:, CONV_DIM:])

    @pl.when(j > 0)
    def _():
        base = pl.multiple_of((j - 1) * CONV_TILE, CONV_TILE)
        win_ref[...] = hbuf_ref[pl.ds(base, CONV_HALO + CONV_TILE), :]
        _conv_shifts(win_ref, shift_ref)
        proj_ref[0] = _dot(u_ref[...], w_ref[...]).astype(proj_ref.dtype)

        def block_body(cb, carry):
            for r0 in range(0, CONV_TILE, SUBLANES * CONV_ROWS_PER_ITER):
                _conv_taps(cb, r0, win_ref, shift_ref, cbuf_ref, wdw_ref, bdw_ref)
            return carry

        lax.fori_loop(0, CONV_DIM // LANES, block_body, 0)
        c = cbuf_ref[...]
        mu = jnp.mean(c, axis=-1, keepdims=True)
        cc = c - mu
        var = jnp.mean(cc * cc, axis=-1, keepdims=True)
        y = (cc * lax.rsqrt(var + EPS) * lng_ref[...] + lnb_ref[...]).astype(BF16)
        y = y * _sigmoid(y)
        yconv_ref[pl.ds(base, CONV_TILE), :] = _dot(y, wpw_ref[...]).astype(yconv_ref.dtype)


def _proj_conv(x2d, g, w_bf, proj_meta, wdw, bdw, lng, lnb, wpw_bf, seq, tm):
    t, d = x2d.shape
    n_steps = D_IN // IN_PROJ_COLS
    assert IN_PROJ_COLS == 2 * CONV_DIM and (n_steps - 1) * CONV_TILE == tm and seq % tm == 0
    const = lambda i, j: (0, 0)
    return pl.pallas_call(
        functools.partial(_proj_conv_kernel, tiles_per_seq=seq // tm),
        grid=(t // tm, n_steps),
        in_specs=[
            pl.BlockSpec((tm, d), lambda i, j: (i, 0)),
            pl.BlockSpec((1, d), const),
            pl.BlockSpec((d, IN_PROJ_COLS), lambda i, j: (0, j)),
            pl.BlockSpec((N_META, CONV_DIM), lambda i, j: (0, 0)),
            pl.BlockSpec((N_META, CONV_DIM), lambda i, j: (0, 1)),
            pl.BlockSpec((CONV_WIDTH, CONV_DIM), const),
            pl.BlockSpec((1, CONV_DIM), const),
            pl.BlockSpec((1, CONV_DIM), const),
            pl.BlockSpec((1, CONV_DIM), const),
            pl.BlockSpec((CONV_DIM, D_MODEL), const),
        ],
        out_specs=[
            pl.BlockSpec((1, tm, IN_PROJ_COLS), lambda i, j: (jnp.maximum(j - 1, 0), i, 0)),
            pl.BlockSpec((tm, D_MODEL), lambda i, j: (i, 0)),
        ],
        out_shape=[jax.ShapeDtypeStruct((n_steps - 1, t, IN_PROJ_COLS), BF16),
                   jax.ShapeDtypeStruct((t, D_MODEL), BF16)],
        scratch_shapes=[pltpu.VMEM((tm, d), BF16),
                        pltpu.VMEM((CONV_HALO + tm, CONV_DIM), F32),
                        pltpu.VMEM((CONV_HALO + CONV_TILE, CONV_DIM), F32),
                        pltpu.VMEM((SUBLANES - 1, CONV_HALO + CONV_TILE, CONV_DIM), F32),
                        pltpu.VMEM((CONV_TILE, CONV_DIM), F32)],
        compiler_params=pltpu.CompilerParams(
            dimension_semantics=("arbitrary", "arbitrary"),
            vmem_limit_bytes=PROJ_VMEM_LIMIT),
        name="proj_conv",
    )(x2d, g, w_bf, proj_meta, proj_meta, wdw, bdw, lng, lnb, wpw_bf)


def _rotary(x, cos, sin):
    half = x.shape[-1] // 2
    x1, x2 = x[:, :half], x[:, half:]
    return jnp.concatenate([x1 * cos - x2 * sin, x2 * cos + x1 * sin], axis=-1)


def _ret_kernel(qk_ref, v_ref, gret_ref, gm_ref, yconv_ref, x_ref, cos_ref, sin_ref,
                mk_ref, mv_ref, mcos_ref, msin_ref, dmat_ref, qdec_ref, kdec_ref, mkdec_ref,
                bdec_ref, gn_ref, wo_ref, wout_ref, eg_ref, eu_ref, ed_ref,
                o_ref, egu_ref, edn_ref, state_ref, gated_ref):
    i = pl.program_id(1)
    k_scale = RET_QK_DIM ** -0.5

    egu_ref[:, :D_EXPERT] = eg_ref[...].astype(BF16)
    egu_ref[:, D_EXPERT:] = eu_ref[...].astype(BF16)
    edn_ref[...] = ed_ref[...].astype(BF16)

    @pl.when(i == 0)
    def _():
        mcos, msin = mcos_ref[...], msin_ref[...]
        for h in range(RET_HEADS):
            mk = mk_ref[:, h * RET_QK_DIM:(h + 1) * RET_QK_DIM].astype(F32)
            mk = _rotary(mk, mcos, msin) * k_scale * mkdec_ref[h]
            mv = mv_ref[:, h * RET_V_DIM:(h + 1) * RET_V_DIM]
            state_ref[h] = lax.dot_general(mk.astype(BF16), mv, (((0,), (0,)), ((), ())),
                                           preferred_element_type=F32)

    for sb in range(RET_STEP // RET_BLOCK):
        rows = slice(sb * RET_BLOCK, (sb + 1) * RET_BLOCK)
        cos, sin = cos_ref[rows, :], sin_ref[rows, :]
        for h in range(RET_HEADS):
            qq = slice(h * RET_QK_DIM, (h + 1) * RET_QK_DIM)
            kk = slice((RET_HEADS + h) * RET_QK_DIM, (RET_HEADS + h + 1) * RET_QK_DIM)
            vv = slice(h * RET_V_DIM, (h + 1) * RET_V_DIM)
            q = _rotary(qk_ref[0, rows, qq].astype(F32), cos, sin)
            k = _rotary(qk_ref[0, rows, kk].astype(F32), cos, sin) * k_scale
            v = v_ref[0, rows, vv]
            q_bf = q.astype(BF16)
            scores = lax.dot_general(q_bf, k.astype(BF16), (((1,), (1,)), ((), ())),
                                     preferred_element_type=F32) * dmat_ref[h]
            state = state_ref[h]
            o = _dot(scores.astype(BF16), v) + _dot(q_bf, state.astype(BF16)) * qdec_ref[h]
            state_ref[h] = state * bdec_ref[h] + lax.dot_general(
                (k * kdec_ref[h]).astype(BF16), v, (((0,), (0,)), ((), ())),
                preferred_element_type=F32)
            mu = jnp.mean(o, axis=-1, keepdims=True)
            oc = o - mu
            var = jnp.mean(oc * oc, axis=-1, keepdims=True)
            on = oc * lax.rsqrt(var + EPS) * gn_ref[:, vv]
            gr = gret_ref[0, rows, vv]
            gated_ref[rows, vv] = gr * _sigmoid(gr) * on.astype(BF16)

    y_ret = _dot(gated_ref[...], wo_ref[...])
    ga = gm_ref[0, :, :D_MODEL]
    gb = gm_ref[0, :, D_MODEL:]
    merged = _sigmoid(ga) * yconv_ref[...] + _sigmoid(gb) * y_ret.astype(BF16)
    o_ref[...] = x_ref[...] + _dot(merged, wout_ref[...])


def _retention(proj, proj_meta, y_conv, x2d, tables, gn, wo_bf, wout_bf, w_gate, w_up, w_down,
               batch, seq):
    t = proj.shape[1]
    nb = seq // RET_STEP
    n_steps = batch * nb
    gu_rows, dn_rows = N_EXPERTS * D_MODEL // n_steps, N_EXPERTS * D_EXPERT // n_steps
    assert gu_rows * n_steps == N_EXPERTS * D_MODEL and dn_rows * n_steps == N_EXPERTS * D_EXPERT
    assert gu_rows % (2 * SUBLANES) == 0 and dn_rows % (2 * SUBLANES) == 0
    hq = RET_HEADS * RET_QK_DIM
    hv = RET_HEADS * RET_V_DIM
    assert 2 * hq == hv == 2 * D_MODEL == proj.shape[2]
    slab = lambda s: pl.BlockSpec((1, RET_STEP, hv), lambda b, i: (s, row(b, i), 0))
    mk_col, mv_col = 2 * CONV_DIM // hq + 1, (2 * CONV_DIM + 2 * hq) // hv
    row = lambda b, i: b * nb + i
    const2 = lambda b, i: (0, 0)
    const3 = lambda b, i: (0, 0, 0)
    cos, sin, mcos, msin, dmat, qdec, kdec, mkdec, bdec = tables
    return pl.pallas_call(
        _ret_kernel,
        grid=(batch, nb),
        in_specs=[
            slab(0), slab(1), slab(2), slab(3),
            pl.BlockSpec((RET_STEP, D_MODEL), lambda b, i: (row(b, i), 0)),
            pl.BlockSpec((RET_STEP, D_MODEL), lambda b, i: (row(b, i), 0)),
            pl.BlockSpec((RET_STEP, RET_QK_DIM // 2), lambda b, i: (i, 0)),
            pl.BlockSpec((RET_STEP, RET_QK_DIM // 2), lambda b, i: (i, 0)),
            pl.BlockSpec((N_META, hq), lambda b, i: (0, mk_col)),
            pl.BlockSpec((N_META, hv), lambda b, i: (0, mv_col)),
            pl.BlockSpec((N_META, RET_QK_DIM // 2), const2),
            pl.BlockSpec((N_META, RET_QK_DIM // 2), const2),
            pl.BlockSpec((RET_HEADS, RET_BLOCK, RET_BLOCK), const3),
            pl.BlockSpec((RET_HEADS, RET_BLOCK, 1), const3),
            pl.BlockSpec((RET_HEADS, RET_BLOCK, 1), const3),
            pl.BlockSpec((RET_HEADS, N_META, 1), const3),
            pl.BlockSpec((RET_HEADS, 1, 1), const3),
            pl.BlockSpec((1, hv), const2),
            pl.BlockSpec((hv, D_MODEL), const2),
            pl.BlockSpec((D_MODEL, D_MODEL), const2),
            pl.BlockSpec((gu_rows, D_EXPERT), lambda b, i: (row(b, i), 0)),
            pl.BlockSpec((gu_rows, D_EXPERT), lambda b, i: (row(b, i), 0)),
            pl.BlockSpec((dn_rows, D_MODEL), lambda b, i: (row(b, i), 0)),
        ],
        out_specs=[
            pl.BlockSpec((RET_STEP, D_MODEL), lambda b, i: (row(b, i), 0)),
            pl.BlockSpec((gu_rows, 2 * D_EXPERT), lambda b, i: (row(b, i), 0)),
            pl.BlockSpec((dn_rows, D_MODEL), lambda b, i: (row(b, i), 0)),
        ],
        out_shape=[jax.ShapeDtypeStruct((t, D_MODEL), F32),
                   jax.ShapeDtypeStruct((N_EXPERTS * D_MODEL, 2 * D_EXPERT), BF16),
                   jax.ShapeDtypeStruct((N_EXPERTS * D_EXPERT, D_MODEL), BF16)],
        scratch_shapes=[pltpu.VMEM((RET_HEADS, RET_QK_DIM, RET_V_DIM), F32),
                        pltpu.VMEM((RET_STEP, hv), BF16)],
        compiler_params=pltpu.CompilerParams(
            dimension_semantics=("parallel", "arbitrary"),
            vmem_limit_bytes=PROJ_VMEM_LIMIT),
        name="retention_mix",
    )(proj, proj, proj, proj, y_conv, x2d, cos, sin, proj_meta, proj_meta, mcos, msin,
      dmat, qdec, kdec, mkdec, bdec, gn, wo_bf, wout_bf,
      w_gate.reshape(N_EXPERTS * D_MODEL, D_EXPERT), w_up.reshape(N_EXPERTS * D_MODEL, D_EXPERT),
      w_down.reshape(N_EXPERTS * D_EXPERT, D_MODEL))


def _retention_tables(seq):
    half = RET_QK_DIM // 2
    inv = ROPE_BASE ** (-jnp.arange(half, dtype=F32) / half)
    pos = jnp.arange(N_META + seq, dtype=F32)
    ang = pos[:, None] * inv[None, :]
    cos_all, sin_all = jnp.cos(ang), jnp.sin(ang)
    log_gamma = jnp.log(1.0 - 2.0 ** (-5.0 - jnp.arange(RET_HEADS, dtype=F32)))
    idx = jnp.arange(RET_BLOCK, dtype=F32)
    chunk = jnp.arange(RET_BLOCK, dtype=jnp.int32) // CHUNK
    visible = chunk[None, :] <= chunk[:, None]
    dmat = jnp.where(visible[None],
                     jnp.exp(log_gamma[:, None, None] * jnp.abs(idx[:, None] - idx[None, :])),
                     0.0)
    qdec = jnp.exp(log_gamma[:, None] * (idx + 1.0))[:, :, None]
    kdec = jnp.exp(log_gamma[:, None] * (RET_BLOCK - 1.0 - idx))[:, :, None]
    midx = jnp.arange(N_META, dtype=F32)
    mkdec = jnp.exp(log_gamma[:, None] * (N_META - 1.0 - midx))[:, :, None]
    bdec = jnp.exp(log_gamma * RET_BLOCK)[:, None, None]
    return (cos_all[N_META:], sin_all[N_META:], cos_all[:N_META], sin_all[:N_META],
            dmat, qdec, kdec, mkdec, bdec)


def _route(logits):
    lane = lax.broadcasted_iota(jnp.int32, logits.shape, 1)
    neg = jnp.float32(-jnp.inf)
    big = jnp.int32(ROUTER_LANES)

    def first_max(masked):
        val = jnp.max(masked, axis=-1, keepdims=True)
        idx = jnp.min(jnp.where(masked == val, lane, big), axis=-1, keepdims=True)
        return val, idx

    gmask = lane < N_GROUPS
    gmax, gidx = first_max(jnp.where(gmask, logits, neg))
    denom = jnp.sum(jnp.where(gmask, jnp.exp(logits - gmax), 0.0), axis=-1, keepdims=True)
    p_group = 1.0 / denom
    assert EXPERTS_PER_GROUP & (EXPERTS_PER_GROUP - 1) == 0
    shift = EXPERTS_PER_GROUP.bit_length() - 1
    lane_group = (lane - N_GROUPS) >> shift
    in_group = jnp.where(lane_group == gidx, logits, neg)
    v1, i1 = first_max(in_group)
    v2, i2 = first_max(jnp.where(lane == i1, neg, in_group))
    e2 = jnp.exp(v2 - v1)
    w1 = p_group / (1.0 + e2)
    w2 = p_group * e2 / (1.0 + e2)
    return jnp.where(lane == i1, w1, 0.0) + jnp.where(lane == i2, w2, 0.0), gidx


def _moe_kernel(h_ref, g_ref, wr_ref, br_ref, tri_ref, wgu_ref, wd_ref, gf_ref, o_ref,
                xs_ref, cs_ref, pos_ref, y_ref, seg_ref):
    grp = pl.program_id(1)
    tm = h_ref.shape[0]

    @pl.when(grp == 0)
    def _():
        h = h_ref[...]
        ms = jnp.mean(h * h, axis=-1, keepdims=True)
        u = h * lax.rsqrt(ms + EPS) * g_ref[...]
        u_hi = u.astype(BF16)
        u_lo = (u - u_hi.astype(F32)).astype(BF16)
        hi_part = _dot(u_hi, wr_ref[...])
        lo_part = _dot(u_lo, wr_ref[:, :ROUTER_LANES])
        logits = (hi_part[:, :ROUTER_LANES] + (hi_part[:, ROUTER_LANES:] + lo_part)
                  + br_ref[...])
        comb, gidx = _route(logits)

        lane = lax.broadcasted_iota(jnp.int32, (tm, ROUTER_LANES), 1)
        onehot = jnp.where(lane == gidx, 1.0, 0.0)
        counts = jnp.sum(onehot, axis=0, keepdims=True)
        row = lax.broadcasted_iota(jnp.int32, (tm, tm), 0)
        prefix = _dot(tri_ref[...], onehot.astype(BF16))
        lane_row = lax.broadcasted_iota(jnp.int32, (1, ROUTER_LANES), 1)
        start = jnp.int32(0)
        starts = jnp.zeros((1, ROUTER_LANES), F32)
        for gg in range(N_GROUPS):
            seg_ref[gg] = start
            starts = starts + jnp.where(lane_row == gg, start.astype(F32), 0.0)
            start = start + jnp.sum(jnp.where(lane_row == gg, counts, 0.0)).astype(jnp.int32)
        seg_ref[N_GROUPS] = start
        pos = jnp.sum(onehot * (prefix + starts), axis=-1, keepdims=True)
        pos_lanes = jnp.broadcast_to(pos, (tm, ROUTER_LANES))
        pos_ref[...] = pos_lanes
        pos_row = pos_lanes.T[0:1, :].astype(jnp.int32)
        perm = jnp.where(row == pos_row, 1.0, 0.0).astype(BF16)
        c_hi = comb.astype(BF16)
        c_lo = (comb - c_hi.astype(F32)).astype(BF16)
        moved = _dot(perm, jnp.concatenate([u_hi, c_hi, c_lo], axis=-1))
        xs_ref[...] = moved[:, :D_MODEL].astype(BF16)
        cs_ref[...] = (moved[:, D_MODEL:D_MODEL + ROUTER_LANES]
                       + moved[:, D_MODEL + ROUTER_LANES:])
        y_ref[...] = jnp.zeros_like(y_ref)

    seg_lo, seg_hi = seg_ref[grp], seg_ref[grp + 1]

    def experts(row0, n_rows):
        rows = pl.ds(pl.multiple_of(row0, MOE_ALIGN), n_rows)
        xb = xs_ref[rows, :]
        cb = cs_ref[rows, :]
        lane = lax.broadcasted_iota(jnp.int32, cb.shape, 1)
        acc = jnp.zeros((n_rows, D_MODEL), F32)
        for e in range(EXPERTS_PER_GROUP):
            expert_lane = N_GROUPS + grp * EXPERTS_PER_GROUP + e
            w_e = jnp.sum(jnp.where(lane == expert_lane, cb, 0.0), axis=-1, keepdims=True)
            gu = _dot(xb, wgu_ref[e])
            gate, up = gu[:, :D_EXPERT], gu[:, D_EXPERT:]
            hid = gate * _sigmoid(gate) * up * w_e
            acc = acc + _dot(hid.astype(BF16), wd_ref[e])
        y_ref[rows, :] += acc

    lo = seg_lo // MOE_ALIGN
    span = (seg_hi + MOE_ALIGN - 1) // MOE_ALIGN - lo
    nonempty = seg_hi > seg_lo
    for k, n_rows in enumerate(MOE_SPANS):
        fits = span <= n_rows // MOE_ALIGN
        if k > 0:
            fits = jnp.logical_and(fits, span > MOE_SPANS[k - 1] // MOE_ALIGN)

        @pl.when(jnp.logical_and(nonempty, fits))
        def _(n_rows=n_rows):
            experts(jnp.minimum(lo * MOE_ALIGN, tm - n_rows), n_rows)

    @pl.when(span > MOE_SPANS[-1] // MOE_ALIGN)
    def _():
        blk_lo = seg_lo // MOE_BLOCK
        n_blk = (seg_hi + MOE_BLOCK - 1) // MOE_BLOCK - blk_lo
        odd = n_blk % 2 == 1
        n_pairs = jnp.where(odd, n_blk - 3, n_blk) // 2

        def pair_body(p, carry):
            experts((blk_lo + 2 * p) * MOE_BLOCK, 2 * MOE_BLOCK)
            return carry

        lax.fori_loop(0, n_pairs, pair_body, 0)

        @pl.when(odd)
        def _():
            experts((blk_lo + n_blk - 3) * MOE_BLOCK, 3 * MOE_BLOCK)

    @pl.when(grp == N_GROUPS - 1)
    def _():
        pos = pos_ref[:, 0:1].astype(jnp.int32)
        col = lax.broadcasted_iota(jnp.int32, (tm, tm), 1)
        unperm = jnp.where(col == pos, 1.0, 0.0).astype(BF16)
        h2 = h_ref[...] + _dot(unperm, y_ref[...].astype(BF16))
        ms = jnp.mean(h2 * h2, axis=-1, keepdims=True)
        o_ref[...] = h2 * lax.rsqrt(ms + EPS) * gf_ref[...]


def _moe(h1, g, wr2, br, wgu_bf, wd_bf, gf, tm):
    t = h1.shape[0]
    const2 = lambda i, e: (0, 0)
    token = jnp.arange(tm, dtype=jnp.int32)
    earlier = (token[None, :] < token[:, None]).astype(BF16)
    return pl.pallas_call(
        _moe_kernel,
        grid=(t // tm, N_GROUPS),
        in_specs=[
            pl.BlockSpec((tm, D_MODEL), lambda i, e: (i, 0)),
            pl.BlockSpec((1, D_MODEL), const2),
            pl.BlockSpec((D_MODEL, 2 * ROUTER_LANES), const2),
            pl.BlockSpec((1, ROUTER_LANES), const2),
            pl.BlockSpec((tm, tm), const2),
            pl.BlockSpec((EXPERTS_PER_GROUP, D_MODEL, 2 * D_EXPERT), lambda i, e: (e, 0, 0)),
            pl.BlockSpec((EXPERTS_PER_GROUP, D_EXPERT, D_MODEL), lambda i, e: (e, 0, 0)),
            pl.BlockSpec((1, D_MODEL), const2),
        ],
        out_specs=pl.BlockSpec((tm, D_MODEL), lambda i, e: (i, 0)),
        out_shape=jax.ShapeDtypeStruct((t, D_MODEL), F32),
        scratch_shapes=[pltpu.VMEM((tm, D_MODEL), BF16),
                        pltpu.VMEM((tm, ROUTER_LANES), F32),
                        pltpu.VMEM((tm, ROUTER_LANES), F32),
                        pltpu.VMEM((tm, D_MODEL), F32),
                        pltpu.SMEM((SUBLANES,), jnp.int32)],
        compiler_params=pltpu.CompilerParams(
            dimension_semantics=("parallel", "arbitrary"),
            vmem_limit_bytes=MOE_VMEM_LIMIT),
        name="hier_moe",
    )(h1, g, wr2, br, earlier, wgu_bf, wd_bf, gf)


def _split_bf16(w):
    hi = w.astype(BF16)
    lo = (w - hi.astype(F32)).astype(BF16)
    return jnp.concatenate([hi, lo], axis=-1)


def kernel(x, meta_tokens, norm_mix_g, w_in, conv_dw_w, conv_dw_b, conv_ln_g, conv_ln_b,
           conv_pw_w, ret_gn_g, ret_w_o, w_out, norm_ffn_g, w_group_router, b_group_router,
           w_expert_router, b_expert_router, w_expert_gate, w_expert_up, w_expert_down,
           norm_final_g):
    batch, seq, d = x.shape
    assert d == D_MODEL and seq % RET_STEP == 0 and w_in.shape[0] == 1
    t = batch * seq
    x2d = x.reshape(t, d)
    row = lambda v: v.reshape(1, -1)

    proj_meta, w_in_bf = _meta_proj(meta_tokens, row(norm_mix_g[0]), w_in[0], IN_PROJ_COLS)
    proj, y_conv = _proj_conv(x2d, row(norm_mix_g[0]), w_in_bf, proj_meta, conv_dw_w[0],
                              row(conv_dw_b[0]), row(conv_ln_g[0]), row(conv_ln_b[0]),
                              conv_pw_w[0].astype(BF16), seq, PROJ_TILE)
    h1, w_gu, w_dn = _retention(proj, proj_meta, y_conv, x2d, _retention_tables(seq),
                                row(ret_gn_g[0]), ret_w_o[0].astype(BF16), w_out[0].astype(BF16),
                                w_expert_gate[0], w_expert_up[0], w_expert_down[0], batch, seq)
    w_gu = w_gu.reshape(N_EXPERTS, D_MODEL, 2 * D_EXPERT)
    w_dn = w_dn.reshape(N_EXPERTS, D_EXPERT, D_MODEL)

    w_router = jnp.concatenate([w_group_router[0], w_expert_router[0]], axis=1)
    w_router = jnp.pad(w_router, ((0, 0), (0, ROUTER_LANES - w_router.shape[1])))
    b_router = jnp.concatenate([b_group_router[0], b_expert_router[0]])
    b_router = jnp.pad(b_router, (0, ROUTER_LANES - b_router.shape[0])).reshape(1, -1)
    out = _moe(h1, row(norm_ffn_g[0]), _split_bf16(w_router), b_router, w_gu, w_dn,
               row(norm_final_g), min(1024, t))
    return out.reshape(batch, seq, d)
```

```python
import functools

import jax
import jax.numpy as jnp
from jax import lax
from jax.experimental import pallas as pl
from jax.experimental.pallas import tpu as pltpu

D_MODEL = 1024
CHUNK = 64
N_META = 16
CONV_DIM = 1024
CONV_WIDTH = 31
RET_HEADS = 4
RET_QK_DIM = 256
RET_V_DIM = 512
ROPE_BASE = 10000.0
N_GROUPS = 4
EXPERTS_PER_GROUP = 4
N_EXPERTS = N_GROUPS * EXPERTS_PER_GROUP
D_EXPERT = 512
EPS = 1e-6
D_IN = 2 * CONV_DIM + 2 * RET_HEADS * RET_QK_DIM + 2 * RET_HEADS * RET_V_DIM + 2 * D_MODEL

LANES = 128
SUBLANES = 8
CONV_HALO = 32
IN_PROJ_COLS = 2048
PROJ_TILE = 1024
CONV_TILE = 256
CONV_ROWS_PER_ITER = 4
RET_BLOCK = 256
RET_STEP = 512
ROUTER_LANES = LANES
MOE_BLOCK = 128
MOE_ALIGN = 16
MOE_SPANS = (256, 320, 384)
VMEM_LIMIT = 48 * 1024 * 1024
PROJ_VMEM_LIMIT = 56 * 1024 * 1024
MOE_VMEM_LIMIT = 60 * 1024 * 1024

F32 = jnp.float32
BF16 = jnp.bfloat16


def _sigmoid(x):
    return 1.0 / (1.0 + jnp.exp(-x))


def _dot(a, b):
    return jnp.dot(a, b, preferred_element_type=F32)


def _meta_proj_kernel(x_ref, g_ref, w_ref, o_ref, wbf_ref, u_ref):
    @pl.when(pl.program_id(0) == 0)
    def _():
        x = x_ref[...]
        ms = jnp.mean(x * x, axis=-1, keepdims=True)
        u_ref[...] = (x * lax.rsqrt(ms + EPS) * g_ref[...]).astype(BF16)

    w = w_ref[...].astype(BF16)
    wbf_ref[...] = w
    o_ref[...] = _dot(u_ref[...], w).astype(o_ref.dtype)


def _meta_proj(meta, g, w, tn):
    t, d = meta.shape
    n = w.shape[1]
    return pl.pallas_call(
        _meta_proj_kernel,
        grid=(n // tn,),
        in_specs=[
            pl.BlockSpec((t, d), lambda j: (0, 0)),
            pl.BlockSpec((1, d), lambda j: (0, 0)),
            pl.BlockSpec((d, tn), lambda j: (0, j)),
        ],
        out_specs=[pl.BlockSpec((t, tn), lambda j: (0, j)),
                   pl.BlockSpec((d, tn), lambda j: (0, j))],
        out_shape=[jax.ShapeDtypeStruct((t, n), BF16),
                   jax.ShapeDtypeStruct((d, n), BF16)],
        scratch_shapes=[pltpu.VMEM((t, d), BF16)],
        compiler_params=pltpu.CompilerParams(
            dimension_semantics=("arbitrary",),
            vmem_limit_bytes=VMEM_LIMIT),
        name="meta_proj",
    )(meta, g, w)


def _conv_shifts(win_ref, shift_ref):
    span = CONV_TILE + CONV_HALO - SUBLANES
    for s in range(1, SUBLANES):
        shift_ref[s - 1, 0:span, :] = win_ref[s:s + span, :]


def _conv_taps(cb, r0, win_ref, shift_ref, cbuf_ref, wdw_ref, bdw_ref):
    def group(m):
        return slice(r0 + m * SUBLANES, r0 + (m + 1) * SUBLANES)

    first = CONV_HALO - (CONV_WIDTH - 1)
    cols = pl.ds(pl.multiple_of(cb * LANES, LANES), LANES)
    bias = jnp.broadcast_to(bdw_ref[:, cols], (SUBLANES, LANES))
    accs = [bias] * CONV_ROWS_PER_ITER
    for shift in range(SUBLANES):
        js = [j for j in range(CONV_WIDTH) if (first + j) % SUBLANES == shift]
        tiles = [(first + j) // SUBLANES for j in js]
        taps = [jnp.broadcast_to(wdw_ref[j:j + 1, cols], (SUBLANES, LANES)) for j in js]
        wins = {}
        for m in range(min(tiles), max(tiles) + CONV_ROWS_PER_ITER):
            wins[m] = (win_ref[group(m), cols] if shift == 0
                       else shift_ref[shift - 1, group(m), cols])
        for c in range(CONV_ROWS_PER_ITER):
            for tap, m in zip(taps, tiles):
                accs[c] = accs[c] + tap * wins[m + c]
    for c in range(CONV_ROWS_PER_ITER):
        cbuf_ref[group(c), cols] = accs[c]


def _proj_conv_kernel(x_ref, g_ref, w_ref, ma_ref, mg_ref, wdw_ref, bdw_ref, lng_ref, lnb_ref,
                      wpw_ref, proj_ref, yconv_ref, u_ref, hbuf_ref, win_ref, shift_ref,
                      cbuf_ref, *, tiles_per_seq):
    i, j = pl.program_id(0), pl.program_id(1)
    tm = x_ref.shape[0]

    @pl.when(j == 0)
    def _():
        x = x_ref[...]
        ms = jnp.mean(x * x, axis=-1, keepdims=True)
        u = (x * lax.rsqrt(ms + EPS) * g_ref[...]).astype(BF16)
        u_ref[...] = u

        @pl.when(i % tiles_per_seq == 0)
        def _():
            hbuf_ref[0:CONV_HALO - N_META, :] = jnp.zeros((CONV_HALO - N_META, CONV_DIM), F32)
            ma = ma_ref[...].astype(F32)
            mg = mg_ref[...].astype(F32)
            hbuf_ref[CONV_HALO - N_META:CONV_HALO, :] = ma * _sigmoid(mg)

        @pl.when(i % tiles_per_seq != 0)
        def _():
            hbuf_ref[0:CONV_HALO, :] = hbuf_ref[tm:tm + CONV_HALO, :]

        glu = _dot(u, w_ref[...])
        hbuf_ref[CONV_HALO:CONV_HALO + tm, :] = glu[:, :CONV_DIM] * _sigmoid(glu[:, CONV_DIM:])

    @pl.when(j > 0)
    def _():
        base = pl.multiple_of((j - 1) * CONV_TILE, CONV_TILE)
        win_ref[...] = hbuf_ref[pl.ds(base, CONV_HALO + CONV_TILE), :]
        _conv_shifts(win_ref, shift_ref)
        proj_ref[0] = _dot(u_ref[...], w_ref[...]).astype(proj_ref.dtype)

        def block_body(cb, carry):
            for r0 in range(0, CONV_TILE, SUBLANES * CONV_ROWS_PER_ITER):
                _conv_taps(cb, r0, win_ref, shift_ref, cbuf_ref, wdw_ref, bdw_ref)
            return carry

        lax.fori_loop(0, CONV_DIM // LANES, block_body, 0)
        c = cbuf_ref[...]
        mu = jnp.mean(c, axis=-1, keepdims=True)
        cc = c - mu
        var = jnp.mean(cc * cc, axis=-1, keepdims=True)
        y = (cc * lax.rsqrt(var + EPS) * lng_ref[...] + lnb_ref[...]).astype(BF16)
        y = y * _sigmoid(y)
        yconv_ref[pl.ds(base, CONV_TILE), :] = _dot(y, wpw_ref[...]).astype(yconv_ref.dtype)


def _proj_conv(x2d, g, w_bf, proj_meta, wdw, bdw, lng, lnb, wpw_bf, seq, tm):
    t, d = x2d.shape
    n_steps = D_IN // IN_PROJ_COLS
    assert IN_PROJ_COLS == 2 * CONV_DIM and (n_steps - 1) * CONV_TILE == tm and seq % tm == 0
    const = lambda i, j: (0, 0)
    return pl.pallas_call(
        functools.partial(_proj_conv_kernel, tiles_per_seq=seq // tm),
        grid=(t // tm, n_steps),
        in_specs=[
            pl.BlockSpec((tm, d), lambda i, j: (i, 0)),
            pl.BlockSpec((1, d), const),
            pl.BlockSpec((d, IN_PROJ_COLS), lambda i, j: (0, j)),
            pl.BlockSpec((N_META, CONV_DIM), lambda i, j: (0, 0)),
            pl.BlockSpec((N_META, CONV_DIM), lambda i, j: (0, 1)),
            pl.BlockSpec((CONV_WIDTH, CONV_DIM), const),
            pl.BlockSpec((1, CONV_DIM), const),
            pl.BlockSpec((1, CONV_DIM), const),
            pl.BlockSpec((1, CONV_DIM), const),
            pl.BlockSpec((CONV_DIM, D_MODEL), const),
        ],
        out_specs=[
            pl.BlockSpec((1, tm, IN_PROJ_COLS), lambda i, j: (jnp.maximum(j - 1, 0), i, 0)),
            pl.BlockSpec((tm, D_MODEL), lambda i, j: (i, 0)),
        ],
        out_shape=[jax.ShapeDtypeStruct((n_steps - 1, t, IN_PROJ_COLS), BF16),
                   jax.ShapeDtypeStruct((t, D_MODEL), BF16)],
        scratch_shapes=[pltpu.VMEM((tm, d), BF16),
                        pltpu.VMEM((CONV_HALO + tm, CONV_DIM), F32),
                        pltpu.VMEM((CONV_HALO + CONV_TILE, CONV_DIM), F32),
                        pltpu.VMEM((SUBLANES - 1, CONV_HALO + CONV_TILE, CONV_DIM), F32),
                        pltpu.VMEM((CONV_TILE, CONV_DIM), F32)],
        compiler_params=pltpu.CompilerParams(
            dimension_semantics=("arbitrary", "arbitrary"),
            vmem_limit_bytes=PROJ_VMEM_LIMIT),
        name="proj_conv",
    )(x2d, g, w_bf, proj_meta, proj_meta, wdw, bdw, lng, lnb, wpw_bf)


def _rotary(x, cos, sin):
    half = x.shape[-1] // 2
    x1, x2 = x[:, :half], x[:, half:]
    return jnp.concatenate([x1 * cos - x2 * sin, x2 * cos + x1 * sin], axis=-1)


def _ret_kernel(qk_ref, v_ref, gret_ref, gm_ref, yconv_ref, x_ref, cos_ref, sin_ref,
                mk_ref, mv_ref, mcos_ref, msin_ref, dmat_ref, qdec_ref, kdec_ref, mkdec_ref,
                bdec_ref, gn_ref, wo_ref, wout_ref, eg_ref, eu_ref, ed_ref,
                o_ref, egu_ref, edn_ref, state_ref, gated_ref):
    i = pl.program_id(1)
    k_scale = RET_QK_DIM ** -0.5

    egu_ref[:, :D_EXPERT] = eg_ref[...].astype(BF16)
    egu_ref[:, D_EXPERT:] = eu_ref[...].astype(BF16)
    edn_ref[...] = ed_ref[...].astype(BF16)

    @pl.when(i == 0)
    def _():
        mcos, msin = mcos_ref[...], msin_ref[...]
        for h in range(RET_HEADS):
            mk = mk_ref[:, h * RET_QK_DIM:(h + 1) * RET_QK_DIM].astype(F32)
            mk = _rotary(mk, mcos, msin) * k_scale * mkdec_ref[h]
            mv = mv_ref[:, h * RET_V_DIM:(h + 1) * RET_V_DIM]
            state_ref[h] = lax.dot_general(mk.astype(BF16), mv, (((0,), (0,)), ((), ())),
                                           preferred_element_type=F32)

    for sb in range(RET_STEP // RET_BLOCK):
        rows = slice(sb * RET_BLOCK, (sb + 1) * RET_BLOCK)
        cos, sin = cos_ref[rows, :], sin_ref[rows, :]
        for h in range(RET_HEADS):
            qq = slice(h * RET_QK_DIM, (h + 1) * RET_QK_DIM)
            kk = slice((RET_HEADS + h) * RET_QK_DIM, (RET_HEADS + h + 1) * RET_QK_DIM)
            vv = slice(h * RET_V_DIM, (h + 1) * RET_V_DIM)
            q = _rotary(qk_ref[0, rows, qq].astype(F32), cos, sin)
            k = _rotary(qk_ref[0, rows, kk].astype(F32), cos, sin) * k_scale
            v = v_ref[0, rows, vv]
            q_bf = q.astype(BF16)
            scores = lax.dot_general(q_bf, k.astype(BF16), (((1,), (1,)), ((), ())),
                                     preferred_element_type=F32) * dmat_ref[h]
            state = state_ref[h]
            o = _dot(scores.astype(BF16), v) + _dot(q_bf, state.astype(BF16)) * qdec_ref[h]
            state_ref[h] = state * bdec_ref[h] + lax.dot_general(
                (k * kdec_ref[h]).astype(BF16), v, (((0,), (0,)), ((), ())),
                preferred_element_type=F32)
            mu = jnp.mean(o, axis=-1, keepdims=True)
            oc = o - mu
            var = jnp.mean(oc * oc, axis=-1, keepdims=True)
            on = oc * lax.rsqrt(var + EPS) * gn_ref[:, vv]
            gr = gret_ref[0, rows, vv]
            gated_ref[rows, vv] = gr * _sigmoid(gr) * on.astype(BF16)

    y_ret = _dot(gated_ref[...], wo_ref[...])
    ga = gm_ref[0, :, :D_MODEL]
    gb = gm_ref[0, :, D_MODEL:]
    merged = _sigmoid(ga) * yconv_ref[...] + _sigmoid(gb) * y_ret.astype(BF16)
    o_ref[...] = x_ref[...] + _dot(merged, wout_ref[...])


def _retention(proj, proj_meta, y_conv, x2d, tables, gn, wo_bf, wout_bf, w_gate, w_up, w_down,
               batch, seq):
    t = proj.shape[1]
    nb = seq // RET_STEP
    n_steps = batch * nb
    gu_rows, dn_rows = N_EXPERTS * D_MODEL // n_steps, N_EXPERTS * D_EXPERT // n_steps
    assert gu_rows * n_steps == N_EXPERTS * D_MODEL and dn_rows * n_steps == N_EXPERTS * D_EXPERT
    assert gu_rows % (2 * SUBLANES) == 0 and dn_rows % (2 * SUBLANES) == 0
    hq = RET_HEADS * RET_QK_DIM
    hv = RET_HEADS * RET_V_DIM
    assert 2 * hq == hv == 2 * D_MODEL == proj.shape[2]
    slab = lambda s: pl.BlockSpec((1, RET_STEP, hv), lambda b, i: (s, row(b, i), 0))
    mk_col, mv_col = 2 * CONV_DIM // hq + 1, (2 * CONV_DIM + 2 * hq) // hv
    row = lambda b, i: b * nb + i
    const2 = lambda b, i: (0, 0)
    const3 = lambda b, i: (0, 0, 0)
    cos, sin, mcos, msin, dmat, qdec, kdec, mkdec, bdec = tables
    return pl.pallas_call(
        _ret_kernel,
        grid=(batch, nb),
        in_specs=[
            slab(0), slab(1), slab(2), slab(3),
            pl.BlockSpec((RET_STEP, D_MODEL), lambda b, i: (row(b, i), 0)),
            pl.BlockSpec((RET_STEP, D_MODEL), lambda b, i: (row(b, i), 0)),
            pl.BlockSpec((RET_STEP, RET_QK_DIM // 2), lambda b, i: (i, 0)),
            pl.BlockSpec((RET_STEP, RET_QK_DIM // 2), lambda b, i: (i, 0)),
            pl.BlockSpec((N_META, hq), lambda b, i: (0, mk_col)),
            pl.BlockSpec((N_META, hv), lambda b, i: (0, mv_col)),
            pl.BlockSpec((N_META, RET_QK_DIM // 2), const2),
            pl.BlockSpec((N_META, RET_QK_DIM // 2), const2),
            pl.BlockSpec((RET_HEADS, RET_BLOCK, RET_BLOCK), const3),
            pl.BlockSpec((RET_HEADS, RET_BLOCK, 1), const3),
            pl.BlockSpec((RET_HEADS, RET_BLOCK, 1), const3),
            pl.BlockSpec((RET_HEADS, N_META, 1), const3),
            pl.BlockSpec((RET_HEADS, 1, 1), const3),
            pl.BlockSpec((1, hv), const2),
            pl.BlockSpec((hv, D_MODEL), const2),
            pl.BlockSpec((D_MODEL, D_MODEL), const2),
            pl.BlockSpec((gu_rows, D_EXPERT), lambda b, i: (row(b, i), 0)),
            pl.BlockSpec((gu_rows, D_EXPERT), lambda b, i: (row(b, i), 0)),
            pl.BlockSpec((dn_rows, D_MODEL), lambda b, i: (row(b, i), 0)),
        ],
        out_specs=[
            pl.BlockSpec((RET_STEP, D_MODEL), lambda b, i: (row(b, i), 0)),
            pl.BlockSpec((gu_rows, 2 * D_EXPERT), lambda b, i: (row(b, i), 0)),
            pl.BlockSpec((dn_rows, D_MODEL), lambda b, i: (row(b, i), 0)),
        ],
        out_shape=[jax.ShapeDtypeStruct((t, D_MODEL), F32),
                   jax.ShapeDtypeStruct((N_EXPERTS * D_MODEL, 2 * D_EXPERT), BF16),
                   jax.ShapeDtypeStruct((N_EXPERTS * D_EXPERT, D_MODEL), BF16)],
        scratch_shapes=[pltpu.VMEM((RET_HEADS, RET_QK_DIM, RET_V_DIM), F32),
                        pltpu.VMEM((RET_STEP, hv), BF16)],
        compiler_params=pltpu.CompilerParams(
            dimension_semantics=("parallel", "arbitrary"),
            vmem_limit_bytes=PROJ_VMEM_LIMIT),
        name="retention_mix",
    )(proj, proj, proj, proj, y_conv, x2d, cos, sin, proj_meta, proj_meta, mcos, msin,
      dmat, qdec, kdec, mkdec, bdec, gn, wo_bf, wout_bf,
      w_gate.reshape(N_EXPERTS * D_MODEL, D_EXPERT), w_up.reshape(N_EXPERTS * D_MODEL, D_EXPERT),
      w_down.reshape(N_EXPERTS * D_EXPERT, D_MODEL))


def _retention_tables(seq):
    half = RET_QK_DIM // 2
    inv = ROPE_BASE ** (-jnp.arange(half, dtype=F32) / half)
    pos = jnp.arange(N_META + seq, dtype=F32)
    ang = pos[:, None] * inv[None, :]
    cos_all, sin_all = jnp.cos(ang), jnp.sin(ang)
    log_gamma = jnp.log(1.0 - 2.0 ** (-5.0 - jnp.arange(RET_HEADS, dtype=F32)))
    idx = jnp.arange(RET_BLOCK, dtype=F32)
    chunk = jnp.arange(RET_BLOCK, dtype=jnp.int32) // CHUNK
    visible = chunk[None, :] <= chunk[:, None]
    dmat = jnp.where(visible[None],
                     jnp.exp(log_gamma[:, None, None] * jnp.abs(idx[:, None] - idx[None, :])),
                     0.0)
    qdec = jnp.exp(log_gamma[:, None] * (idx + 1.0))[:, :, None]
    kdec = jnp.exp(log_gamma[:, None] * (RET_BLOCK - 1.0 - idx))[:, :, None]
    midx = jnp.arange(N_META, dtype=F32)
    mkdec = jnp.exp(log_gamma[:, None] * (N_META - 1.0 - midx))[:, :, None]
    bdec = jnp.exp(log_gamma * RET_BLOCK)[:, None, None]
    return (cos_all[N_META:], sin_all[N_META:], cos_all[:N_META], sin_all[:N_META],
            dmat, qdec, kdec, mkdec, bdec)


def _route(logits):
    lane = lax.broadcasted_iota(jnp.int32, logits.shape, 1)
    neg = jnp.float32(-jnp.inf)
    big = jnp.int32(ROUTER_LANES)

    def first_max(masked):
        val = jnp.max(masked, axis=-1, keepdims=True)
        idx = jnp.min(jnp.where(masked == val, lane, big), axis=-1, keepdims=True)
        return val, idx

    gmask = lane < N_GROUPS
    gmax, gidx = first_max(jnp.where(gmask, logits, neg))
    denom = jnp.sum(jnp.where(gmask, jnp.exp(logits - gmax), 0.0), axis=-1, keepdims=True)
    p_group = 1.0 / denom
    assert EXPERTS_PER_GROUP & (EXPERTS_PER_GROUP - 1) == 0
    shift = EXPERTS_PER_GROUP.bit_length() - 1
    lane_group = (lane - N_GROUPS) >> shift
    in_group = jnp.where(lane_group == gidx, logits, neg)
    v1, i1 = first_max(in_group)
    v2, i2 = first_max(jnp.where(lane == i1, neg, in_group))
    e2 = jnp.exp(v2 - v1)
    w1 = p_group / (1.0 + e2)
    w2 = p_group * e2 / (1.0 + e2)
    return jnp.where(lane == i1, w1, 0.0) + jnp.where(lane == i2, w2, 0.0), gidx


def _moe_kernel(h_ref, g_ref, wr_ref, br_ref, tri_ref, wgu_ref, wd_ref, gf_ref, o_ref,
                xs_ref, cs_ref, pos_ref, y_ref, seg_ref):
    grp = pl.program_id(1)
    tm = h_ref.shape[0]

    @pl.when(grp == 0)
    def _():
        h = h_ref[...]
        ms = jnp.mean(h * h, axis=-1, keepdims=True)
        u = h * lax.rsqrt(ms + EPS) * g_ref[...]
        u_hi = u.astype(BF16)
        u_lo = (u - u_hi.astype(F32)).astype(BF16)
        hi_part = _dot(u_hi, wr_ref[...])
        lo_part = _dot(u_lo, wr_ref[:, :ROUTER_LANES])
        logits = (hi_part[:, :ROUTER_LANES] + (hi_part[:, ROUTER_LANES:] + lo_part)
                  + br_ref[...])
        comb, gidx = _route(logits)

        lane = lax.broadcasted_iota(jnp.int32, (tm, ROUTER_LANES), 1)
        onehot = jnp.where(lane == gidx, 1.0, 0.0)
        counts = jnp.sum(onehot, axis=0, keepdims=True)
        row = lax.broadcasted_iota(jnp.int32, (tm, tm), 0)
        prefix = _dot(tri_ref[...], onehot.astype(BF16))
        lane_row = lax.broadcasted_iota(jnp.int32, (1, ROUTER_LANES), 1)
        start = jnp.int32(0)
        starts = jnp.zeros((1, ROUTER_LANES), F32)
        for gg in range(N_GROUPS):
            seg_ref[gg] = start
            starts = starts + jnp.where(lane_row == gg, start.astype(F32), 0.0)
            start = start + jnp.sum(jnp.where(lane_row == gg, counts, 0.0)).astype(jnp.int32)
        seg_ref[N_GROUPS] = start
        pos = jnp.sum(onehot * (prefix + starts), axis=-1, keepdims=True)
        pos_lanes = jnp.broadcast_to(pos, (tm, ROUTER_LANES))
        pos_ref[...] = pos_lanes
        pos_row = pos_lanes.T[0:1, :].astype(jnp.int32)
        perm = jnp.where(row == pos_row, 1.0, 0.0).astype(BF16)
        c_hi = comb.astype(BF16)
        c_lo = (comb - c_hi.astype(F32)).astype(BF16)
        moved = _dot(perm, jnp.concatenate([u_hi, c_hi, c_lo], axis=-1))
        xs_ref[...] = moved[:, :D_MODEL].astype(BF16)
        cs_ref[...] = (moved[:, D_MODEL:D_MODEL + ROUTER_LANES]
                       + moved[:, D_MODEL + ROUTER_LANES:])
        y_ref[...] = jnp.zeros_like(y_ref)

    seg_lo, seg_hi = seg_ref[grp], seg_ref[grp + 1]

    def experts(row0, n_rows):
        rows = pl.ds(pl.multiple_of(row0, MOE_ALIGN), n_rows)
        xb = xs_ref[rows, :]
        cb = cs_ref[rows, :]
        lane = lax.broadcasted_iota(jnp.int32, cb.shape, 1)
        hidden = []
        for e in range(EXPERTS_PER_GROUP):
            expert_lane = N_GROUPS + grp * EXPERTS_PER_GROUP + e
            w_e = jnp.sum(jnp.where(lane == expert_lane, cb, 0.0), axis=-1, keepdims=True)
            gu = _dot(xb, wgu_ref[e])
            gate, up = gu[:, :D_EXPERT], gu[:, D_EXPERT:]
            hidden.append((gate * _sigmoid(gate) * up * w_e).astype(BF16))
        w_down = wd_ref[...].reshape(EXPERTS_PER_GROUP * D_EXPERT, D_MODEL)
        y_ref[rows, :] += _dot(jnp.concatenate(hidden, axis=-1), w_down)

    lo = seg_lo // MOE_ALIGN
    span = (seg_hi + MOE_ALIGN - 1) // MOE_ALIGN - lo
    nonempty = seg_hi > seg_lo
    for k, n_rows in enumerate(MOE_SPANS):
        fits = span <= n_rows // MOE_ALIGN
        if k > 0:
            fits = jnp.logical_and(fits, span > MOE_SPANS[k - 1] // MOE_ALIGN)

        @pl.when(jnp.logical_and(nonempty, fits))
        def _(n_rows=n_rows):
            experts(jnp.minimum(lo * MOE_ALIGN, tm - n_rows), n_rows)

    @pl.when(span > MOE_SPANS[-1] // MOE_ALIGN)
    def _():
        blk_lo = seg_lo // MOE_BLOCK
        n_blk = (seg_hi + MOE_BLOCK - 1) // MOE_BLOCK - blk_lo
        odd = n_blk % 2 == 1
        n_pairs = jnp.where(odd, n_blk - 3, n_blk) // 2

        def pair_body(p, carry):
            experts((blk_lo + 2 * p) * MOE_BLOCK, 2 * MOE_BLOCK)
            return carry

        lax.fori_loop(0, n_pairs, pair_body, 0)

        @pl.when(odd)
        def _():
            experts((blk_lo + n_blk - 3) * MOE_BLOCK, 3 * MOE_BLOCK)

    @pl.when(grp == N_GROUPS - 1)
    def _():
        pos = pos_ref[:, 0:1].astype(jnp.int32)
        col = lax.broadcasted_iota(jnp.int32, (tm, tm), 1)
        unperm = jnp.where(col == pos, 1.0, 0.0).astype(BF16)
        h2 = h_ref[...] + _dot(unperm, y_ref[...].astype(BF16))
        ms = jnp.mean(h2 * h2, axis=-1, keepdims=True)
        o_ref[...] = h2 * lax.rsqrt(ms + EPS) * gf_ref[...]


def _moe(h1, g, wr2, br, wgu_bf, wd_bf, gf, tm):
    t = h1.shape[0]
    const2 = lambda i, e: (0, 0)
    token = jnp.arange(tm, dtype=jnp.int32)
    earlier = (token[None, :] < token[:, None]).astype(BF16)
    return pl.pallas_call(
        _moe_kernel,
        grid=(t // tm, N_GROUPS),
        in_specs=[
            pl.BlockSpec((tm, D_MODEL), lambda i, e: (i, 0)),
            pl.BlockSpec((1, D_MODEL), const2),
            pl.BlockSpec((D_MODEL, 2 * ROUTER_LANES), const2),
            pl.BlockSpec((1, ROUTER_LANES), const2),
            pl.BlockSpec((tm, tm), const2),
            pl.BlockSpec((EXPERTS_PER_GROUP, D_MODEL, 2 * D_EXPERT), lambda i, e: (e, 0, 0)),
            pl.BlockSpec((EXPERTS_PER_GROUP, D_EXPERT, D_MODEL), lambda i, e: (e, 0, 0)),
            pl.BlockSpec((1, D_MODEL), const2),
        ],
        out_specs=pl.BlockSpec((tm, D_MODEL), lambda i, e: (i, 0)),
        out_shape=jax.ShapeDtypeStruct((t, D_MODEL), F32),
        scratch_shapes=[pltpu.VMEM((tm, D_MODEL), BF16),
                        pltpu.VMEM((tm, ROUTER_LANES), F32),
                        pltpu.VMEM((tm, ROUTER_LANES), F32),
                        pltpu.VMEM((tm, D_MODEL), F32),
                        pltpu.SMEM((SUBLANES,), jnp.int32)],
        compiler_params=pltpu.CompilerParams(
            dimension_semantics=("parallel", "arbitrary"),
            vmem_limit_bytes=MOE_VMEM_LIMIT),
        name="hier_moe",
    )(h1, g, wr2, br, earlier, wgu_bf, wd_bf, gf)


def _split_bf16(w):
    hi = w.astype(BF16)
    lo = (w - hi.astype(F32)).astype(BF16)
    return jnp.concatenate([hi, lo], axis=-1)


def kernel(x, meta_tokens, norm_mix_g, w_in, conv_dw_w, conv_dw_b, conv_ln_g, conv_ln_b,
           conv_pw_w, ret_gn_g, ret_w_o, w_out, norm_ffn_g, w_group_router, b_group_router,
           w_expert_router, b_expert_router, w_expert_gate, w_expert_up, w_expert_down,
           norm_final_g):
    batch, seq, d = x.shape
    assert d == D_MODEL and seq % RET_STEP == 0 and w_in.shape[0] == 1
    t = batch * seq
    x2d = x.reshape(t, d)
    row = lambda v: v.reshape(1, -1)

    proj_meta, w_in_bf = _meta_proj(meta_tokens, row(norm_mix_g[0]), w_in[0], IN_PROJ_COLS)
    proj, y_conv = _proj_conv(x2d, row(norm_mix_g[0]), w_in_bf, proj_meta, conv_dw_w[0],
                              row(conv_dw_b[0]), row(conv_ln_g[0]), row(conv_ln_b[0]),
                              conv_pw_w[0].astype(BF16), seq, PROJ_TILE)
    h1, w_gu, w_dn = _retention(proj, proj_meta, y_conv, x2d, _retention_tables(seq),
                                row(ret_gn_g[0]), ret_w_o[0].astype(BF16), w_out[0].astype(BF16),
                                w_expert_gate[0], w_expert_up[0], w_expert_down[0], batch, seq)
    w_gu = w_gu.reshape(N_EXPERTS, D_MODEL, 2 * D_EXPERT)
    w_dn = w_dn.reshape(N_EXPERTS, D_EXPERT, D_MODEL)

    w_router = jnp.concatenate([w_group_router[0], w_expert_router[0]], axis=1)
    w_router = jnp.pad(w_router, ((0, 0), (0, ROUTER_LANES - w_router.shape[1])))
    b_router = jnp.concatenate([b_group_router[0], b_expert_router[0]])
    b_router = jnp.pad(b_router, (0, ROUTER_LANES - b_router.shape[0])).reshape(1, -1)
    out = _moe(h1, row(norm_ffn_g[0]), _split_bf16(w_router), b_router, w_gu, w_dn,
               row(norm_final_g), min(1024, t))
    return out.reshape(batch, seq, d)
```

```python
import functools

import jax
import jax.numpy as jnp
from jax import lax
from jax.experimental import pallas as pl
from jax.experimental.pallas import tpu as pltpu

D_MODEL = 1024
CHUNK = 64
N_META = 16
CONV_DIM = 1024
CONV_WIDTH = 31
RET_HEADS = 4
RET_QK_DIM = 256
RET_V_DIM = 512
ROPE_BASE = 10000.0
N_GROUPS = 4
EXPERTS_PER_GROUP = 4
N_EXPERTS = N_GROUPS * EXPERTS_PER_GROUP
D_EXPERT = 512
EPS = 1e-6
D_IN = 2 * CONV_DIM + 2 * RET_HEADS * RET_QK_DIM + 2 * RET_HEADS * RET_V_DIM + 2 * D_MODEL

LANES = 128
SUBLANES = 8
CONV_HALO = 32
IN_PROJ_COLS = 2048
PROJ_TILE = 1024
CONV_TILE = 256
CONV_ROWS_PER_ITER = 4
RET_BLOCK = 256
RET_STEP = 512
ROUTER_LANES = LANES
MOE_BLOCK = 128
MOE_ALIGN = 16
MOE_SPANS = (256, 320, 384)
VMEM_LIMIT = 48 * 1024 * 1024
PROJ_VMEM_LIMIT = 56 * 1024 * 1024
MOE_VMEM_LIMIT = 60 * 1024 * 1024

F32 = jnp.float32
BF16 = jnp.bfloat16


def _sigmoid(x):
    return 1.0 / (1.0 + jnp.exp(-x))


def _dot(a, b):
    return jnp.dot(a, b, preferred_element_type=F32)


def _meta_proj_kernel(x_ref, g_ref, w_ref, o_ref, wbf_ref, u_ref):
    @pl.when(pl.program_id(0) == 0)
    def _():
        x = x_ref[...]
        ms = jnp.mean(x * x, axis=-1, keepdims=True)
        u_ref[...] = (x * lax.rsqrt(ms + EPS) * g_ref[...]).astype(BF16)

    w = w_ref[...].astype(BF16)
    wbf_ref[...] = w
    o_ref[...] = _dot(u_ref[...], w).astype(o_ref.dtype)


def _meta_proj(meta, g, w, tn):
    t, d = meta.shape
    n = w.shape[1]
    return pl.pallas_call(
        _meta_proj_kernel,
        grid=(n // tn,),
        in_specs=[
            pl.BlockSpec((t, d), lambda j: (0, 0)),
            pl.BlockSpec((1, d), lambda j: (0, 0)),
            pl.BlockSpec((d, tn), lambda j: (0, j)),
        ],
        out_specs=[pl.BlockSpec((t, tn), lambda j: (0, j)),
                   pl.BlockSpec((d, tn), lambda j: (0, j))],
        out_shape=[jax.ShapeDtypeStruct((t, n), BF16),
                   jax.ShapeDtypeStruct((d, n), BF16)],
        scratch_shapes=[pltpu.VMEM((t, d), BF16)],
        compiler_params=pltpu.CompilerParams(
            dimension_semantics=("arbitrary",),
            vmem_limit_bytes=VMEM_LIMIT),
        name="meta_proj",
    )(meta, g, w)


def _conv_shifts(win_ref, shift_ref):
    span = CONV_TILE + CONV_HALO - SUBLANES
    for s in range(1, SUBLANES):
        shift_ref[s - 1, 0:span, :] = win_ref[s:s + span, :]


def _conv_taps(cb, r0, win_ref, shift_ref, cbuf_ref, wdw_ref, bdw_ref):
    def group(m):
        return slice(r0 + m * SUBLANES, r0 + (m + 1) * SUBLANES)

    first = CONV_HALO - (CONV_WIDTH - 1)
    cols = pl.ds(pl.multiple_of(cb * LANES, LANES), LANES)
    bias = jnp.broadcast_to(bdw_ref[:, cols], (SUBLANES, LANES))
    accs = [bias] * CONV_ROWS_PER_ITER
    for shift in range(SUBLANES):
        js = [j for j in range(CONV_WIDTH) if (first + j) % SUBLANES == shift]
        tiles = [(first + j) // SUBLANES for j in js]
        taps = [jnp.broadcast_to(wdw_ref[j:j + 1, cols], (SUBLANES, LANES)) for j in js]
        wins = {}
        for m in range(min(tiles), max(tiles) + CONV_ROWS_PER_ITER):
            wins[m] = (win_ref[group(m), cols] if shift == 0
                       else shift_ref[shift - 1, group(m), cols])
        for c in range(CONV_ROWS_PER_ITER):
            for tap, m in zip(taps, tiles):
                accs[c] = accs[c] + tap * wins[m + c]
    for c in range(CONV_ROWS_PER_ITER):
        cbuf_ref[group(c), cols] = accs[c]


def _proj_conv_kernel(x_ref, g_ref, w_ref, ma_ref, mg_ref, wdw_ref, bdw_ref, lng_ref, lnb_ref,
                      wpw_ref, proj_ref, yconv_ref, u_ref, hbuf_ref, win_ref, shift_ref,
                      cbuf_ref, *, tiles_per_seq):
    i, j = pl.program_id(0), pl.program_id(1)
    tm = x_ref.shape[0]

    @pl.when(j == 0)
    def _():
        x = x_ref[...]
        ms = jnp.mean(x * x, axis=-1, keepdims=True)
        u = (x * lax.rsqrt(ms + EPS) * g_ref[...]).astype(BF16)
        u_ref[...] = u

        @pl.when(i % tiles_per_seq == 0)
        def _():
            hbuf_ref[0:CONV_HALO - N_META, :] = jnp.zeros((CONV_HALO - N_META, CONV_DIM), F32)
            ma = ma_ref[...].astype(F32)
            mg = mg_ref[...].astype(F32)
            hbuf_ref[CONV_HALO - N_META:CONV_HALO, :] = ma * _sigmoid(mg)

        @pl.when(i % tiles_per_seq != 0)
        def _():
            hbuf_ref[0:CONV_HALO, :] = hbuf_ref[tm:tm + CONV_HALO, :]

        glu = _dot(u, w_ref[...])
        hbuf_ref[CONV_HALO:CONV_HALO + tm, :] = glu[:, :CONV_DIM] * _sigmoid(glu[:, CONV_DIM:])

    @pl.when(j > 0)
    def _():
        base = pl.multiple_of((j - 1) * CONV_TILE, CONV_TILE)
        win_ref[...] = hbuf_ref[pl.ds(base, CONV_HALO + CONV_TILE), :]
        _conv_shifts(win_ref, shift_ref)
        proj_ref[0] = _dot(u_ref[...], w_ref[...]).astype(proj_ref.dtype)

        def block_body(cb, carry):
            for r0 in range(0, CONV_TILE, SUBLANES * CONV_ROWS_PER_ITER):
                _conv_taps(cb, r0, win_ref, shift_ref, cbuf_ref, wdw_ref, bdw_ref)
            return carry

        lax.fori_loop(0, CONV_DIM // LANES, block_body, 0)
        c = cbuf_ref[...]
        mu = jnp.mean(c, axis=-1, keepdims=True)
        cc = c - mu
        var = jnp.mean(cc * cc, axis=-1, keepdims=True)
        y = (cc * lax.rsqrt(var + EPS) * lng_ref[...] + lnb_ref[...]).astype(BF16)
        y = y * _sigmoid(y)
        yconv_ref[pl.ds(base, CONV_TILE), :] = _dot(y, wpw_ref[...]).astype(yconv_ref.dtype)


def _proj_conv(x2d, g, w_bf, proj_meta, wdw, bdw, lng, lnb, wpw_bf, seq, tm):
    t, d = x2d.shape
    n_steps = D_IN // IN_PROJ_COLS
    assert IN_PROJ_COLS == 2 * CONV_DIM and (n_steps - 1) * CONV_TILE == tm and seq % tm == 0
    const = lambda i, j: (0, 0)
    return pl.pallas_call(
        functools.partial(_proj_conv_kernel, tiles_per_seq=seq // tm),
        grid=(t // tm, n_steps),
        in_specs=[
            pl.BlockSpec((tm, d), lambda i, j: (i, 0)),
            pl.BlockSpec((1, d), const),
            pl.BlockSpec((d, IN_PROJ_COLS), lambda i, j: (0, j)),
            pl.BlockSpec((N_META, CONV_DIM), lambda i, j: (0, 0)),
            pl.BlockSpec((N_META, CONV_DIM), lambda i, j: (0, 1)),
            pl.BlockSpec((CONV_WIDTH, CONV_DIM), const),
            pl.BlockSpec((1, CONV_DIM), const),
            pl.BlockSpec((1, CONV_DIM), const),
            pl.BlockSpec((1, CONV_DIM), const),
            pl.BlockSpec((CONV_DIM, D_MODEL), const),
        ],
        out_specs=[
            pl.BlockSpec((1, tm, IN_PROJ_COLS), lambda i, j: (jnp.maximum(j - 1, 0), i, 0)),
            pl.BlockSpec((tm, D_MODEL), lambda i, j: (i, 0)),
        ],
        out_shape=[jax.ShapeDtypeStruct((n_steps - 1, t, IN_PROJ_COLS), BF16),
                   jax.ShapeDtypeStruct((t, D_MODEL), BF16)],
        scratch_shapes=[pltpu.VMEM((tm, d), BF16),
                        pltpu.VMEM((CONV_HALO + tm, CONV_DIM), F32),
                        pltpu.VMEM((CONV_HALO + CONV_TILE, CONV_DIM), F32),
                        pltpu.VMEM((SUBLANES - 1, CONV_HALO + CONV_TILE, CONV_DIM), F32),
                        pltpu.VMEM((CONV_TILE, CONV_DIM), F32)],
        compiler_params=pltpu.CompilerParams(
            dimension_semantics=("arbitrary", "arbitrary"),
            vmem_limit_bytes=PROJ_VMEM_LIMIT),
        name="proj_conv",
    )(x2d, g, w_bf, proj_meta, proj_meta, wdw, bdw, lng, lnb, wpw_bf)


def _rotary(x, cos, sin):
    half = x.shape[-1] // 2
    x1, x2 = x[:, :half], x[:, half:]
    return jnp.concatenate([x1 * cos - x2 * sin, x2 * cos + x1 * sin], axis=-1)


def _ret_kernel(qk_ref, v_ref, gret_ref, gm_ref, yconv_ref, x_ref, cos_ref, sin_ref,
                mk_ref, mv_ref, mcos_ref, msin_ref, dmat_ref, qdec_ref, kdec_ref, mkdec_ref,
                bdec_ref, gn_ref, wo_ref, wout_ref, eg_ref, eu_ref, ed_ref,
                o_ref, egu_ref, edn_ref, state_ref, gated_ref):
    i = pl.program_id(1)
    k_scale = RET_QK_DIM ** -0.5

    egu_ref[:, :D_EXPERT] = eg_ref[...].astype(BF16)
    egu_ref[:, D_EXPERT:] = eu_ref[...].astype(BF16)
    edn_ref[...] = ed_ref[...].astype(BF16)

    @pl.when(i == 0)
    def _():
        mcos, msin = mcos_ref[...], msin_ref[...]
        for h in range(RET_HEADS):
            mk = mk_ref[:, h * RET_QK_DIM:(h + 1) * RET_QK_DIM].astype(F32)
            mk = _rotary(mk, mcos, msin) * k_scale * mkdec_ref[h]
            mv = mv_ref[:, h * RET_V_DIM:(h + 1) * RET_V_DIM]
            state_ref[h] = lax.dot_general(mk.astype(BF16), mv, (((0,), (0,)), ((), ())),
                                           preferred_element_type=F32)

    for sb in range(RET_STEP // RET_BLOCK):
        rows = slice(sb * RET_BLOCK, (sb + 1) * RET_BLOCK)
        cos, sin = cos_ref[rows, :], sin_ref[rows, :]
        for h in range(RET_HEADS):
            qq = slice(h * RET_QK_DIM, (h + 1) * RET_QK_DIM)
            kk = slice((RET_HEADS + h) * RET_QK_DIM, (RET_HEADS + h + 1) * RET_QK_DIM)
            vv = slice(h * RET_V_DIM, (h + 1) * RET_V_DIM)
            q = _rotary(qk_ref[0, rows, qq].astype(F32), cos, sin)
            k = _rotary(qk_ref[0, rows, kk].astype(F32), cos, sin) * k_scale
            v = v_ref[0, rows, vv]
            q_bf = q.astype(BF16)
            scores = lax.dot_general(q_bf, k.astype(BF16), (((1,), (1,)), ((), ())),
                                     preferred_element_type=F32) * dmat_ref[h]
            state = state_ref[h]
            lhs = jnp.concatenate([scores.astype(BF16), (q * qdec_ref[h]).astype(BF16)], axis=1)
            o = _dot(lhs, jnp.concatenate([v, state.astype(BF16)], axis=0))
            state_ref[h] = state * bdec_ref[h] + lax.dot_general(
                (k * kdec_ref[h]).astype(BF16), v, (((0,), (0,)), ((), ())),
                preferred_element_type=F32)
            mu = jnp.mean(o, axis=-1, keepdims=True)
            oc = o - mu
            var = jnp.mean(oc * oc, axis=-1, keepdims=True)
            on = oc * lax.rsqrt(var + EPS) * gn_ref[:, vv]
            gr = gret_ref[0, rows, vv]
            gated_ref[rows, vv] = gr * _sigmoid(gr) * on.astype(BF16)

    y_ret = _dot(gated_ref[...], wo_ref[...])
    ga = gm_ref[0, :, :D_MODEL]
    gb = gm_ref[0, :, D_MODEL:]
    merged = _sigmoid(ga) * yconv_ref[...] + _sigmoid(gb) * y_ret.astype(BF16)
    o_ref[...] = x_ref[...] + _dot(merged, wout_ref[...])


def _retention(proj, proj_meta, y_conv, x2d, tables, gn, wo_bf, wout_bf, w_gate, w_up, w_down,
               batch, seq):
    t = proj.shape[1]
    nb = seq // RET_STEP
    n_steps = batch * nb
    gu_rows, dn_rows = N_EXPERTS * D_MODEL // n_steps, N_EXPERTS * D_EXPERT // n_steps
    assert gu_rows * n_steps == N_EXPERTS * D_MODEL and dn_rows * n_steps == N_EXPERTS * D_EXPERT
    assert gu_rows % (2 * SUBLANES) == 0 and dn_rows % (2 * SUBLANES) == 0
    hq = RET_HEADS * RET_QK_DIM
    hv = RET_HEADS * RET_V_DIM
    assert 2 * hq == hv == 2 * D_MODEL == proj.shape[2]
    slab = lambda s: pl.BlockSpec((1, RET_STEP, hv), lambda b, i: (s, row(b, i), 0))
    mk_col, mv_col = 2 * CONV_DIM // hq + 1, (2 * CONV_DIM + 2 * hq) // hv
    row = lambda b, i: b * nb + i
    const2 = lambda b, i: (0, 0)
    const3 = lambda b, i: (0, 0, 0)
    cos, sin, mcos, msin, dmat, qdec, kdec, mkdec, bdec = tables
    return pl.pallas_call(
        _ret_kernel,
        grid=(batch, nb),
        in_specs=[
            slab(0), slab(1), slab(2), slab(3),
            pl.BlockSpec((RET_STEP, D_MODEL), lambda b, i: (row(b, i), 0)),
            pl.BlockSpec((RET_STEP, D_MODEL), lambda b, i: (row(b, i), 0)),
            pl.BlockSpec((RET_STEP, RET_QK_DIM // 2), lambda b, i: (i, 0)),
            pl.BlockSpec((RET_STEP, RET_QK_DIM // 2), lambda b, i: (i, 0)),
            pl.BlockSpec((N_META, hq), lambda b, i: (0, mk_col)),
            pl.BlockSpec((N_META, hv), lambda b, i: (0, mv_col)),
            pl.BlockSpec((N_META, RET_QK_DIM // 2), const2),
            pl.BlockSpec((N_META, RET_QK_DIM // 2), const2),
            pl.BlockSpec((RET_HEADS, RET_BLOCK, RET_BLOCK), const3),
            pl.BlockSpec((RET_HEADS, RET_BLOCK, 1), const3),
            pl.BlockSpec((RET_HEADS, RET_BLOCK, 1), const3),
            pl.BlockSpec((RET_HEADS, N_META, 1), const3),
            pl.BlockSpec((RET_HEADS, 1, 1), const3),
            pl.BlockSpec((1, hv), const2),
            pl.BlockSpec((hv, D_MODEL), const2),
            pl.BlockSpec((D_MODEL, D_MODEL), const2),
            pl.BlockSpec((gu_rows, D_EXPERT), lambda b, i: (row(b, i), 0)),
            pl.BlockSpec((gu_rows, D_EXPERT), lambda b, i: (row(b, i), 0)),
            pl.BlockSpec((dn_rows, D_MODEL), lambda b, i: (row(b, i), 0)),
        ],
        out_specs=[
            pl.BlockSpec((RET_STEP, D_MODEL), lambda b, i: (row(b, i), 0)),
            pl.BlockSpec((gu_rows, 2 * D_EXPERT), lambda b, i: (row(b, i), 0)),
            pl.BlockSpec((dn_rows, D_MODEL), lambda b, i: (row(b, i), 0)),
        ],
        out_shape=[jax.ShapeDtypeStruct((t, D_MODEL), F32),
                   jax.ShapeDtypeStruct((N_EXPERTS * D_MODEL, 2 * D_EXPERT), BF16),
                   jax.ShapeDtypeStruct((N_EXPERTS * D_EXPERT, D_MODEL), BF16)],
        scratch_shapes=[pltpu.VMEM((RET_HEADS, RET_QK_DIM, RET_V_DIM), F32),
                        pltpu.VMEM((RET_STEP, hv), BF16)],
        compiler_params=pltpu.CompilerParams(
            dimension_semantics=("parallel", "arbitrary"),
            vmem_limit_bytes=PROJ_VMEM_LIMIT),
        name="retention_mix",
    )(proj, proj, proj, proj, y_conv, x2d, cos, sin, proj_meta, proj_meta, mcos, msin,
      dmat, qdec, kdec, mkdec, bdec, gn, wo_bf, wout_bf,
      w_gate.reshape(N_EXPERTS * D_MODEL, D_EXPERT), w_up.reshape(N_EXPERTS * D_MODEL, D_EXPERT),
      w_down.reshape(N_EXPERTS * D_EXPERT, D_MODEL))


def _retention_tables(seq):
    half = RET_QK_DIM // 2
    inv = ROPE_BASE ** (-jnp.arange(half, dtype=F32) / half)
    pos = jnp.arange(N_META + seq, dtype=F32)
    ang = pos[:, None] * inv[None, :]
    cos_all, sin_all = jnp.cos(ang), jnp.sin(ang)
    log_gamma = jnp.log(1.0 - 2.0 ** (-5.0 - jnp.arange(RET_HEADS, dtype=F32)))
    idx = jnp.arange(RET_BLOCK, dtype=F32)
    chunk = jnp.arange(RET_BLOCK, dtype=jnp.int32) // CHUNK
    visible = chunk[None, :] <= chunk[:, None]
    dmat = jnp.where(visible[None],
                     jnp.exp(log_gamma[:, None, None] * jnp.abs(idx[:, None] - idx[None, :])),
                     0.0)
    qdec = jnp.exp(log_gamma[:, None] * (idx + 1.0))[:, :, None]
    kdec = jnp.exp(log_gamma[:, None] * (RET_BLOCK - 1.0 - idx))[:, :, None]
    midx = jnp.arange(N_META, dtype=F32)
    mkdec = jnp.exp(log_gamma[:, None] * (N_META - 1.0 - midx))[:, :, None]
    bdec = jnp.exp(log_gamma * RET_BLOCK)[:, None, None]
    return (cos_all[N_META:], sin_all[N_META:], cos_all[:N_META], sin_all[:N_META],
            dmat, qdec, kdec, mkdec, bdec)


def _route(logits):
    lane = lax.broadcasted_iota(jnp.int32, logits.shape, 1)
    neg = jnp.float32(-jnp.inf)
    big = jnp.int32(ROUTER_LANES)

    def first_max(masked):
        val = jnp.max(masked, axis=-1, keepdims=True)
        idx = jnp.min(jnp.where(masked == val, lane, big), axis=-1, keepdims=True)
        return val, idx

    gmask = lane < N_GROUPS
    gmax, gidx = first_max(jnp.where(gmask, logits, neg))
    denom = jnp.sum(jnp.where(gmask, jnp.exp(logits - gmax), 0.0), axis=-1, keepdims=True)
    p_group = 1.0 / denom
    assert EXPERTS_PER_GROUP & (EXPERTS_PER_GROUP - 1) == 0
    shift = EXPERTS_PER_GROUP.bit_length() - 1
    lane_group = (lane - N_GROUPS) >> shift
    in_group = jnp.where(lane_group == gidx, logits, neg)
    v1, i1 = first_max(in_group)
    v2, i2 = first_max(jnp.where(lane == i1, neg, in_group))
    e2 = jnp.exp(v2 - v1)
    w1 = p_group / (1.0 + e2)
    w2 = p_group * e2 / (1.0 + e2)
    return jnp.where(lane == i1, w1, 0.0) + jnp.where(lane == i2, w2, 0.0), gidx


def _moe_kernel(h_ref, g_ref, wr_ref, br_ref, tri_ref, wgu_ref, wd_ref, gf_ref, o_ref,
                xs_ref, cs_ref, pos_ref, y_ref, seg_ref):
    grp = pl.program_id(1)
    tm = h_ref.shape[0]

    @pl.when(grp == 0)
    def _():
        h = h_ref[...]
        ms = jnp.mean(h * h, axis=-1, keepdims=True)
        u = h * lax.rsqrt(ms + EPS) * g_ref[...]
        u_hi = u.astype(BF16)
        u_lo = (u - u_hi.astype(F32)).astype(BF16)
        hi_part = _dot(u_hi, wr_ref[...])
        lo_part = _dot(u_lo, wr_ref[:, :ROUTER_LANES])
        logits = (hi_part[:, :ROUTER_LANES] + (hi_part[:, ROUTER_LANES:] + lo_part)
                  + br_ref[...])
        comb, gidx = _route(logits)

        lane = lax.broadcasted_iota(jnp.int32, (tm, ROUTER_LANES), 1)
        onehot = jnp.where(lane == gidx, 1.0, 0.0)
        counts = jnp.sum(onehot, axis=0, keepdims=True)
        row = lax.broadcasted_iota(jnp.int32, (tm, tm), 0)
        prefix = _dot(tri_ref[...], onehot.astype(BF16))
        lane_row = lax.broadcasted_iota(jnp.int32, (1, ROUTER_LANES), 1)
        start = jnp.int32(0)
        starts = jnp.zeros((1, ROUTER_LANES), F32)
        for gg in range(N_GROUPS):
            seg_ref[gg] = start
            starts = starts + jnp.where(lane_row == gg, start.astype(F32), 0.0)
            start = start + jnp.sum(jnp.where(lane_row == gg, counts, 0.0)).astype(jnp.int32)
        seg_ref[N_GROUPS] = start
        pos = jnp.sum(onehot * (prefix + starts), axis=-1, keepdims=True)
        pos_lanes = jnp.broadcast_to(pos, (tm, ROUTER_LANES))
        pos_ref[...] = pos_lanes
        pos_row = pos_lanes.T[0:1, :].astype(jnp.int32)
        perm = jnp.where(row == pos_row, 1.0, 0.0).astype(BF16)
        c_hi = comb.astype(BF16)
        c_lo = (comb - c_hi.astype(F32)).astype(BF16)
        moved = _dot(perm, jnp.concatenate([u_hi, c_hi, c_lo], axis=-1))
        xs_ref[...] = moved[:, :D_MODEL].astype(BF16)
        cs_ref[...] = (moved[:, D_MODEL:D_MODEL + ROUTER_LANES]
                       + moved[:, D_MODEL + ROUTER_LANES:])
        y_ref[...] = jnp.zeros_like(y_ref)

    seg_lo, seg_hi = seg_ref[grp], seg_ref[grp + 1]

    def experts(row0, n_rows):
        rows = pl.ds(pl.multiple_of(row0, MOE_ALIGN), n_rows)
        xb = xs_ref[rows, :]
        cb = cs_ref[rows, :]
        lane = lax.broadcasted_iota(jnp.int32, cb.shape, 1)
        hidden = []
        for e in range(EXPERTS_PER_GROUP):
            expert_lane = N_GROUPS + grp * EXPERTS_PER_GROUP + e
            w_e = jnp.sum(jnp.where(lane == expert_lane, cb, 0.0), axis=-1, keepdims=True)
            gu = _dot(xb, wgu_ref[e])
            gate, up = gu[:, :D_EXPERT], gu[:, D_EXPERT:]
            hidden.append((gate * _sigmoid(gate) * up * w_e).astype(BF16))
        w_down = wd_ref[...].reshape(EXPERTS_PER_GROUP * D_EXPERT, D_MODEL)
        y_ref[rows, :] += _dot(jnp.concatenate(hidden, axis=-1), w_down)

    lo = seg_lo // MOE_ALIGN
    span = (seg_hi + MOE_ALIGN - 1) // MOE_ALIGN - lo
    nonempty = seg_hi > seg_lo
    for k, n_rows in enumerate(MOE_SPANS):
        fits = span <= n_rows // MOE_ALIGN
        if k > 0:
            fits = jnp.logical_and(fits, span > MOE_SPANS[k - 1] // MOE_ALIGN)

        @pl.when(jnp.logical_and(nonempty, fits))
        def _(n_rows=n_rows):
            experts(jnp.minimum(lo * MOE_ALIGN, tm - n_rows), n_rows)

    @pl.when(span > MOE_SPANS[-1] // MOE_ALIGN)
    def _():
        blk_lo = seg_lo // MOE_BLOCK
        n_blk = (seg_hi + MOE_BLOCK - 1) // MOE_BLOCK - blk_lo
        odd = n_blk % 2 == 1
        n_pairs = jnp.where(odd, n_blk - 3, n_blk) // 2

        def pair_body(p, carry):
            experts((blk_lo + 2 * p) * MOE_BLOCK, 2 * MOE_BLOCK)
            return carry

        lax.fori_loop(0, n_pairs, pair_body, 0)

        @pl.when(odd)
        def _():
            experts((blk_lo + n_blk - 3) * MOE_BLOCK, 3 * MOE_BLOCK)

    @pl.when(grp == N_GROUPS - 1)
    def _():
        pos = pos_ref[:, 0:1].astype(jnp.int32)
        col = lax.broadcasted_iota(jnp.int32, (tm, tm), 1)
        unperm = jnp.where(col == pos, 1.0, 0.0).astype(BF16)
        h2 = h_ref[...] + _dot(unperm, y_ref[...].astype(BF16))
        ms = jnp.mean(h2 * h2, axis=-1, keepdims=True)
        o_ref[...] = h2 * lax.rsqrt(ms + EPS) * gf_ref[...]


def _moe(h1, g, wr2, br, wgu_bf, wd_bf, gf, tm):
    t = h1.shape[0]
    const2 = lambda i, e: (0, 0)
    token = jnp.arange(tm, dtype=jnp.int32)
    earlier = (token[None, :] < token[:, None]).astype(BF16)
    return pl.pallas_call(
        _moe_kernel,
        grid=(t // tm, N_GROUPS),
        in_specs=[
            pl.BlockSpec((tm, D_MODEL), lambda i, e: (i, 0)),
            pl.BlockSpec((1, D_MODEL), const2),
            pl.BlockSpec((D_MODEL, 2 * ROUTER_LANES), const2),
            pl.BlockSpec((1, ROUTER_LANES), const2),
            pl.BlockSpec((tm, tm), const2),
            pl.BlockSpec((EXPERTS_PER_GROUP, D_MODEL, 2 * D_EXPERT), lambda i, e: (e, 0, 0)),
            pl.BlockSpec((EXPERTS_PER_GROUP, D_EXPERT, D_MODEL), lambda i, e: (e, 0, 0)),
            pl.BlockSpec((1, D_MODEL), const2),
        ],
        out_specs=pl.BlockSpec((tm, D_MODEL), lambda i, e: (i, 0)),
        out_shape=jax.ShapeDtypeStruct((t, D_MODEL), F32),
        scratch_shapes=[pltpu.VMEM((tm, D_MODEL), BF16),
                        pltpu.VMEM((tm, ROUTER_LANES), F32),
                        pltpu.VMEM((tm, ROUTER_LANES), F32),
                        pltpu.VMEM((tm, D_MODEL), F32),
                        pltpu.SMEM((SUBLANES,), jnp.int32)],
        compiler_params=pltpu.CompilerParams(
            dimension_semantics=("parallel", "arbitrary"),
            vmem_limit_bytes=MOE_VMEM_LIMIT),
        name="hier_moe",
    )(h1, g, wr2, br, earlier, wgu_bf, wd_bf, gf)


def _split_bf16(w):
    hi = w.astype(BF16)
    lo = (w - hi.astype(F32)).astype(BF16)
    return jnp.concatenate([hi, lo], axis=-1)


def kernel(x, meta_tokens, norm_mix_g, w_in, conv_dw_w, conv_dw_b, conv_ln_g, conv_ln_b,
           conv_pw_w, ret_gn_g, ret_w_o, w_out, norm_ffn_g, w_group_router, b_group_router,
           w_expert_router, b_expert_router, w_expert_gate, w_expert_up, w_expert_down,
           norm_final_g):
    batch, seq, d = x.shape
    assert d == D_MODEL and seq % RET_STEP == 0 and w_in.shape[0] == 1
    t = batch * seq
    x2d = x.reshape(t, d)
    row = lambda v: v.reshape(1, -1)

    proj_meta, w_in_bf = _meta_proj(meta_tokens, row(norm_mix_g[0]), w_in[0], IN_PROJ_COLS)
    proj, y_conv = _proj_conv(x2d, row(norm_mix_g[0]), w_in_bf, proj_meta, conv_dw_w[0],
                              row(conv_dw_b[0]), row(conv_ln_g[0]), row(conv_ln_b[0]),
                              conv_pw_w[0].astype(BF16), seq, PROJ_TILE)
    h1, w_gu, w_dn = _retention(proj, proj_meta, y_conv, x2d, _retention_tables(seq),
                                row(ret_gn_g[0]), ret_w_o[0].astype(BF16), w_out[0].astype(BF16),
                                w_expert_gate[0], w_expert_up[0], w_expert_down[0], batch, seq)
    w_gu = w_gu.reshape(N_EXPERTS, D_MODEL, 2 * D_EXPERT)
    w_dn = w_dn.reshape(N_EXPERTS, D_EXPERT, D_MODEL)

    w_router = jnp.concatenate([w_group_router[0], w_expert_router[0]], axis=1)
    w_router = jnp.pad(w_router, ((0, 0), (0, ROUTER_LANES - w_router.shape[1])))
    b_router = jnp.concatenate([b_group_router[0], b_expert_router[0]])
    b_router = jnp.pad(b_router, (0, ROUTER_LANES - b_router.shape[0])).reshape(1, -1)
    out = _moe(h1, row(norm_ffn_g[0]), _split_bf16(w_router), b_router, w_gu, w_dn,
               row(norm_final_g), min(1024, t))
    return out.reshape(batch, seq, d)
```

```python
import functools

import jax
import jax.numpy as jnp
from jax import lax
from jax.experimental import pallas as pl
from jax.experimental.pallas import tpu as pltpu

D_MODEL = 1024
CHUNK = 64
N_META = 16
CONV_DIM = 1024
CONV_WIDTH = 31
RET_HEADS = 4
RET_QK_DIM = 256
RET_V_DIM = 512
ROPE_BASE = 10000.0
N_GROUPS = 4
EXPERTS_PER_GROUP = 4
N_EXPERTS = N_GROUPS * EXPERTS_PER_GROUP
D_EXPERT = 512
EPS = 1e-6
D_IN = 2 * CONV_DIM + 2 * RET_HEADS * RET_QK_DIM + 2 * RET_HEADS * RET_V_DIM + 2 * D_MODEL

LANES = 128
SUBLANES = 8
CONV_HALO = 32
IN_PROJ_COLS = 2048
PROJ_TILE = 1024
CONV_TILE = 256
CONV_ROWS_PER_ITER = 4
RET_BLOCK = 256
RET_STEP = 512
ROUTER_LANES = LANES
MOE_BLOCK = 128
MOE_ALIGN = 16
MOE_SPANS = (256, 320, 384)
VMEM_LIMIT = 48 * 1024 * 1024
PROJ_VMEM_LIMIT = 56 * 1024 * 1024
MOE_VMEM_LIMIT = 60 * 1024 * 1024

F32 = jnp.float32
BF16 = jnp.bfloat16


def _sigmoid(x):
    return 1.0 / (1.0 + jnp.exp(-x))


def _dot(a, b):
    return jnp.dot(a, b, preferred_element_type=F32)


def _meta_proj_kernel(x_ref, g_ref, w_ref, o_ref, wbf_ref, u_ref):
    @pl.when(pl.program_id(0) == 0)
    def _():
        x = x_ref[...]
        ms = jnp.mean(x * x, axis=-1, keepdims=True)
        u_ref[...] = (x * lax.rsqrt(ms + EPS) * g_ref[...]).astype(BF16)

    w = w_ref[...].astype(BF16)
    wbf_ref[...] = w
    o_ref[...] = _dot(u_ref[...], w).astype(o_ref.dtype)


def _meta_proj(meta, g, w, tn):
    t, d = meta.shape
    n = w.shape[1]
    return pl.pallas_call(
        _meta_proj_kernel,
        grid=(n // tn,),
        in_specs=[
            pl.BlockSpec((t, d), lambda j: (0, 0)),
            pl.BlockSpec((1, d), lambda j: (0, 0)),
            pl.BlockSpec((d, tn), lambda j: (0, j)),
        ],
        out_specs=[pl.BlockSpec((t, tn), lambda j: (0, j)),
                   pl.BlockSpec((d, tn), lambda j: (0, j))],
        out_shape=[jax.ShapeDtypeStruct((t, n), BF16),
                   jax.ShapeDtypeStruct((d, n), BF16)],
        scratch_shapes=[pltpu.VMEM((t, d), BF16)],
        compiler_params=pltpu.CompilerParams(
            dimension_semantics=("arbitrary",),
            vmem_limit_bytes=VMEM_LIMIT),
        name="meta_proj",
    )(meta, g, w)


def _conv_shifts(win_ref, shift_ref):
    span = CONV_TILE + CONV_HALO - SUBLANES
    for cb in range(CONV_DIM // LANES):
        cols = slice(cb * LANES, (cb + 1) * LANES)
        for s in range(1, SUBLANES):
            shift_ref[s - 1, 0:span, cols] = win_ref[s:s + span, cols]


def _conv_taps(cb, r0, win_ref, shift_ref, cbuf_ref, wdw_ref, bdw_ref):
    def group(m):
        return slice(r0 + m * SUBLANES, r0 + (m + 1) * SUBLANES)

    first = CONV_HALO - (CONV_WIDTH - 1)
    cols = pl.ds(pl.multiple_of(cb * LANES, LANES), LANES)
    bias = jnp.broadcast_to(bdw_ref[:, cols], (SUBLANES, LANES))
    accs = [bias] * CONV_ROWS_PER_ITER
    for shift in range(SUBLANES):
        js = [j for j in range(CONV_WIDTH) if (first + j) % SUBLANES == shift]
        tiles = [(first + j) // SUBLANES for j in js]
        taps = [jnp.broadcast_to(wdw_ref[j:j + 1, cols], (SUBLANES, LANES)) for j in js]
        wins = {}
        for m in range(min(tiles), max(tiles) + CONV_ROWS_PER_ITER):
            wins[m] = (win_ref[group(m), cols] if shift == 0
                       else shift_ref[shift - 1, group(m), cols])
        for c in range(CONV_ROWS_PER_ITER):
            for tap, m in zip(taps, tiles):
                accs[c] = accs[c] + tap * wins[m + c]
    for c in range(CONV_ROWS_PER_ITER):
        cbuf_ref[group(c), cols] = accs[c]


def _proj_conv_kernel(x_ref, g_ref, w_ref, ma_ref, mg_ref, wdw_ref, bdw_ref, lng_ref, lnb_ref,
                      wpw_ref, proj_ref, yconv_ref, u_ref, hbuf_ref, win_ref, shift_ref,
                      cbuf_ref, *, tiles_per_seq):
    i, j = pl.program_id(0), pl.program_id(1)
    tm = x_ref.shape[0]

    @pl.when(j == 0)
    def _():
        x = x_ref[...]
        ms = jnp.mean(x * x, axis=-1, keepdims=True)
        u = (x * lax.rsqrt(ms + EPS) * g_ref[...]).astype(BF16)
        u_ref[...] = u

        @pl.when(i % tiles_per_seq == 0)
        def _():
            hbuf_ref[0:CONV_HALO - N_META, :] = jnp.zeros((CONV_HALO - N_META, CONV_DIM), F32)
            ma = ma_ref[...].astype(F32)
            mg = mg_ref[...].astype(F32)
            hbuf_ref[CONV_HALO - N_META:CONV_HALO, :] = ma * _sigmoid(mg)

        @pl.when(i % tiles_per_seq != 0)
        def _():
            hbuf_ref[0:CONV_HALO, :] = hbuf_ref[tm:tm + CONV_HALO, :]

        glu = _dot(u, w_ref[...])
        hbuf_ref[CONV_HALO:CONV_HALO + tm, :] = glu[:, :CONV_DIM] * _sigmoid(glu[:, CONV_DIM:])

    @pl.when(j > 0)
    def _():
        base = pl.multiple_of((j - 1) * CONV_TILE, CONV_TILE)
        win_ref[...] = hbuf_ref[pl.ds(base, CONV_HALO + CONV_TILE), :]
        _conv_shifts(win_ref, shift_ref)
        proj_ref[0] = _dot(u_ref[...], w_ref[...]).astype(proj_ref.dtype)

        def block_body(cb, carry):
            for r0 in range(0, CONV_TILE, SUBLANES * CONV_ROWS_PER_ITER):
                _conv_taps(cb, r0, win_ref, shift_ref, cbuf_ref, wdw_ref, bdw_ref)
            return carry

        lax.fori_loop(0, CONV_DIM // LANES, block_body, 0)
        c = cbuf_ref[...]
        mu = jnp.mean(c, axis=-1, keepdims=True)
        cc = c - mu
        var = jnp.mean(cc * cc, axis=-1, keepdims=True)
        y = (cc * lax.rsqrt(var + EPS) * lng_ref[...] + lnb_ref[...]).astype(BF16)
        y = y * _sigmoid(y)
        yconv_ref[pl.ds(base, CONV_TILE), :] = _dot(y, wpw_ref[...]).astype(yconv_ref.dtype)


def _proj_conv(x2d, g, w_bf, proj_meta, wdw, bdw, lng, lnb, wpw_bf, seq, tm):
    t, d = x2d.shape
    n_steps = D_IN // IN_PROJ_COLS
    assert IN_PROJ_COLS == 2 * CONV_DIM and (n_steps - 1) * CONV_TILE == tm and seq % tm == 0
    const = lambda i, j: (0, 0)
    return pl.pallas_call(
        functools.partial(_proj_conv_kernel, tiles_per_seq=seq // tm),
        grid=(t // tm, n_steps),
        in_specs=[
            pl.BlockSpec((tm, d), lambda i, j: (i, 0)),
            pl.BlockSpec((1, d), const),
            pl.BlockSpec((d, IN_PROJ_COLS), lambda i, j: (0, j)),
            pl.BlockSpec((N_META, CONV_DIM), lambda i, j: (0, 0)),
            pl.BlockSpec((N_META, CONV_DIM), lambda i, j: (0, 1)),
            pl.BlockSpec((CONV_WIDTH, CONV_DIM), const),
            pl.BlockSpec((1, CONV_DIM), const),
            pl.BlockSpec((1, CONV_DIM), const),
            pl.BlockSpec((1, CONV_DIM), const),
            pl.BlockSpec((CONV_DIM, D_MODEL), const),
        ],
        out_specs=[
            pl.BlockSpec((1, tm, IN_PROJ_COLS), lambda i, j: (jnp.maximum(j - 1, 0), i, 0)),
            pl.BlockSpec((tm, D_MODEL), lambda i, j: (i, 0)),
        ],
        out_shape=[jax.ShapeDtypeStruct((n_steps - 1, t, IN_PROJ_COLS), BF16),
                   jax.ShapeDtypeStruct((t, D_MODEL), BF16)],
        scratch_shapes=[pltpu.VMEM((tm, d), BF16),
                        pltpu.VMEM((CONV_HALO + tm, CONV_DIM), F32),
                        pltpu.VMEM((CONV_HALO + CONV_TILE, CONV_DIM), F32),
                        pltpu.VMEM((SUBLANES - 1, CONV_HALO + CONV_TILE, CONV_DIM), F32),
                        pltpu.VMEM((CONV_TILE, CONV_DIM), F32)],
        compiler_params=pltpu.CompilerParams(
            dimension_semantics=("arbitrary", "arbitrary"),
            vmem_limit_bytes=PROJ_VMEM_LIMIT),
        name="proj_conv",
    )(x2d, g, w_bf, proj_meta, proj_meta, wdw, bdw, lng, lnb, wpw_bf)


def _rotary(x, cos, sin):
    half = x.shape[-1] // 2
    x1, x2 = x[:, :half], x[:, half:]
    return jnp.concatenate([x1 * cos - x2 * sin, x2 * cos + x1 * sin], axis=-1)


def _ret_kernel(qk_ref, v_ref, gret_ref, gm_ref, yconv_ref, x_ref, cos_ref, sin_ref,
                mk_ref, mv_ref, mcos_ref, msin_ref, dmat_ref, qdec_ref, kdec_ref, mkdec_ref,
                bdec_ref, gn_ref, wo_ref, wout_ref, eg_ref, eu_ref, ed_ref,
                o_ref, egu_ref, edn_ref, state_ref, gated_ref):
    i = pl.program_id(1)
    k_scale = RET_QK_DIM ** -0.5

    egu_ref[:, :D_EXPERT] = eg_ref[...].astype(BF16)
    egu_ref[:, D_EXPERT:] = eu_ref[...].astype(BF16)
    edn_ref[...] = ed_ref[...].astype(BF16)

    @pl.when(i == 0)
    def _():
        mcos, msin = mcos_ref[...], msin_ref[...]
        for h in range(RET_HEADS):
            mk = mk_ref[:, h * RET_QK_DIM:(h + 1) * RET_QK_DIM].astype(F32)
            mk = _rotary(mk, mcos, msin) * k_scale * mkdec_ref[h]
            mv = mv_ref[:, h * RET_V_DIM:(h + 1) * RET_V_DIM]
            state_ref[h] = lax.dot_general(mk.astype(BF16), mv, (((0,), (0,)), ((), ())),
                                           preferred_element_type=F32)

    for sb in range(RET_STEP // RET_BLOCK):
        rows = slice(sb * RET_BLOCK, (sb + 1) * RET_BLOCK)
        cos, sin = cos_ref[rows, :], sin_ref[rows, :]
        for h in range(RET_HEADS):
            qq = slice(h * RET_QK_DIM, (h + 1) * RET_QK_DIM)
            kk = slice((RET_HEADS + h) * RET_QK_DIM, (RET_HEADS + h + 1) * RET_QK_DIM)
            vv = slice(h * RET_V_DIM, (h + 1) * RET_V_DIM)
            q = _rotary(qk_ref[0, rows, qq].astype(F32), cos, sin)
            k = _rotary(qk_ref[0, rows, kk].astype(F32), cos, sin) * k_scale
            v = v_ref[0, rows, vv]
            q_bf = q.astype(BF16)
            scores = lax.dot_general(q_bf, k.astype(BF16), (((1,), (1,)), ((), ())),
                                     preferred_element_type=F32) * dmat_ref[h]
            state = state_ref[h]
            lhs = jnp.concatenate([scores.astype(BF16), (q * qdec_ref[h]).astype(BF16)], axis=1)
            o = _dot(lhs, jnp.concatenate([v, state.astype(BF16)], axis=0))
            state_ref[h] = state * bdec_ref[h] + lax.dot_general(
                (k * kdec_ref[h]).astype(BF16), v, (((0,), (0,)), ((), ())),
                preferred_element_type=F32)
            mu = jnp.mean(o, axis=-1, keepdims=True)
            oc = o - mu
            var = jnp.mean(oc * oc, axis=-1, keepdims=True)
            on = oc * lax.rsqrt(var + EPS) * gn_ref[:, vv]
            gr = gret_ref[0, rows, vv]
            gated_ref[rows, vv] = gr * _sigmoid(gr) * on.astype(BF16)

    y_ret = _dot(gated_ref[...], wo_ref[...])
    ga = gm_ref[0, :, :D_MODEL]
    gb = gm_ref[0, :, D_MODEL:]
    merged = _sigmoid(ga) * yconv_ref[...] + _sigmoid(gb) * y_ret.astype(BF16)
    o_ref[...] = x_ref[...] + _dot(merged, wout_ref[...])


def _retention(proj, proj_meta, y_conv, x2d, tables, gn, wo_bf, wout_bf, w_gate, w_up, w_down,
               batch, seq):
    t = proj.shape[1]
    nb = seq // RET_STEP
    n_steps = batch * nb
    gu_rows, dn_rows = N_EXPERTS * D_MODEL // n_steps, N_EXPERTS * D_EXPERT // n_steps
    assert gu_rows * n_steps == N_EXPERTS * D_MODEL and dn_rows * n_steps == N_EXPERTS * D_EXPERT
    assert gu_rows % (2 * SUBLANES) == 0 and dn_rows % (2 * SUBLANES) == 0
    hq = RET_HEADS * RET_QK_DIM
    hv = RET_HEADS * RET_V_DIM
    assert 2 * hq == hv == 2 * D_MODEL == proj.shape[2]
    slab = lambda s: pl.BlockSpec((1, RET_STEP, hv), lambda b, i: (s, row(b, i), 0))
    mk_col, mv_col = 2 * CONV_DIM // hq + 1, (2 * CONV_DIM + 2 * hq) // hv
    row = lambda b, i: b * nb + i
    const2 = lambda b, i: (0, 0)
    const3 = lambda b, i: (0, 0, 0)
    cos, sin, mcos, msin, dmat, qdec, kdec, mkdec, bdec = tables
    return pl.pallas_call(
        _ret_kernel,
        grid=(batch, nb),
        in_specs=[
            slab(0), slab(1), slab(2), slab(3),
            pl.BlockSpec((RET_STEP, D_MODEL), lambda b, i: (row(b, i), 0)),
            pl.BlockSpec((RET_STEP, D_MODEL), lambda b, i: (row(b, i), 0)),
            pl.BlockSpec((RET_STEP, RET_QK_DIM // 2), lambda b, i: (i, 0)),
            pl.BlockSpec((RET_STEP, RET_QK_DIM // 2), lambda b, i: (i, 0)),
            pl.BlockSpec((N_META, hq), lambda b, i: (0, mk_col)),
            pl.BlockSpec((N_META, hv), lambda b, i: (0, mv_col)),
            pl.BlockSpec((N_META, RET_QK_DIM // 2), const2),
            pl.BlockSpec((N_META, RET_QK_DIM // 2), const2),
            pl.BlockSpec((RET_HEADS, RET_BLOCK, RET_BLOCK), const3),
            pl.BlockSpec((RET_HEADS, RET_BLOCK, 1), const3),
            pl.BlockSpec((RET_HEADS, RET_BLOCK, 1), const3),
            pl.BlockSpec((RET_HEADS, N_META, 1), const3),
            pl.BlockSpec((RET_HEADS, 1, 1), const3),
            pl.BlockSpec((1, hv), const2),
            pl.BlockSpec((hv, D_MODEL), const2),
            pl.BlockSpec((D_MODEL, D_MODEL), const2),
            pl.BlockSpec((gu_rows, D_EXPERT), lambda b, i: (row(b, i), 0)),
            pl.BlockSpec((gu_rows, D_EXPERT), lambda b, i: (row(b, i), 0)),
            pl.BlockSpec((dn_rows, D_MODEL), lambda b, i: (row(b, i), 0)),
        ],
        out_specs=[
            pl.BlockSpec((RET_STEP, D_MODEL), lambda b, i: (row(b, i), 0)),
            pl.BlockSpec((gu_rows, 2 * D_EXPERT), lambda b, i: (row(b, i), 0)),
            pl.BlockSpec((dn_rows, D_MODEL), lambda b, i: (row(b, i), 0)),
        ],
        out_shape=[jax.ShapeDtypeStruct((t, D_MODEL), F32),
                   jax.ShapeDtypeStruct((N_EXPERTS * D_MODEL, 2 * D_EXPERT), BF16),
                   jax.ShapeDtypeStruct((N_EXPERTS * D_EXPERT, D_MODEL), BF16)],
        scratch_shapes=[pltpu.VMEM((RET_HEADS, RET_QK_DIM, RET_V_DIM), F32),
                        pltpu.VMEM((RET_STEP, hv), BF16)],
        compiler_params=pltpu.CompilerParams(
            dimension_semantics=("parallel", "arbitrary"),
            vmem_limit_bytes=PROJ_VMEM_LIMIT),
        name="retention_mix",
    )(proj, proj, proj, proj, y_conv, x2d, cos, sin, proj_meta, proj_meta, mcos, msin,
      dmat, qdec, kdec, mkdec, bdec, gn, wo_bf, wout_bf,
      w_gate.reshape(N_EXPERTS * D_MODEL, D_EXPERT), w_up.reshape(N_EXPERTS * D_MODEL, D_EXPERT),
      w_down.reshape(N_EXPERTS * D_EXPERT, D_MODEL))


def _retention_tables(seq):
    half = RET_QK_DIM // 2
    inv = ROPE_BASE ** (-jnp.arange(half, dtype=F32) / half)
    pos = jnp.arange(N_META + seq, dtype=F32)
    ang = pos[:, None] * inv[None, :]
    cos_all, sin_all = jnp.cos(ang), jnp.sin(ang)
    log_gamma = jnp.log(1.0 - 2.0 ** (-5.0 - jnp.arange(RET_HEADS, dtype=F32)))
    idx = jnp.arange(RET_BLOCK, dtype=F32)
    chunk = jnp.arange(RET_BLOCK, dtype=jnp.int32) // CHUNK
    visible = chunk[None, :] <= chunk[:, None]
    dmat = jnp.where(visible[None],
                     jnp.exp(log_gamma[:, None, None] * jnp.abs(idx[:, None] - idx[None, :])),
                     0.0)
    qdec = jnp.exp(log_gamma[:, None] * (idx + 1.0))[:, :, None]
    kdec = jnp.exp(log_gamma[:, None] * (RET_BLOCK - 1.0 - idx))[:, :, None]
    midx = jnp.arange(N_META, dtype=F32)
    mkdec = jnp.exp(log_gamma[:, None] * (N_META - 1.0 - midx))[:, :, None]
    bdec = jnp.exp(log_gamma * RET_BLOCK)[:, None, None]
    return (cos_all[N_META:], sin_all[N_META:], cos_all[:N_META], sin_all[:N_META],
            dmat, qdec, kdec, mkdec, bdec)


def _route(logits):
    lane = lax.broadcasted_iota(jnp.int32, logits.shape, 1)
    neg = jnp.float32(-jnp.inf)
    big = jnp.int32(ROUTER_LANES)

    def first_max(masked):
        val = jnp.max(masked, axis=-1, keepdims=True)
        idx = jnp.min(jnp.where(masked == val, lane, big), axis=-1, keepdims=True)
        return val, idx

    gmask = lane < N_GROUPS
    gmax, gidx = first_max(jnp.where(gmask, logits, neg))
    denom = jnp.sum(jnp.where(gmask, jnp.exp(logits - gmax), 0.0), axis=-1, keepdims=True)
    p_group = 1.0 / denom
    assert EXPERTS_PER_GROUP & (EXPERTS_PER_GROUP - 1) == 0
    shift = EXPERTS_PER_GROUP.bit_length() - 1
    lane_group = (lane - N_GROUPS) >> shift
    in_group = jnp.where(lane_group == gidx, logits, neg)
    v1, i1 = first_max(in_group)
    v2, i2 = first_max(jnp.where(lane == i1, neg, in_group))
    e2 = jnp.exp(v2 - v1)
    w1 = p_group / (1.0 + e2)
    w2 = p_group * e2 / (1.0 + e2)
    return jnp.where(lane == i1, w1, 0.0) + jnp.where(lane == i2, w2, 0.0), gidx


def _moe_kernel(h_ref, g_ref, wr_ref, br_ref, tri_ref, wgu_ref, wd_ref, gf_ref, o_ref,
                xs_ref, cs_ref, pos_ref, y_ref, seg_ref):
    grp = pl.program_id(1)
    tm = h_ref.shape[0]

    @pl.when(grp == 0)
    def _():
        h = h_ref[...]
        ms = jnp.mean(h * h, axis=-1, keepdims=True)
        u = h * lax.rsqrt(ms + EPS) * g_ref[...]
        u_hi = u.astype(BF16)
        u_lo = (u - u_hi.astype(F32)).astype(BF16)
        hi_part = _dot(u_hi, wr_ref[...])
        lo_part = _dot(u_lo, wr_ref[:, :ROUTER_LANES])
        logits = (hi_part[:, :ROUTER_LANES] + (hi_part[:, ROUTER_LANES:] + lo_part)
                  + br_ref[...])
        comb, gidx = _route(logits)

        lane = lax.broadcasted_iota(jnp.int32, (tm, ROUTER_LANES), 1)
        onehot = jnp.where(lane == gidx, 1.0, 0.0)
        counts = jnp.sum(onehot, axis=0, keepdims=True)
        row = lax.broadcasted_iota(jnp.int32, (tm, tm), 0)
        prefix = _dot(tri_ref[...], onehot.astype(BF16))
        lane_row = lax.broadcasted_iota(jnp.int32, (1, ROUTER_LANES), 1)
        start = jnp.int32(0)
        starts = jnp.zeros((1, ROUTER_LANES), F32)
        for gg in range(N_GROUPS):
            seg_ref[gg] = start
            starts = starts + jnp.where(lane_row == gg, start.astype(F32), 0.0)
            start = start + jnp.sum(jnp.where(lane_row == gg, counts, 0.0)).astype(jnp.int32)
        seg_ref[N_GROUPS] = start
        pos = jnp.sum(onehot * (prefix + starts), axis=-1, keepdims=True)
        pos_lanes = jnp.broadcast_to(pos, (tm, ROUTER_LANES))
        pos_ref[...] = pos_lanes
        pos_row = pos_lanes.T[0:1, :].astype(jnp.int32)
        perm = jnp.where(row == pos_row, 1.0, 0.0).astype(BF16)
        c_hi = comb.astype(BF16)
        c_lo = (comb - c_hi.astype(F32)).astype(BF16)
        moved = _dot(perm, jnp.concatenate([u_hi, c_hi, c_lo], axis=-1))
        xs_ref[...] = moved[:, :D_MODEL].astype(BF16)
        cs_ref[...] = (moved[:, D_MODEL:D_MODEL + ROUTER_LANES]
                       + moved[:, D_MODEL + ROUTER_LANES:])
        y_ref[...] = jnp.zeros_like(y_ref)

    seg_lo, seg_hi = seg_ref[grp], seg_ref[grp + 1]

    def experts(row0, n_rows):
        rows = pl.ds(pl.multiple_of(row0, MOE_ALIGN), n_rows)
        xb = xs_ref[rows, :]
        cb = cs_ref[rows, :]
        lane = lax.broadcasted_iota(jnp.int32, cb.shape, 1)
        hidden = []
        for e in range(EXPERTS_PER_GROUP):
            expert_lane = N_GROUPS + grp * EXPERTS_PER_GROUP + e
            w_e = jnp.sum(jnp.where(lane == expert_lane, cb, 0.0), axis=-1, keepdims=True)
            gu = _dot(xb, wgu_ref[e])
            gate, up = gu[:, :D_EXPERT], gu[:, D_EXPERT:]
            hidden.append((gate * _sigmoid(gate) * up * w_e).astype(BF16))
        w_down = wd_ref[...].reshape(EXPERTS_PER_GROUP * D_EXPERT, D_MODEL)
        y_ref[rows, :] += _dot(jnp.concatenate(hidden, axis=-1), w_down)

    lo = seg_lo // MOE_ALIGN
    span = (seg_hi + MOE_ALIGN - 1) // MOE_ALIGN - lo
    nonempty = seg_hi > seg_lo
    for k, n_rows in enumerate(MOE_SPANS):
        fits = span <= n_rows // MOE_ALIGN
        if k > 0:
            fits = jnp.logical_and(fits, span > MOE_SPANS[k - 1] // MOE_ALIGN)

        @pl.when(jnp.logical_and(nonempty, fits))
        def _(n_rows=n_rows):
            experts(jnp.minimum(lo * MOE_ALIGN, tm - n_rows), n_rows)

    @pl.when(span > MOE_SPANS[-1] // MOE_ALIGN)
    def _():
        blk_lo = seg_lo // MOE_BLOCK
        n_blk = (seg_hi + MOE_BLOCK - 1) // MOE_BLOCK - blk_lo
        odd = n_blk % 2 == 1
        n_pairs = jnp.where(odd, n_blk - 3, n_blk) // 2

        def pair_body(p, carry):
            experts((blk_lo + 2 * p) * MOE_BLOCK, 2 * MOE_BLOCK)
            return carry

        lax.fori_loop(0, n_pairs, pair_body, 0)

        @pl.when(odd)
        def _():
            experts((blk_lo + n_blk - 3) * MOE_BLOCK, 3 * MOE_BLOCK)

    @pl.when(grp == N_GROUPS - 1)
    def _():
        pos = pos_ref[:, 0:1].astype(jnp.int32)
        col = lax.broadcasted_iota(jnp.int32, (tm, tm), 1)
        unperm = jnp.where(col == pos, 1.0, 0.0).astype(BF16)
        h2 = h_ref[...] + _dot(unperm, y_ref[...].astype(BF16))
        ms = jnp.mean(h2 * h2, axis=-1, keepdims=True)
        o_ref[...] = h2 * lax.rsqrt(ms + EPS) * gf_ref[...]


def _moe(h1, g, wr2, br, wgu_bf, wd_bf, gf, tm):
    t = h1.shape[0]
    const2 = lambda i, e: (0, 0)
    token = jnp.arange(tm, dtype=jnp.int32)
    earlier = (token[None, :] < token[:, None]).astype(BF16)
    return pl.pallas_call(
        _moe_kernel,
        grid=(t // tm, N_GROUPS),
        in_specs=[
            pl.BlockSpec((tm, D_MODEL), lambda i, e: (i, 0)),
            pl.BlockSpec((1, D_MODEL), const2),
            pl.BlockSpec((D_MODEL, 2 * ROUTER_LANES), const2),
            pl.BlockSpec((1, ROUTER_LANES), const2),
            pl.BlockSpec((tm, tm), const2),
            pl.BlockSpec((EXPERTS_PER_GROUP, D_MODEL, 2 * D_EXPERT), lambda i, e: (e, 0, 0)),
            pl.BlockSpec((EXPERTS_PER_GROUP, D_EXPERT, D_MODEL), lambda i, e: (e, 0, 0)),
            pl.BlockSpec((1, D_MODEL), const2),
        ],
        out_specs=pl.BlockSpec((tm, D_MODEL), lambda i, e: (i, 0)),
        out_shape=jax.ShapeDtypeStruct((t, D_MODEL), F32),
        scratch_shapes=[pltpu.VMEM((tm, D_MODEL), BF16),
                        pltpu.VMEM((tm, ROUTER_LANES), F32),
                        pltpu.VMEM((tm, ROUTER_LANES), F32),
                        pltpu.VMEM((tm, D_MODEL), F32),
                        pltpu.SMEM((SUBLANES,), jnp.int32)],
        compiler_params=pltpu.CompilerParams(
            dimension_semantics=("parallel", "arbitrary"),
            vmem_limit_bytes=MOE_VMEM_LIMIT),
        name="hier_moe",
    )(h1, g, wr2, br, earlier, wgu_bf, wd_bf, gf)


def _split_bf16(w):
    hi = w.astype(BF16)
    lo = (w - hi.astype(F32)).astype(BF16)
    return jnp.concatenate([hi, lo], axis=-1)


def kernel(x, meta_tokens, norm_mix_g, w_in, conv_dw_w, conv_dw_b, conv_ln_g, conv_ln_b,
           conv_pw_w, ret_gn_g, ret_w_o, w_out, norm_ffn_g, w_group_router, b_group_router,
           w_expert_router, b_expert_router, w_expert_gate, w_expert_up, w_expert_down,
           norm_final_g):
    batch, seq, d = x.shape
    assert d == D_MODEL and seq % RET_STEP == 0 and w_in.shape[0] == 1
    t = batch * seq
    x2d = x.reshape(t, d)
    row = lambda v: v.reshape(1, -1)

    proj_meta, w_in_bf = _meta_proj(meta_tokens, row(norm_mix_g[0]), w_in[0], IN_PROJ_COLS)
    proj, y_conv = _proj_conv(x2d, row(norm_mix_g[0]), w_in_bf, proj_meta, conv_dw_w[0],
                              row(conv_dw_b[0]), row(conv_ln_g[0]), row(conv_ln_b[0]),
                              conv_pw_w[0].astype(BF16), seq, PROJ_TILE)
    h1, w_gu, w_dn = _retention(proj, proj_meta, y_conv, x2d, _retention_tables(seq),
                                row(ret_gn_g[0]), ret_w_o[0].astype(BF16), w_out[0].astype(BF16),
                                w_expert_gate[0], w_expert_up[0], w_expert_down[0], batch, seq)
    w_gu = w_gu.reshape(N_EXPERTS, D_MODEL, 2 * D_EXPERT)
    w_dn = w_dn.reshape(N_EXPERTS, D_EXPERT, D_MODEL)

    w_router = jnp.concatenate([w_group_router[0], w_expert_router[0]], axis=1)
    w_router = jnp.pad(w_router, ((0, 0), (0, ROUTER_LANES - w_router.shape[1])))
    b_router = jnp.concatenate([b_group_router[0], b_expert_router[0]])
    b_router = jnp.pad(b_router, (0, ROUTER_LANES - b_router.shape[0])).reshape(1, -1)
    out = _moe(h1, row(norm_ffn_g[0]), _split_bf16(w_router), b_router, w_gu, w_dn,
               row(norm_final_g), min(1024, t))
    return out.reshape(batch, seq, d)
```
